```python
import math
import jax
import jax.numpy as jnp
from jax import lax
import numpy as np

D_MODEL = 1024
BATCH = 8
SEQ = 2048
DEPTH = 1
DEC_BATCH = 128
DEC_SEQ = 8
PAST_LEN = 8192
PAGE_SIZE = 128

HEAD_DIM = 64
NSA_HEADS = 8
NSA_KV = 2
NSA_HPG = NSA_HEADS // NSA_KV
CMP_LEN = 32
CMP_STRIDE = 16
CMP_HIDDEN = 64
SLC_LEN = 64
SLC_TOPN = 16
WINDOW = 512
FORCE_BONUS = 1000.0
MOBA_HEADS = 8
MOBA_KV = 2
MOBA_HPG = MOBA_HEADS // MOBA_KV
MOBA_BLOCK = 256
MOBA_TOPK = 3
MEM_LEN = 256
MEM_HEADS = 4
MEM_HEAD_DIM = 128
BRANCH_WIDTH = 512
N_BRANCH = 3
REL_BUCKETS = 32
REL_MAX_DIST = 128
PEER_HEADS = 8
PEER_NKEYS = 128
PEER_EXPERTS = PEER_NKEYS * PEER_NKEYS
PEER_DKEY = 128
PEER_TOPK = 16
PEER_CHUNK = 256
Q_BLOCK = 128
RMS_EPS = 1e-6
NEG_INF = -1e30
NSA_KV_W = NSA_KV * HEAD_DIM
MOBA_KV_W = MOBA_KV * HEAD_DIM
IN_SPLITS = (NSA_HEADS * HEAD_DIM, NSA_KV_W, NSA_KV_W, NSA_KV_W, NSA_KV_W, NSA_KV_W, NSA_KV_W,
             3 * NSA_HEADS, MOBA_HEADS * HEAD_DIM, MOBA_KV_W, MOBA_KV_W,
             MEM_HEADS * MEM_HEAD_DIM, N_BRANCH * D_MODEL)
IN_COLS = sum(IN_SPLITS)

kernel_name = 'hybrid_nsa_moba_peer_decode_step'


def rms_norm(x, g):
    xf = x.astype(jnp.float32)
    y = xf * lax.rsqrt(jnp.mean(xf * xf, axis=-1, keepdims=True) + RMS_EPS)
    return (y * g.astype(jnp.float32)).astype(x.dtype)


def sigmoid32(x):
    return jax.nn.sigmoid(x.astype(jnp.float32)).astype(x.dtype)


def masked_softmax(s, mask):
    s = jnp.where(mask, s.astype(jnp.float32), NEG_INF)
    return jnp.where(mask, jax.nn.softmax(s, axis=-1), 0.0)


def rel_bucket(dist):
    n = jnp.maximum(dist, 0)
    exact = REL_BUCKETS // 2
    nf = jnp.maximum(n, 1).astype(jnp.float32)
    large = exact + (jnp.log(nf / exact) / math.log(REL_MAX_DIST / exact)
                     * (REL_BUCKETS - exact)).astype(jnp.int32)
    return jnp.where(n < exact, n, jnp.minimum(large, REL_BUCKETS - 1))


def gather_pages(pool, page_table):
    g = pool[page_table]
    return g.reshape(page_table.shape[0], -1, pool.shape[2], pool.shape[3])


def compress(x_raw, pe, w1, b1, w2):
    B, Tk, G, dh = x_raw.shape
    r_sub = CMP_LEN // CMP_STRIDE
    n_sub = Tk // CMP_STRIDE
    n_cmp = n_sub - r_sub + 1
    xs = x_raw[:, :n_sub * CMP_STRIDE].reshape(B, n_sub, CMP_STRIDE, G, dh)
    w1r = w1.reshape(r_sub, CMP_STRIDE, dh, CMP_HIDDEN)
    per = pe.reshape(r_sub, CMP_STRIDE, dh)
    h = b1
    for r in range(r_sub):
        a = jnp.einsum('bnlgd,lde->bnge', xs, w1r[r]) + jnp.einsum('ld,lde->e', per[r], w1r[r])
        h = h + a[:, r:r + n_cmp]
    return jnp.einsum('bnge,ed->bngd', jax.nn.gelu(h), w2)


def nsa_mixer(q, gate_logit, kc_raw, vc_raw, k_slc, v_slc, k_win, v_win, q_start, win_start, w):
    B, Tq = q.shape[0], q.shape[1]
    Tk = kc_raw.shape[1]
    scale = HEAD_DIM ** -0.5
    qg = q.reshape(B, Tq, NSA_KV, NSA_HPG, HEAD_DIM)
    t = q_start + jnp.arange(Tq)
    tbl_g = w['rel_bias_table'][:, :NSA_HEADS].reshape(REL_BUCKETS, NSA_KV, NSA_HPG)

    kc = rms_norm(compress(kc_raw, w['cmp_pos_emb'][0], w['cmp_w1'][0], w['cmp_b1'][0], w['cmp_w2'][0]),
                  w['nsa_k_gain'][0])
    vc = compress(vc_raw, w['cmp_pos_emb'][1], w['cmp_w1'][1], w['cmp_b1'][1], w['cmp_w2'][1])
    n_cmp = kc.shape[1]
    cmp_end = jnp.arange(n_cmp) * CMP_STRIDE + CMP_LEN - 1
    dist_c = t[:, None] - cmp_end[None, :]
    bias_c = jnp.transpose(tbl_g[rel_bucket(dist_c)], (2, 3, 0, 1))
    s_c = jnp.einsum('btghd,bngd->bghtn', qg, kc) * scale + bias_c
    p_c = masked_softmax(s_c, dist_c >= 0)
    o_c = jnp.einsum('bghtn,bngd->btghd', p_c.astype(vc.dtype), vc)

    imp = jnp.transpose(p_c.sum(axis=2), (0, 2, 1, 3))
    n_slc = -(-Tk // SLC_LEN)
    ratio = SLC_LEN // CMP_STRIDE
    c_sub = CMP_LEN // CMP_STRIDE
    imp_pad = jnp.pad(imp, ((0, 0), (0, 0), (0, 0), (c_sub - 1, ratio * n_slc + ratio - n_cmp)))
    p_slc = 0.0
    for o in range(-(c_sub - 1), ratio):
        wgt = float(min(o + c_sub, ratio) - max(o, 0))
        st = c_sub - 1 + o
        p_slc = p_slc + wgt * imp_pad[..., st:st + ratio * n_slc:ratio]
    blk = jnp.arange(n_slc)[None, :]
    qb = (t // SLC_LEN)[:, None]
    forced = ((blk == 0) | (blk == qb) | (blk == qb - 1)).astype(jnp.float32)
    future = blk > qb
    score = jnp.where(future[None, :, None, :], NEG_INF, p_slc + FORCE_BONUS * forced[None, :, None, :])
    k_sel = min(SLC_TOPN, n_slc)
    sel_val, sel_idx = lax.top_k(score, k_sel)
    sel_ok = sel_val > 0.5 * NEG_INF
    pad_s = n_slc * SLC_LEN - Tk
    ks_b = jnp.pad(k_slc, ((0, 0), (0, pad_s), (0, 0), (0, 0))).reshape(B, n_slc, SLC_LEN, NSA_KV, HEAD_DIM)
    vs_b = jnp.pad(v_slc, ((0, 0), (0, pad_s), (0, 0), (0, 0))).reshape(B, n_slc, SLC_LEN, NSA_KV, HEAD_DIM)

    kw_p = jnp.pad(k_win, ((0, 0), (WINDOW, 0), (0, 0), (0, 0)))
    vw_p = jnp.pad(v_win, ((0, 0), (WINDOW, 0), (0, 0), (0, 0)))
    qblk = min(Q_BLOCK, Tq)
    n_chunks = Tq // qblk
    lw = WINDOW + qblk - 1
    bi = jnp.arange(B)[:, None, None, None]
    gi = jnp.arange(NSA_KV)[None, None, :, None]
    gi5 = jnp.arange(NSA_KV)[None, None, :, None, None]

    def chunk(c):
        off = c * qblk
        t0 = q_start + off
        tc = t0 + jnp.arange(qblk)
        qc = lax.dynamic_slice_in_dim(qg, off, qblk, 1)
        idc = lax.dynamic_slice_in_dim(sel_idx, off, qblk, 1)
        okc = lax.dynamic_slice_in_dim(sel_ok, off, qblk, 1)
        gk = ks_b[bi, idc, :, gi, :]
        gv = vs_b[bi, idc, :, gi, :]
        kpos = idc[..., None] * SLC_LEN + jnp.arange(SLC_LEN)
        dsel = tc[None, :, None, None, None] - kpos
        bias_s = jnp.moveaxis(tbl_g[rel_bucket(dsel), gi5], -1, 3)
        s_s = jnp.einsum('bqghd,bqgkld->bqghkl', qc, gk) * scale + bias_s
        m_s = (okc[..., None] & (dsel >= 0)).reshape(B, qblk, NSA_KV, 1, -1)
        sh = s_s.shape
        p_s = masked_softmax(s_s.reshape(sh[:4] + (-1,)), m_s).reshape(sh)
        o_s = jnp.einsum('bqghkl,bqgkld->bqghd', p_s.astype(gv.dtype), gv)
        kwc = lax.dynamic_slice_in_dim(kw_p, t0 - win_start + 1, lw, 1)
        vwc = lax.dynamic_slice_in_dim(vw_p, t0 - win_start + 1, lw, 1)
        kp = t0 - WINDOW + 1 + jnp.arange(lw)
        dw = tc[:, None] - kp[None, :]
        m_w = (kp[None, :] >= 0) & (dw >= 0) & (dw < WINDOW)
        bias_w = jnp.transpose(tbl_g[rel_bucket(dw)], (2, 3, 0, 1))
        s_w = jnp.einsum('bqghd,bkgd->bghqk', qc, kwc) * scale + bias_w
        p_w = masked_softmax(s_w, m_w)
        o_w = jnp.einsum('bghqk,bkgd->bqghd', p_w.astype(vwc.dtype), vwc)
        return o_s, o_w

    o_s, o_w = lax.map(chunk, jnp.arange(n_chunks))
    o_s = jnp.moveaxis(o_s, 0, 1).reshape(B, Tq, NSA_KV, NSA_HPG, HEAD_DIM)
    o_w = jnp.moveaxis(o_w, 0, 1).reshape(B, Tq, NSA_KV, NSA_HPG, HEAD_DIM)
    g = sigmoid32(gate_logit).reshape(B, Tq, 3, NSA_KV, NSA_HPG)[..., None]
    o = g[:, :, 0] * o_c + g[:, :, 1] * o_s + g[:, :, 2] * o_w
    return o.reshape(B, Tq, NSA_HEADS * HEAD_DIM)


def moba_mixer(q, k, v, q_start, rel_table):
    B, Tq = q.shape[0], q.shape[1]
    Tk = k.shape[1]
    scale = HEAD_DIM ** -0.5
    qg = q.reshape(B, Tq, MOBA_KV, MOBA_HPG, HEAD_DIM)
    t = q_start + jnp.arange(Tq)
    tbl_g = rel_table[:, NSA_HEADS:].reshape(REL_BUCKETS, MOBA_KV, MOBA_HPG)
    nb = -(-Tk // MOBA_BLOCK)
    pad = nb * MOBA_BLOCK - Tk
    kp = jnp.pad(k, ((0, 0), (0, pad), (0, 0), (0, 0)))
    vp = jnp.pad(v, ((0, 0), (0, pad), (0, 0), (0, 0)))
    kb = kp.reshape(B, nb, MOBA_BLOCK, MOBA_KV, HEAD_DIM)
    vb = vp.reshape(B, nb, MOBA_BLOCK, MOBA_KV, HEAD_DIM)
    n_top = min(MOBA_TOPK, nb - 1)
    qblk = min(Q_BLOCK, Tq)
    n_chunks = Tq // qblk
    bi = jnp.arange(B)[:, None, None, None, None]
    gi = jnp.arange(MOBA_KV)[None, None, :, None, None]
    gi6 = jnp.arange(MOBA_KV)[None, None, :, None, None, None]
    hi6 = jnp.arange(MOBA_HPG)[None, None, None, :, None, None]
    if n_top > 0:
        kmean = jnp.mean(kb.astype(jnp.float32), axis=2).astype(k.dtype)
        gs = jnp.einsum('btghd,bngd->btghn', qg, kmean)
        past = jnp.arange(nb)[None, :] < (t // MOBA_BLOCK)[:, None]
        gs = jnp.where(past[None, :, None, None, :], gs.astype(jnp.float32), NEG_INF)
        top_val, top_idx = lax.top_k(gs, n_top)
        top_ok = top_val > 0.5 * NEG_INF

    def chunk(c):
        off = c * qblk
        t0 = q_start + off
        tc = t0 + jnp.arange(qblk)
        qc = lax.dynamic_slice_in_dim(qg, off, qblk, 1)
        ob = (t0 // MOBA_BLOCK) * MOBA_BLOCK
        ko = lax.dynamic_slice_in_dim(kp, ob, MOBA_BLOCK, 1)
        vo = lax.dynamic_slice_in_dim(vp, ob, MOBA_BLOCK, 1)
        do = tc[:, None] - (ob + jnp.arange(MOBA_BLOCK))[None, :]
        bias_o = jnp.transpose(tbl_g[rel_bucket(do)], (0, 2, 3, 1))
        s_o = jnp.einsum('bqghd,bkgd->bqghk', qc, ko) * scale + bias_o
        m_o = jnp.broadcast_to((do >= 0)[:, None, None, :], s_o.shape)
        if n_top > 0:
            idc = lax.dynamic_slice_in_dim(top_idx, off, qblk, 1)
            okc = lax.dynamic_slice_in_dim(top_ok, off, qblk, 1)
            gk = kb[bi, idc, :, gi, :]
            gv = vb[bi, idc, :, gi, :]
            dp = tc[None, :, None, None, None, None] - (idc[..., None] * MOBA_BLOCK + jnp.arange(MOBA_BLOCK))
            bias_p = tbl_g[rel_bucket(dp), gi6, hi6]
            s_p = jnp.einsum('bqghd,bqghnkd->bqghnk', qc, gk) * scale + bias_p
            n_pk = n_top * MOBA_BLOCK
            s_all = jnp.concatenate([s_p.reshape(s_o.shape[:4] + (n_pk,)), s_o], axis=-1)
            m_all = jnp.concatenate([jnp.broadcast_to(okc[..., None], s_p.shape).reshape(s_o.shape[:4] + (n_pk,)), m_o], axis=-1)
            p = masked_softmax(s_all, m_all).astype(v.dtype)
            o = (jnp.einsum('bqghnk,bqghnkd->bqghd', p[..., :n_pk].reshape(s_p.shape), gv)
                 + jnp.einsum('bqghk,bkgd->bqghd', p[..., n_pk:], vo))
        else:
            p = masked_softmax(s_o, m_o).astype(v.dtype)
            o = jnp.einsum('bqghk,bkgd->bqghd', p, vo)
        return o

    o = lax.map(chunk, jnp.arange(n_chunks))
    return jnp.moveaxis(o, 0, 1).reshape(B, Tq, MOBA_HEADS * HEAD_DIM)


def memory_kv(mem, w):
    B, M, _ = mem.shape
    kv = rms_norm(mem, w['mem_norm_gain']) @ w['w_mem_kv']
    k, v = jnp.split(kv, 2, axis=-1)
    k = rms_norm(k.reshape(B, M, MEM_HEADS, MEM_HEAD_DIM), w['mem_k_gain'])
    return k, v.reshape(B, M, MEM_HEADS, MEM_HEAD_DIM)


def memory_attend(q, mk, mv):
    B, T = q.shape[0], q.shape[1]
    s = jnp.einsum('bthd,bmhd->bhtm', q, mk) * (MEM_HEAD_DIM ** -0.5)
    p = jax.nn.softmax(s.astype(jnp.float32), axis=-1).astype(mv.dtype)
    return jnp.einsum('bhtm,bmhd->bthd', p, mv).reshape(B, T, MEM_HEADS * MEM_HEAD_DIM)


def peer_ffn(x, w):
    B, T, D = x.shape
    n = B * T
    xf = x.reshape(n, D)
    q = (xf @ w['peer_w_q']).reshape(n, PEER_HEADS, 2, PEER_DKEY // 2)
    s = jnp.einsum('nhpd,hpkd->nhpk', q, w['peer_sub_keys']).astype(jnp.float32)
    v1, i1 = lax.top_k(s[:, :, 0], PEER_TOPK)
    v2, i2 = lax.top_k(s[:, :, 1], PEER_TOPK)
    cand = (v1[..., :, None] + v2[..., None, :]).reshape(n, PEER_HEADS, -1)
    cid = (i1[..., :, None] * PEER_NKEYS + i2[..., None, :]).reshape(n, PEER_HEADS, -1)
    top_s, pos = lax.top_k(cand, PEER_TOPK)
    eid = jnp.take_along_axis(cid, pos, axis=-1)
    gate = jax.nn.softmax(top_s, axis=-1).astype(x.dtype)
    ch = min(PEER_CHUNK, n)
    n_ch = -(-n // ch)
    padn = n_ch * ch - n
    xp = jnp.pad(xf, ((0, padn), (0, 0))).reshape(n_ch, ch, D)
    ep = jnp.pad(eid, ((0, padn), (0, 0), (0, 0))).reshape(n_ch, ch, PEER_HEADS, PEER_TOPK)
    gp = jnp.pad(gate, ((0, padn), (0, 0), (0, 0))).reshape(n_ch, ch, PEER_HEADS, PEER_TOPK)

    def chunk(args):
        xc, ec, gc = args
        u = w['peer_u'][ec]
        a = gc * jax.nn.gelu(jnp.einsum('cd,chkd->chk', xc, u))
        return jnp.einsum('chk,chkd->cd', a, w['peer_v'][ec])

    out = lax.map(chunk, (xp, ep, gp))
    return out.reshape(-1, D)[:n].reshape(B, T, D)


def trunk_layer(x, q_start, past, win_k_past, win_v_past, mem_k, mem_v, w):
    B, T, D = x.shape
    h = rms_norm(x, w['attn_norm_gain'])
    z = h @ w['w_in']
    cuts = np.cumsum(IN_SPLITS)[:-1].tolist()
    (q_n, kc, vc, ks, vs, kw, vw, g_n, q_m, k_m, v_m, q_x, g_merge) = jnp.split(z, cuts, axis=-1)
    q_n = rms_norm(q_n.reshape(B, T, NSA_HEADS, HEAD_DIM), w['nsa_q_gain'])
    kc = kc.reshape(B, T, NSA_KV, HEAD_DIM)
    vc = vc.reshape(B, T, NSA_KV, HEAD_DIM)
    ks = rms_norm(ks.reshape(B, T, NSA_KV, HEAD_DIM), w['nsa_k_gain'][1])
    vs = vs.reshape(B, T, NSA_KV, HEAD_DIM)
    kw = rms_norm(kw.reshape(B, T, NSA_KV, HEAD_DIM), w['nsa_k_gain'][2])
    vw = vw.reshape(B, T, NSA_KV, HEAD_DIM)
    q_m = rms_norm(q_m.reshape(B, T, MOBA_HEADS, HEAD_DIM), w['moba_q_gain'])
    k_m = rms_norm(k_m.reshape(B, T, MOBA_KV, HEAD_DIM), w['moba_k_gain'])
    v_m = v_m.reshape(B, T, MOBA_KV, HEAD_DIM)
    q_x = rms_norm(q_x.reshape(B, T, MEM_HEADS, MEM_HEAD_DIM), w['mem_q_gain'])
    new_rows = (kc, vc, ks, vs, k_m, v_m)
    if past is None:
        full = new_rows
        kw_all, vw_all = kw, vw
    else:
        full = tuple(jnp.concatenate([p, r], axis=1) for p, r in zip(past, new_rows))
        kw_all = jnp.concatenate([win_k_past, kw], axis=1)
        vw_all = jnp.concatenate([win_v_past, vw], axis=1)
    win_start = q_start + T - kw_all.shape[1]
    o_nsa = nsa_mixer(q_n, g_n, full[0], full[1], full[2], full[3], kw_all, vw_all, q_start, win_start, w)
    o_moba = moba_mixer(q_m, full[4], full[5], q_start, w['rel_bias_table'])
    o_mem = memory_attend(q_x, mem_k, mem_v)
    br = jnp.stack([o_nsa, o_moba, o_mem], axis=2)
    proj = jnp.einsum('btcm,cmd->btcd', br, w['w_branch'])
    gates = sigmoid32(g_merge.reshape(B, T, N_BRANCH, D))
    x = x + (gates * proj).sum(axis=2) @ w['w_out']
    y = x + peer_ffn(rms_norm(x, w['ffn_norm_gain']), w)
    keep = min(WINDOW, q_start + T)
    n_all = kw_all.shape[1]
    return y, new_rows, kw_all[:, n_all - keep:], vw_all[:, n_all - keep:]


def setup_inputs(seed: int = 0) -> dict:
    key = jax.random.key(seed)
    ks = iter(jax.random.split(key, 48))

    def nrm(shape, s=1.0):
        return jax.random.normal(next(ks), shape, jnp.float32) * s

    def gain(shape):
        return 1.0 + 0.02 * nrm(shape)

    n_pages = PAST_LEN // PAGE_SIZE
    n_phys = (DEC_BATCH * n_pages * 5) // 4
    win_len = min(WINDOW, PAST_LEN)
    perm = jax.random.permutation(next(ks), n_phys)
    page_table = perm[:DEC_BATCH * n_pages].reshape(DEC_BATCH, n_pages).astype(jnp.int32)
    pool_nsa = (DEPTH, n_phys, PAGE_SIZE, NSA_KV, HEAD_DIM)
    pool_moba = (DEPTH, n_phys, PAGE_SIZE, MOBA_KV, HEAD_DIM)
    return {
        'x_prompt': nrm((BATCH, SEQ, D_MODEL)),
        'x_sample': nrm((DEC_BATCH, DEC_SEQ, D_MODEL)),
        'cache_cmp_k': nrm(pool_nsa),
        'cache_cmp_v': nrm(pool_nsa),
        'cache_slc_k': nrm(pool_nsa),
        'cache_slc_v': nrm(pool_nsa),
        'cache_moba_k': nrm(pool_moba),
        'cache_moba_v': nrm(pool_moba),
        'cache_win_k': nrm((DEPTH, DEC_BATCH, win_len, NSA_KV, HEAD_DIM)),
        'cache_win_v': nrm((DEPTH, DEC_BATCH, win_len, NSA_KV, HEAD_DIM)),
        'cache_mem_k': nrm((DEPTH, DEC_BATCH, MEM_LEN, MEM_HEADS, MEM_HEAD_DIM)),
        'cache_mem_v': nrm((DEPTH, DEC_BATCH, MEM_LEN, MEM_HEADS, MEM_HEAD_DIM)),
        'page_table': page_table,
        'mem_prompt': nrm((BATCH, MEM_LEN, D_MODEL)),
        'attn_norm_gain': gain((DEPTH, D_MODEL)),
        'w_in': nrm((DEPTH, D_MODEL, IN_COLS), D_MODEL ** -0.5),
        'nsa_q_gain': gain((DEPTH, HEAD_DIM)),
        'nsa_k_gain': gain((DEPTH, 3, HEAD_DIM)),
        'cmp_pos_emb': nrm((DEPTH, 2, CMP_LEN, HEAD_DIM), 0.5),
        'cmp_w1': nrm((DEPTH, 2, CMP_LEN * HEAD_DIM, CMP_HIDDEN), (CMP_LEN * HEAD_DIM) ** -0.5),
        'cmp_b1': nrm((DEPTH, 2, CMP_HIDDEN), 0.02),
        'cmp_w2': nrm((DEPTH, 2, CMP_HIDDEN, HEAD_DIM), CMP_HIDDEN ** -0.5),
        'moba_q_gain': gain((DEPTH, HEAD_DIM)),
        'moba_k_gain': gain((DEPTH, HEAD_DIM)),
        'mem_norm_gain': gain((DEPTH, D_MODEL)),
        'w_mem_kv': nrm((DEPTH, D_MODEL, 2 * MEM_HEADS * MEM_HEAD_DIM), D_MODEL ** -0.5),
        'mem_q_gain': gain((DEPTH, MEM_HEAD_DIM)),
        'mem_k_gain': gain((DEPTH, MEM_HEAD_DIM)),
        'rel_bias_table': nrm((REL_BUCKETS, NSA_HEADS + MOBA_HEADS), 0.2),
        'w_branch': nrm((DEPTH, N_BRANCH, BRANCH_WIDTH, D_MODEL), BRANCH_WIDTH ** -0.5),
        'w_out': nrm((DEPTH, D_MODEL, D_MODEL), D_MODEL ** -0.5),
        'ffn_norm_gain': gain((DEPTH, D_MODEL)),
        'peer_w_q': nrm((DEPTH, D_MODEL, PEER_HEADS * PEER_DKEY), D_MODEL ** -0.5),
        'peer_sub_keys': nrm((DEPTH, PEER_HEADS, 2, PEER_NKEYS, PEER_DKEY // 2), (PEER_DKEY // 2) ** -0.5),
        'peer_u': nrm((DEPTH, PEER_EXPERTS, D_MODEL), D_MODEL ** -0.5),
        'peer_v': nrm((DEPTH, PEER_EXPERTS, D_MODEL), D_MODEL ** -0.5),
    }


def reference(x_prompt, x_sample, cache_cmp_k, cache_cmp_v, cache_slc_k, cache_slc_v,
              cache_moba_k, cache_moba_v, cache_win_k, cache_win_v, cache_mem_k, cache_mem_v,
              page_table, mem_prompt, attn_norm_gain, w_in, nsa_q_gain, nsa_k_gain,
              cmp_pos_emb, cmp_w1, cmp_b1, cmp_w2, moba_q_gain, moba_k_gain,
              mem_norm_gain, w_mem_kv, mem_q_gain, mem_k_gain, rel_bias_table,
              w_branch, w_out, ffn_norm_gain, peer_w_q, peer_sub_keys, peer_u, peer_v):
    xp, xs = x_prompt, x_sample
    prompt_states, sample_states = [], []
    for l in range(DEPTH):
        w = {
            'attn_norm_gain': attn_norm_gain[l], 'w_in': w_in[l],
            'nsa_q_gain': nsa_q_gain[l], 'nsa_k_gain': nsa_k_gain[l],
            'cmp_pos_emb': cmp_pos_emb[l], 'cmp_w1': cmp_w1[l], 'cmp_b1': cmp_b1[l], 'cmp_w2': cmp_w2[l],
            'moba_q_gain': moba_q_gain[l], 'moba_k_gain': moba_k_gain[l],
            'mem_norm_gain': mem_norm_gain[l], 'w_mem_kv': w_mem_kv[l],
            'mem_q_gain': mem_q_gain[l], 'mem_k_gain': mem_k_gain[l],
            'rel_bias_table': rel_bias_table,
            'w_branch': w_branch[l], 'w_out': w_out[l], 'ffn_norm_gain': ffn_norm_gain[l],
            'peer_w_q': peer_w_q[l], 'peer_sub_keys': peer_sub_keys[l],
            'peer_u': peer_u[l], 'peer_v': peer_v[l],
        }
        mk_p, mv_p = memory_kv(mem_prompt, w)
        xp, rows_p, wk_p, wv_p = trunk_layer(xp, 0, None, None, None, mk_p, mv_p, w)
        past = tuple(gather_pages(c[l], page_table) for c in
                     (cache_cmp_k, cache_cmp_v, cache_slc_k, cache_slc_v, cache_moba_k, cache_moba_v))
        xs, rows_s, wk_s, wv_s = trunk_layer(xs, PAST_LEN, past, cache_win_k[l], cache_win_v[l],
                                             cache_mem_k[l], cache_mem_v[l], w)
        prompt_states.append(rows_p + (wk_p, wv_p, mk_p, mv_p))
        sample_states.append(rows_s + (wk_s, wv_s))
    (p_cmp_k, p_cmp_v, p_slc_k, p_slc_v, p_moba_k, p_moba_v,
     p_win_k, p_win_v, p_mem_k, p_mem_v) = [jnp.stack(a) for a in zip(*prompt_states)]
    (s_cmp_k, s_cmp_v, s_slc_k, s_slc_v, s_moba_k, s_moba_v,
     s_win_k, s_win_v) = [jnp.stack(a) for a in zip(*sample_states)]
    return (xp, xs, p_cmp_k, p_cmp_v, p_slc_k, p_slc_v, p_moba_k, p_moba_v, p_win_k, p_win_v,
            p_mem_k, p_mem_v, s_cmp_k, s_cmp_v, s_slc_k, s_slc_v, s_moba_k, s_moba_v, s_win_k, s_win_v)
```

```python
import functools
import math

import jax
import jax.numpy as jnp
import numpy as np
from jax import lax
from jax.experimental import pallas as pl
from jax.experimental.pallas import tpu as pltpu

HEAD_DIM = 64
NSA_HEADS = 8
NSA_KV = 2
CMP_LEN = 32
CMP_STRIDE = 16
SLC_LEN = 64
SLC_TOPN = 16
WINDOW = 512
FORCE_BONUS = 1000.0
MOBA_HEADS = 8
MOBA_KV = 2
MOBA_BLOCK = 256
MOBA_TOPK = 3
MEM_HEADS = 4
MEM_HEAD_DIM = 128
REL_BUCKETS = 32
REL_MAX_DIST = 128
PEER_HEADS = 8
PEER_NKEYS = 128
PEER_TOPK = 16
RMS_EPS = 1e-6
NEG_INF = -1e30
LANES = 128
VMEM_LIMIT = 56 * 1024 * 1024

F32 = jnp.float32
BF16 = jnp.bfloat16
HI = lax.Precision.HIGHEST


def _rms(x, g):
    return x * lax.rsqrt(jnp.mean(x * x, axis=-1, keepdims=True) + RMS_EPS) * g


def _rel_bucket(dist):
    n = jnp.maximum(dist, 0)
    exact = REL_BUCKETS // 2
    nf = jnp.maximum(n, 1).astype(F32)
    large = exact + (jnp.log(nf / exact) / math.log(REL_MAX_DIST / exact) * (REL_BUCKETS - exact)).astype(jnp.int32)
    return jnp.where(n < exact, n, jnp.minimum(large, REL_BUCKETS - 1))


def _masked_softmax(s, mask):
    s = jnp.where(mask, s, NEG_INF)
    return jnp.where(mask, jax.nn.softmax(s, axis=-1), 0.0)


def _gelu_tanh(x):
    inner = x * (0.7978845608028654 + 0.035677408136300125 * (x * x))
    return (0.5 * x) * (1.0 + jnp.tanh(inner))


_Z_GROUPS = (
    ("q_n", 512, 64), ("kc", 128, 0), ("vc", 128, 0), ("ks", 128, 64), ("vs", 128, 0),
    ("kw", 128, 64), ("vw", 128, 0), ("q_m", 512, 64), ("k_m", 128, 64), ("v_m", 128, 0),
    ("q_x", 512, 128), ("g_n", 128, 0), ("g_merge", 3072, 0),
)
_Z_OFF = {}
_o = 0
for _n, _w, _h in _Z_GROUPS:
    _Z_OFF[_n] = _o
    _o += _w
Z_COLS = _o
IN_SPLITS = (512, 128, 128, 128, 128, 128, 128, 24, 512, 128, 128, 512, 3072)
_SRC_NAMES = ("q_n", "kc", "vc", "ks", "vs", "kw", "vw", "g_n", "q_m", "k_m", "v_m", "q_x", "g_merge")


def _pack_w_in(w_in):
    cuts = np.cumsum((0,) + IN_SPLITS)
    parts = {n: w_in[:, cuts[i]:cuts[i + 1]] for i, n in enumerate(_SRC_NAMES)}
    parts["g_n"] = jnp.pad(parts["g_n"], ((0, 0), (0, 128 - 24)))
    return jnp.concatenate([parts[n] for n, _, _ in _Z_GROUPS], axis=1).astype(BF16)


def _head_gain_row(gains):
    g = {
        "q_n": jnp.tile(gains["nsa_q"], 8), "ks": jnp.tile(gains["nsa_k1"], 2), "kw": jnp.tile(gains["nsa_k2"], 2),
        "q_m": jnp.tile(gains["moba_q"], 8), "k_m": jnp.tile(gains["moba_k"], 2), "q_x": jnp.tile(gains["mem_q"], 4),
    }
    return jnp.concatenate([g.get(n, jnp.ones((w,), F32)) for n, w, _ in _Z_GROUPS])[None, :]


def _seg_matrix(hd):
    i = np.arange(LANES)
    return jnp.asarray((i[:, None] // hd) == (i[None, :] // hd), BF16)


def _split_hi_lo(x):
    hi = x.astype(BF16)
    lo = (x - hi.astype(F32)).astype(BF16)
    return hi, lo


def _inproj_kernel(x_ref, gain_ref, w_ref, hgain_ref, seg64_ref, seg128_ref, z_ref):
    x = x_ref[...]
    ms = jnp.mean(x * x, axis=-1, keepdims=True)
    xb = (x * lax.rsqrt(ms + RMS_EPS) * gain_ref[...]).astype(BF16)
    for name, width, hd in _Z_GROUPS:
        off = _Z_OFF[name]
        for c in range(0, width, 512):
            cw = min(512, width - c)
            lo_c, hi_c = off + c, off + c + cw
            z = jnp.dot(xb, w_ref[:, lo_c:hi_c], preferred_element_type=F32)
            if hd:
                seg = seg64_ref[...] if hd == 64 else seg128_ref[...]
                for j in range(0, cw, LANES):
                    zj = z[:, j:j + LANES]
                    sq_hi, sq_lo = _split_hi_lo(zj * zj)
                    ss = (jnp.dot(sq_hi, seg, preferred_element_type=F32)
                          + jnp.dot(sq_lo, seg, preferred_element_type=F32))
                    zj = zj * lax.rsqrt(ss * (1.0 / hd) + RMS_EPS) * hgain_ref[:, lo_c + j:lo_c + j + LANES]
                    z_ref[:, lo_c + j:lo_c + j + LANES] = zj
            else:
                z_ref[:, lo_c:hi_c] = z


def _inproj(x, gain, w_packed, hgain, tm=256):
    n, d = x.shape
    assert n % tm == 0
    return pl.pallas_call(
        _inproj_kernel,
        grid=(n // tm,),
        in_specs=[
            pl.BlockSpec((tm, d), lambda i: (i, 0)),
            pl.BlockSpec((1, d), lambda i: (0, 0)),
            pl.BlockSpec((d, Z_COLS), lambda i: (0, 0)),
            pl.BlockSpec((1, Z_COLS), lambda i: (0, 0)),
            pl.BlockSpec((LANES, LANES), lambda i: (0, 0)),
            pl.BlockSpec((LANES, LANES), lambda i: (0, 0)),
        ],
        out_specs=pl.BlockSpec((tm, Z_COLS), lambda i: (i, 0)),
        out_shape=jax.ShapeDtypeStruct((n, Z_COLS), F32),
        compiler_params=pltpu.CompilerParams(dimension_semantics=("arbitrary",), vmem_limit_bytes=VMEM_LIMIT),
        name="inproj",
    )(x, gain, w_packed, hgain, _seg_matrix(64), _seg_matrix(128))


def _peer_dense_kernel(xT_ref, u_ref, vT_ref, s1_ref, s2_ref, e1_ref, e2_ref, tau_ref, o_ref, a_ref, *, ec, tt):
    c = pl.program_id(1)
    nk = ec // LANES

    @pl.when(c == 0)
    def _():
        o_ref[...] = jnp.zeros_like(o_ref)

    h_t = jnp.dot(u_ref[...], xT_ref[...], preferred_element_type=F32)
    for k in range(nk):
        for lc in range(0, tt, LANES):
            wk = jnp.zeros((LANES, LANES), F32)
            for h in range(PEER_HEADS):
                s1row = s1_ref[k, h:h + 1, lc:lc + LANES]
                e1row = e1_ref[k, h:h + 1, lc:lc + LANES]
                csum = s2_ref[h, :, lc:lc + LANES] + s1row
                sel = jnp.where(csum >= tau_ref[h:h + 1, lc:lc + LANES], e2_ref[h, :, lc:lc + LANES], 0.0)
                wk = wk + sel * e1row
            g = _gelu_tanh(h_t[k * LANES:(k + 1) * LANES, lc:lc + LANES])
            a_ref[k * LANES:(k + 1) * LANES, lc:lc + LANES] = (wk * g).astype(BF16)
    o_ref[...] += jnp.dot(vT_ref[...], a_ref[...], preferred_element_type=F32)


def _peer_dense(xT, u_b, vT_b, s1T, s2T, e1T, e2T, tauT, ec=512):
    d, n = xT.shape
    tt = 512 if n % 512 == 0 else 256
    assert n % tt == 0
    ne = u_b.shape[0]
    kern = functools.partial(_peer_dense_kernel, ec=ec, tt=tt)
    return pl.pallas_call(
        kern,
        grid=(n // tt, ne // ec),
        in_specs=[
            pl.BlockSpec((d, tt), lambda i, c: (0, i)),
            pl.BlockSpec((ec, d), lambda i, c: (c, 0)),
            pl.BlockSpec((d, ec), lambda i, c: (0, c)),
            pl.BlockSpec((ec // LANES, PEER_HEADS, tt), lambda i, c: (c, 0, i)),
            pl.BlockSpec((PEER_HEADS, PEER_NKEYS, tt), lambda i, c: (0, 0, i)),
            pl.BlockSpec((ec // LANES, PEER_HEADS, tt), lambda i, c: (c, 0, i)),
            pl.BlockSpec((PEER_HEADS, PEER_NKEYS, tt), lambda i, c: (0, 0, i)),
            pl.BlockSpec((PEER_HEADS, tt), lambda i, c: (0, i)),
        ],
        out_specs=pl.BlockSpec((d, tt), lambda i, c: (0, i)),
        out_shape=jax.ShapeDtypeStruct((d, n), F32),
        scratch_shapes=[pltpu.VMEM((ec, tt), BF16)],
        compiler_params=pltpu.CompilerParams(dimension_semantics=("arbitrary", "arbitrary"), vmem_limit_bytes=VMEM_LIMIT),
        name="peer_dense",
    )(xT, u_b, vT_b, s1T, s2T, e1T, e2T, tauT)


def _peer(x1, ffn_gain, peer_w_q, sub_keys, u_b, vT_b):
    n, d = x1.shape
    xn = _rms(x1, ffn_gain)
    q = jnp.dot(xn, peer_w_q, precision=HI).reshape(n, PEER_HEADS, 2, HEAD_DIM)
    s = jnp.einsum("nhpd,hpkd->nhpk", q, sub_keys, precision=HI)
    s1, s2 = s[:, :, 0], s[:, :, 1]
    v1, _ = lax.top_k(s1, PEER_TOPK)
    v2, _ = lax.top_k(s2, PEER_TOPK)
    cand = (v1[..., :, None] + v2[..., None, :]).reshape(n, PEER_HEADS, -1)
    top_s, _ = lax.top_k(cand, PEER_TOPK)
    tau = top_s[..., -1]
    a0, b0 = v1[..., 0], v2[..., 0]
    zsum = jnp.sum(jnp.exp(top_s - top_s[..., :1]), axis=-1)
    e1 = jnp.exp(s1 - a0[..., None]) / zsum[..., None]
    e2 = jnp.exp(s2 - b0[..., None])
    tr1 = lambda a: jnp.transpose(a, (2, 1, 0))
    tr2 = lambda a: jnp.transpose(a, (1, 2, 0))
    out_t = _peer_dense(xn.astype(BF16).T, u_b, vT_b, tr1(s1), tr2(s2), tr1(e1), tr2(e2), tau.T)
    return x1 + out_t.T


def _compress(x_raw, pe, w1, b1, w2):
    b, tk, _ = x_raw.shape
    n_sub = tk // CMP_STRIDE
    n_cmp = n_sub - 1
    xs = x_raw[:, :n_sub * CMP_STRIDE].reshape(b, n_sub, CMP_STRIDE * LANES)
    w1r = w1.reshape(2, CMP_STRIDE, HEAD_DIM, -1)
    wbig = jnp.einsum("rlde,gh->lgdrhe", w1r, jnp.eye(2, dtype=F32)).reshape(CMP_STRIDE * LANES, 2 * LANES)
    a = jnp.dot(xs, wbig, precision=HI)
    pos = jnp.einsum("rld,rlde->e", pe.reshape(2, CMP_STRIDE, HEAD_DIM), w1r, precision=HI)
    h = a[:, :n_cmp, :LANES] + a[:, 1:, LANES:] + jnp.tile(b1 + pos, 2)
    w2big = jnp.einsum("ed,gh->gehd", w2, jnp.eye(2, dtype=F32)).reshape(LANES, LANES)
    return jnp.dot(jax.nn.gelu(h), w2big, precision=HI)


def _slc_matrix(n_cmp, n_slc):
    ratio, c_sub = SLC_LEN // CMP_STRIDE, CMP_LEN // CMP_STRIDE
    m = np.zeros((n_cmp, n_slc), np.float32)
    for j in range(n_slc):
        for o in range(-(c_sub - 1), ratio):
            k = ratio * j + o
            if 0 <= k < n_cmp:
                m[k, j] += float(min(o + c_sub, ratio) - max(o, 0))
    return jnp.asarray(m)


def _nsa_dense(q, glog, kc_raw, vc_raw, ks, vs, kw_all, vw_all, q_start, win_start, w):
    b, t_q, _ = q.shape
    tk = kc_raw.shape[1]
    scale = HEAD_DIM ** -0.5
    qg = q.reshape(b, t_q, NSA_KV, 4, HEAD_DIM)
    t = q_start + jnp.arange(t_q)
    tbl = w["rel_bias_table"][:, :NSA_HEADS].reshape(REL_BUCKETS, NSA_KV, 4)

    kc = _compress(kc_raw, w["cmp_pos_emb"][0], w["cmp_w1"][0], w["cmp_b1"][0], w["cmp_w2"][0])
    n_cmp = kc.shape[1]
    kc = _rms(kc.reshape(b, n_cmp, NSA_KV, HEAD_DIM), w["nsa_k_gain"][0])
    vc = _compress(vc_raw, w["cmp_pos_emb"][1], w["cmp_w1"][1], w["cmp_b1"][1], w["cmp_w2"][1])
    vc = vc.reshape(b, n_cmp, NSA_KV, HEAD_DIM)
    cmp_end = jnp.arange(n_cmp) * CMP_STRIDE + CMP_LEN - 1
    dist_c = t[:, None] - cmp_end[None, :]
    bias_c = jnp.transpose(tbl[_rel_bucket(dist_c)], (2, 3, 0, 1))
    s_c = jnp.einsum("btghd,bngd->bghtn", qg, kc, precision=HI) * scale + bias_c
    p_c = _masked_softmax(s_c, dist_c >= 0)
    o_c = jnp.einsum("bghtn,bngd->btghd", p_c, vc, precision=HI)

    imp = jnp.transpose(p_c.sum(axis=2), (0, 2, 1, 3))
    n_slc = -(-tk // SLC_LEN)
    p_slc = jnp.dot(imp, _slc_matrix(n_cmp, n_slc), precision=HI)
    blk = jnp.arange(n_slc)[None, :]
    qb = (t // SLC_LEN)[:, None]
    forced = ((blk == 0) | (blk == qb) | (blk == qb - 1)).astype(F32)
    future = blk > qb
    score = jnp.where(future[None, :, None, :], NEG_INF, p_slc + FORCE_BONUS * forced[None, :, None, :])
    k_sel = min(SLC_TOPN, n_slc)
    sel_val, sel_idx = lax.top_k(score, k_sel)
    sel_ok = sel_val > 0.5 * NEG_INF
    selm = (jax.nn.one_hot(sel_idx, n_slc, dtype=F32) * sel_ok[..., None].astype(F32)).sum(axis=-2) > 0.5

    kpos = jnp.arange(tk)
    d_s = t[:, None] - kpos[None, :]
    bias_s = jnp.transpose(tbl[_rel_bucket(d_s)], (2, 3, 0, 1))
    ksg = ks.reshape(b, tk, NSA_KV, HEAD_DIM)
    vsg = vs.reshape(b, tk, NSA_KV, HEAD_DIM)
    s_s = jnp.einsum("btghd,bkgd->bghtk", qg, ksg, precision=HI) * scale + bias_s
    m_s = jnp.transpose(selm, (0, 2, 1, 3))[..., kpos // SLC_LEN] & (d_s >= 0)
    p_s = _masked_softmax(s_s, m_s[:, :, None])
    o_s = jnp.einsum("bghtk,bkgd->btghd", p_s, vsg, precision=HI)

    tw = kw_all.shape[1]
    wpos = win_start + jnp.arange(tw)
    d_w = t[:, None] - wpos[None, :]
    m_w = (wpos[None, :] >= 0) & (d_w >= 0) & (d_w < WINDOW)
    bias_w = jnp.transpose(tbl[_rel_bucket(d_w)], (2, 3, 0, 1))
    kwg = kw_all.reshape(b, tw, NSA_KV, HEAD_DIM)
    vwg = vw_all.reshape(b, tw, NSA_KV, HEAD_DIM)
    s_w = jnp.einsum("btghd,bkgd->bghtk", qg, kwg, precision=HI) * scale + bias_w
    p_w = _masked_softmax(s_w, m_w)
    o_w = jnp.einsum("bghtk,bkgd->btghd", p_w, vwg, precision=HI)

    g = jax.nn.sigmoid(glog).reshape(b, t_q, 3, NSA_KV, 4)[..., None]
    o = g[:, :, 0] * o_c + g[:, :, 1] * o_s + g[:, :, 2] * o_w
    return o.reshape(b, t_q, NSA_HEADS * HEAD_DIM)


def _moba_dense(q, k, v, q_start, rel_table):
    b, t_q, _ = q.shape
    tk = k.shape[1]
    scale = HEAD_DIM ** -0.5
    qg = q.reshape(b, t_q, MOBA_KV, 4, HEAD_DIM)
    kg = k.reshape(b, tk, MOBA_KV, HEAD_DIM)
    vg = v.reshape(b, tk, MOBA_KV, HEAD_DIM)
    t = q_start + jnp.arange(t_q)
    tbl = rel_table[:, NSA_HEADS:].reshape(REL_BUCKETS, MOBA_KV, 4)
    nb = -(-tk // MOBA_BLOCK)
    pad = nb * MOBA_BLOCK - tk
    kp = jnp.pad(kg, ((0, 0), (0, pad), (0, 0), (0, 0)))
    n_top = min(MOBA_TOPK, nb - 1)
    kpos = jnp.arange(tk)
    kblk = kpos // MOBA_BLOCK
    d = t[:, None] - kpos[None, :]
    m_own = (kblk[None, :] == (t // MOBA_BLOCK)[:, None]) & (d >= 0)
    mask = jnp.broadcast_to(m_own[None, None, None], (b, MOBA_KV, 4, t_q, tk))
    if n_top > 0:
        kmean = jnp.mean(kp.reshape(b, nb, MOBA_BLOCK, MOBA_KV, HEAD_DIM), axis=2)
        gs = jnp.einsum("btghd,bngd->btghn", qg, kmean, precision=HI)
        past = jnp.arange(nb)[None, :] < (t // MOBA_BLOCK)[:, None]
        gs = jnp.where(past[None, :, None, None, :], gs, NEG_INF)
        top_val, top_idx = lax.top_k(gs, n_top)
        top_ok = top_val > 0.5 * NEG_INF
        selm = (jax.nn.one_hot(top_idx, nb, dtype=F32) * top_ok[..., None].astype(F32)).sum(axis=-2) > 0.5
        mask = mask | jnp.transpose(selm, (0, 2, 3, 1, 4))[..., kblk]
    bias = jnp.transpose(tbl[_rel_bucket(d)], (2, 3, 0, 1))
    s = jnp.einsum("btghd,bkgd->bghtk", qg, kg, precision=HI) * scale + bias
    p = _masked_softmax(s, mask)
    o = jnp.einsum("bghtk,bkgd->btghd", p, vg, precision=HI)
    return o.reshape(b, t_q, MOBA_HEADS * HEAD_DIM)


def _mem_attend(q, mk, mv):
    b, t_q, _ = q.shape
    qh = q.reshape(b, t_q, MEM_HEADS, MEM_HEAD_DIM)
    s = jnp.einsum("bthd,bmhd->bhtm", qh, mk, precision=HI) * (MEM_HEAD_DIM ** -0.5)
    p = jax.nn.softmax(s, axis=-1)
    return jnp.einsum("bhtm,bmhd->bthd", p, mv, precision=HI).reshape(b, t_q, MEM_HEADS * MEM_HEAD_DIM)


def _memory_kv(mem, w):
    b, m, _ = mem.shape
    kv = jnp.dot(_rms(mem, w["mem_norm_gain"]), w["w_mem_kv"], precision=HI)
    k, v = jnp.split(kv, 2, axis=-1)
    k = _rms(k.reshape(b, m, MEM_HEADS, MEM_HEAD_DIM), w["mem_k_gain"])
    return k, v.reshape(b, m, MEM_HEADS, MEM_HEAD_DIM)


def _gather_pages(pool, page_table):
    g = pool[page_table]
    return g.reshape(page_table.shape[0], -1, LANES)


def _zcol(z, name, width):
    return z[..., _Z_OFF[name]:_Z_OFF[name] + width]


def kernel(x_prompt, x_sample, cache_cmp_k, cache_cmp_v, cache_slc_k, cache_slc_v, cache_moba_k, cache_moba_v, cache_win_k, cache_win_v, cache_mem_k, cache_mem_v, page_table, mem_prompt, attn_norm_gain, w_in, nsa_q_gain, nsa_k_gain, cmp_pos_emb, cmp_w1, cmp_b1, cmp_w2, moba_q_gain, moba_k_gain, mem_norm_gain, w_mem_kv, mem_q_gain, mem_k_gain, rel_bias_table, w_branch, w_out, ffn_norm_gain, peer_w_q, peer_sub_keys, peer_u, peer_v):
    depth = w_in.shape[0]
    assert depth == 1
    l = 0
    bp, tp, d = x_prompt.shape
    bs, ts, _ = x_sample.shape
    past_len = page_table.shape[1] * cache_cmp_k.shape[2]
    n_p, n_s = bp * tp, bs * ts
    w = {
        "rel_bias_table": rel_bias_table, "cmp_pos_emb": cmp_pos_emb[l], "cmp_w1": cmp_w1[l], "cmp_b1": cmp_b1[l],
        "cmp_w2": cmp_w2[l], "nsa_k_gain": nsa_k_gain[l], "mem_norm_gain": mem_norm_gain[l], "w_mem_kv": w_mem_kv[l],
        "mem_k_gain": mem_k_gain[l],
    }

    x_all = jnp.concatenate([x_prompt.reshape(n_p, d), x_sample.reshape(n_s, d)], axis=0)
    hgain = _head_gain_row({"nsa_q": nsa_q_gain[l], "nsa_k1": nsa_k_gain[l, 1], "nsa_k2": nsa_k_gain[l, 2],
                            "moba_q": moba_q_gain[l], "moba_k": moba_k_gain[l], "mem_q": mem_q_gain[l]})
    z = _inproj(x_all, attn_norm_gain[l][None, :], _pack_w_in(w_in[l]), hgain)
    zp = z[:n_p].reshape(bp, tp, Z_COLS)
    zs = z[n_p:].reshape(bs, ts, Z_COLS)

    def rows(zz, name):
        return _zcol(zz, name, 128)

    mk_p, mv_p = _memory_kv(mem_prompt, w)
    o_nsa_p = _nsa_dense(_zcol(zp, "q_n", 512), _zcol(zp, "g_n", 24), rows(zp, "kc"), rows(zp, "vc"), rows(zp, "ks"),
                         rows(zp, "vs"), rows(zp, "kw"), rows(zp, "vw"), 0, 0, w)
    o_moba_p = _moba_dense(_zcol(zp, "q_m", 512), rows(zp, "k_m"), rows(zp, "v_m"), 0, rel_bias_table)
    o_mem_p = _mem_attend(_zcol(zp, "q_x", 512), mk_p, mv_p)

    past = [_gather_pages(c[l].reshape(c.shape[1], c.shape[2], LANES), page_table)
            for c in (cache_cmp_k, cache_cmp_v, cache_slc_k, cache_slc_v, cache_moba_k, cache_moba_v)]
    full = [jnp.concatenate([p, rows(zs, nm)], axis=1)
            for p, nm in zip(past, ("kc", "vc", "ks", "vs", "k_m", "v_m"))]
    win_len = cache_win_k.shape[2]
    kw_all = jnp.concatenate([cache_win_k[l].reshape(bs, win_len, LANES), rows(zs, "kw")], axis=1)
    vw_all = jnp.concatenate([cache_win_v[l].reshape(bs, win_len, LANES), rows(zs, "vw")], axis=1)
    win_start = past_len + ts - kw_all.shape[1]
    o_nsa_s = _nsa_dense(_zcol(zs, "q_n", 512), _zcol(zs, "g_n", 24), full[0], full[1], full[2], full[3],
                         kw_all, vw_all, past_len, win_start, w)
    o_moba_s = _moba_dense(_zcol(zs, "q_m", 512), full[4], full[5], past_len, rel_bias_table)
    o_mem_s = _mem_attend(_zcol(zs, "q_x", 512), cache_mem_k[l], cache_mem_v[l])

    br = jnp.stack([jnp.concatenate([a.reshape(n_p, 512), b_.reshape(n_s, 512)], axis=0)
                    for a, b_ in ((o_nsa_p, o_nsa_s), (o_moba_p, o_moba_s), (o_mem_p, o_mem_s))], axis=1)
    proj = jnp.einsum("ncm,cmd->ncd", br, w_branch[l], precision=HI)
    gates = jax.nn.sigmoid(_zcol(z, "g_merge", 3 * d).reshape(-1, 3, d))
    x1 = x_all + jnp.dot((gates * proj).sum(axis=1), w_out[l], precision=HI)
    y = _peer(x1, ffn_norm_gain[l], peer_w_q[l], peer_sub_keys[l], peer_u[l].astype(BF16), peer_v[l].astype(BF16).T)
    y_p = y[:n_p].reshape(bp, tp, d)
    y_s = y[n_p:].reshape(bs, ts, d)

    def st(zz, name, b_, t_):
        return rows(zz, name).reshape(1, b_, t_, 2, HEAD_DIM)

    keep_p = min(WINDOW, tp)
    p_win_k = rows(zp, "kw")[:, tp - keep_p:].reshape(1, bp, keep_p, 2, HEAD_DIM)
    p_win_v = rows(zp, "vw")[:, tp - keep_p:].reshape(1, bp, keep_p, 2, HEAD_DIM)
    keep_s = min(WINDOW, past_len + ts)
    s_win_k = kw_all[:, kw_all.shape[1] - keep_s:].reshape(1, bs, keep_s, 2, HEAD_DIM)
    s_win_v = vw_all[:, vw_all.shape[1] - keep_s:].reshape(1, bs, keep_s, 2, HEAD_DIM)
    return (y_p, y_s,
            st(zp, "kc", bp, tp), st(zp, "vc", bp, tp), st(zp, "ks", bp, tp), st(zp, "vs", bp, tp),
            st(zp, "k_m", bp, tp), st(zp, "v_m", bp, tp), p_win_k, p_win_v, mk_p[None], mv_p[None],
            st(zs, "kc", bs, ts), st(zs, "vc", bs, ts), st(zs, "ks", bs, ts), st(zs, "vs", bs, ts),
            st(zs, "k_m", bs, ts), st(zs, "v_m", bs, ts), s_win_k, s_win_v)
```

```python
import functools
import math

import jax
import jax.numpy as jnp
import numpy as np
from jax import lax
from jax.experimental import pallas as pl
from jax.experimental.pallas import tpu as pltpu

HEAD_DIM = 64
NSA_HEADS = 8
NSA_KV = 2
HPG = 4
CMP_LEN = 32
CMP_STRIDE = 16
SLC_LEN = 64
SLC_TOPN = 16
WINDOW = 512
FORCE_BONUS = 1000.0
MOBA_HEADS = 8
MOBA_KV = 2
MOBA_BLOCK = 256
MOBA_TOPK = 3
MEM_HEADS = 4
MEM_HEAD_DIM = 128
REL_BUCKETS = 32
REL_MAX_DIST = 128
PEER_HEADS = 8
PEER_NKEYS = 128
PEER_TOPK = 16
RMS_EPS = 1e-6
NEG_INF = -1e30
HALF_NEG = 0.5 * NEG_INF
KNOCKED = -3.0e38
LANES = 128
VMEM_LIMIT = 56 * 1024 * 1024
ATT_TQ = 128
ATT_TK = 128

F32 = jnp.float32
BF16 = jnp.bfloat16
HI = lax.Precision.HIGHEST
_NT = (((1,), (1,)), ((), ()))


def _rms(x, g):
    return x * lax.rsqrt(jnp.mean(x * x, axis=-1, keepdims=True) + RMS_EPS) * g


def _rel_bucket(dist):
    n = jnp.maximum(dist, 0)
    exact = REL_BUCKETS // 2
    nf = jnp.maximum(n, 1).astype(F32)
    large = exact + (jnp.log(nf / exact) / math.log(REL_MAX_DIST / exact) * (REL_BUCKETS - exact)).astype(jnp.int32)
    return jnp.where(n < exact, n, jnp.minimum(large, REL_BUCKETS - 1))


def _masked_softmax(s, mask):
    s = jnp.where(mask, s, NEG_INF)
    return jnp.where(mask, jax.nn.softmax(s, axis=-1), 0.0)


def _gelu_tanh(x):
    inner = x * (0.7978845608028654 + 0.035677408136300125 * (x * x))
    return (0.5 * x) * (1.0 + jnp.tanh(inner))


def _split_hi_lo(x):
    hi = x.astype(BF16)
    lo = (x - hi.astype(F32)).astype(BF16)
    return hi, lo


def _dot3(a, b_hi, b_lo):
    a_hi, a_lo = _split_hi_lo(a)
    return (jnp.dot(a_hi, b_hi, preferred_element_type=F32) + jnp.dot(a_lo, b_hi, preferred_element_type=F32)
            + jnp.dot(a_hi, b_lo, preferred_element_type=F32))


def _seg_matrix(hd):
    i = np.arange(LANES)
    return jnp.asarray((i[:, None] // hd) == (i[None, :] // hd), BF16)


def _group_sumsq(z, seg):
    sq_hi, sq_lo = _split_hi_lo(z * z)
    return jnp.dot(sq_hi, seg, preferred_element_type=F32) + jnp.dot(sq_lo, seg, preferred_element_type=F32)


def _params(sem):
    return pltpu.CompilerParams(dimension_semantics=sem, vmem_limit_bytes=VMEM_LIMIT)


_Z_GROUPS = (
    ("g_merge", 3072, 0), ("q_n", 512, 64), ("q_m", 512, 64), ("q_x", 512, 128),
    ("kc", 128, 0), ("vc", 128, 0), ("ks", 128, 64), ("vs", 128, 0), ("kw", 128, 64), ("vw", 128, 0),
    ("k_m", 128, 64), ("v_m", 128, 0), ("g_n", 128, 0),
)
_Z_OFF = {}
_o = 0
for _n, _w, _h in _Z_GROUPS:
    _Z_OFF[_n] = _o
    _o += _w
Z_COLS = _o
IN_SPLITS = (512, 128, 128, 128, 128, 128, 128, 24, 512, 128, 128, 512, 3072)
_SRC_NAMES = ("q_n", "kc", "vc", "ks", "vs", "kw", "vw", "g_n", "q_m", "k_m", "v_m", "q_x", "g_merge")


def _pack_w_in(w_in):
    cuts = np.cumsum((0,) + IN_SPLITS)
    parts = {n: w_in[:, cuts[i]:cuts[i + 1]] for i, n in enumerate(_SRC_NAMES)}
    parts["g_n"] = jnp.pad(parts["g_n"], ((0, 0), (0, LANES - 24)))
    return jnp.concatenate([parts[n] for n, _, _ in _Z_GROUPS], axis=1).astype(BF16)


def _head_gain_row(gains):
    g = {
        "q_n": jnp.tile(gains["nsa_q"], 8), "ks": jnp.tile(gains["nsa_k1"], 2), "kw": jnp.tile(gains["nsa_k2"], 2),
        "q_m": jnp.tile(gains["moba_q"], 8), "k_m": jnp.tile(gains["moba_k"], 2), "q_x": jnp.tile(gains["mem_q"], 4),
    }
    return jnp.concatenate([g.get(n, jnp.ones((w,), F32)) for n, w, _ in _Z_GROUPS])[None, :]


def _inproj_kernel(x_ref, gain_ref, w_ref, hgain_ref, seg64_ref, seg128_ref, z_ref):
    x = x_ref[...]
    ms = jnp.mean(x * x, axis=-1, keepdims=True)
    xb = (x * lax.rsqrt(ms + RMS_EPS) * gain_ref[...]).astype(BF16)
    for name, width, hd in _Z_GROUPS:
        off = _Z_OFF[name]
        for c in range(0, width, 512):
            cw = min(512, width - c)
            lo_c, hi_c = off + c, off + c + cw
            z = jnp.dot(xb, w_ref[:, lo_c:hi_c], preferred_element_type=F32)
            if hd:
                seg = seg64_ref[...] if hd == 64 else seg128_ref[...]
                for j in range(0, cw, LANES):
                    zj = z[:, j:j + LANES]
                    ss = _group_sumsq(zj, seg)
                    z_ref[:, lo_c + j:lo_c + j + LANES] = (
                        zj * lax.rsqrt(ss * (1.0 / hd) + RMS_EPS) * hgain_ref[:, lo_c + j:lo_c + j + LANES])
            else:
                z_ref[:, lo_c:hi_c] = z


def _inproj(x, gain, w_packed, hgain, tm=256):
    n, d = x.shape
    assert n % tm == 0
    const = lambda i: (0, 0)
    return pl.pallas_call(
        _inproj_kernel,
        grid=(n // tm,),
        in_specs=[
            pl.BlockSpec((tm, d), lambda i: (i, 0)),
            pl.BlockSpec((1, d), const),
            pl.BlockSpec((d, Z_COLS), const),
            pl.BlockSpec((1, Z_COLS), const),
            pl.BlockSpec((LANES, LANES), const),
            pl.BlockSpec((LANES, LANES), const),
        ],
        out_specs=pl.BlockSpec((tm, Z_COLS), lambda i: (i, 0)),
        out_shape=jax.ShapeDtypeStruct((n, Z_COLS), F32),
        compiler_params=_params(("arbitrary",)),
        name="inproj",
    )(x, gain, w_packed, hgain, _seg_matrix(64), _seg_matrix(128))


def _compress_weights(pe, w1, b1, w2):
    w1r = w1.reshape(2, CMP_STRIDE, HEAD_DIM, -1)
    eye = jnp.eye(2, dtype=F32)
    wbig = jnp.einsum("rlde,gh->lgdrhe", w1r, eye).reshape(CMP_STRIDE * LANES, 2 * LANES)
    pos = jnp.einsum("rld,rlde->e", pe.reshape(2, CMP_STRIDE, HEAD_DIM), w1r, precision=HI)
    cvec = jnp.tile(b1 + pos, 2)[None, :]
    w2big = jnp.einsum("ed,gh->gehd", w2, eye).reshape(LANES, LANES)
    return _split_hi_lo(wbig) + (cvec,) + _split_hi_lo(w2big)


def _compress_kernel(x_ref, wh_ref, wl_ref, c_ref, w2h_ref, w2l_ref, seg_ref, gain_ref, o_ref, *, norm):
    ns = x_ref.shape[1]
    a = _dot3(x_ref[0], wh_ref[...], wl_ref[...])
    nxt = pltpu.roll(a[:, LANES:], ns - 1, axis=0)
    h = a[:, :LANES] + nxt + c_ref[...]
    o = _dot3(_gelu_tanh(h), w2h_ref[...], w2l_ref[...])
    if norm:
        o = o * lax.rsqrt(_group_sumsq(o, seg_ref[...]) * (1.0 / HEAD_DIM) + RMS_EPS) * gain_ref[...]
    o_ref[0] = o


def _compress(x_sub, weights, gain, norm):
    b, ns, kk = x_sub.shape
    wh, wl, cvec, w2h, w2l = weights
    const = lambda i: (0, 0)
    return pl.pallas_call(
        functools.partial(_compress_kernel, norm=norm),
        grid=(b,),
        in_specs=[
            pl.BlockSpec((1, ns, kk), lambda i: (i, 0, 0)),
            pl.BlockSpec((kk, 2 * LANES), const), pl.BlockSpec((kk, 2 * LANES), const),
            pl.BlockSpec((1, LANES), const),
            pl.BlockSpec((LANES, LANES), const), pl.BlockSpec((LANES, LANES), const),
            pl.BlockSpec((LANES, LANES), const), pl.BlockSpec((1, LANES), const),
        ],
        out_specs=pl.BlockSpec((1, ns, LANES), lambda i: (i, 0, 0)),
        out_shape=jax.ShapeDtypeStruct((b, ns, LANES), F32),
        compiler_params=_params(("arbitrary",)),
        name="nsa_compress",
    )(x_sub, wh, wl, cvec, w2h, w2l, _seg_matrix(HEAD_DIM), gain)


def _toeplitz_tiles(tbl, deltas, tq, tk, hi):
    i = jnp.arange(tq)[:, None]
    j = jnp.arange(tk)[None, :]
    tiles = []
    for dl in deltas:
        d = dl + i - j
        b = tbl[_rel_bucket(d)]
        b = jnp.where(((d >= 0) & (d < hi))[..., None, None], b, NEG_INF)
        tiles.append(jnp.transpose(b, (2, 3, 0, 1)).reshape(tbl.shape[1], HPG * tq, tk))
    return jnp.stack(tiles, axis=1)


def _stack_queries(zq_ref, q_scr, tq):
    lane = lax.broadcasted_iota(jnp.int32, (tq, LANES), 1)
    for g in range(2):
        keep = (lane < HEAD_DIM) if g == 0 else (lane >= HEAD_DIM)
        for h in range(HPG):
            hd = HPG * g + h
            chunk = zq_ref[:, LANES * (hd // 2):LANES * (hd // 2 + 1)] * (HEAD_DIM ** -0.5)
            if hd % 2 != g:
                chunk = pltpu.roll(chunk, HEAD_DIM, axis=1)
            q_scr[g, h * tq:(h + 1) * tq, :] = jnp.where(keep, chunk, 0.0).astype(BF16)


def _unstack_heads(parts, o_ref, tq):
    lane = lax.broadcasted_iota(jnp.int32, (tq, LANES), 1)
    for j in range(4):
        g = j // 2
        he = (2 * j) % HPG
        left = parts[g][he * tq:(he + 1) * tq, :]
        right = parts[g][(he + 1) * tq:(he + 2) * tq, :]
        if g == 0:
            right = pltpu.roll(right, HEAD_DIM, axis=1)
        else:
            left = pltpu.roll(left, HEAD_DIM, axis=1)
        o_ref[:, LANES * j:LANES * (j + 1)] = jnp.where(lane < HEAD_DIM, left, right)


def _flash(q, k_ref, v_ref, bias_ref, g, n_bias, bm, blk_shift, rep, kt_lo, kt_hi, qt, m_scr, l_scr, acc_scr, tk):
    m_scr[...] = jnp.full(m_scr.shape, NEG_INF, F32)
    l_scr[...] = jnp.zeros(l_scr.shape, F32)
    acc_scr[...] = jnp.zeros(acc_scr.shape, F32)
    blk_row = lax.broadcasted_iota(jnp.int32, (LANES, tk), 0)
    key_lane = lax.broadcasted_iota(jnp.int32, (LANES, tk), 1)

    def body(kt, carry):
        k0 = pl.multiple_of(kt * tk, tk)
        k = k_ref[pl.ds(k0, tk), :].astype(BF16)
        v = v_ref[pl.ds(k0, tk), :].astype(BF16)
        s = lax.dot_general(q, k, _NT, preferred_element_type=F32)
        s = s + bias_ref[g, jnp.minimum(qt - kt, n_bias - 1)]
        if bm is not None:
            expand = jnp.where(blk_row == ((k0 + key_lane) >> blk_shift), 1.0, 0.0).astype(BF16)
            vis = jnp.dot(bm, expand, preferred_element_type=F32)
            add = jnp.where(vis > 0.5, 0.0, NEG_INF)
            if rep > 1:
                add = jnp.concatenate([add] * rep, axis=0)
            s = s + add
        m_old = m_scr[...]
        m_new = jnp.maximum(m_old, jnp.max(s, axis=-1, keepdims=True))
        p = jnp.where(s > HALF_NEG, jnp.exp(s - m_new), 0.0)
        alpha = jnp.exp(m_old - m_new)
        l_scr[...] = alpha * l_scr[...] + jnp.sum(p, axis=-1, keepdims=True)
        acc_scr[...] = alpha * acc_scr[...] + jnp.dot(p.astype(BF16), v, preferred_element_type=F32)
        m_scr[...] = m_new
        return carry

    lax.fori_loop(kt_lo, kt_hi, body, 0)
    l = l_scr[...]
    return acc_scr[...] / jnp.where(l > 0.0, l, 1.0)


def _take_topk(score, count, lane):
    sel = jnp.zeros(score.shape, F32)
    cur = score
    lane_f = lane.astype(F32)
    for _ in range(count):
        mx = jnp.max(cur, axis=-1, keepdims=True)
        idx = jnp.min(jnp.where(cur == mx, lane_f, float(LANES)), axis=-1, keepdims=True)
        hit = lane_f == idx
        sel = jnp.where(hit, jnp.where(mx > HALF_NEG, 1.0, 0.0), sel)
        cur = jnp.where(hit, KNOCKED, cur)
    return sel


def _nsa_prompt_kernel(zq_ref, gn_ref, kc_ref, vc_ref, ks_ref, vs_ref, kw_ref, vw_ref, biasc_ref, bslc_ref, bwin_ref,
                       mslc_ref, o_ref, q_scr, m_scr, l_scr, acc_scr, comb_scr, *, tq, tk, n_slc_tiles, n_win_tiles):
    qt = pl.program_id(1)
    q0 = qt * tq
    _stack_queries(zq_ref, q_scr, tq)
    gn = jax.nn.sigmoid(gn_ref[...])
    lane = lax.broadcasted_iota(jnp.int32, (tq, LANES), 1)
    tpos = q0 + lax.broadcasted_iota(jnp.int32, (tq, LANES), 0)
    qb = tpos >> 6

    def gate_col(c, g):
        return jnp.concatenate([gn[:, c * 8 + g * HPG + h:c * 8 + g * HPG + h + 1] for h in range(HPG)], axis=0)

    kc = kc_ref[0].astype(BF16)
    vc = vc_ref[0].astype(BF16)
    for g in range(2):
        q = q_scr[g]
        s_c = lax.dot_general(q, kc, _NT, preferred_element_type=F32) + biasc_ref[g]
        mx = jnp.max(s_c, axis=-1, keepdims=True)
        p = jnp.where(s_c > HALF_NEG, jnp.exp(s_c - mx), 0.0)
        den = jnp.sum(p, axis=-1, keepdims=True)
        p_c = p / jnp.where(den > 0.0, den, 1.0)
        comb_scr[g] = gate_col(0, g) * jnp.dot(p_c.astype(BF16), vc, preferred_element_type=F32)
        imp = p_c[0:tq] + p_c[tq:2 * tq] + p_c[2 * tq:3 * tq] + p_c[3 * tq:4 * tq]
        imp_hi, imp_lo = _split_hi_lo(imp)
        p_slc = (jnp.dot(imp_hi, mslc_ref[...], preferred_element_type=F32)
                 + jnp.dot(imp_lo, mslc_ref[...], preferred_element_type=F32))
        forced = (lane == 0) | (lane == qb) | (lane == qb - 1)
        score = jnp.where(lane > qb, NEG_INF, p_slc + jnp.where(forced, FORCE_BONUS, 0.0))
        bm = _take_topk(score, SLC_TOPN, lane).astype(BF16)
        o_s = _flash(q, ks_ref, vs_ref, bslc_ref, g, n_slc_tiles, bm, 6, HPG, 0, qt + 1, qt,
                     m_scr, l_scr, acc_scr, tk)
        comb_scr[g] = comb_scr[g] + gate_col(1, g) * o_s
        kt_lo = jnp.maximum(qt - (n_win_tiles - 1), 0)
        o_w = _flash(q, kw_ref, vw_ref, bwin_ref, g, n_win_tiles, None, 0, 1, kt_lo, qt + 1, qt,
                     m_scr, l_scr, acc_scr, tk)
        comb_scr[g] = comb_scr[g] + gate_col(2, g) * o_w
    _unstack_heads([comb_scr[0], comb_scr[1]], o_ref, tq)


def _zspec(rows, name, width, row_map):
    cb = _Z_OFF[name] // width
    assert _Z_OFF[name] % width == 0
    return pl.BlockSpec((rows, width), lambda b, i: (row_map(b, i), cb))


def _nsa_prompt(z, kc, vc, rel_table, bp, tp):
    tq, tk = ATT_TQ, ATT_TK
    nq = tp // tq
    n_cmp = tp // CMP_STRIDE - 1
    n_slc = tp // SLC_LEN
    assert tp % tq == 0 and n_cmp < LANES and n_slc <= LANES and kc.shape[1] == LANES
    tbl = rel_table[:, :NSA_HEADS].reshape(REL_BUCKETS, NSA_KV, HPG)
    t = jnp.arange(tp)
    dist_c = t[:, None] - (jnp.arange(LANES) * CMP_STRIDE + CMP_LEN - 1)[None, :]
    valid_c = (dist_c >= 0) & (jnp.arange(LANES) < n_cmp)[None, :]
    biasc = jnp.where(valid_c[..., None, None], tbl[_rel_bucket(dist_c)], NEG_INF)
    biasc = jnp.transpose(biasc.reshape(nq, tq, LANES, NSA_KV, HPG), (0, 3, 4, 1, 2)).reshape(nq, NSA_KV, HPG * tq, LANES)
    slc_deltas = (0, tq, 2 * tq)
    win_deltas = tuple(range(0, WINDOW + 1, tq))
    bslc = _toeplitz_tiles(tbl, slc_deltas, tq, tk, 1 << 30)
    bwin = _toeplitz_tiles(tbl, win_deltas, tq, tk, WINDOW)
    mslc = jnp.pad(_slc_matrix(n_cmp, n_slc), ((0, LANES - n_cmp), (0, LANES - n_slc))).astype(BF16)
    kern = functools.partial(_nsa_prompt_kernel, tq=tq, tk=tk, n_slc_tiles=len(slc_deltas), n_win_tiles=len(win_deltas))
    seq = lambda b, i: b
    tile = lambda b, i: b * nq + i
    full4 = lambda b, i: (0, 0, 0, 0)
    return pl.pallas_call(
        kern,
        grid=(bp, nq),
        in_specs=[
            _zspec(tq, "q_n", 512, tile), _zspec(tq, "g_n", LANES, tile),
            pl.BlockSpec((1, LANES, LANES), lambda b, i: (b, 0, 0)), pl.BlockSpec((1, LANES, LANES), lambda b, i: (b, 0, 0)),
        ] + [pl.BlockSpec((tp, LANES), (lambda b, i, cb=_Z_OFF[nm] // LANES: (b, cb))) for nm in ("ks", "vs", "kw", "vw")] + [
            pl.BlockSpec((None, NSA_KV, HPG * tq, LANES), lambda b, i: (i, 0, 0, 0)),
            pl.BlockSpec(bslc.shape, full4), pl.BlockSpec(bwin.shape, full4),
            pl.BlockSpec((LANES, LANES), lambda b, i: (0, 0)),
        ],
        out_specs=pl.BlockSpec((tq, 512), lambda b, i: (b * nq + i, 0)),
        out_shape=jax.ShapeDtypeStruct((bp * tp, 512), F32),
        scratch_shapes=[
            pltpu.VMEM((NSA_KV, HPG * tq, LANES), BF16),
            pltpu.VMEM((HPG * tq, 1), F32), pltpu.VMEM((HPG * tq, 1), F32), pltpu.VMEM((HPG * tq, LANES), F32),
            pltpu.VMEM((NSA_KV, HPG * tq, LANES), F32),
        ],
        compiler_params=_params(("arbitrary", "arbitrary")),
        name="nsa_prompt",
    )(z, z, kc, vc, z, z, z, z, biasc, bslc, bwin, mslc)


def _slc_matrix(n_cmp, n_slc):
    ratio, c_sub = SLC_LEN // CMP_STRIDE, CMP_LEN // CMP_STRIDE
    m = np.zeros((n_cmp, n_slc), np.float32)
    for j in range(n_slc):
        for o in range(-(c_sub - 1), ratio):
            k = ratio * j + o
            if 0 <= k < n_cmp:
                m[k, j] += float(min(o + c_sub, ratio) - max(o, 0))
    return jnp.asarray(m)


def _moba_prompt_kernel(zq_ref, k_ref, v_ref, bias_ref, o_ref, q_scr, m_scr, l_scr, acc_scr, res_scr, *, tq, tk, nb, n_tiles):
    qt = pl.program_id(1)
    q0 = qt * tq
    _stack_queries(zq_ref, q_scr, tq)
    rows = HPG * tq
    lane = lax.broadcasted_iota(jnp.int32, (rows, LANES), 1)
    trow = q0 + (lax.broadcasted_iota(jnp.int32, (rows, LANES), 0) & (tq - 1))
    own = trow >> 8
    kmean = jnp.concatenate(
        [jnp.mean(k_ref[n * MOBA_BLOCK:(n + 1) * MOBA_BLOCK, :], axis=0, keepdims=True) for n in range(nb)]
        + [jnp.zeros((LANES - nb, LANES), F32)], axis=0)
    km_hi, km_lo = _split_hi_lo(kmean)
    for g in range(2):
        q = q_scr[g]
        gs = (lax.dot_general(q, km_hi, _NT, preferred_element_type=F32)
              + lax.dot_general(q, km_lo, _NT, preferred_element_type=F32))
        gs = jnp.where(lane < own, gs, NEG_INF)
        sel = _take_topk(gs, min(MOBA_TOPK, nb - 1), lane)
        bm = jnp.where(lane == own, 1.0, sel).astype(BF16)
        res_scr[g] = _flash(q, k_ref, v_ref, bias_ref, g, n_tiles, bm, 8, 1, 0, qt + 1, qt, m_scr, l_scr, acc_scr, tk)
    _unstack_heads([res_scr[0], res_scr[1]], o_ref, tq)


def _moba_prompt(z, rel_table, bp, tp):
    tq, tk = ATT_TQ, ATT_TK
    nq = tp // tq
    nb = tp // MOBA_BLOCK
    assert tp % MOBA_BLOCK == 0 and tq & (tq - 1) == 0
    tbl = rel_table[:, NSA_HEADS:].reshape(REL_BUCKETS, MOBA_KV, HPG)
    deltas = (0, tq, 2 * tq)
    bias = _toeplitz_tiles(tbl, deltas, tq, tk, 1 << 30)
    kern = functools.partial(_moba_prompt_kernel, tq=tq, tk=tk, nb=nb, n_tiles=len(deltas))
    return pl.pallas_call(
        kern,
        grid=(bp, nq),
        in_specs=[
            _zspec(tq, "q_m", 512, lambda b, i: b * nq + i),
            pl.BlockSpec((tp, LANES), lambda b, i: (b, _Z_OFF["k_m"] // LANES)),
            pl.BlockSpec((tp, LANES), lambda b, i: (b, _Z_OFF["v_m"] // LANES)),
            pl.BlockSpec(bias.shape, lambda b, i: (0, 0, 0, 0)),
        ],
        out_specs=pl.BlockSpec((tq, 512), lambda b, i: (b * nq + i, 0)),
        out_shape=jax.ShapeDtypeStruct((bp * tp, 512), F32),
        scratch_shapes=[
            pltpu.VMEM((MOBA_KV, HPG * tq, LANES), BF16),
            pltpu.VMEM((HPG * tq, 1), F32), pltpu.VMEM((HPG * tq, 1), F32), pltpu.VMEM((HPG * tq, LANES), F32),
            pltpu.VMEM((MOBA_KV, HPG * tq, LANES), F32),
        ],
        compiler_params=_params(("arbitrary", "arbitrary")),
        name="moba_prompt",
    )(z, z, z, bias)


def _merge_kernel(on_ref, om_ref, ox_ref, gm_ref, x_ref, wb_ref, wo_ref, fg_ref, x1_ref, xn_ref):
    d = x_ref.shape[1]
    merged = jnp.zeros(x_ref.shape, F32)
    for c, o_ref in enumerate((on_ref, om_ref, ox_ref)):
        proj = jnp.dot(o_ref[...].astype(BF16), wb_ref[c], preferred_element_type=F32)
        merged = merged + jax.nn.sigmoid(gm_ref[:, c * d:(c + 1) * d]) * proj
    x1 = x_ref[...] + jnp.dot(merged.astype(BF16), wo_ref[...], preferred_element_type=F32)
    x1_ref[...] = x1
    ms = jnp.mean(x1 * x1, axis=-1, keepdims=True)
    xn_ref[...] = (x1 * lax.rsqrt(ms + RMS_EPS) * fg_ref[...]).astype(BF16)


def _merge(o_nsa, o_moba, o_mem, z, x, w_branch, w_out, ffn_gain, tm=256):
    n, d = x.shape
    assert n % tm == 0 and _Z_OFF["g_merge"] == 0
    row = lambda i: (i, 0)
    return pl.pallas_call(
        _merge_kernel,
        grid=(n // tm,),
        in_specs=[
            pl.BlockSpec((tm, 512), row), pl.BlockSpec((tm, 512), row), pl.BlockSpec((tm, 512), row),
            pl.BlockSpec((tm, 3 * d), row), pl.BlockSpec((tm, d), row),
            pl.BlockSpec((3, 512, d), lambda i: (0, 0, 0)), pl.BlockSpec((d, d), lambda i: (0, 0)),
            pl.BlockSpec((1, d), lambda i: (0, 0)),
        ],
        out_specs=[pl.BlockSpec((tm, d), row), pl.BlockSpec((tm, d), row)],
        out_shape=[jax.ShapeDtypeStruct((n, d), F32), jax.ShapeDtypeStruct((n, d), BF16)],
        compiler_params=_params(("arbitrary",)),
        name="merge_outproj",
    )(o_nsa, o_moba, o_mem, z, x, w_branch.astype(BF16), w_out.astype(BF16), ffn_gain[None, :])


def _peer_route_kernel(xT_ref, wqh_ref, wql_ref, skh_ref, skl_ref, s1_ref, s2_ref, e1_ref, e2_ref, tau_ref):
    xT = xT_ref[...]
    qT = jnp.dot(wqh_ref[...], xT, preferred_element_type=F32) + jnp.dot(wql_ref[...], xT, preferred_element_type=F32)
    half = HEAD_DIM

    def top_sorted(s):
        vals, cur = [], s
        for _ in range(PEER_TOPK):
            mx = jnp.max(cur, axis=0, keepdims=True)
            vals.append(mx)
            cur = jnp.where(cur == mx, KNOCKED, cur)
        return vals

    taus = []
    for h in range(PEER_HEADS):
        sc = []
        for p in range(2):
            r0 = (h * 2 + p) * half
            q_hi, q_lo = _split_hi_lo(qT[r0:r0 + half, :])
            sc.append(jnp.dot(skh_ref[h, p], q_hi, preferred_element_type=F32)
                      + jnp.dot(skl_ref[h, p], q_hi, preferred_element_type=F32)
                      + jnp.dot(skh_ref[h, p], q_lo, preferred_element_type=F32))
        a, b = top_sorted(sc[0]), top_sorted(sc[1])
        cand = jnp.concatenate([a[i] + b[j] for i in range(PEER_TOPK) for j in range(PEER_TOPK // (i + 1))], axis=0)
        cur = cand
        for _ in range(PEER_TOPK):
            tau = jnp.max(cur, axis=0, keepdims=True)
            cur = jnp.where(cur == tau, KNOCKED, cur)
        top = a[0] + b[0]
        zsum = jnp.sum(jnp.where(cand >= tau, jnp.exp(cand - top), 0.0), axis=0, keepdims=True)
        s1_ref[h] = sc[0]
        s2_ref[h] = sc[1]
        e1_ref[h] = jnp.exp(sc[0] - a[0]) / zsum
        e2_ref[h] = jnp.exp(sc[1] - b[0])
        taus.append(tau)
    tau_ref[...] = jnp.concatenate(taus, axis=0)


def _peer_route(xT, peer_w_q, sub_keys, tt=256):
    d, n = xT.shape
    assert n % tt == 0
    wqh, wql = _split_hi_lo(peer_w_q.T)
    skh, skl = _split_hi_lo(sub_keys)
    tok3 = lambda i: (0, 0, i)
    big = jax.ShapeDtypeStruct((PEER_HEADS, PEER_NKEYS, n), F32)
    return pl.pallas_call(
        _peer_route_kernel,
        grid=(n // tt,),
        in_specs=[
            pl.BlockSpec((d, tt), lambda i: (0, i)),
            pl.BlockSpec((d, d), lambda i: (0, 0)), pl.BlockSpec((d, d), lambda i: (0, 0)),
            pl.BlockSpec(sub_keys.shape, lambda i: (0, 0, 0, 0)), pl.BlockSpec(sub_keys.shape, lambda i: (0, 0, 0, 0)),
        ],
        out_specs=[pl.BlockSpec((PEER_HEADS, PEER_NKEYS, tt), tok3)] * 4 + [pl.BlockSpec((PEER_HEADS, tt), lambda i: (0, i))],
        out_shape=[big] * 4 + [jax.ShapeDtypeStruct((PEER_HEADS, n), F32)],
        compiler_params=_params(("arbitrary",)),
        name="peer_route",
    )(xT, wqh, wql, skh, skl)


def _peer_dense_kernel(xT_ref, u_ref, vT_ref, s1_ref, s2_ref, e1_ref, e2_ref, tau_ref, o_ref, a_ref, *, ec, tt):
    c = pl.program_id(1)

    @pl.when(c == 0)
    def _():
        o_ref[...] = jnp.zeros_like(o_ref)

    h_t = jnp.dot(u_ref[...], xT_ref[...], preferred_element_type=F32)
    for k in range(ec // LANES):
        for lc in range(0, tt, LANES):
            wk = jnp.zeros((LANES, LANES), F32)
            for h in range(PEER_HEADS):
                s1row = s1_ref[h, k:k + 1, lc:lc + LANES]
                e1row = e1_ref[h, k:k + 1, lc:lc + LANES]
                csum = s2_ref[h, :, lc:lc + LANES] + s1row
                sel = jnp.where(csum >= tau_ref[h:h + 1, lc:lc + LANES], e2_ref[h, :, lc:lc + LANES], 0.0)
                wk = wk + sel * e1row
            g = _gelu_tanh(h_t[k * LANES:(k + 1) * LANES, lc:lc + LANES])
            a_ref[k * LANES:(k + 1) * LANES, lc:lc + LANES] = (wk * g).astype(BF16)
    o_ref[...] += jnp.dot(vT_ref[...], a_ref[...], preferred_element_type=F32)


def _peer_dense(xT, u_b, vT_b, s1T, s2T, e1T, e2T, tauT, ec=1024):
    d, n = xT.shape
    tt = 512 if n % 512 == 0 else 256
    assert n % tt == 0 and ec == 8 * PEER_NKEYS
    ne = u_b.shape[0]
    kern = functools.partial(_peer_dense_kernel, ec=ec, tt=tt)
    allk = pl.BlockSpec((PEER_HEADS, PEER_NKEYS, tt), lambda i, c: (0, 0, i))
    chunk = pl.BlockSpec((PEER_HEADS, ec // PEER_NKEYS, tt), lambda i, c: (0, c, i))
    return pl.pallas_call(
        kern,
        grid=(n // tt, ne // ec),
        in_specs=[
            pl.BlockSpec((d, tt), lambda i, c: (0, i)),
            pl.BlockSpec((ec, d), lambda i, c: (c, 0)),
            pl.BlockSpec((d, ec), lambda i, c: (0, c)),
            chunk, allk, chunk, allk,
            pl.BlockSpec((PEER_HEADS, tt), lambda i, c: (0, i)),
        ],
        out_specs=pl.BlockSpec((d, tt), lambda i, c: (0, i)),
        out_shape=jax.ShapeDtypeStruct((d, n), F32),
        scratch_shapes=[pltpu.VMEM((ec, tt), BF16)],
        compiler_params=_params(("arbitrary", "arbitrary")),
        name="peer_dense",
    )(xT, u_b, vT_b, s1T, s2T, e1T, e2T, tauT)


def _peer(x1, xn_b, peer_w_q, sub_keys, u_b, vT_b):
    xT = xn_b.T
    s1T, s2T, e1T, e2T, tauT = _peer_route(xT, peer_w_q, sub_keys)
    out_t = _peer_dense(xT, u_b, vT_b, s1T, s2T, e1T, e2T, tauT)
    return x1 + out_t.T


def _compress_dense(x_raw, pe, w1, b1, w2):
    b, tk, _ = x_raw.shape
    n_sub = tk // CMP_STRIDE
    n_cmp = n_sub - 1
    xs = x_raw[:, :n_sub * CMP_STRIDE].reshape(b, n_sub, CMP_STRIDE * LANES)
    w1r = w1.reshape(2, CMP_STRIDE, HEAD_DIM, -1)
    wbig = jnp.einsum("rlde,gh->lgdrhe", w1r, jnp.eye(2, dtype=F32)).reshape(CMP_STRIDE * LANES, 2 * LANES)
    a = jnp.dot(xs, wbig, precision=HI)
    pos = jnp.einsum("rld,rlde->e", pe.reshape(2, CMP_STRIDE, HEAD_DIM), w1r, precision=HI)
    h = a[:, :n_cmp, :LANES] + a[:, 1:, LANES:] + jnp.tile(b1 + pos, 2)
    w2big = jnp.einsum("ed,gh->gehd", w2, jnp.eye(2, dtype=F32)).reshape(LANES, LANES)
    return jnp.dot(jax.nn.gelu(h), w2big, precision=HI)


def _nsa_dense(q, glog, kc_raw, vc_raw, ks, vs, kw_all, vw_all, q_start, win_start, w):
    b, t_q, _ = q.shape
    tk = kc_raw.shape[1]
    scale = HEAD_DIM ** -0.5
    qg = q.reshape(b, t_q, NSA_KV, HPG, HEAD_DIM)
    t = q_start + jnp.arange(t_q)
    tbl = w["rel_bias_table"][:, :NSA_HEADS].reshape(REL_BUCKETS, NSA_KV, HPG)

    kc = _compress_dense(kc_raw, w["cmp_pos_emb"][0], w["cmp_w1"][0], w["cmp_b1"][0], w["cmp_w2"][0])
    n_cmp = kc.shape[1]
    kc = _rms(kc.reshape(b, n_cmp, NSA_KV, HEAD_DIM), w["nsa_k_gain"][0])
    vc = _compress_dense(vc_raw, w["cmp_pos_emb"][1], w["cmp_w1"][1], w["cmp_b1"][1], w["cmp_w2"][1])
    vc = vc.reshape(b, n_cmp, NSA_KV, HEAD_DIM)
    cmp_end = jnp.arange(n_cmp) * CMP_STRIDE + CMP_LEN - 1
    dist_c = t[:, None] - cmp_end[None, :]
    bias_c = jnp.transpose(tbl[_rel_bucket(dist_c)], (2, 3, 0, 1))
    s_c = jnp.einsum("btghd,bngd->bghtn", qg, kc, precision=HI) * scale + bias_c
    p_c = _masked_softmax(s_c, dist_c >= 0)
    o_c = jnp.einsum("bghtn,bngd->btghd", p_c, vc, precision=HI)

    imp = jnp.transpose(p_c.sum(axis=2), (0, 2, 1, 3))
    n_slc = -(-tk // SLC_LEN)
    p_slc = jnp.dot(imp, _slc_matrix(n_cmp, n_slc), precision=HI)
    blk = jnp.arange(n_slc)[None, :]
    qb = (t // SLC_LEN)[:, None]
    forced = ((blk == 0) | (blk == qb) | (blk == qb - 1)).astype(F32)
    future = blk > qb
    score = jnp.where(future[None, :, None, :], NEG_INF, p_slc + FORCE_BONUS * forced[None, :, None, :])
    k_sel = min(SLC_TOPN, n_slc)
    sel_val, sel_idx = lax.top_k(score, k_sel)
    sel_ok = sel_val > HALF_NEG
    selm = (jax.nn.one_hot(sel_idx, n_slc, dtype=F32) * sel_ok[..., None].astype(F32)).sum(axis=-2) > 0.5

    kpos = jnp.arange(tk)
    d_s = t[:, None] - kpos[None, :]
    bias_s = jnp.transpose(tbl[_rel_bucket(d_s)], (2, 3, 0, 1))
    ksg = ks.reshape(b, tk, NSA_KV, HEAD_DIM)
    vsg = vs.reshape(b, tk, NSA_KV, HEAD_DIM)
    s_s = jnp.einsum("btghd,bkgd->bghtk", qg, ksg, precision=HI) * scale + bias_s
    m_s = jnp.repeat(jnp.transpose(selm, (0, 2, 1, 3)), SLC_LEN, axis=-1)[..., :tk] & (d_s >= 0)
    p_s = _masked_softmax(s_s, m_s[:, :, None])
    o_s = jnp.einsum("bghtk,bkgd->btghd", p_s, vsg, precision=HI)

    tw = kw_all.shape[1]
    wpos = win_start + jnp.arange(tw)
    d_w = t[:, None] - wpos[None, :]
    m_w = (wpos[None, :] >= 0) & (d_w >= 0) & (d_w < WINDOW)
    bias_w = jnp.transpose(tbl[_rel_bucket(d_w)], (2, 3, 0, 1))
    kwg = kw_all.reshape(b, tw, NSA_KV, HEAD_DIM)
    vwg = vw_all.reshape(b, tw, NSA_KV, HEAD_DIM)
    s_w = jnp.einsum("btghd,bkgd->bghtk", qg, kwg, precision=HI) * scale + bias_w
    p_w = _masked_softmax(s_w, m_w)
    o_w = jnp.einsum("bghtk,bkgd->btghd", p_w, vwg, precision=HI)

    g = jax.nn.sigmoid(glog).reshape(b, t_q, 3, NSA_KV, HPG)[..., None]
    o = g[:, :, 0] * o_c + g[:, :, 1] * o_s + g[:, :, 2] * o_w
    return o.reshape(b, t_q, NSA_HEADS * HEAD_DIM)


def _moba_dense(q, k, v, q_start, rel_table):
    b, t_q, _ = q.shape
    tk = k.shape[1]
    scale = HEAD_DIM ** -0.5
    qg = q.reshape(b, t_q, MOBA_KV, HPG, HEAD_DIM)
    kg = k.reshape(b, tk, MOBA_KV, HEAD_DIM)
    vg = v.reshape(b, tk, MOBA_KV, HEAD_DIM)
    t = q_start + jnp.arange(t_q)
    tbl = rel_table[:, NSA_HEADS:].reshape(REL_BUCKETS, MOBA_KV, HPG)
    nb = -(-tk // MOBA_BLOCK)
    pad = nb * MOBA_BLOCK - tk
    kp = jnp.pad(kg, ((0, 0), (0, pad), (0, 0), (0, 0)))
    n_top = min(MOBA_TOPK, nb - 1)
    kpos = jnp.arange(tk)
    kblk = kpos // MOBA_BLOCK
    d = t[:, None] - kpos[None, :]
    m_own = (kblk[None, :] == (t // MOBA_BLOCK)[:, None]) & (d >= 0)
    mask = jnp.broadcast_to(m_own[None, None, None], (b, MOBA_KV, HPG, t_q, tk))
    if n_top > 0:
        kmean = jnp.mean(kp.reshape(b, nb, MOBA_BLOCK, MOBA_KV, HEAD_DIM), axis=2)
        gs = jnp.einsum("btghd,bngd->btghn", qg, kmean, precision=HI)
        past = jnp.arange(nb)[None, :] < (t // MOBA_BLOCK)[:, None]
        gs = jnp.where(past[None, :, None, None, :], gs, NEG_INF)
        top_val, top_idx = lax.top_k(gs, n_top)
        top_ok = top_val > HALF_NEG
        selm = (jax.nn.one_hot(top_idx, nb, dtype=F32) * top_ok[..., None].astype(F32)).sum(axis=-2) > 0.5
        mask = mask | jnp.repeat(jnp.transpose(selm, (0, 2, 3, 1, 4)), MOBA_BLOCK, axis=-1)[..., :tk]
    bias = jnp.transpose(tbl[_rel_bucket(d)], (2, 3, 0, 1))
    s = jnp.einsum("btghd,bkgd->bghtk", qg, kg, precision=HI) * scale + bias
    p = _masked_softmax(s, mask)
    o = jnp.einsum("bghtk,bkgd->btghd", p, vg, precision=HI)
    return o.reshape(b, t_q, MOBA_HEADS * HEAD_DIM)


def _mem_attend(q, mk, mv):
    b, t_q, _ = q.shape
    qh = q.reshape(b, t_q, MEM_HEADS, MEM_HEAD_DIM)
    s = jnp.einsum("bthd,bmhd->bhtm", qh, mk, precision=HI) * (MEM_HEAD_DIM ** -0.5)
    p = jax.nn.softmax(s, axis=-1)
    return jnp.einsum("bhtm,bmhd->bthd", p, mv, precision=HI).reshape(b, t_q, MEM_HEADS * MEM_HEAD_DIM)


def _memory_kv(mem, w):
    b, m, _ = mem.shape
    kv = jnp.dot(_rms(mem, w["mem_norm_gain"]), w["w_mem_kv"], precision=HI)
    k, v = jnp.split(kv, 2, axis=-1)
    k = _rms(k.reshape(b, m, MEM_HEADS, MEM_HEAD_DIM), w["mem_k_gain"])
    return k, v.reshape(b, m, MEM_HEADS, MEM_HEAD_DIM)


def _gather_pages(pool, page_table):
    g = pool[page_table]
    return g.reshape(page_table.shape[0], -1, LANES)


def _zcol(z, name, width):
    return z[..., _Z_OFF[name]:_Z_OFF[name] + width]


def kernel(x_prompt, x_sample, cache_cmp_k, cache_cmp_v, cache_slc_k, cache_slc_v, cache_moba_k, cache_moba_v, cache_win_k, cache_win_v, cache_mem_k, cache_mem_v, page_table, mem_prompt, attn_norm_gain, w_in, nsa_q_gain, nsa_k_gain, cmp_pos_emb, cmp_w1, cmp_b1, cmp_w2, moba_q_gain, moba_k_gain, mem_norm_gain, w_mem_kv, mem_q_gain, mem_k_gain, rel_bias_table, w_branch, w_out, ffn_norm_gain, peer_w_q, peer_sub_keys, peer_u, peer_v):
    assert w_in.shape[0] == 1
    l = 0
    bp, tp, d = x_prompt.shape
    bs, ts, _ = x_sample.shape
    past_len = page_table.shape[1] * cache_cmp_k.shape[2]
    n_p, n_s = bp * tp, bs * ts
    w = {
        "rel_bias_table": rel_bias_table, "cmp_pos_emb": cmp_pos_emb[l], "cmp_w1": cmp_w1[l], "cmp_b1": cmp_b1[l],
        "cmp_w2": cmp_w2[l], "nsa_k_gain": nsa_k_gain[l], "mem_norm_gain": mem_norm_gain[l], "w_mem_kv": w_mem_kv[l],
        "mem_k_gain": mem_k_gain[l],
    }

    x_all = jnp.concatenate([x_prompt.reshape(n_p, d), x_sample.reshape(n_s, d)], axis=0)
    hgain = _head_gain_row({"nsa_q": nsa_q_gain[l], "nsa_k1": nsa_k_gain[l, 1], "nsa_k2": nsa_k_gain[l, 2],
                            "moba_q": moba_q_gain[l], "moba_k": moba_k_gain[l], "mem_q": mem_q_gain[l]})
    z = _inproj(x_all, attn_norm_gain[l][None, :], _pack_w_in(w_in[l]), hgain)
    zp = z[:n_p].reshape(bp, tp, Z_COLS)
    zs = z[n_p:].reshape(bs, ts, Z_COLS)

    def rows(zz, name):
        return _zcol(zz, name, LANES)

    kc_rows_p, vc_rows_p = rows(zp, "kc"), rows(zp, "vc")
    wk = _compress_weights(cmp_pos_emb[l, 0], cmp_w1[l, 0], cmp_b1[l, 0], cmp_w2[l, 0])
    wv = _compress_weights(cmp_pos_emb[l, 1], cmp_w1[l, 1], cmp_b1[l, 1], cmp_w2[l, 1])
    sub = lambda r: r.reshape(bp, tp // CMP_STRIDE, CMP_STRIDE * LANES)
    kgain = jnp.tile(nsa_k_gain[l, 0], 2)[None, :]
    kc_p = _compress(sub(kc_rows_p), wk, kgain, True)
    vc_p = _compress(sub(vc_rows_p), wv, kgain, False)
    o_nsa_p = _nsa_prompt(z, kc_p, vc_p, rel_bias_table, bp, tp)
    o_moba_p = _moba_prompt(z, rel_bias_table, bp, tp)
    mk_p, mv_p = _memory_kv(mem_prompt, w)
    o_mem_p = _mem_attend(_zcol(zp, "q_x", 512), mk_p, mv_p)

    past = [_gather_pages(c[l].reshape(c.shape[1], c.shape[2], LANES), page_table)
            for c in (cache_cmp_k, cache_cmp_v, cache_slc_k, cache_slc_v, cache_moba_k, cache_moba_v)]
    full = [jnp.concatenate([p, rows(zs, nm)], axis=1)
            for p, nm in zip(past, ("kc", "vc", "ks", "vs", "k_m", "v_m"))]
    win_len = cache_win_k.shape[2]
    kw_all = jnp.concatenate([cache_win_k[l].reshape(bs, win_len, LANES), rows(zs, "kw")], axis=1)
    vw_all = jnp.concatenate([cache_win_v[l].reshape(bs, win_len, LANES), rows(zs, "vw")], axis=1)
    win_start = past_len + ts - kw_all.shape[1]
    o_nsa_s = _nsa_dense(_zcol(zs, "q_n", 512), _zcol(zs, "g_n", 24), full[0], full[1], full[2], full[3],
                         kw_all, vw_all, past_len, win_start, w)
    o_moba_s = _moba_dense(_zcol(zs, "q_m", 512), full[4], full[5], past_len, rel_bias_table)
    o_mem_s = _mem_attend(_zcol(zs, "q_x", 512), cache_mem_k[l], cache_mem_v[l])

    cat = lambda a, b_: jnp.concatenate([a.reshape(n_p, 512), b_.reshape(n_s, 512)], axis=0)
    x1, xn_b = _merge(cat(o_nsa_p, o_nsa_s), cat(o_moba_p, o_moba_s), cat(o_mem_p, o_mem_s), z, x_all,
                      w_branch[l], w_out[l], ffn_norm_gain[l])
    y = _peer(x1, xn_b, peer_w_q[l], peer_sub_keys[l], peer_u[l].astype(BF16), peer_v[l].astype(BF16).T)
    y_p = y[:n_p].reshape(bp, tp, d)
    y_s = y[n_p:].reshape(bs, ts, d)

    def st(r, b_, t_):
        return r.reshape(1, b_, t_, 2, HEAD_DIM)

    keep_p = min(WINDOW, tp)
    p_win_k = st(rows(zp, "kw")[:, tp - keep_p:], bp, keep_p)
    p_win_v = st(rows(zp, "vw")[:, tp - keep_p:], bp, keep_p)
    keep_s = min(WINDOW, past_len + ts)
    s_win_k = st(kw_all[:, kw_all.shape[1] - keep_s:], bs, keep_s)
    s_win_v = st(vw_all[:, vw_all.shape[1] - keep_s:], bs, keep_s)
    return (y_p, y_s,
            st(kc_rows_p, bp, tp), st(vc_rows_p, bp, tp), st(rows(zp, "ks"), bp, tp), st(rows(zp, "vs"), bp, tp),
            st(rows(zp, "k_m"), bp, tp), st(rows(zp, "v_m"), bp, tp), p_win_k, p_win_v, mk_p[None], mv_p[None],
            st(rows(zs, "kc"), bs, ts), st(rows(zs, "vc"), bs, ts), st(rows(zs, "ks"), bs, ts), st(rows(zs, "vs"), bs, ts),
            st(rows(zs, "k_m"), bs, ts), st(rows(zs, "v_m"), bs, ts), s_win_k, s_win_v)
```

```python
import functools
import math

import jax
import jax.numpy as jnp
import numpy as np
from jax import lax
from jax.experimental import pallas as pl
from jax.experimental.pallas import tpu as pltpu

HEAD_DIM = 64
NSA_HEADS = 8
NSA_KV = 2
HPG = 4
CMP_LEN = 32
CMP_STRIDE = 16
SLC_LEN = 64
SLC_TOPN = 16
WINDOW = 512
FORCE_BONUS = 1000.0
MOBA_HEADS = 8
MOBA_KV = 2
MOBA_BLOCK = 256
MOBA_TOPK = 3
MEM_HEADS = 4
MEM_HEAD_DIM = 128
REL_BUCKETS = 32
REL_MAX_DIST = 128
PEER_HEADS = 8
PEER_NKEYS = 128
PEER_TOPK = 16
RMS_EPS = 1e-6
NEG_INF = -1e30
HALF_NEG = 0.5 * NEG_INF
KNOCKED = -3.0e38
LANES = 128
VMEM_LIMIT = 56 * 1024 * 1024
ATT_TQ = 128
ATT_TK = 128

F32 = jnp.float32
BF16 = jnp.bfloat16
HI = lax.Precision.HIGHEST
_NT = (((1,), (1,)), ((), ()))


def _rms(x, g):
    return x * lax.rsqrt(jnp.mean(x * x, axis=-1, keepdims=True) + RMS_EPS) * g


def _rel_bucket(dist):
    n = jnp.maximum(dist, 0)
    exact = REL_BUCKETS // 2
    nf = jnp.maximum(n, 1).astype(F32)
    large = exact + (jnp.log(nf / exact) / math.log(REL_MAX_DIST / exact) * (REL_BUCKETS - exact)).astype(jnp.int32)
    return jnp.where(n < exact, n, jnp.minimum(large, REL_BUCKETS - 1))


def _masked_softmax(s, mask):
    s = jnp.where(mask, s, NEG_INF)
    return jnp.where(mask, jax.nn.softmax(s, axis=-1), 0.0)


def _gelu_tanh(x):
    inner = x * (0.7978845608028654 + 0.035677408136300125 * (x * x))
    return (0.5 * x) * (1.0 + jnp.tanh(inner))


def _split_hi_lo(x):
    hi = x.astype(BF16)
    lo = (x - hi.astype(F32)).astype(BF16)
    return hi, lo


def _dot3(a, b_hi, b_lo):
    a_hi, a_lo = _split_hi_lo(a)
    return (jnp.dot(a_hi, b_hi, preferred_element_type=F32) + jnp.dot(a_lo, b_hi, preferred_element_type=F32)
            + jnp.dot(a_hi, b_lo, preferred_element_type=F32))


def _seg_matrix(hd):
    i = np.arange(LANES)
    return jnp.asarray((i[:, None] // hd) == (i[None, :] // hd), BF16)


def _group_sumsq(z, seg):
    sq_hi, sq_lo = _split_hi_lo(z * z)
    return jnp.dot(sq_hi, seg, preferred_element_type=F32) + jnp.dot(sq_lo, seg, preferred_element_type=F32)


def _params(sem):
    return pltpu.CompilerParams(dimension_semantics=sem, vmem_limit_bytes=VMEM_LIMIT)


_Z_GROUPS = (
    ("g_merge", 3072, 0), ("q_n", 512, 64), ("q_m", 512, 64), ("q_x", 512, 128),
    ("kc", 128, 0), ("vc", 128, 0), ("ks", 128, 64), ("vs", 128, 0), ("kw", 128, 64), ("vw", 128, 0),
    ("k_m", 128, 64), ("v_m", 128, 0), ("g_n", 128, 0),
)
_Z_OFF = {}
_o = 0
for _n, _w, _h in _Z_GROUPS:
    _Z_OFF[_n] = _o
    _o += _w
Z_COLS = _o
IN_SPLITS = (512, 128, 128, 128, 128, 128, 128, 24, 512, 128, 128, 512, 3072)
_SRC_NAMES = ("q_n", "kc", "vc", "ks", "vs", "kw", "vw", "g_n", "q_m", "k_m", "v_m", "q_x", "g_merge")


def _pack_w_in(w_in):
    cuts = np.cumsum((0,) + IN_SPLITS)
    parts = {n: w_in[:, cuts[i]:cuts[i + 1]] for i, n in enumerate(_SRC_NAMES)}
    parts["g_n"] = jnp.pad(parts["g_n"], ((0, 0), (0, LANES - 24)))
    return jnp.concatenate([parts[n] for n, _, _ in _Z_GROUPS], axis=1).astype(BF16)


def _head_gain_row(gains):
    g = {
        "q_n": jnp.tile(gains["nsa_q"], 8), "ks": jnp.tile(gains["nsa_k1"], 2), "kw": jnp.tile(gains["nsa_k2"], 2),
        "q_m": jnp.tile(gains["moba_q"], 8), "k_m": jnp.tile(gains["moba_k"], 2), "q_x": jnp.tile(gains["mem_q"], 4),
    }
    return jnp.concatenate([g.get(n, jnp.ones((w,), F32)) for n, w, _ in _Z_GROUPS])[None, :]


def _inproj_kernel(x_ref, gain_ref, w_ref, hgain_ref, seg64_ref, seg128_ref, z_ref):
    x = x_ref[...]
    ms = jnp.mean(x * x, axis=-1, keepdims=True)
    xb = (x * lax.rsqrt(ms + RMS_EPS) * gain_ref[...]).astype(BF16)
    for name, width, hd in _Z_GROUPS:
        off = _Z_OFF[name]
        for c in range(0, width, 512):
            cw = min(512, width - c)
            lo_c, hi_c = off + c, off + c + cw
            z = jnp.dot(xb, w_ref[:, lo_c:hi_c], preferred_element_type=F32)
            if hd:
                seg = seg64_ref[...] if hd == 64 else seg128_ref[...]
                for j in range(0, cw, LANES):
                    zj = z[:, j:j + LANES]
                    ss = _group_sumsq(zj, seg)
                    z_ref[:, lo_c + j:lo_c + j + LANES] = (
                        zj * lax.rsqrt(ss * (1.0 / hd) + RMS_EPS) * hgain_ref[:, lo_c + j:lo_c + j + LANES])
            else:
                z_ref[:, lo_c:hi_c] = z


def _inproj(x, gain, w_packed, hgain, tm=256):
    n, d = x.shape
    assert n % tm == 0
    const = lambda i: (0, 0)
    return pl.pallas_call(
        _inproj_kernel,
        grid=(n // tm,),
        in_specs=[
            pl.BlockSpec((tm, d), lambda i: (i, 0)),
            pl.BlockSpec((1, d), const),
            pl.BlockSpec((d, Z_COLS), const),
            pl.BlockSpec((1, Z_COLS), const),
            pl.BlockSpec((LANES, LANES), const),
            pl.BlockSpec((LANES, LANES), const),
        ],
        out_specs=pl.BlockSpec((tm, Z_COLS), lambda i: (i, 0)),
        out_shape=jax.ShapeDtypeStruct((n, Z_COLS), F32),
        compiler_params=_params(("arbitrary",)),
        name="inproj",
    )(x, gain, w_packed, hgain, _seg_matrix(64), _seg_matrix(128))


def _compress_weights(pe, w1, b1, w2):
    w1r = w1.reshape(2, CMP_STRIDE, HEAD_DIM, -1)
    eye = jnp.eye(2, dtype=F32)
    wbig = jnp.einsum("rlde,gh->lgdrhe", w1r, eye).reshape(CMP_STRIDE * LANES, 2 * LANES)
    pos = jnp.einsum("rld,rlde->e", pe.reshape(2, CMP_STRIDE, HEAD_DIM), w1r, precision=HI)
    cvec = jnp.tile(b1 + pos, 2)[None, :]
    w2big = jnp.einsum("ed,gh->gehd", w2, eye).reshape(LANES, LANES)
    return _split_hi_lo(wbig) + (cvec,) + _split_hi_lo(w2big)


def _compress_kernel(x_ref, wh_ref, wl_ref, c_ref, w2h_ref, w2l_ref, seg_ref, gain_ref, o_ref, *, norm):
    ns = x_ref.shape[1]
    a = _dot3(x_ref[0], wh_ref[...], wl_ref[...])
    nxt = pltpu.roll(a[:, LANES:], ns - 1, axis=0)
    h = a[:, :LANES] + nxt + c_ref[...]
    o = _dot3(_gelu_tanh(h), w2h_ref[...], w2l_ref[...])
    if norm:
        o = o * lax.rsqrt(_group_sumsq(o, seg_ref[...]) * (1.0 / HEAD_DIM) + RMS_EPS) * gain_ref[...]
    o_ref[0] = o


def _compress(x_sub, weights, gain, norm):
    b, ns, kk = x_sub.shape
    wh, wl, cvec, w2h, w2l = weights
    const = lambda i: (0, 0)
    return pl.pallas_call(
        functools.partial(_compress_kernel, norm=norm),
        grid=(b,),
        in_specs=[
            pl.BlockSpec((1, ns, kk), lambda i: (i, 0, 0)),
            pl.BlockSpec((kk, 2 * LANES), const), pl.BlockSpec((kk, 2 * LANES), const),
            pl.BlockSpec((1, LANES), const),
            pl.BlockSpec((LANES, LANES), const), pl.BlockSpec((LANES, LANES), const),
            pl.BlockSpec((LANES, LANES), const), pl.BlockSpec((1, LANES), const),
        ],
        out_specs=pl.BlockSpec((1, ns, LANES), lambda i: (i, 0, 0)),
        out_shape=jax.ShapeDtypeStruct((b, ns, LANES), F32),
        compiler_params=_params(("arbitrary",)),
        name="nsa_compress",
    )(x_sub, wh, wl, cvec, w2h, w2l, _seg_matrix(HEAD_DIM), gain)


def _toeplitz_tiles(tbl, deltas, tq, tk, hi):
    i = jnp.arange(tq)[:, None]
    j = jnp.arange(tk)[None, :]
    tiles = []
    for dl in deltas:
        d = dl + i - j
        b = tbl[_rel_bucket(d)]
        b = jnp.where(((d >= 0) & (d < hi))[..., None, None], b, NEG_INF)
        tiles.append(jnp.transpose(b, (2, 3, 0, 1)).reshape(tbl.shape[1], HPG * tq, tk))
    return jnp.stack(tiles, axis=1)


def _stack_queries(zq_ref, write, tq):
    lane = lax.broadcasted_iota(jnp.int32, (tq, LANES), 1)
    for g in range(2):
        keep = (lane < HEAD_DIM) if g == 0 else (lane >= HEAD_DIM)
        for h in range(HPG):
            hd = HPG * g + h
            chunk = zq_ref[:, LANES * (hd // 2):LANES * (hd // 2 + 1)] * (HEAD_DIM ** -0.5)
            if hd % 2 != g:
                chunk = pltpu.roll(chunk, HEAD_DIM, axis=1)
            write(g, h, jnp.where(keep, chunk, 0.0))


def _stack_queries_prompt(zq_ref, q_scr, tq):
    def write(g, h, val):
        q_scr[g, h * tq:(h + 1) * tq, :] = val.astype(BF16)
    _stack_queries(zq_ref, write, tq)


def _stack_queries_sample(zq_ref, q_scr, ts):
    def write(g, h, val):
        q_scr[(g * HPG + h) * ts:(g * HPG + h + 1) * ts, :] = val
    _stack_queries(zq_ref, write, ts)


def _unstack_heads(parts, o_ref, tq):
    lane = lax.broadcasted_iota(jnp.int32, (tq, LANES), 1)
    for j in range(4):
        g = j // 2
        he = (2 * j) % HPG
        left = parts[g][he * tq:(he + 1) * tq, :]
        right = parts[g][(he + 1) * tq:(he + 2) * tq, :]
        if g == 0:
            right = pltpu.roll(right, HEAD_DIM, axis=1)
        else:
            left = pltpu.roll(left, HEAD_DIM, axis=1)
        o_ref[:, LANES * j:LANES * (j + 1)] = jnp.where(lane < HEAD_DIM, left, right)


def _flash(q, k_ref, v_ref, bias_ref, g, n_bias, bm, blk_shift, rep, kt_lo, kt_hi, qt, m_scr, l_scr, acc_scr, tk):
    m_scr[...] = jnp.full(m_scr.shape, NEG_INF, F32)
    l_scr[...] = jnp.zeros(l_scr.shape, F32)
    acc_scr[...] = jnp.zeros(acc_scr.shape, F32)
    blk_row = lax.broadcasted_iota(jnp.int32, (LANES, tk), 0)
    key_lane = lax.broadcasted_iota(jnp.int32, (LANES, tk), 1)

    def body(kt, carry):
        k0 = pl.multiple_of(kt * tk, tk)
        k = k_ref[pl.ds(k0, tk), :].astype(BF16)
        v = v_ref[pl.ds(k0, tk), :].astype(BF16)
        s = lax.dot_general(q, k, _NT, preferred_element_type=F32)
        s = s + bias_ref[g, jnp.minimum(qt - kt, n_bias - 1)]
        if bm is not None:
            expand = jnp.where(blk_row == ((k0 + key_lane) >> blk_shift), 1.0, 0.0).astype(BF16)
            vis = jnp.dot(bm, expand, preferred_element_type=F32)
            add = jnp.where(vis > 0.5, 0.0, NEG_INF)
            if rep > 1:
                add = jnp.concatenate([add] * rep, axis=0)
            s = s + add
        m_old = m_scr[...]
        m_new = jnp.maximum(m_old, jnp.max(s, axis=-1, keepdims=True))
        p = jnp.where(s > HALF_NEG, jnp.exp(s - m_new), 0.0)
        alpha = jnp.exp(m_old - m_new)
        l_scr[...] = alpha * l_scr[...] + jnp.sum(p, axis=-1, keepdims=True)
        acc_scr[...] = alpha * acc_scr[...] + jnp.dot(p.astype(BF16), v, preferred_element_type=F32)
        m_scr[...] = m_new
        return carry

    lax.fori_loop(kt_lo, kt_hi, body, 0)
    l = l_scr[...]
    return acc_scr[...] / jnp.where(l > 0.0, l, 1.0)


def _take_topk(score, count, lane):
    sel = jnp.zeros(score.shape, F32)
    cur = score
    lane_f = lane.astype(F32)
    for _ in range(count):
        mx = jnp.max(cur, axis=-1, keepdims=True)
        idx = jnp.min(jnp.where(cur == mx, lane_f, float(score.shape[-1])), axis=-1, keepdims=True)
        hit = lane_f == idx
        sel = jnp.where(hit, jnp.where(mx > HALF_NEG, 1.0, 0.0), sel)
        cur = jnp.where(hit, KNOCKED, cur)
    return sel


def _nsa_prompt_kernel(zq_ref, gn_ref, kc_ref, vc_ref, ks_ref, vs_ref, kw_ref, vw_ref, biasc_ref, bslc_ref, bwin_ref,
                       mslc_ref, o_ref, q_scr, m_scr, l_scr, acc_scr, comb_scr, *, tq, tk, n_slc_tiles, n_win_tiles):
    qt = pl.program_id(1)
    q0 = qt * tq
    _stack_queries_prompt(zq_ref, q_scr, tq)
    gn = jax.nn.sigmoid(gn_ref[...])
    lane = lax.broadcasted_iota(jnp.int32, (tq, LANES), 1)
    tpos = q0 + lax.broadcasted_iota(jnp.int32, (tq, LANES), 0)
    qb = tpos >> 6

    def gate_col(c, g):
        return jnp.concatenate([gn[:, c * 8 + g * HPG + h:c * 8 + g * HPG + h + 1] for h in range(HPG)], axis=0)

    kc = kc_ref[0].astype(BF16)
    vc = vc_ref[0].astype(BF16)
    for g in range(2):
        q = q_scr[g]
        s_c = lax.dot_general(q, kc, _NT, preferred_element_type=F32) + biasc_ref[g]
        mx = jnp.max(s_c, axis=-1, keepdims=True)
        p = jnp.where(s_c > HALF_NEG, jnp.exp(s_c - mx), 0.0)
        den = jnp.sum(p, axis=-1, keepdims=True)
        p_c = p / jnp.where(den > 0.0, den, 1.0)
        comb_scr[g] = gate_col(0, g) * jnp.dot(p_c.astype(BF16), vc, preferred_element_type=F32)
        imp = p_c[0:tq] + p_c[tq:2 * tq] + p_c[2 * tq:3 * tq] + p_c[3 * tq:4 * tq]
        imp_hi, imp_lo = _split_hi_lo(imp)
        p_slc = (jnp.dot(imp_hi, mslc_ref[...], preferred_element_type=F32)
                 + jnp.dot(imp_lo, mslc_ref[...], preferred_element_type=F32))
        forced = (lane == 0) | (lane == qb) | (lane == qb - 1)
        score = jnp.where(lane > qb, NEG_INF, p_slc + jnp.where(forced, FORCE_BONUS, 0.0))
        bm = _take_topk(score, SLC_TOPN, lane).astype(BF16)
        o_s = _flash(q, ks_ref, vs_ref, bslc_ref, g, n_slc_tiles, bm, 6, HPG, 0, qt + 1, qt,
                     m_scr, l_scr, acc_scr, tk)
        comb_scr[g] = comb_scr[g] + gate_col(1, g) * o_s
        kt_lo = jnp.maximum(qt - (n_win_tiles - 1), 0)
        o_w = _flash(q, kw_ref, vw_ref, bwin_ref, g, n_win_tiles, None, 0, 1, kt_lo, qt + 1, qt,
                     m_scr, l_scr, acc_scr, tk)
        comb_scr[g] = comb_scr[g] + gate_col(2, g) * o_w
    _unstack_heads([comb_scr[0], comb_scr[1]], o_ref, tq)


def _zspec(rows, name, width, row_map):
    cb = _Z_OFF[name] // width
    assert _Z_OFF[name] % width == 0
    return pl.BlockSpec((rows, width), lambda b, i: (row_map(b, i), cb))


def _nsa_prompt(z, kc, vc, rel_table, bp, tp):
    tq, tk = ATT_TQ, ATT_TK
    nq = tp // tq
    n_cmp = tp // CMP_STRIDE - 1
    n_slc = tp // SLC_LEN
    assert tp % tq == 0 and n_cmp < LANES and n_slc <= LANES and kc.shape[1] == LANES
    tbl = rel_table[:, :NSA_HEADS].reshape(REL_BUCKETS, NSA_KV, HPG)
    t = jnp.arange(tp)
    dist_c = t[:, None] - (jnp.arange(LANES) * CMP_STRIDE + CMP_LEN - 1)[None, :]
    valid_c = (dist_c >= 0) & (jnp.arange(LANES) < n_cmp)[None, :]
    biasc = jnp.where(valid_c[..., None, None], tbl[_rel_bucket(dist_c)], NEG_INF)
    biasc = jnp.transpose(biasc.reshape(nq, tq, LANES, NSA_KV, HPG), (0, 3, 4, 1, 2)).reshape(nq, NSA_KV, HPG * tq, LANES)
    slc_deltas = (0, tq, 2 * tq)
    win_deltas = tuple(range(0, WINDOW + 1, tq))
    bslc = _toeplitz_tiles(tbl, slc_deltas, tq, tk, 1 << 30)
    bwin = _toeplitz_tiles(tbl, win_deltas, tq, tk, WINDOW)
    mslc = jnp.pad(_slc_matrix(n_cmp, n_slc), ((0, LANES - n_cmp), (0, LANES - n_slc))).astype(BF16)
    kern = functools.partial(_nsa_prompt_kernel, tq=tq, tk=tk, n_slc_tiles=len(slc_deltas), n_win_tiles=len(win_deltas))
    seq = lambda b, i: b
    tile = lambda b, i: b * nq + i
    full4 = lambda b, i: (0, 0, 0, 0)
    return pl.pallas_call(
        kern,
        grid=(bp, nq),
        in_specs=[
            _zspec(tq, "q_n", 512, tile), _zspec(tq, "g_n", LANES, tile),
            pl.BlockSpec((1, LANES, LANES), lambda b, i: (b, 0, 0)), pl.BlockSpec((1, LANES, LANES), lambda b, i: (b, 0, 0)),
        ] + [pl.BlockSpec((tp, LANES), (lambda b, i, cb=_Z_OFF[nm] // LANES: (b, cb))) for nm in ("ks", "vs", "kw", "vw")] + [
            pl.BlockSpec((None, NSA_KV, HPG * tq, LANES), lambda b, i: (i, 0, 0, 0)),
            pl.BlockSpec(bslc.shape, full4), pl.BlockSpec(bwin.shape, full4),
            pl.BlockSpec((LANES, LANES), lambda b, i: (0, 0)),
        ],
        out_specs=pl.BlockSpec((tq, 512), lambda b, i: (b * nq + i, 0)),
        out_shape=jax.ShapeDtypeStruct((bp * tp, 512), F32),
        scratch_shapes=[
            pltpu.VMEM((NSA_KV, HPG * tq, LANES), BF16),
            pltpu.VMEM((HPG * tq, 1), F32), pltpu.VMEM((HPG * tq, 1), F32), pltpu.VMEM((HPG * tq, LANES), F32),
            pltpu.VMEM((NSA_KV, HPG * tq, LANES), F32),
        ],
        compiler_params=_params(("arbitrary", "arbitrary")),
        name="nsa_prompt",
    )(z, z, kc, vc, z, z, z, z, biasc, bslc, bwin, mslc)


def _slc_matrix(n_cmp, n_slc):
    ratio, c_sub = SLC_LEN // CMP_STRIDE, CMP_LEN // CMP_STRIDE
    m = np.zeros((n_cmp, n_slc), np.float32)
    for j in range(n_slc):
        for o in range(-(c_sub - 1), ratio):
            k = ratio * j + o
            if 0 <= k < n_cmp:
                m[k, j] += float(min(o + c_sub, ratio) - max(o, 0))
    return jnp.asarray(m)


def _moba_prompt_kernel(zq_ref, k_ref, v_ref, bias_ref, o_ref, q_scr, m_scr, l_scr, acc_scr, res_scr, *, tq, tk, nb, n_tiles):
    qt = pl.program_id(1)
    q0 = qt * tq
    _stack_queries_prompt(zq_ref, q_scr, tq)
    rows = HPG * tq
    lane = lax.broadcasted_iota(jnp.int32, (rows, LANES), 1)
    trow = q0 + (lax.broadcasted_iota(jnp.int32, (rows, LANES), 0) & (tq - 1))
    own = trow >> 8
    kmean = jnp.concatenate(
        [jnp.mean(k_ref[n * MOBA_BLOCK:(n + 1) * MOBA_BLOCK, :], axis=0, keepdims=True) for n in range(nb)]
        + [jnp.zeros((LANES - nb, LANES), F32)], axis=0)
    km_hi, km_lo = _split_hi_lo(kmean)
    for g in range(2):
        q = q_scr[g]
        gs = (lax.dot_general(q, km_hi, _NT, preferred_element_type=F32)
              + lax.dot_general(q, km_lo, _NT, preferred_element_type=F32))
        gs = jnp.where(lane < own, gs, NEG_INF)
        sel = _take_topk(gs, min(MOBA_TOPK, nb - 1), lane)
        bm = jnp.where(lane == own, 1.0, sel).astype(BF16)
        res_scr[g] = _flash(q, k_ref, v_ref, bias_ref, g, n_tiles, bm, 8, 1, 0, qt + 1, qt, m_scr, l_scr, acc_scr, tk)
    _unstack_heads([res_scr[0], res_scr[1]], o_ref, tq)


def _moba_prompt(z, rel_table, bp, tp):
    tq, tk = ATT_TQ, ATT_TK
    nq = tp // tq
    nb = tp // MOBA_BLOCK
    assert tp % MOBA_BLOCK == 0 and tq & (tq - 1) == 0
    tbl = rel_table[:, NSA_HEADS:].reshape(REL_BUCKETS, MOBA_KV, HPG)
    deltas = (0, tq, 2 * tq)
    bias = _toeplitz_tiles(tbl, deltas, tq, tk, 1 << 30)
    kern = functools.partial(_moba_prompt_kernel, tq=tq, tk=tk, nb=nb, n_tiles=len(deltas))
    return pl.pallas_call(
        kern,
        grid=(bp, nq),
        in_specs=[
            _zspec(tq, "q_m", 512, lambda b, i: b * nq + i),
            pl.BlockSpec((tp, LANES), lambda b, i: (b, _Z_OFF["k_m"] // LANES)),
            pl.BlockSpec((tp, LANES), lambda b, i: (b, _Z_OFF["v_m"] // LANES)),
            pl.BlockSpec(bias.shape, lambda b, i: (0, 0, 0, 0)),
        ],
        out_specs=pl.BlockSpec((tq, 512), lambda b, i: (b * nq + i, 0)),
        out_shape=jax.ShapeDtypeStruct((bp * tp, 512), F32),
        scratch_shapes=[
            pltpu.VMEM((MOBA_KV, HPG * tq, LANES), BF16),
            pltpu.VMEM((HPG * tq, 1), F32), pltpu.VMEM((HPG * tq, 1), F32), pltpu.VMEM((HPG * tq, LANES), F32),
            pltpu.VMEM((MOBA_KV, HPG * tq, LANES), F32),
        ],
        compiler_params=_params(("arbitrary", "arbitrary")),
        name="moba_prompt",
    )(z, z, z, bias)


def _merge_kernel(on_ref, om_ref, ox_ref, gm_ref, x_ref, wb_ref, wo_ref, fg_ref, x1_ref, xn_ref):
    d = x_ref.shape[1]
    merged = jnp.zeros(x_ref.shape, F32)
    for c, o_ref in enumerate((on_ref, om_ref, ox_ref)):
        proj = jnp.dot(o_ref[...].astype(BF16), wb_ref[c], preferred_element_type=F32)
        merged = merged + jax.nn.sigmoid(gm_ref[:, c * d:(c + 1) * d]) * proj
    x1 = x_ref[...] + jnp.dot(merged.astype(BF16), wo_ref[...], preferred_element_type=F32)
    x1_ref[...] = x1
    ms = jnp.mean(x1 * x1, axis=-1, keepdims=True)
    xn_ref[...] = (x1 * lax.rsqrt(ms + RMS_EPS) * fg_ref[...]).astype(BF16)


def _merge(o_nsa, o_moba, o_mem, z, x, w_branch, w_out, ffn_gain, tm=256):
    n, d = x.shape
    assert n % tm == 0 and _Z_OFF["g_merge"] == 0
    row = lambda i: (i, 0)
    return pl.pallas_call(
        _merge_kernel,
        grid=(n // tm,),
        in_specs=[
            pl.BlockSpec((tm, 512), row), pl.BlockSpec((tm, 512), row), pl.BlockSpec((tm, 512), row),
            pl.BlockSpec((tm, 3 * d), row), pl.BlockSpec((tm, d), row),
            pl.BlockSpec((3, 512, d), lambda i: (0, 0, 0)), pl.BlockSpec((d, d), lambda i: (0, 0)),
            pl.BlockSpec((1, d), lambda i: (0, 0)),
        ],
        out_specs=[pl.BlockSpec((tm, d), row), pl.BlockSpec((tm, d), row)],
        out_shape=[jax.ShapeDtypeStruct((n, d), F32), jax.ShapeDtypeStruct((n, d), BF16)],
        compiler_params=_params(("arbitrary",)),
        name="merge_outproj",
    )(o_nsa, o_moba, o_mem, z, x, w_branch.astype(BF16), w_out.astype(BF16), ffn_gain[None, :])


def _peer_route_kernel(xT_ref, wqh_ref, wql_ref, skh_ref, skl_ref, s1_ref, s2_ref, e1_ref, e2_ref, tau_ref):
    xT = xT_ref[...]
    qT = jnp.dot(wqh_ref[...], xT, preferred_element_type=F32) + jnp.dot(wql_ref[...], xT, preferred_element_type=F32)
    half = HEAD_DIM

    def top_sorted(s):
        vals, cur = [], s
        for _ in range(PEER_TOPK):
            mx = jnp.max(cur, axis=0, keepdims=True)
            vals.append(mx)
            cur = jnp.where(cur == mx, KNOCKED, cur)
        return vals

    taus = []
    for h in range(PEER_HEADS):
        sc = []
        for p in range(2):
            r0 = (h * 2 + p) * half
            q_hi, q_lo = _split_hi_lo(qT[r0:r0 + half, :])
            sc.append(jnp.dot(skh_ref[h, p], q_hi, preferred_element_type=F32)
                      + jnp.dot(skl_ref[h, p], q_hi, preferred_element_type=F32)
                      + jnp.dot(skh_ref[h, p], q_lo, preferred_element_type=F32))
        a, b = top_sorted(sc[0]), top_sorted(sc[1])
        cand = jnp.concatenate([a[i] + b[j] for i in range(PEER_TOPK) for j in range(PEER_TOPK // (i + 1))], axis=0)
        cur = cand
        for _ in range(PEER_TOPK):
            tau = jnp.max(cur, axis=0, keepdims=True)
            cur = jnp.where(cur == tau, KNOCKED, cur)
        top = a[0] + b[0]
        zsum = jnp.sum(jnp.where(cand >= tau, jnp.exp(cand - top), 0.0), axis=0, keepdims=True)
        s1_ref[h] = sc[0]
        s2_ref[h] = sc[1]
        e1_ref[h] = jnp.exp(sc[0] - a[0]) / zsum
        e2_ref[h] = jnp.exp(sc[1] - b[0])
        taus.append(tau)
    tau_ref[...] = jnp.concatenate(taus, axis=0)


def _peer_route(xT, peer_w_q, sub_keys, tt=256):
    d, n = xT.shape
    assert n % tt == 0
    wqh, wql = _split_hi_lo(peer_w_q.T)
    skh, skl = _split_hi_lo(sub_keys)
    tok3 = lambda i: (0, 0, i)
    big = jax.ShapeDtypeStruct((PEER_HEADS, PEER_NKEYS, n), F32)
    return pl.pallas_call(
        _peer_route_kernel,
        grid=(n // tt,),
        in_specs=[
            pl.BlockSpec((d, tt), lambda i: (0, i)),
            pl.BlockSpec((d, d), lambda i: (0, 0)), pl.BlockSpec((d, d), lambda i: (0, 0)),
            pl.BlockSpec(sub_keys.shape, lambda i: (0, 0, 0, 0)), pl.BlockSpec(sub_keys.shape, lambda i: (0, 0, 0, 0)),
        ],
        out_specs=[pl.BlockSpec((PEER_HEADS, PEER_NKEYS, tt), tok3)] * 4 + [pl.BlockSpec((PEER_HEADS, tt), lambda i: (0, i))],
        out_shape=[big] * 4 + [jax.ShapeDtypeStruct((PEER_HEADS, n), F32)],
        compiler_params=_params(("arbitrary",)),
        name="peer_route",
    )(xT, wqh, wql, skh, skl)


def _peer_dense_kernel(xT_ref, u_ref, vT_ref, s1_ref, s2_ref, e1_ref, e2_ref, tau_ref, o_ref, a_ref, *, ec, tt):
    c = pl.program_id(1)

    @pl.when(c == 0)
    def _():
        o_ref[...] = jnp.zeros_like(o_ref)

    h_t = jnp.dot(u_ref[...], xT_ref[...], preferred_element_type=F32)
    for k in range(ec // LANES):
        for lc in range(0, tt, LANES):
            wk = jnp.zeros((LANES, LANES), F32)
            for h in range(PEER_HEADS):
                s1row = s1_ref[h, k:k + 1, lc:lc + LANES]
                e1row = e1_ref[h, k:k + 1, lc:lc + LANES]
                csum = s2_ref[h, :, lc:lc + LANES] + s1row
                sel = jnp.where(csum >= tau_ref[h:h + 1, lc:lc + LANES], e2_ref[h, :, lc:lc + LANES], 0.0)
                wk = wk + sel * e1row
            g = _gelu_tanh(h_t[k * LANES:(k + 1) * LANES, lc:lc + LANES])
            a_ref[k * LANES:(k + 1) * LANES, lc:lc + LANES] = (wk * g).astype(BF16)
    o_ref[...] += jnp.dot(vT_ref[...], a_ref[...], preferred_element_type=F32)


def _peer_dense(xT, u_b, vT_b, s1T, s2T, e1T, e2T, tauT, ec=1024):
    d, n = xT.shape
    tt = 512 if n % 512 == 0 else 256
    assert n % tt == 0 and ec == 8 * PEER_NKEYS
    ne = u_b.shape[0]
    kern = functools.partial(_peer_dense_kernel, ec=ec, tt=tt)
    allk = pl.BlockSpec((PEER_HEADS, PEER_NKEYS, tt), lambda i, c: (0, 0, i))
    chunk = pl.BlockSpec((PEER_HEADS, ec // PEER_NKEYS, tt), lambda i, c: (0, c, i))
    return pl.pallas_call(
        kern,
        grid=(n // tt, ne // ec),
        in_specs=[
            pl.BlockSpec((d, tt), lambda i, c: (0, i)),
            pl.BlockSpec((ec, d), lambda i, c: (c, 0)),
            pl.BlockSpec((d, ec), lambda i, c: (0, c)),
            chunk, allk, chunk, allk,
            pl.BlockSpec((PEER_HEADS, tt), lambda i, c: (0, i)),
        ],
        out_specs=pl.BlockSpec((d, tt), lambda i, c: (0, i)),
        out_shape=jax.ShapeDtypeStruct((d, n), F32),
        scratch_shapes=[pltpu.VMEM((ec, tt), BF16)],
        compiler_params=_params(("arbitrary", "arbitrary")),
        name="peer_dense",
    )(xT, u_b, vT_b, s1T, s2T, e1T, e2T, tauT)


def _peer(x1, xn_b, peer_w_q, sub_keys, u_b, vT_b):
    xT = xn_b.T
    s1T, s2T, e1T, e2T, tauT = _peer_route(xT, peer_w_q, sub_keys)
    out_t = _peer_dense(xT, u_b, vT_b, s1T, s2T, e1T, e2T, tauT)
    return x1 + out_t.T


PAGES_PER_STEP = 8


def _page_specs(n_lead, inner):
    zeros = (0,) * len(inner)
    return [pl.BlockSpec((1,) + inner, (lambda b, j, pt, p=p: (pt[b, j * PAGES_PER_STEP + p],) + zeros))
            for p in range(n_lead)]


def _compress_paged_kernel(pt_ref, *refs, norm):
    del pt_ref
    p_n = PAGES_PER_STEP
    x_refs = refs[:p_n]
    wh_ref, wl_ref, c_ref, w2h_ref, w2l_ref, seg_ref, gain_ref, o_ref, x_scr = refs[p_n:]
    j = pl.program_id(1)
    sub = x_refs[0].shape[1]
    for p in range(p_n):
        x_scr[pl.ds(pl.multiple_of((j * p_n + p) * sub, sub), sub), :] = x_refs[p][0]

    @pl.when(j == pl.num_programs(1) - 1)
    def _():
        ns = x_scr.shape[0]
        a = _dot3(x_scr[...], wh_ref[...], wl_ref[...])
        nxt = pltpu.roll(a[:, LANES:], ns - 1, axis=0)
        h = a[:, :LANES] + nxt + c_ref[...]
        o = _dot3(_gelu_tanh(h), w2h_ref[...], w2l_ref[...])
        if norm:
            o = o * lax.rsqrt(_group_sumsq(o, seg_ref[...]) * (1.0 / HEAD_DIM) + RMS_EPS) * gain_ref[...]
        o_ref[0] = o


def _compress_paged(pool, page_table, weights, gain, norm):
    n_phys, page, _ = pool.shape
    b, n_pages = page_table.shape
    sub = page // CMP_STRIDE
    kk = CMP_STRIDE * LANES
    ns = n_pages * sub
    assert n_pages % PAGES_PER_STEP == 0 and page % CMP_STRIDE == 0
    wh, wl, cvec, w2h, w2l = weights
    const = lambda b_, j, pt: (0, 0)
    pool_sub = pool.reshape(n_phys, sub, kk)
    grid_spec = pltpu.PrefetchScalarGridSpec(
        num_scalar_prefetch=1,
        grid=(b, n_pages // PAGES_PER_STEP),
        in_specs=_page_specs(PAGES_PER_STEP, (sub, kk)) + [
            pl.BlockSpec((kk, 2 * LANES), const), pl.BlockSpec((kk, 2 * LANES), const),
            pl.BlockSpec((1, LANES), const),
            pl.BlockSpec((LANES, LANES), const), pl.BlockSpec((LANES, LANES), const),
            pl.BlockSpec((LANES, LANES), const), pl.BlockSpec((1, LANES), const),
        ],
        out_specs=pl.BlockSpec((1, ns, LANES), lambda b_, j, pt: (b_, 0, 0)),
        scratch_shapes=[pltpu.VMEM((ns, kk), F32)],
    )
    return pl.pallas_call(
        functools.partial(_compress_paged_kernel, norm=norm),
        grid_spec=grid_spec,
        out_shape=jax.ShapeDtypeStruct((b, ns, LANES), F32),
        compiler_params=_params(("arbitrary", "arbitrary")),
        name="nsa_compress_paged",
    )(page_table, *([pool_sub] * PAGES_PER_STEP), wh, wl, cvec, w2h, w2l, _seg_matrix(HEAD_DIM), gain)


def _softmax_tile(s):
    mx = jnp.max(s, axis=-1, keepdims=True)
    p = jnp.where(s > HALF_NEG, jnp.exp(s - mx), 0.0)
    return mx, p, jnp.sum(p, axis=-1, keepdims=True)


def _online_update(s, v, m_scr, l_scr, acc_scr):
    m_old = m_scr[...]
    m_new = jnp.maximum(m_old, jnp.max(s, axis=-1, keepdims=True))
    p = jnp.where(s > HALF_NEG, jnp.exp(s - m_new), 0.0)
    alpha = jnp.exp(m_old - m_new)
    l_scr[...] = alpha * l_scr[...] + jnp.sum(p, axis=-1, keepdims=True)
    acc_scr[...] = alpha * acc_scr[...] + jnp.dot(p.astype(BF16), v, preferred_element_type=F32)
    m_scr[...] = m_new


def _pad_rows(x, rows):
    return jnp.concatenate([x, jnp.zeros((rows - x.shape[0], x.shape[1]), x.dtype)], axis=0)


def _sample_row_bias(tbl, ts, kpos, qpos0, lo_ok):
    d = (qpos0 + jnp.arange(ts))[:, None] - kpos[None, :]
    b = jnp.where(((d >= 0) & lo_ok)[..., None, None], tbl[_rel_bucket(d)], NEG_INF)
    return jnp.transpose(b, (2, 3, 0, 1)).reshape(-1, kpos.shape[0])


def _nsa_sample_kernel(pt_ref, *refs, ts, past_len, slc_lanes):
    del pt_ref
    p_n = PAGES_PER_STEP
    k_pages, v_pages = refs[:p_n], refs[p_n:2 * p_n]
    (zq_ref, gn_ref, kc_ref, vc_ref, ksn_ref, vsn_ref, wk_ref, wv_ref, kwn_ref, vwn_ref,
     biasc_ref, bpast_ref, bnew_ref, bwin_ref, mslc_ref, o_ref,
     q_scr, m_scr, l_scr, acc_scr, comb_scr, bm_scr) = refs[2 * p_n:]
    j = pl.program_id(1)
    page = k_pages[0].shape[1]
    rows = NSA_KV * HPG * ts
    half = HPG * ts

    def gate_col(gn, c):
        return jnp.concatenate([gn[:, c * 8 + hd:c * 8 + hd + 1] for hd in range(NSA_HEADS)], axis=0)

    @pl.when(j == 0)
    def _():
        _stack_queries_sample(zq_ref, q_scr, ts)
        q = q_scr[...].astype(BF16)
        gn = jax.nn.sigmoid(gn_ref[...])
        s_c = lax.dot_general(q, kc_ref[0].astype(BF16), _NT, preferred_element_type=F32) + biasc_ref[...]
        mx, p, den = _softmax_tile(s_c)
        p_c = p / jnp.where(den > 0.0, den, 1.0)
        comb_scr[...] = gate_col(gn, 0) * jnp.dot(p_c.astype(BF16), vc_ref[0].astype(BF16), preferred_element_type=F32)
        imp = jnp.concatenate(
            [sum(p_c[g * half + h * ts:g * half + (h + 1) * ts] for h in range(HPG)) for g in range(NSA_KV)], axis=0)
        imp_hi, imp_lo = _split_hi_lo(imp)
        p_slc = (jnp.dot(imp_hi, mslc_ref[...], preferred_element_type=F32)
                 + jnp.dot(imp_lo, mslc_ref[...], preferred_element_type=F32))
        lane = lax.broadcasted_iota(jnp.int32, p_slc.shape, 1)
        tpos = past_len + (lax.broadcasted_iota(jnp.int32, p_slc.shape, 0) & (ts - 1))
        qb = tpos >> 6
        forced = (lane == 0) | (lane == qb) | (lane == qb - 1)
        score = jnp.where(lane > qb, NEG_INF, p_slc + jnp.where(forced, FORCE_BONUS, 0.0))
        sel = _take_topk(score, SLC_TOPN, lane)
        bm_scr[...] = jnp.concatenate(
            [sel[g * ts:(g + 1) * ts] for g in range(NSA_KV) for _ in range(HPG)], axis=0).astype(BF16)
        m_scr[...] = jnp.full(m_scr.shape, NEG_INF, F32)
        l_scr[...] = jnp.zeros(l_scr.shape, F32)
        acc_scr[...] = jnp.zeros(acc_scr.shape, F32)
        s_w = lax.dot_general(q, wk_ref[0].astype(BF16), _NT, preferred_element_type=F32) + bwin_ref[...]
        _online_update(s_w, wv_ref[0].astype(BF16), m_scr, l_scr, acc_scr)
        s_n = lax.dot_general(q, _pad_rows(kwn_ref[...], LANES).astype(BF16), _NT, preferred_element_type=F32) + bnew_ref[...]
        _online_update(s_n, _pad_rows(vwn_ref[...], LANES).astype(BF16), m_scr, l_scr, acc_scr)
        l = l_scr[...]
        comb_scr[...] = comb_scr[...] + gate_col(gn, 2) * (acc_scr[...] / jnp.where(l > 0.0, l, 1.0))
        m_scr[...] = jnp.full(m_scr.shape, NEG_INF, F32)
        l_scr[...] = jnp.zeros(l_scr.shape, F32)
        acc_scr[...] = jnp.zeros(acc_scr.shape, F32)

    q = q_scr[...].astype(BF16)
    nk = p_n * page
    k = jnp.concatenate([r[0].astype(BF16) for r in k_pages], axis=0)
    v = jnp.concatenate([r[0].astype(BF16) for r in v_pages], axis=0)
    blk_row = lax.broadcasted_iota(jnp.int32, (slc_lanes, nk), 0)
    key_pos = j * nk + lax.broadcasted_iota(jnp.int32, (slc_lanes, nk), 1)
    expand = jnp.where(blk_row == (key_pos >> 6), 1.0, 0.0).astype(BF16)
    vis = jnp.dot(bm_scr[...], expand, preferred_element_type=F32)
    s = (lax.dot_general(q, k, _NT, preferred_element_type=F32) + bpast_ref[0]
         + jnp.where(vis > 0.5, 0.0, NEG_INF))
    _online_update(s, v, m_scr, l_scr, acc_scr)

    @pl.when(j == pl.num_programs(1) - 1)
    def _():
        gn = jax.nn.sigmoid(gn_ref[...])
        s_n = lax.dot_general(q, _pad_rows(ksn_ref[...], LANES).astype(BF16), _NT, preferred_element_type=F32) + bnew_ref[...]
        _online_update(s_n, _pad_rows(vsn_ref[...], LANES).astype(BF16), m_scr, l_scr, acc_scr)
        l = l_scr[...]
        comb = comb_scr[...] + gate_col(gn, 1) * (acc_scr[...] / jnp.where(l > 0.0, l, 1.0))
        _unstack_heads([comb[:half], comb[half:]], o_ref, ts)


def _nsa_sample(z, n_p, kc, vc, pool_k, pool_v, win_k, win_v, page_table, rel_table, ts):
    bs, n_pages = page_table.shape
    n_phys, page, _ = pool_k.shape
    past_len = n_pages * page
    p_n = PAGES_PER_STEP
    n_steps = n_pages // p_n
    nk = p_n * page
    n_sub = kc.shape[1]
    n_cmp = n_sub - 1
    n_slc = -(-(past_len + ts) // SLC_LEN)
    slc_lanes = -(-n_slc // LANES) * LANES
    win_len = win_k.shape[1]
    assert n_pages % p_n == 0 and ts & (ts - 1) == 0 and ts <= SLC_LEN and past_len % SLC_LEN == 0 and n_p % ts == 0
    assert n_sub % LANES == 0 and win_len + ts >= WINDOW
    tbl = rel_table[:, :NSA_HEADS].reshape(REL_BUCKETS, NSA_KV, HPG)
    rows = NSA_HEADS * ts
    true_ = lambda nkeys: jnp.ones((ts, nkeys), bool)
    cmp_end = jnp.arange(n_sub) * CMP_STRIDE + CMP_LEN - 1
    biasc = _sample_row_bias(tbl, ts, cmp_end, past_len, jnp.broadcast_to(jnp.arange(n_sub) < n_cmp, (ts, n_sub)))
    bpast = _sample_row_bias(tbl, ts, jnp.arange(past_len), past_len, true_(past_len))
    bpast = jnp.transpose(bpast.reshape(rows, n_steps, nk), (1, 0, 2))
    bnew = _sample_row_bias(tbl, ts, past_len + jnp.arange(LANES), past_len, jnp.broadcast_to(jnp.arange(LANES) < ts, (ts, LANES)))
    wpos = past_len - win_len + jnp.arange(win_len)
    in_win = ((past_len + jnp.arange(ts))[:, None] - wpos[None, :]) < WINDOW
    bwin = _sample_row_bias(tbl, ts, wpos, past_len, in_win)
    mslc = jnp.pad(_slc_matrix(n_cmp, n_slc), ((0, n_sub - n_cmp), (0, slc_lanes - n_slc))).astype(BF16)
    row_blk = n_p // ts
    zrow = lambda name, width: pl.BlockSpec((ts, width), (lambda b, j, pt, cb=_Z_OFF[name] // width: (row_blk + b, cb)))
    seq3 = lambda shape: pl.BlockSpec((1,) + shape, lambda b, j, pt: (b, 0, 0))
    const2 = lambda shape: pl.BlockSpec(shape, lambda b, j, pt: (0, 0))
    grid_spec = pltpu.PrefetchScalarGridSpec(
        num_scalar_prefetch=1,
        grid=(bs, n_steps),
        in_specs=_page_specs(p_n, (page, LANES)) + _page_specs(p_n, (page, LANES)) + [
            zrow("q_n", 512), zrow("g_n", LANES),
            seq3((n_sub, LANES)), seq3((n_sub, LANES)),
            zrow("ks", LANES), zrow("vs", LANES),
            seq3((win_len, LANES)), seq3((win_len, LANES)),
            zrow("kw", LANES), zrow("vw", LANES),
            const2((rows, n_sub)),
            pl.BlockSpec((1, rows, nk), lambda b, j, pt: (j, 0, 0)),
            const2((rows, LANES)), const2((rows, win_len)), const2((n_sub, slc_lanes)),
        ],
        out_specs=pl.BlockSpec((ts, 512), lambda b, j, pt: (b, 0)),
        scratch_shapes=[
            pltpu.VMEM((rows, LANES), F32),
            pltpu.VMEM((rows,1), F32), pltpu.VMEM((rows, 1), F32), pltpu.VMEM((rows, LANES), F32),
            pltpu.VMEM((rows, LANES), F32), pltpu.VMEM((rows, slc_lanes), BF16),
        ],
    )
    return pl.pallas_call(
        functools.partial(_nsa_sample_kernel, ts=ts, past_len=past_len, slc_lanes=slc_lanes),
        grid_spec=grid_spec,
        out_shape=jax.ShapeDtypeStruct((bs * ts, 512), F32),
        compiler_params=_params(("arbitrary", "arbitrary")),
        name="nsa_sample",
    )(page_table, *([pool_k] * p_n), *([pool_v] * p_n), z, z, kc, vc, z, z, win_k, win_v, z, z,
      biasc, bpast, bnew, bwin, mslc)


def _moba_sample_kernel(pt_ref, *refs, ts, n_past_blocks):
    del pt_ref
    p_n = PAGES_PER_STEP
    k_pages, v_pages = refs[:p_n], refs[p_n:2 * p_n]
    (zq_ref, kn_ref, vn_ref, bpast_ref, bnew_ref, o_ref, q_scr, m_all, l_all, acc_all, km_scr) = refs[2 * p_n:]
    j = pl.program_id(1)
    page = k_pages[0].shape[1]
    ppb = MOBA_BLOCK // page
    bps = p_n // ppb
    rows = MOBA_KV * HPG * ts
    half = HPG * ts
    lane = lax.broadcasted_iota(jnp.int32, (rows, LANES), 1)
    krow = lax.broadcasted_iota(jnp.int32, (LANES, LANES), 0)

    @pl.when(j == 0)
    def _():
        _stack_queries_sample(zq_ref, q_scr, ts)
        m_all[...] = jnp.full(m_all.shape, NEG_INF, F32)
        l_all[...] = jnp.zeros(l_all.shape, F32)
        km_scr[...] = jnp.zeros(km_scr.shape, F32)

    q = q_scr[...].astype(BF16)
    for i in range(bps):
        n = j * bps + i
        kf = jnp.concatenate([k_pages[i * ppb + r][0] for r in range(ppb)], axis=0)
        v = jnp.concatenate([v_pages[i * ppb + r][0].astype(BF16) for r in range(ppb)], axis=0)
        s = (lax.dot_general(q, kf.astype(BF16), _NT, preferred_element_type=F32)
             + bpast_ref[0, :, i * MOBA_BLOCK:(i + 1) * MOBA_BLOCK])
        mx, p, den = _softmax_tile(s)
        m_all[...] = jnp.where(lane == n, mx, m_all[...])
        l_all[...] = jnp.where(lane == n, den, l_all[...])
        acc_all[n] = jnp.dot(p.astype(BF16), v, preferred_element_type=F32)
        km_scr[...] = jnp.where(krow == n, jnp.mean(kf, axis=0, keepdims=True), km_scr[...])

    @pl.when(j == pl.num_programs(1) - 1)
    def _():
        s_o = lax.dot_general(q, _pad_rows(kn_ref[...], LANES).astype(BF16), _NT, preferred_element_type=F32) + bnew_ref[...]
        m_o, p_o, l_o = _softmax_tile(s_o)
        acc_o = jnp.dot(p_o.astype(BF16), _pad_rows(vn_ref[...], LANES).astype(BF16), preferred_element_type=F32)
        km_hi, km_lo = _split_hi_lo(km_scr[...])
        gs = (lax.dot_general(q, km_hi, _NT, preferred_element_type=F32)
              + lax.dot_general(q, km_lo, _NT, preferred_element_type=F32))
        gs = jnp.where(lane < n_past_blocks, gs, NEG_INF)
        sel = _take_topk(gs, min(MOBA_TOPK, n_past_blocks), lane)
        m_sel = jnp.where(sel > 0.5, m_all[...], NEG_INF)
        m_fin = jnp.maximum(jnp.max(m_sel, axis=-1, keepdims=True), m_o)
        wgt = jnp.where(sel > 0.5, jnp.exp(m_sel - m_fin), 0.0)
        w_o = jnp.exp(m_o - m_fin)
        den = jnp.sum(wgt * l_all[...], axis=-1, keepdims=True) + w_o * l_o
        num = w_o * acc_o
        for n in range(n_past_blocks):
            num = num + wgt[:, n:n + 1] * acc_all[n]
        res = num / den
        _unstack_heads([res[:half], res[half:]], o_ref, ts)


def _moba_sample(z, n_p, pool_k, pool_v, page_table, rel_table, ts):
    bs, n_pages = page_table.shape
    n_phys, page, _ = pool_k.shape
    past_len = n_pages * page
    p_n = PAGES_PER_STEP
    n_steps = n_pages // p_n
    nk = p_n * page
    nb_past = past_len // MOBA_BLOCK
    assert (MOBA_BLOCK % page == 0 and p_n % (MOBA_BLOCK // page) == 0 and past_len % MOBA_BLOCK == 0
            and ts <= MOBA_BLOCK and nb_past <= LANES and n_pages % p_n == 0 and n_p % ts == 0)
    tbl = rel_table[:, NSA_HEADS:].reshape(REL_BUCKETS, MOBA_KV, HPG)
    rows = MOBA_HEADS * ts
    bpast = _sample_row_bias(tbl, ts, jnp.arange(past_len), past_len, jnp.ones((ts, past_len), bool))
    bpast = jnp.transpose(bpast.reshape(rows, n_steps, nk), (1, 0, 2))
    bnew = _sample_row_bias(tbl, ts, past_len + jnp.arange(LANES), past_len, jnp.broadcast_to(jnp.arange(LANES) < ts, (ts, LANES)))
    row_blk = n_p // ts
    zrow = lambda name, width: pl.BlockSpec((ts, width), (lambda b, j, pt, cb=_Z_OFF[name] // width: (row_blk + b, cb)))
    grid_spec = pltpu.PrefetchScalarGridSpec(
        num_scalar_prefetch=1,
        grid=(bs, n_steps),
        in_specs=_page_specs(p_n, (page, LANES)) + _page_specs(p_n, (page, LANES)) + [
            zrow("q_m", 512), zrow("k_m", LANES), zrow("v_m", LANES),
            pl.BlockSpec((1, rows, nk), lambda b, j, pt: (j, 0, 0)),
            pl.BlockSpec((rows, LANES), lambda b, j, pt: (0, 0)),
        ],
        out_specs=pl.BlockSpec((ts, 512), lambda b, j, pt: (b, 0)),
        scratch_shapes=[
            pltpu.VMEM((rows, LANES), F32),
            pltpu.VMEM((rows,LANES), F32), pltpu.VMEM((rows, LANES), F32),
            pltpu.VMEM((nb_past, rows, LANES), F32), pltpu.VMEM((LANES, LANES), F32),
        ],
    )
    return pl.pallas_call(
        functools.partial(_moba_sample_kernel, ts=ts, n_past_blocks=nb_past),
        grid_spec=grid_spec,
        out_shape=jax.ShapeDtypeStruct((bs * ts, 512), F32),
        compiler_params=_params(("arbitrary", "arbitrary")),
        name="moba_sample",
    )(page_table, *([pool_k] * p_n), *([pool_v] * p_n), z, z, z, bpast, bnew)


def _memkv_kernel(m_ref, gain_ref, w_ref, kgain_ref, seg_ref, k_ref, v_ref):
    x = m_ref[...]
    xb = (x * lax.rsqrt(jnp.mean(x * x, axis=-1, keepdims=True) + RMS_EPS) * gain_ref[...]).astype(BF16)
    kvw = k_ref.shape[1]
    k = jnp.dot(xb, w_ref[:, :kvw], preferred_element_type=F32)
    for h in range(MEM_HEADS):
        kh = k[:, h * MEM_HEAD_DIM:(h + 1) * MEM_HEAD_DIM]
        ss = _group_sumsq(kh, seg_ref[...])
        k_ref[:, h * MEM_HEAD_DIM:(h + 1) * MEM_HEAD_DIM] = kh * lax.rsqrt(ss * (1.0 / MEM_HEAD_DIM) + RMS_EPS) * kgain_ref[...]
    v_ref[...] = jnp.dot(xb, w_ref[:, kvw:], preferred_element_type=F32)


def _memory_kv_pallas(mem, norm_gain, w_mem_kv, k_gain, tm=256):
    n, d = mem.shape
    kvw = w_mem_kv.shape[1] // 2
    assert n % tm == 0
    row = lambda i: (i, 0)
    const = lambda i: (0, 0)
    return pl.pallas_call(
        _memkv_kernel,
        grid=(n // tm,),
        in_specs=[pl.BlockSpec((tm, d), row), pl.BlockSpec((1, d), const), pl.BlockSpec((d, 2 * kvw), const),
                  pl.BlockSpec((1, MEM_HEAD_DIM), const), pl.BlockSpec((LANES, LANES), const)],
        out_specs=[pl.BlockSpec((tm, kvw), row), pl.BlockSpec((tm, kvw), row)],
        out_shape=[jax.ShapeDtypeStruct((n, kvw), F32)] * 2,
        compiler_params=_params(("arbitrary",)),
        name="memory_kv",
    )(mem, norm_gain[None, :], w_mem_kv.astype(BF16), k_gain[None, :], _seg_matrix(MEM_HEAD_DIM))


def _mem_attend_kernel(zq_ref, mk_ref, mv_ref, o_ref):
    for h in range(MEM_HEADS):
        sl = slice(h * MEM_HEAD_DIM, (h + 1) * MEM_HEAD_DIM)
        q = (zq_ref[:, sl] * (MEM_HEAD_DIM ** -0.5)).astype(BF16)
        s = lax.dot_general(q, mk_ref[0, :, sl].astype(BF16), _NT, preferred_element_type=F32)
        p = jnp.exp(s - jnp.max(s, axis=-1, keepdims=True))
        o = jnp.dot(p.astype(BF16), mv_ref[0, :, sl].astype(BF16), preferred_element_type=F32)
        o_ref[:, sl] = o / jnp.sum(p, axis=-1, keepdims=True)


def _mem_attend_pallas(z, row0, n_seq, t_seq, tq, mk, mv):
    nq = t_seq // tq
    assert t_seq % tq == 0 and row0 % tq == 0
    m_len = mk.shape[1]
    width = MEM_HEADS * MEM_HEAD_DIM
    cb = _Z_OFF["q_x"] // width
    return pl.pallas_call(
        _mem_attend_kernel,
        grid=(n_seq, nq),
        in_specs=[pl.BlockSpec((tq, width), lambda b, i: (row0 // tq + b * nq + i, cb)),
                  pl.BlockSpec((1, m_len, width), lambda b, i: (b, 0, 0)),
                  pl.BlockSpec((1, m_len, width), lambda b, i: (b, 0, 0))],
        out_specs=pl.BlockSpec((tq, width), lambda b, i: (b * nq + i, 0)),
        out_shape=jax.ShapeDtypeStruct((n_seq * t_seq, width), F32),
        compiler_params=_params(("arbitrary", "arbitrary")),
        name="mem_attend",
    )(z, mk, mv)


def _compress_dense(x_raw, pe, w1, b1, w2):
    b, tk, _ = x_raw.shape
    n_sub = tk // CMP_STRIDE
    n_cmp = n_sub - 1
    xs = x_raw[:, :n_sub * CMP_STRIDE].reshape(b, n_sub, CMP_STRIDE * LANES)
    w1r = w1.reshape(2, CMP_STRIDE, HEAD_DIM, -1)
    wbig = jnp.einsum("rlde,gh->lgdrhe", w1r, jnp.eye(2, dtype=F32)).reshape(CMP_STRIDE * LANES, 2 * LANES)
    a = jnp.dot(xs, wbig, precision=HI)
    pos = jnp.einsum("rld,rlde->e", pe.reshape(2, CMP_STRIDE, HEAD_DIM), w1r, precision=HI)
    h = a[:, :n_cmp, :LANES] + a[:, 1:, LANES:] + jnp.tile(b1 + pos, 2)
    w2big = jnp.einsum("ed,gh->gehd", w2, jnp.eye(2, dtype=F32)).reshape(LANES, LANES)
    return jnp.dot(jax.nn.gelu(h), w2big, precision=HI)


def _nsa_dense(q, glog, kc_raw, vc_raw, ks, vs, kw_all, vw_all, q_start, win_start, w):
    b, t_q, _ = q.shape
    tk = kc_raw.shape[1]
    scale = HEAD_DIM ** -0.5
    qg = q.reshape(b, t_q, NSA_KV, HPG, HEAD_DIM)
    t = q_start + jnp.arange(t_q)
    tbl = w["rel_bias_table"][:, :NSA_HEADS].reshape(REL_BUCKETS, NSA_KV, HPG)

    kc = _compress_dense(kc_raw, w["cmp_pos_emb"][0], w["cmp_w1"][0], w["cmp_b1"][0], w["cmp_w2"][0])
    n_cmp = kc.shape[1]
    kc = _rms(kc.reshape(b, n_cmp, NSA_KV, HEAD_DIM), w["nsa_k_gain"][0])
    vc = _compress_dense(vc_raw, w["cmp_pos_emb"][1], w["cmp_w1"][1], w["cmp_b1"][1], w["cmp_w2"][1])
    vc = vc.reshape(b, n_cmp, NSA_KV, HEAD_DIM)
    cmp_end = jnp.arange(n_cmp) * CMP_STRIDE + CMP_LEN - 1
    dist_c = t[:, None] - cmp_end[None, :]
    bias_c = jnp.transpose(tbl[_rel_bucket(dist_c)], (2, 3, 0, 1))
    s_c = jnp.einsum("btghd,bngd->bghtn", qg, kc, precision=HI) * scale + bias_c
    p_c = _masked_softmax(s_c, dist_c >= 0)
    o_c = jnp.einsum("bghtn,bngd->btghd", p_c, vc, precision=HI)

    imp = jnp.transpose(p_c.sum(axis=2), (0, 2, 1, 3))
    n_slc = -(-tk // SLC_LEN)
    p_slc = jnp.dot(imp, _slc_matrix(n_cmp, n_slc), precision=HI)
    blk = jnp.arange(n_slc)[None, :]
    qb = (t // SLC_LEN)[:, None]
    forced = ((blk == 0) | (blk == qb) | (blk == qb - 1)).astype(F32)
    future = blk > qb
    score = jnp.where(future[None, :, None, :], NEG_INF, p_slc + FORCE_BONUS * forced[None, :, None, :])
    k_sel = min(SLC_TOPN, n_slc)
    sel_val, sel_idx = lax.top_k(score, k_sel)
    sel_ok = sel_val > HALF_NEG
    selm = (jax.nn.one_hot(sel_idx, n_slc, dtype=F32) * sel_ok[..., None].astype(F32)).sum(axis=-2) > 0.5

    kpos = jnp.arange(tk)
    d_s = t[:, None] - kpos[None, :]
    bias_s = jnp.transpose(tbl[_rel_bucket(d_s)], (2, 3, 0, 1))
    ksg = ks.reshape(b, tk, NSA_KV, HEAD_DIM)
    vsg = vs.reshape(b, tk, NSA_KV, HEAD_DIM)
    s_s = jnp.einsum("btghd,bkgd->bghtk", qg, ksg, precision=HI) * scale + bias_s
    m_s = jnp.repeat(jnp.transpose(selm, (0, 2, 1, 3)), SLC_LEN, axis=-1)[..., :tk] & (d_s >= 0)
    p_s = _masked_softmax(s_s, m_s[:, :, None])
    o_s = jnp.einsum("bghtk,bkgd->btghd", p_s, vsg, precision=HI)

    tw = kw_all.shape[1]
    wpos = win_start + jnp.arange(tw)
    d_w = t[:, None] - wpos[None, :]
    m_w = (wpos[None, :] >= 0) & (d_w >= 0) & (d_w < WINDOW)
    bias_w = jnp.transpose(tbl[_rel_bucket(d_w)], (2, 3, 0, 1))
    kwg = kw_all.reshape(b, tw, NSA_KV, HEAD_DIM)
    vwg = vw_all.reshape(b, tw, NSA_KV, HEAD_DIM)
    s_w = jnp.einsum("btghd,bkgd->bghtk", qg, kwg, precision=HI) * scale + bias_w
    p_w = _masked_softmax(s_w, m_w)
    o_w = jnp.einsum("bghtk,bkgd->btghd", p_w, vwg, precision=HI)

    g = jax.nn.sigmoid(glog).reshape(b, t_q, 3, NSA_KV, HPG)[..., None]
    o = g[:, :, 0] * o_c + g[:, :, 1] * o_s + g[:, :, 2] * o_w
    return o.reshape(b, t_q, NSA_HEADS * HEAD_DIM)


def _moba_dense(q, k, v, q_start, rel_table):
    b, t_q, _ = q.shape
    tk = k.shape[1]
    scale = HEAD_DIM ** -0.5
    qg = q.reshape(b, t_q, MOBA_KV, HPG, HEAD_DIM)
    kg = k.reshape(b, tk, MOBA_KV, HEAD_DIM)
    vg = v.reshape(b, tk, MOBA_KV, HEAD_DIM)
    t = q_start + jnp.arange(t_q)
    tbl = rel_table[:, NSA_HEADS:].reshape(REL_BUCKETS, MOBA_KV, HPG)
    nb = -(-tk // MOBA_BLOCK)
    pad = nb * MOBA_BLOCK - tk
    kp = jnp.pad(kg, ((0, 0), (0, pad), (0, 0), (0, 0)))
    n_top = min(MOBA_TOPK, nb - 1)
    kpos = jnp.arange(tk)
    kblk = kpos // MOBA_BLOCK
    d = t[:, None] - kpos[None, :]
    m_own = (kblk[None, :] == (t // MOBA_BLOCK)[:, None]) & (d >= 0)
    mask = jnp.broadcast_to(m_own[None, None, None], (b, MOBA_KV, HPG, t_q, tk))
    if n_top > 0:
        kmean = jnp.mean(kp.reshape(b, nb, MOBA_BLOCK, MOBA_KV, HEAD_DIM), axis=2)
        gs = jnp.einsum("btghd,bngd->btghn", qg, kmean, precision=HI)
        past = jnp.arange(nb)[None, :] < (t // MOBA_BLOCK)[:, None]
        gs = jnp.where(past[None, :, None, None, :], gs, NEG_INF)
        top_val, top_idx = lax.top_k(gs, n_top)
        top_ok = top_val > HALF_NEG
        selm = (jax.nn.one_hot(top_idx, nb, dtype=F32) * top_ok[..., None].astype(F32)).sum(axis=-2) > 0.5
        mask = mask | jnp.repeat(jnp.transpose(selm, (0, 2, 3, 1, 4)), MOBA_BLOCK, axis=-1)[..., :tk]
    bias = jnp.transpose(tbl[_rel_bucket(d)], (2, 3, 0, 1))
    s = jnp.einsum("btghd,bkgd->bghtk", qg, kg, precision=HI) * scale + bias
    p = _masked_softmax(s, mask)
    o = jnp.einsum("bghtk,bkgd->btghd", p, vg, precision=HI)
    return o.reshape(b, t_q, MOBA_HEADS * HEAD_DIM)


def _mem_attend(q, mk, mv):
    b, t_q, _ = q.shape
    qh = q.reshape(b, t_q, MEM_HEADS, MEM_HEAD_DIM)
    s = jnp.einsum("bthd,bmhd->bhtm", qh, mk, precision=HI) * (MEM_HEAD_DIM ** -0.5)
    p = jax.nn.softmax(s, axis=-1)
    return jnp.einsum("bhtm,bmhd->bthd", p, mv, precision=HI).reshape(b, t_q, MEM_HEADS * MEM_HEAD_DIM)


def _memory_kv(mem, w):
    b, m, _ = mem.shape
    kv = jnp.dot(_rms(mem, w["mem_norm_gain"]), w["w_mem_kv"], precision=HI)
    k, v = jnp.split(kv, 2, axis=-1)
    k = _rms(k.reshape(b, m, MEM_HEADS, MEM_HEAD_DIM), w["mem_k_gain"])
    return k, v.reshape(b, m, MEM_HEADS, MEM_HEAD_DIM)


def _gather_pages(pool, page_table):
    g = pool[page_table]
    return g.reshape(page_table.shape[0], -1, LANES)


def _zcol(z, name, width):
    return z[..., _Z_OFF[name]:_Z_OFF[name] + width]


def kernel(x_prompt, x_sample, cache_cmp_k, cache_cmp_v, cache_slc_k, cache_slc_v, cache_moba_k, cache_moba_v, cache_win_k, cache_win_v, cache_mem_k, cache_mem_v, page_table, mem_prompt, attn_norm_gain, w_in, nsa_q_gain, nsa_k_gain, cmp_pos_emb, cmp_w1, cmp_b1, cmp_w2, moba_q_gain, moba_k_gain, mem_norm_gain, w_mem_kv, mem_q_gain, mem_k_gain, rel_bias_table, w_branch, w_out, ffn_norm_gain, peer_w_q, peer_sub_keys, peer_u, peer_v):
    assert w_in.shape[0] == 1
    l = 0
    bp, tp, d = x_prompt.shape
    bs, ts, _ = x_sample.shape
    past_len = page_table.shape[1] * cache_cmp_k.shape[2]
    n_p, n_s = bp * tp, bs * ts
    w = {
        "rel_bias_table": rel_bias_table, "cmp_pos_emb": cmp_pos_emb[l], "cmp_w1": cmp_w1[l], "cmp_b1": cmp_b1[l],
        "cmp_w2": cmp_w2[l], "nsa_k_gain": nsa_k_gain[l], "mem_norm_gain": mem_norm_gain[l], "w_mem_kv": w_mem_kv[l],
        "mem_k_gain": mem_k_gain[l],
    }

    x_all = jnp.concatenate([x_prompt.reshape(n_p, d), x_sample.reshape(n_s, d)], axis=0)
    hgain = _head_gain_row({"nsa_q": nsa_q_gain[l], "nsa_k1": nsa_k_gain[l, 1], "nsa_k2": nsa_k_gain[l, 2],
                            "moba_q": moba_q_gain[l], "moba_k": moba_k_gain[l], "mem_q": mem_q_gain[l]})
    z = _inproj(x_all, attn_norm_gain[l][None, :], _pack_w_in(w_in[l]), hgain)
    zp = z[:n_p].reshape(bp, tp, Z_COLS)
    zs = z[n_p:].reshape(bs, ts, Z_COLS)

    def rows(zz, name):
        return _zcol(zz, name, LANES)

    kc_rows_p, vc_rows_p = rows(zp, "kc"), rows(zp, "vc")
    wk = _compress_weights(cmp_pos_emb[l, 0], cmp_w1[l, 0], cmp_b1[l, 0], cmp_w2[l, 0])
    wv = _compress_weights(cmp_pos_emb[l, 1], cmp_w1[l, 1], cmp_b1[l, 1], cmp_w2[l, 1])
    sub = lambda r: r.reshape(bp, tp // CMP_STRIDE, CMP_STRIDE * LANES)
    kgain = jnp.tile(nsa_k_gain[l, 0], 2)[None, :]
    kc_p = _compress(sub(kc_rows_p), wk, kgain, True)
    vc_p = _compress(sub(vc_rows_p), wv, kgain, False)
    o_nsa_p = _nsa_prompt(z, kc_p, vc_p, rel_bias_table, bp, tp)
    o_moba_p = _moba_prompt(z, rel_bias_table, bp, tp)
    m_len = mem_prompt.shape[1]
    mk_p, mv_p = _memory_kv_pallas(mem_prompt.reshape(bp * m_len, d), mem_norm_gain[l], w_mem_kv[l], mem_k_gain[l])
    mem_w = MEM_HEADS * MEM_HEAD_DIM
    o_mem_p = _mem_attend_pallas(z, 0, bp, tp, 256, mk_p.reshape(bp, m_len, mem_w), mv_p.reshape(bp, m_len, mem_w))
    mk_p = mk_p.reshape(bp, m_len, MEM_HEADS, MEM_HEAD_DIM)
    mv_p = mv_p.reshape(bp, m_len, MEM_HEADS, MEM_HEAD_DIM)

    pool = lambda c: c[l].reshape(c.shape[1], c.shape[2], LANES)
    kc_s = _compress_paged(pool(cache_cmp_k), page_table, wk, kgain, True)
    vc_s = _compress_paged(pool(cache_cmp_v), page_table, wv, kgain, False)
    win_len = cache_win_k.shape[2]
    win_k = cache_win_k[l].reshape(bs, win_len, LANES)
    win_v = cache_win_v[l].reshape(bs, win_len, LANES)
    o_nsa_s = _nsa_sample(z, n_p, kc_s, vc_s, pool(cache_slc_k), pool(cache_slc_v), win_k, win_v, page_table,
                          rel_bias_table, ts)
    o_moba_s = _moba_sample(z, n_p, pool(cache_moba_k), pool(cache_moba_v), page_table, rel_bias_table, ts)
    o_mem_s = _mem_attend_pallas(z, n_p, bs, ts, ts, cache_mem_k[l].reshape(bs, -1, mem_w), cache_mem_v[l].reshape(bs, -1, mem_w))
    kw_all = jnp.concatenate([win_k, rows(zs, "kw")], axis=1)
    vw_all = jnp.concatenate([win_v, rows(zs, "vw")], axis=1)

    cat = lambda a, b_: jnp.concatenate([a.reshape(n_p, 512), b_.reshape(n_s, 512)], axis=0)
    x1, xn_b = _merge(cat(o_nsa_p, o_nsa_s), cat(o_moba_p, o_moba_s), cat(o_mem_p, o_mem_s), z, x_all,
                      w_branch[l], w_out[l], ffn_norm_gain[l])
    y = _peer(x1, xn_b, peer_w_q[l], peer_sub_keys[l], peer_u[l].astype(BF16), peer_v[l].astype(BF16).T)
    y_p = y[:n_p].reshape(bp, tp, d)
    y_s = y[n_p:].reshape(bs, ts, d)

    def st(r, b_, t_):
        return r.reshape(1, b_, t_, 2, HEAD_DIM)

    keep_p = min(WINDOW, tp)
    p_win_k = st(rows(zp, "kw")[:, tp - keep_p:], bp, keep_p)
    p_win_v = st(rows(zp, "vw")[:, tp - keep_p:], bp, keep_p)
    keep_s = min(WINDOW, past_len + ts)
    s_win_k = st(kw_all[:, kw_all.shape[1] - keep_s:], bs, keep_s)
    s_win_v = st(vw_all[:, vw_all.shape[1] - keep_s:], bs, keep_s)
    return (y_p, y_s,
            st(kc_rows_p, bp, tp), st(vc_rows_p, bp, tp), st(rows(zp, "ks"), bp, tp), st(rows(zp, "vs"), bp, tp),
            st(rows(zp, "k_m"), bp, tp), st(rows(zp, "v_m"), bp, tp), p_win_k, p_win_v, mk_p[None], mv_p[None],
            st(rows(zs, "kc"), bs, ts), st(rows(zs, "vc"), bs, ts), st(rows(zs, "ks"), bs, ts), st(rows(zs, "vs"), bs, ts),
            st(rows(zs, "k_m"), bs, ts), st(rows(zs, "v_m"), bs, ts), s_win_k, s_win_v)
```

```python
import functools
import math

import jax
import jax.numpy as jnp
import numpy as np
from jax import lax
from jax.experimental import pallas as pl
from jax.experimental.pallas import tpu as pltpu

HEAD_DIM = 64
NSA_HEADS = 8
NSA_KV = 2
HPG = 4
CMP_LEN = 32
CMP_STRIDE = 16
SLC_LEN = 64
SLC_TOPN = 16
WINDOW = 512
FORCE_BONUS = 1000.0
MOBA_HEADS = 8
MOBA_KV = 2
MOBA_BLOCK = 256
MOBA_TOPK = 3
MEM_HEADS = 4
MEM_HEAD_DIM = 128
REL_BUCKETS = 32
REL_MAX_DIST = 128
PEER_HEADS = 8
PEER_NKEYS = 128
PEER_TOPK = 16
RMS_EPS = 1e-6
NEG_INF = -1e30
HALF_NEG = 0.5 * NEG_INF
KNOCKED = -3.0e38
LOG2E = 1.4426950408889634
LANES = 128
VMEM_LIMIT = 56 * 1024 * 1024
ATT_TQ = 128
ATT_TK = 128

F32 = jnp.float32
BF16 = jnp.bfloat16
HI = lax.Precision.HIGHEST
_NT = (((1,), (1,)), ((), ()))


def _rms(x, g):
    return x * lax.rsqrt(jnp.mean(x * x, axis=-1, keepdims=True) + RMS_EPS) * g


def _rel_bucket(dist):
    n = jnp.maximum(dist, 0)
    exact = REL_BUCKETS // 2
    nf = jnp.maximum(n, 1).astype(F32)
    large = exact + (jnp.log(nf / exact) / math.log(REL_MAX_DIST / exact) * (REL_BUCKETS - exact)).astype(jnp.int32)
    return jnp.where(n < exact, n, jnp.minimum(large, REL_BUCKETS - 1))


def _bias_lookup(tbl, dist):
    onehot = jax.nn.one_hot(_rel_bucket(dist), REL_BUCKETS, dtype=F32)
    return jnp.tensordot(onehot, tbl, axes=1, precision=HI)


def _masked_softmax(s, mask):
    s = jnp.where(mask, s, NEG_INF)
    return jnp.where(mask, jax.nn.softmax(s, axis=-1), 0.0)


def _gelu_tanh(x):
    inner = x * (0.7978845608028654 + 0.035677408136300125 * (x * x))
    return (0.5 * x) * (1.0 + jnp.tanh(inner))


def _split_hi_lo(x):
    hi = x.astype(BF16)
    lo = (x - hi.astype(F32)).astype(BF16)
    return hi, lo


def _dot3(a, b_hi, b_lo):
    a_hi, a_lo = _split_hi_lo(a)
    return (jnp.dot(a_hi, b_hi, preferred_element_type=F32) + jnp.dot(a_lo, b_hi, preferred_element_type=F32)
            + jnp.dot(a_hi, b_lo, preferred_element_type=F32))


def _seg_matrix(hd):
    i = np.arange(LANES)
    return jnp.asarray((i[:, None] // hd) == (i[None, :] // hd), BF16)


def _group_sumsq(z, seg):
    sq_hi, sq_lo = _split_hi_lo(z * z)
    return jnp.dot(sq_hi, seg, preferred_element_type=F32) + jnp.dot(sq_lo, seg, preferred_element_type=F32)


def _params(sem):
    return pltpu.CompilerParams(dimension_semantics=sem, vmem_limit_bytes=VMEM_LIMIT)


_Z_GROUPS = (
    ("g_merge", 3072, 0), ("q_n", 512, 64), ("q_m", 512, 64), ("q_x", 512, 128),
    ("kc", 128, 0), ("vc", 128, 0), ("ks", 128, 64), ("vs", 128, 0), ("kw", 128, 64), ("vw", 128, 0),
    ("k_m", 128, 64), ("v_m", 128, 0), ("g_n", 128, 0),
)
_Z_OFF = {}
_o = 0
for _n, _w, _h in _Z_GROUPS:
    _Z_OFF[_n] = _o
    _o += _w
Z_COLS = _o
IN_SPLITS = (512, 128, 128, 128, 128, 128, 128, 24, 512, 128, 128, 512, 3072)
_SRC_NAMES = ("q_n", "kc", "vc", "ks", "vs", "kw", "vw", "g_n", "q_m", "k_m", "v_m", "q_x", "g_merge")


def _pack_w_in(w_in):
    cuts = np.cumsum((0,) + IN_SPLITS)
    parts = {n: w_in[:, cuts[i]:cuts[i + 1]] for i, n in enumerate(_SRC_NAMES)}
    parts["g_n"] = jnp.pad(parts["g_n"], ((0, 0), (0, LANES - 24)))
    return jnp.concatenate([parts[n] for n, _, _ in _Z_GROUPS], axis=1).astype(BF16)


def _head_gain_row(gains):
    g = {
        "q_n": jnp.tile(gains["nsa_q"], 8), "ks": jnp.tile(gains["nsa_k1"], 2), "kw": jnp.tile(gains["nsa_k2"], 2),
        "q_m": jnp.tile(gains["moba_q"], 8), "k_m": jnp.tile(gains["moba_k"], 2), "q_x": jnp.tile(gains["mem_q"], 4),
    }
    return jnp.concatenate([g.get(n, jnp.ones((w,), F32)) for n, w, _ in _Z_GROUPS])[None, :]


def _inproj_kernel(x_ref, gain_ref, w_ref, hgain_ref, seg64_ref, seg128_ref, z_ref):
    x = x_ref[...]
    ms = jnp.mean(x * x, axis=-1, keepdims=True)
    xb = (x * lax.rsqrt(ms + RMS_EPS) * gain_ref[...]).astype(BF16)
    for name, width, hd in _Z_GROUPS:
        off = _Z_OFF[name]
        for c in range(0, width, 512):
            cw = min(512, width - c)
            lo_c, hi_c = off + c, off + c + cw
            z = jnp.dot(xb, w_ref[:, lo_c:hi_c], preferred_element_type=F32)
            if hd:
                seg = seg64_ref[...] if hd == 64 else seg128_ref[...]
                for j in range(0, cw, LANES):
                    zj = z[:, j:j + LANES]
                    ss = _group_sumsq(zj, seg)
                    z_ref[:, lo_c + j:lo_c + j + LANES] = (
                        zj * lax.rsqrt(ss * (1.0 / hd) + RMS_EPS) * hgain_ref[:, lo_c + j:lo_c + j + LANES])
            else:
                z_ref[:, lo_c:hi_c] = z


def _inproj(x, gain, w_packed, hgain, tm=256):
    n, d = x.shape
    assert n % tm == 0
    const = lambda i: (0, 0)
    return pl.pallas_call(
        _inproj_kernel,
        grid=(n // tm,),
        in_specs=[
            pl.BlockSpec((tm, d), lambda i: (i, 0)),
            pl.BlockSpec((1, d), const),
            pl.BlockSpec((d, Z_COLS), const),
            pl.BlockSpec((1, Z_COLS), const),
            pl.BlockSpec((LANES, LANES), const),
            pl.BlockSpec((LANES, LANES), const),
        ],
        out_specs=pl.BlockSpec((tm, Z_COLS), lambda i: (i, 0)),
        out_shape=jax.ShapeDtypeStruct((n, Z_COLS), F32),
        compiler_params=_params(("arbitrary",)),
        name="inproj",
    )(x, gain, w_packed, hgain, _seg_matrix(64), _seg_matrix(128))


def _compress_weights(pe, w1, b1, w2):
    w1r = w1.reshape(2, CMP_STRIDE, HEAD_DIM, -1)
    eye = jnp.eye(2, dtype=F32)
    wbig = jnp.einsum("rlde,gh->lgdrhe", w1r, eye).reshape(CMP_STRIDE * LANES, 2 * LANES)
    pos = jnp.einsum("rld,rlde->e", pe.reshape(2, CMP_STRIDE, HEAD_DIM), w1r, precision=HI)
    cvec = jnp.tile(b1 + pos, 2)[None, :]
    w2big = jnp.einsum("ed,gh->gehd", w2, eye).reshape(LANES, LANES)
    return _split_hi_lo(wbig) + (cvec,) + _split_hi_lo(w2big)


def _compress_kernel(x_ref, wh_ref, wl_ref, c_ref, w2h_ref, w2l_ref, seg_ref, gain_ref, o_ref, *, norm):
    ns = x_ref.shape[1]
    a = _dot3(x_ref[0], wh_ref[...], wl_ref[...])
    nxt = pltpu.roll(a[:, LANES:], ns - 1, axis=0)
    h = a[:, :LANES] + nxt + c_ref[...]
    o = _dot3(_gelu_tanh(h), w2h_ref[...], w2l_ref[...])
    if norm:
        o = o * lax.rsqrt(_group_sumsq(o, seg_ref[...]) * (1.0 / HEAD_DIM) + RMS_EPS) * gain_ref[...]
    o_ref[0] = o


def _compress(x_sub, weights, gain, norm):
    b, ns, kk = x_sub.shape
    wh, wl, cvec, w2h, w2l = weights
    const = lambda i: (0, 0)
    return pl.pallas_call(
        functools.partial(_compress_kernel, norm=norm),
        grid=(b,),
        in_specs=[
            pl.BlockSpec((1, ns, kk), lambda i: (i, 0, 0)),
            pl.BlockSpec((kk, 2 * LANES), const), pl.BlockSpec((kk, 2 * LANES), const),
            pl.BlockSpec((1, LANES), const),
            pl.BlockSpec((LANES, LANES), const), pl.BlockSpec((LANES, LANES), const),
            pl.BlockSpec((LANES, LANES), const), pl.BlockSpec((1, LANES), const),
        ],
        out_specs=pl.BlockSpec((1, ns, LANES), lambda i: (i, 0, 0)),
        out_shape=jax.ShapeDtypeStruct((b, ns, LANES), F32),
        compiler_params=_params(("arbitrary",)),
        name="nsa_compress",
    )(x_sub, wh, wl, cvec, w2h, w2l, _seg_matrix(HEAD_DIM), gain)


def _toeplitz_tiles(tbl, deltas, tq, tk, hi):
    i = jnp.arange(tq)[:, None]
    j = jnp.arange(tk)[None, :]
    tiles = []
    for dl in deltas:
        d = dl + i - j
        b = _bias_lookup(tbl, d)
        b = jnp.where(((d >= 0) & (d < hi))[..., None, None], b, NEG_INF)
        tiles.append(jnp.transpose(b, (2, 3, 0, 1)).reshape(tbl.shape[1], HPG * tq, tk))
    return jnp.stack(tiles, axis=1)


def _stack_queries(zq_ref, write, tq):
    lane = lax.broadcasted_iota(jnp.int32, (tq, LANES), 1)
    for g in range(2):
        keep = (lane < HEAD_DIM) if g == 0 else (lane >= HEAD_DIM)
        for h in range(HPG):
            hd = HPG * g + h
            chunk = zq_ref[:, LANES * (hd // 2):LANES * (hd // 2 + 1)] * (HEAD_DIM ** -0.5)
            if hd % 2 != g:
                chunk = pltpu.roll(chunk, HEAD_DIM, axis=1)
            write(g, h, jnp.where(keep, chunk, 0.0))


def _stack_queries_prompt(zq_ref, q_scr, tq):
    def write(g, h, val):
        q_scr[g, h * tq:(h + 1) * tq, :] = val.astype(BF16)
    _stack_queries(zq_ref, write, tq)


def _stack_queries_sample(zq_ref, q_scr, ts):
    def write(g, h, val):
        q_scr[(g * HPG + h) * ts:(g * HPG + h + 1) * ts, :] = val
    _stack_queries(zq_ref, write, ts)


def _unstack_heads(parts, o_ref, tq):
    lane = lax.broadcasted_iota(jnp.int32, (tq, LANES), 1)
    for j in range(4):
        g = j // 2
        he = (2 * j) % HPG
        left = parts[g][he * tq:(he + 1) * tq, :]
        right = parts[g][(he + 1) * tq:(he + 2) * tq, :]
        if g == 0:
            right = pltpu.roll(right, HEAD_DIM, axis=1)
        else:
            left = pltpu.roll(left, HEAD_DIM, axis=1)
        o_ref[:, LANES * j:LANES * (j + 1)] = jnp.where(lane < HEAD_DIM, left, right)


def _flash(q, k_ref, v_ref, bias_ref, g, n_bias, bm, blk_shift, rep, kt_lo, kt_hi, qt, m_scr, l_scr, acc_scr, tk):
    m_scr[...] = jnp.full(m_scr.shape, NEG_INF, F32)
    l_scr[...] = jnp.zeros(l_scr.shape, F32)
    acc_scr[...] = jnp.zeros(acc_scr.shape, F32)
    blk_row = lax.broadcasted_iota(jnp.int32, (LANES, tk), 0)
    key_lane = lax.broadcasted_iota(jnp.int32, (LANES, tk), 1)

    def body(kt, carry):
        k0 = pl.multiple_of(kt * tk, tk)
        k = k_ref[pl.ds(k0, tk), :].astype(BF16)
        v = v_ref[pl.ds(k0, tk), :].astype(BF16)
        s = lax.dot_general(q, k, _NT, preferred_element_type=F32)
        s = s + bias_ref[g, jnp.minimum(qt - kt, n_bias - 1)]
        if bm is not None:
            expand = jnp.where(blk_row == ((k0 + key_lane) >> blk_shift), 1.0, 0.0).astype(BF16)
            vis = jnp.dot(bm, expand, preferred_element_type=F32)
            add = jnp.where(vis > 0.5, 0.0, NEG_INF)
            if rep > 1:
                add = jnp.concatenate([add] * rep, axis=0)
            s = s + add
        m_old = m_scr[...]
        m_new = jnp.maximum(m_old, jnp.max(s, axis=-1, keepdims=True))
        p = jnp.where(s > HALF_NEG, jnp.exp(s - m_new), 0.0)
        alpha = jnp.exp(m_old - m_new)
        l_scr[...] = alpha * l_scr[...] + jnp.sum(p, axis=-1, keepdims=True)
        acc_scr[...] = alpha * acc_scr[...] + jnp.dot(p.astype(BF16), v, preferred_element_type=F32)
        m_scr[...] = m_new
        return carry

    lax.fori_loop(kt_lo, kt_hi, body, 0)
    l = l_scr[...]
    return acc_scr[...] / jnp.where(l > 0.0, l, 1.0)


def _take_topk(score, count, lane):
    sel = jnp.zeros(score.shape, F32)
    cur = score
    lane_f = lane.astype(F32)
    for _ in range(count):
        mx = jnp.max(cur, axis=-1, keepdims=True)
        idx = jnp.min(jnp.where(cur == mx, lane_f, float(score.shape[-1])), axis=-1, keepdims=True)
        hit = lane_f == idx
        sel = jnp.where(hit, jnp.where(mx > HALF_NEG, 1.0, 0.0), sel)
        cur = jnp.where(hit, KNOCKED, cur)
    return sel


def _nsa_prompt_kernel(zq_ref, gn_ref, kc_ref, vc_ref, ks_ref, vs_ref, kw_ref, vw_ref, biasc_ref, bslc_ref, bwin_ref,
                       mslc_ref, o_ref, q_scr, m_scr, l_scr, acc_scr, comb_scr, *, tq, tk, n_slc_tiles, n_win_tiles):
    qt = pl.program_id(1)
    q0 = qt * tq
    _stack_queries_prompt(zq_ref, q_scr, tq)
    gn = jax.nn.sigmoid(gn_ref[...])
    lane = lax.broadcasted_iota(jnp.int32, (tq, LANES), 1)
    tpos = q0 + lax.broadcasted_iota(jnp.int32, (tq, LANES), 0)
    qb = tpos >> 6

    def gate_col(c, g):
        return jnp.concatenate([gn[:, c * 8 + g * HPG + h:c * 8 + g * HPG + h + 1] for h in range(HPG)], axis=0)

    kc = kc_ref[0].astype(BF16)
    vc = vc_ref[0].astype(BF16)
    for g in range(2):
        q = q_scr[g]
        s_c = lax.dot_general(q, kc, _NT, preferred_element_type=F32) + biasc_ref[g]
        mx = jnp.max(s_c, axis=-1, keepdims=True)
        p = jnp.where(s_c > HALF_NEG, jnp.exp(s_c - mx), 0.0)
        den = jnp.sum(p, axis=-1, keepdims=True)
        p_c = p / jnp.where(den > 0.0, den, 1.0)
        comb_scr[g] = gate_col(0, g) * jnp.dot(p_c.astype(BF16), vc, preferred_element_type=F32)
        imp = p_c[0:tq] + p_c[tq:2 * tq] + p_c[2 * tq:3 * tq] + p_c[3 * tq:4 * tq]
        imp_hi, imp_lo = _split_hi_lo(imp)
        p_slc = (jnp.dot(imp_hi, mslc_ref[...], preferred_element_type=F32)
                 + jnp.dot(imp_lo, mslc_ref[...], preferred_element_type=F32))
        forced = (lane == 0) | (lane == qb) | (lane == qb - 1)
        score = jnp.where(lane > qb, NEG_INF, p_slc + jnp.where(forced, FORCE_BONUS, 0.0))
        bm = _take_topk(score, SLC_TOPN, lane).astype(BF16)
        o_s = _flash(q, ks_ref, vs_ref, bslc_ref, g, n_slc_tiles, bm, 6, HPG, 0, qt + 1, qt,
                     m_scr, l_scr, acc_scr, tk)
        comb_scr[g] = comb_scr[g] + gate_col(1, g) * o_s
        kt_lo = jnp.maximum(qt - (n_win_tiles - 1), 0)
        o_w = _flash(q, kw_ref, vw_ref, bwin_ref, g, n_win_tiles, None, 0, 1, kt_lo, qt + 1, qt,
                     m_scr, l_scr, acc_scr, tk)
        comb_scr[g] = comb_scr[g] + gate_col(2, g) * o_w
    _unstack_heads([comb_scr[0], comb_scr[1]], o_ref, tq)


def _zspec(rows, name, width, row_map):
    cb = _Z_OFF[name] // width
    assert _Z_OFF[name] % width == 0
    return pl.BlockSpec((rows, width), lambda b, i: (row_map(b, i), cb))


def _nsa_prompt(z, kc, vc, rel_table, bp, tp):
    tq, tk = ATT_TQ, ATT_TK
    nq = tp // tq
    n_cmp = tp // CMP_STRIDE - 1
    n_slc = tp // SLC_LEN
    assert tp % tq == 0 and n_cmp < LANES and n_slc <= LANES and kc.shape[1] == LANES
    tbl = rel_table[:, :NSA_HEADS].reshape(REL_BUCKETS, NSA_KV, HPG)
    t = jnp.arange(tp)
    dist_c = t[:, None] - (jnp.arange(LANES) * CMP_STRIDE + CMP_LEN - 1)[None, :]
    valid_c = (dist_c >= 0) & (jnp.arange(LANES) < n_cmp)[None, :]
    biasc = jnp.where(valid_c[..., None, None], _bias_lookup(tbl, dist_c), NEG_INF)
    biasc = jnp.transpose(biasc.reshape(nq, tq, LANES, NSA_KV, HPG), (0, 3, 4, 1, 2)).reshape(nq, NSA_KV, HPG * tq, LANES)
    slc_deltas = (0, tq, 2 * tq)
    win_deltas = tuple(range(0, WINDOW + 1, tq))
    bslc = _toeplitz_tiles(tbl, slc_deltas, tq, tk, 1 << 30)
    bwin = _toeplitz_tiles(tbl, win_deltas, tq, tk, WINDOW)
    mslc = jnp.pad(_slc_matrix(n_cmp, n_slc), ((0, LANES - n_cmp), (0, LANES - n_slc))).astype(BF16)
    kern = functools.partial(_nsa_prompt_kernel, tq=tq, tk=tk, n_slc_tiles=len(slc_deltas), n_win_tiles=len(win_deltas))
    seq = lambda b, i: b
    tile = lambda b, i: b * nq + i
    full4 = lambda b, i: (0, 0, 0, 0)
    return pl.pallas_call(
        kern,
        grid=(bp, nq),
        in_specs=[
            _zspec(tq, "q_n", 512, tile), _zspec(tq, "g_n", LANES, tile),
            pl.BlockSpec((1, LANES, LANES), lambda b, i: (b, 0, 0)), pl.BlockSpec((1, LANES, LANES), lambda b, i: (b, 0, 0)),
        ] + [pl.BlockSpec((tp, LANES), (lambda b, i, cb=_Z_OFF[nm] // LANES: (b, cb))) for nm in ("ks", "vs", "kw", "vw")] + [
            pl.BlockSpec((None, NSA_KV, HPG * tq, LANES), lambda b, i: (i, 0, 0, 0)),
            pl.BlockSpec(bslc.shape, full4), pl.BlockSpec(bwin.shape, full4),
            pl.BlockSpec((LANES, LANES), lambda b, i: (0, 0)),
        ],
        out_specs=pl.BlockSpec((tq, 512), lambda b, i: (b * nq + i, 0)),
        out_shape=jax.ShapeDtypeStruct((bp * tp, 512), F32),
        scratch_shapes=[
            pltpu.VMEM((NSA_KV, HPG * tq, LANES), BF16),
            pltpu.VMEM((HPG * tq, 1), F32), pltpu.VMEM((HPG * tq, 1), F32), pltpu.VMEM((HPG * tq, LANES), F32),
            pltpu.VMEM((NSA_KV, HPG * tq, LANES), F32),
        ],
        compiler_params=_params(("arbitrary", "arbitrary")),
        name="nsa_prompt",
    )(z, z, kc, vc, z, z, z, z, biasc, bslc, bwin, mslc)


def _slc_matrix(n_cmp, n_slc):
    ratio, c_sub = SLC_LEN // CMP_STRIDE, CMP_LEN // CMP_STRIDE
    m = np.zeros((n_cmp, n_slc), np.float32)
    for j in range(n_slc):
        for o in range(-(c_sub - 1), ratio):
            k = ratio * j + o
            if 0 <= k < n_cmp:
                m[k, j] += float(min(o + c_sub, ratio) - max(o, 0))
    return jnp.asarray(m)


def _moba_prompt_kernel(zq_ref, k_ref, v_ref, bias_ref, o_ref, q_scr, m_scr, l_scr, acc_scr, res_scr, *, tq, tk, nb, n_tiles):
    qt = pl.program_id(1)
    q0 = qt * tq
    _stack_queries_prompt(zq_ref, q_scr, tq)
    rows = HPG * tq
    lane = lax.broadcasted_iota(jnp.int32, (rows, LANES), 1)
    trow = q0 + (lax.broadcasted_iota(jnp.int32, (rows, LANES), 0) & (tq - 1))
    own = trow >> 8
    kmean = jnp.concatenate(
        [jnp.mean(k_ref[n * MOBA_BLOCK:(n + 1) * MOBA_BLOCK, :], axis=0, keepdims=True) for n in range(nb)]
        + [jnp.zeros((LANES - nb, LANES), F32)], axis=0)
    km_hi, km_lo = _split_hi_lo(kmean)
    for g in range(2):
        q = q_scr[g]
        gs = (lax.dot_general(q, km_hi, _NT, preferred_element_type=F32)
              + lax.dot_general(q, km_lo, _NT, preferred_element_type=F32))
        gs = jnp.where(lane < own, gs, NEG_INF)
        sel = _take_topk(gs, min(MOBA_TOPK, nb - 1), lane)
        bm = jnp.where(lane == own, 1.0, sel).astype(BF16)
        res_scr[g] = _flash(q, k_ref, v_ref, bias_ref, g, n_tiles, bm, 8, 1, 0, qt + 1, qt, m_scr, l_scr, acc_scr, tk)
    _unstack_heads([res_scr[0], res_scr[1]], o_ref, tq)


def _moba_prompt(z, rel_table, bp, tp):
    tq, tk = ATT_TQ, ATT_TK
    nq = tp // tq
    nb = tp // MOBA_BLOCK
    assert tp % MOBA_BLOCK == 0 and tq & (tq - 1) == 0
    tbl = rel_table[:, NSA_HEADS:].reshape(REL_BUCKETS, MOBA_KV, HPG)
    deltas = (0, tq, 2 * tq)
    bias = _toeplitz_tiles(tbl, deltas, tq, tk, 1 << 30)
    kern = functools.partial(_moba_prompt_kernel, tq=tq, tk=tk, nb=nb, n_tiles=len(deltas))
    return pl.pallas_call(
        kern,
        grid=(bp, nq),
        in_specs=[
            _zspec(tq, "q_m", 512, lambda b, i: b * nq + i),
            pl.BlockSpec((tp, LANES), lambda b, i: (b, _Z_OFF["k_m"] // LANES)),
            pl.BlockSpec((tp, LANES), lambda b, i: (b, _Z_OFF["v_m"] // LANES)),
            pl.BlockSpec(bias.shape, lambda b, i: (0, 0, 0, 0)),
        ],
        out_specs=pl.BlockSpec((tq, 512), lambda b, i: (b * nq + i, 0)),
        out_shape=jax.ShapeDtypeStruct((bp * tp, 512), F32),
        scratch_shapes=[
            pltpu.VMEM((MOBA_KV, HPG * tq, LANES), BF16),
            pltpu.VMEM((HPG * tq, 1), F32), pltpu.VMEM((HPG * tq, 1), F32), pltpu.VMEM((HPG * tq, LANES), F32),
            pltpu.VMEM((MOBA_KV, HPG * tq, LANES), F32),
        ],
        compiler_params=_params(("arbitrary", "arbitrary")),
        name="moba_prompt",
    )(z, z, z, bias)


def _merge_kernel(on_ref, om_ref, ox_ref, gm_ref, x_ref, wb_ref, wo_ref, fg_ref, x1_ref, xn_ref):
    d = x_ref.shape[1]
    merged = jnp.zeros(x_ref.shape, F32)
    for c, o_ref in enumerate((on_ref, om_ref, ox_ref)):
        proj = jnp.dot(o_ref[...].astype(BF16), wb_ref[c], preferred_element_type=F32)
        merged = merged + jax.nn.sigmoid(gm_ref[:, c * d:(c + 1) * d]) * proj
    x1 = x_ref[...] + jnp.dot(merged.astype(BF16), wo_ref[...], preferred_element_type=F32)
    x1_ref[...] = x1
    ms = jnp.mean(x1 * x1, axis=-1, keepdims=True)
    xn_ref[...] = (x1 * lax.rsqrt(ms + RMS_EPS) * fg_ref[...]).astype(BF16)


def _merge(o_nsa, o_moba, o_mem, z, x, w_branch, w_out, ffn_gain, tm=256):
    n, d = x.shape
    assert n % tm == 0 and _Z_OFF["g_merge"] == 0
    row = lambda i: (i, 0)
    return pl.pallas_call(
        _merge_kernel,
        grid=(n // tm,),
        in_specs=[
            pl.BlockSpec((tm, 512), row), pl.BlockSpec((tm, 512), row), pl.BlockSpec((tm, 512), row),
            pl.BlockSpec((tm, 3 * d), row), pl.BlockSpec((tm, d), row),
            pl.BlockSpec((3, 512, d), lambda i: (0, 0, 0)), pl.BlockSpec((d, d), lambda i: (0, 0)),
            pl.BlockSpec((1, d), lambda i: (0, 0)),
        ],
        out_specs=[pl.BlockSpec((tm, d), row), pl.BlockSpec((tm, d), row)],
        out_shape=[jax.ShapeDtypeStruct((n, d), F32), jax.ShapeDtypeStruct((n, d), BF16)],
        compiler_params=_params(("arbitrary",)),
        name="merge_outproj",
    )(o_nsa, o_moba, o_mem, z, x, w_branch.astype(BF16), w_out.astype(BF16), ffn_gain[None, :])


def _peer_route_kernel(xT_ref, wqh_ref, wql_ref, skh_ref, skl_ref, s1_ref, s2_ref, tau_ref):
    xT = xT_ref[...]
    qT = jnp.dot(wqh_ref[...], xT, preferred_element_type=F32) + jnp.dot(wql_ref[...], xT, preferred_element_type=F32)
    half = HEAD_DIM

    def top_sorted(s):
        vals, cur = [], s
        for _ in range(PEER_TOPK + 1):
            mx = jnp.max(cur, axis=0, keepdims=True)
            vals.append(mx)
            cur = jnp.where(cur == mx, KNOCKED, cur)
        return vals

    taus = []
    for h in range(PEER_HEADS):
        sc = []
        for p in range(2):
            r0 = (h * 2 + p) * half
            q_hi, q_lo = _split_hi_lo(qT[r0:r0 + half, :])
            sc.append(jnp.dot(skh_ref[h, p], q_hi, preferred_element_type=F32)
                      + jnp.dot(skl_ref[h, p], q_hi, preferred_element_type=F32)
                      + jnp.dot(skh_ref[h, p], q_lo, preferred_element_type=F32))
        a, b = top_sorted(sc[0]), top_sorted(sc[1])
        n_rank = PEER_TOPK + 1
        cand = jnp.concatenate([a[i] + b[j] for i in range(n_rank) for j in range(n_rank // (i + 1))], axis=0)
        cur = cand
        for _ in range(PEER_TOPK):
            tau = jnp.max(cur, axis=0, keepdims=True)
            cur = jnp.where(cur == tau, KNOCKED, cur)
        nxt = jnp.max(cur, axis=0, keepdims=True)
        top = a[0] + b[0]
        zsum = jnp.sum(jnp.where(cand >= tau, jnp.exp(cand - top), 0.0), axis=0, keepdims=True)
        shift = top + jnp.log(zsum)
        s1_ref[h] = (sc[0] - shift) * LOG2E
        s2_ref[h] = sc[1] * LOG2E
        taus.append((0.5 * (tau + nxt) - shift) * LOG2E)
    tau_ref[...] = jnp.concatenate(taus, axis=0)


def _peer_route(xT, peer_w_q, sub_keys, tt=256):
    d, n = xT.shape
    assert n % tt == 0
    wqh, wql = _split_hi_lo(peer_w_q.T)
    skh, skl = _split_hi_lo(sub_keys)
    tok3 = lambda i: (0, 0, i)
    big = jax.ShapeDtypeStruct((PEER_HEADS, PEER_NKEYS, n), F32)
    return pl.pallas_call(
        _peer_route_kernel,
        grid=(n // tt,),
        in_specs=[
            pl.BlockSpec((d, tt), lambda i: (0, i)),
            pl.BlockSpec((d, d), lambda i: (0, 0)), pl.BlockSpec((d, d), lambda i: (0, 0)),
            pl.BlockSpec(sub_keys.shape, lambda i: (0, 0, 0, 0)), pl.BlockSpec(sub_keys.shape, lambda i: (0, 0, 0, 0)),
        ],
        out_specs=[pl.BlockSpec((PEER_HEADS, PEER_NKEYS, tt), tok3)] * 2 + [pl.BlockSpec((PEER_HEADS, tt), lambda i: (0, i))],
        out_shape=[big] * 2 + [jax.ShapeDtypeStruct((PEER_HEADS, n), F32)],
        compiler_params=_params(("arbitrary",)),
        name="peer_route",
    )(xT, wqh, wql, skh, skl)


def _peer_dense_kernel(xT_ref, u_ref, vT_ref, s1_ref, s2_ref, tau_ref, o_ref, a_ref, *, ec, tt):
    c = pl.program_id(1)

    @pl.when(c == 0)
    def _():
        o_ref[...] = jnp.zeros_like(o_ref)

    h_t = jnp.dot(u_ref[...], xT_ref[...], preferred_element_type=F32)
    for k in range(ec // LANES):
        for lc in range(0, tt, LANES):
            wk = jnp.zeros((LANES, LANES), F32)
            for h in range(PEER_HEADS):
                csum = s2_ref[h, :, lc:lc + LANES] + s1_ref[h, k:k + 1, lc:lc + LANES]
                wk = wk + jnp.where(csum >= tau_ref[h:h + 1, lc:lc + LANES], jnp.exp2(csum), 0.0)
            g = _gelu_tanh(h_t[k * LANES:(k + 1) * LANES, lc:lc + LANES])
            a_ref[k * LANES:(k + 1) * LANES, lc:lc + LANES] = (wk * g).astype(BF16)
    o_ref[...] += jnp.dot(vT_ref[...], a_ref[...], preferred_element_type=F32)


def _peer_dense(xT, u_b, vT_b, s1T, s2T, tauT, ec=1024):
    d, n = xT.shape
    tt = 512 if n % 512 == 0 else 256
    assert n % tt == 0 and ec == 8 * PEER_NKEYS
    ne = u_b.shape[0]
    kern = functools.partial(_peer_dense_kernel, ec=ec, tt=tt)
    allk = pl.BlockSpec((PEER_HEADS, PEER_NKEYS, tt), lambda i, c: (0, 0, i))
    chunk = pl.BlockSpec((PEER_HEADS, ec // PEER_NKEYS, tt), lambda i, c: (0, c, i))
    return pl.pallas_call(
        kern,
        grid=(n // tt, ne // ec),
        in_specs=[
            pl.BlockSpec((d, tt), lambda i, c: (0, i)),
            pl.BlockSpec((ec, d), lambda i, c: (c, 0)),
            pl.BlockSpec((d, ec), lambda i, c: (0, c)),
            chunk, allk,
            pl.BlockSpec((PEER_HEADS, tt), lambda i, c: (0, i)),
        ],
        out_specs=pl.BlockSpec((d, tt), lambda i, c: (0, i)),
        out_shape=jax.ShapeDtypeStruct((d, n), F32),
        scratch_shapes=[pltpu.VMEM((ec, tt), BF16)],
        compiler_params=_params(("arbitrary", "arbitrary")),
        name="peer_dense",
    )(xT, u_b, vT_b, s1T, s2T, tauT)


def _peer(x1, xn_b, peer_w_q, sub_keys, u_b, vT_b):
    xT = xn_b.T
    s1T, s2T, tauT = _peer_route(xT, peer_w_q, sub_keys)
    out_t = _peer_dense(xT, u_b, vT_b, s1T, s2T, tauT)
    return x1 + out_t.T


PAGES_PER_STEP = 16


def _page_specs(n_lead, inner):
    zeros = (0,) * len(inner)
    return [pl.BlockSpec((1,) + inner, (lambda b, j, pt, p=p: (pt[b, j * PAGES_PER_STEP + p],) + zeros))
            for p in range(n_lead)]


def _compress_paged_kernel(pt_ref, *refs, norm):
    del pt_ref
    p_n = PAGES_PER_STEP
    x_refs = refs[:p_n]
    wh_ref, wl_ref, c_ref, w2h_ref, w2l_ref, seg_ref, gain_ref, o_ref, x_scr = refs[p_n:]
    j = pl.program_id(1)
    sub = x_refs[0].shape[1]
    for p in range(p_n):
        x_scr[pl.ds(pl.multiple_of((j * p_n + p) * sub, sub), sub), :] = x_refs[p][0]

    @pl.when(j == pl.num_programs(1) - 1)
    def _():
        ns = x_scr.shape[0]
        a = jnp.dot(x_scr[...].astype(BF16), wh_ref[...], preferred_element_type=F32)
        nxt = pltpu.roll(a[:, LANES:], ns - 1, axis=0)
        h = a[:, :LANES] + nxt + c_ref[...]
        o = _dot3(_gelu_tanh(h), w2h_ref[...], w2l_ref[...])
        if norm:
            o = o * lax.rsqrt(_group_sumsq(o, seg_ref[...]) * (1.0 / HEAD_DIM) + RMS_EPS) * gain_ref[...]
        o_ref[0] = o


def _compress_paged(pool, page_table, weights, gain, norm):
    n_phys, page, _ = pool.shape
    b, n_pages = page_table.shape
    sub = page // CMP_STRIDE
    kk = CMP_STRIDE * LANES
    ns = n_pages * sub
    assert n_pages % PAGES_PER_STEP == 0 and page % CMP_STRIDE == 0
    wh, wl, cvec, w2h, w2l = weights
    const = lambda b_, j, pt: (0, 0)
    pool_sub = pool.reshape(n_phys, sub, kk)
    grid_spec = pltpu.PrefetchScalarGridSpec(
        num_scalar_prefetch=1,
        grid=(b, n_pages // PAGES_PER_STEP),
        in_specs=_page_specs(PAGES_PER_STEP, (sub, kk)) + [
            pl.BlockSpec((kk, 2 * LANES), const), pl.BlockSpec((kk, 2 * LANES), const),
            pl.BlockSpec((1, LANES), const),
            pl.BlockSpec((LANES, LANES), const), pl.BlockSpec((LANES, LANES), const),
            pl.BlockSpec((LANES, LANES), const), pl.BlockSpec((1, LANES), const),
        ],
        out_specs=pl.BlockSpec((1, ns, LANES), lambda b_, j, pt: (b_, 0, 0)),
        scratch_shapes=[pltpu.VMEM((ns, kk), F32)],
    )
    return pl.pallas_call(
        functools.partial(_compress_paged_kernel, norm=norm),
        grid_spec=grid_spec,
        out_shape=jax.ShapeDtypeStruct((b, ns, LANES), F32),
        compiler_params=_params(("arbitrary", "arbitrary")),
        name="nsa_compress_paged",
    )(page_table, *([pool_sub] * PAGES_PER_STEP), wh, wl, cvec, w2h, w2l, _seg_matrix(HEAD_DIM), gain)


def _softmax_tile(s):
    mx = jnp.max(s, axis=-1, keepdims=True)
    p = jnp.where(s > HALF_NEG, jnp.exp(s - mx), 0.0)
    return mx, p, jnp.sum(p, axis=-1, keepdims=True)


def _pv(p, v, v_t):
    if v_t:
        return lax.dot_general(p.astype(BF16), v, _NT, preferred_element_type=F32)
    return jnp.dot(p.astype(BF16), v, preferred_element_type=F32)


def _online_update(s, v, m_scr, l_scr, acc_scr, v_t=False):
    m_old = m_scr[...]
    m_new = jnp.maximum(m_old, jnp.max(s, axis=-1, keepdims=True))
    p = jnp.where(s > HALF_NEG, jnp.exp(s - m_new), 0.0)
    alpha = jnp.exp(m_old - m_new)
    l_scr[...] = alpha * l_scr[...] + jnp.sum(p, axis=-1, keepdims=True)
    acc_scr[...] = alpha * acc_scr[...] + _pv(p, v, v_t)
    m_scr[...] = m_new


def _pad_rows(x, rows):
    return jnp.concatenate([x, jnp.zeros((rows - x.shape[0], x.shape[1]), x.dtype)], axis=0)


def _sample_row_bias(tbl, ts, kpos, qpos0, lo_ok):
    d = (qpos0 + jnp.arange(ts))[:, None] - kpos[None, :]
    b = jnp.where(((d >= 0) & lo_ok)[..., None, None], _bias_lookup(tbl, d), NEG_INF)
    return jnp.transpose(b, (2, 3, 0, 1)).reshape(-1, kpos.shape[0])


def _past_bias_tiles(tbl, ts, past_len, nk):
    assert nk >= LANES
    far = jnp.repeat(tbl[REL_BUCKETS - 1].reshape(-1), ts)[:, None]
    far_tile = jnp.broadcast_to(far, (far.shape[0], nk))
    near = _sample_row_bias(tbl, ts, past_len - LANES + jnp.arange(LANES), past_len, jnp.ones((ts, LANES), bool))
    return jnp.stack([far_tile, jnp.concatenate([far_tile[:, :nk - LANES], near], axis=1)])


def _nsa_sample_kernel(pt_ref, *refs, ts, past_len, slc_lanes):
    del pt_ref
    p_n = PAGES_PER_STEP
    k_pages, v_pages = refs[:p_n], refs[p_n:2 * p_n]
    (zq_ref, gn_ref, kc_ref, vc_ref, ksn_ref, vsn_ref, wk_ref, wv_ref, kwn_ref, vwn_ref,
     biasc_ref, bpast_ref, bnew_ref, bwin_ref, mslc_ref, o_ref,
     q_scr, m_scr, l_scr, acc_scr, comb_scr, bm_scr) = refs[2 * p_n:]
    j = pl.program_id(1)
    page = k_pages[0].shape[2]
    rows = NSA_KV * HPG * ts
    half = HPG * ts

    def gate_col(gn, c):
        return jnp.concatenate([gn[:, c * 8 + hd:c * 8 + hd + 1] for hd in range(NSA_HEADS)], axis=0)

    @pl.when(j == 0)
    def _():
        _stack_queries_sample(zq_ref, q_scr, ts)
        q = q_scr[...].astype(BF16)
        gn = jax.nn.sigmoid(gn_ref[...])
        s_c = lax.dot_general(q, kc_ref[0].astype(BF16), _NT, preferred_element_type=F32) + biasc_ref[...]
        mx, p, den = _softmax_tile(s_c)
        p_c = p / jnp.where(den > 0.0, den, 1.0)
        comb_scr[...] = gate_col(gn, 0) * jnp.dot(p_c.astype(BF16), vc_ref[0].astype(BF16), preferred_element_type=F32)
        imp = jnp.concatenate(
            [sum(p_c[g * half + h * ts:g * half + (h + 1) * ts] for h in range(HPG)) for g in range(NSA_KV)], axis=0)
        imp_hi, imp_lo = _split_hi_lo(imp)
        p_slc = (jnp.dot(imp_hi, mslc_ref[...], preferred_element_type=F32)
                 + jnp.dot(imp_lo, mslc_ref[...], preferred_element_type=F32))
        lane = lax.broadcasted_iota(jnp.int32, p_slc.shape, 1)
        tpos = past_len + (lax.broadcasted_iota(jnp.int32, p_slc.shape, 0) & (ts - 1))
        qb = tpos >> 6
        forced = (lane == 0) | (lane == qb) | (lane == qb - 1)
        score = jnp.where(lane > qb, NEG_INF, p_slc + jnp.where(forced, FORCE_BONUS, 0.0))
        sel = _take_topk(score, SLC_TOPN, lane)
        bm_scr[...] = jnp.concatenate(
            [sel[g * ts:(g + 1) * ts] for g in range(NSA_KV) for _ in range(HPG)], axis=0).astype(BF16)
        m_scr[...] = jnp.full(m_scr.shape, NEG_INF, F32)
        l_scr[...] = jnp.zeros(l_scr.shape, F32)
        acc_scr[...] = jnp.zeros(acc_scr.shape, F32)
        s_w = jnp.dot(q, wk_ref[0].astype(BF16), preferred_element_type=F32) + bwin_ref[...]
        _online_update(s_w, wv_ref[0].astype(BF16), m_scr, l_scr, acc_scr, v_t=True)
        s_n = lax.dot_general(q, _pad_rows(kwn_ref[...], LANES).astype(BF16), _NT, preferred_element_type=F32) + bnew_ref[...]
        _online_update(s_n, _pad_rows(vwn_ref[...], LANES).astype(BF16), m_scr, l_scr, acc_scr)
        l = l_scr[...]
        comb_scr[...] = comb_scr[...] + gate_col(gn, 2) * (acc_scr[...] / jnp.where(l > 0.0, l, 1.0))
        m_scr[...] = jnp.full(m_scr.shape, NEG_INF, F32)
        l_scr[...] = jnp.zeros(l_scr.shape, F32)
        acc_scr[...] = jnp.zeros(acc_scr.shape, F32)

    q = q_scr[...].astype(BF16)
    nk = p_n * page
    k_t = jnp.concatenate([r[0].astype(BF16) for r in k_pages], axis=1)
    v_t = jnp.concatenate([r[0].astype(BF16) for r in v_pages], axis=1)
    blk_row = lax.broadcasted_iota(jnp.int32, (slc_lanes, nk), 0)
    key_pos = j * nk + lax.broadcasted_iota(jnp.int32, (slc_lanes, nk), 1)
    expand = jnp.where(blk_row == (key_pos >> 6), 1.0, 0.0).astype(BF16)
    vis = jnp.dot(bm_scr[...], expand, preferred_element_type=F32)
    last = pl.num_programs(1) - 1
    s = (jnp.dot(q, k_t, preferred_element_type=F32) + bpast_ref[jnp.where(j == last, 1, 0)]
         + jnp.where(vis > 0.5, 0.0, NEG_INF))
    _online_update(s, v_t, m_scr, l_scr, acc_scr, v_t=True)

    @pl.when(j == pl.num_programs(1) - 1)
    def _():
        gn = jax.nn.sigmoid(gn_ref[...])
        s_n = lax.dot_general(q, _pad_rows(ksn_ref[...], LANES).astype(BF16), _NT, preferred_element_type=F32) + bnew_ref[...]
        _online_update(s_n, _pad_rows(vsn_ref[...], LANES).astype(BF16), m_scr, l_scr, acc_scr)
        l = l_scr[...]
        comb = comb_scr[...] + gate_col(gn, 1) * (acc_scr[...] / jnp.where(l > 0.0, l, 1.0))
        _unstack_heads([comb[:half], comb[half:]], o_ref, ts)


def _nsa_sample(z, n_p, kc, vc, pool_k, pool_v, win_k, win_v, page_table, rel_table, ts):
    bs, n_pages = page_table.shape
    n_phys, _, page = pool_k.shape
    past_len = n_pages * page
    p_n = PAGES_PER_STEP
    n_steps = n_pages // p_n
    nk = p_n * page
    n_sub = kc.shape[1]
    n_cmp = n_sub - 1
    n_slc = -(-(past_len + ts) // SLC_LEN)
    slc_lanes = -(-n_slc // LANES) * LANES
    win_len = win_k.shape[2]
    assert n_pages % p_n == 0 and ts & (ts - 1) == 0 and ts <= SLC_LEN and past_len % SLC_LEN == 0 and n_p % ts == 0
    assert n_sub % LANES == 0 and win_len + ts >= WINDOW
    tbl = rel_table[:, :NSA_HEADS].reshape(REL_BUCKETS, NSA_KV, HPG)
    rows = NSA_HEADS * ts
    cmp_end = jnp.arange(n_sub) * CMP_STRIDE + CMP_LEN - 1
    biasc = _sample_row_bias(tbl, ts, cmp_end, past_len, jnp.broadcast_to(jnp.arange(n_sub) < n_cmp, (ts, n_sub)))
    bpast = _past_bias_tiles(tbl, ts, past_len, nk)
    bnew = _sample_row_bias(tbl, ts, past_len + jnp.arange(LANES), past_len, jnp.broadcast_to(jnp.arange(LANES) < ts, (ts, LANES)))
    wpos = past_len - win_len + jnp.arange(win_len)
    in_win = ((past_len + jnp.arange(ts))[:, None] - wpos[None, :]) < WINDOW
    bwin = _sample_row_bias(tbl, ts, wpos, past_len, in_win)
    mslc = jnp.pad(_slc_matrix(n_cmp, n_slc), ((0, n_sub - n_cmp), (0, slc_lanes - n_slc))).astype(BF16)
    row_blk = n_p // ts
    zrow = lambda name, width: pl.BlockSpec((ts, width), (lambda b, j, pt, cb=_Z_OFF[name] // width: (row_blk + b, cb)))
    seq3 = lambda shape: pl.BlockSpec((1,) + shape, lambda b, j, pt: (b, 0, 0))
    const2 = lambda shape: pl.BlockSpec(shape, lambda b, j, pt: (0, 0))
    grid_spec = pltpu.PrefetchScalarGridSpec(
        num_scalar_prefetch=1,
        grid=(bs, n_steps),
        in_specs=_page_specs(p_n, (LANES, page)) + _page_specs(p_n, (LANES, page)) + [
            zrow("q_n", 512), zrow("g_n", LANES),
            seq3((n_sub, LANES)), seq3((n_sub, LANES)),
            zrow("ks", LANES), zrow("vs", LANES),
            seq3((LANES, win_len)), seq3((LANES, win_len)),
            zrow("kw", LANES), zrow("vw", LANES),
            const2((rows, n_sub)),
            pl.BlockSpec((2, rows, nk), lambda b, j, pt: (0, 0, 0)),
            const2((rows, LANES)), const2((rows, win_len)), const2((n_sub, slc_lanes)),
        ],
        out_specs=pl.BlockSpec((ts, 512), lambda b, j, pt: (b, 0)),
        scratch_shapes=[
            pltpu.VMEM((rows, LANES), F32),
            pltpu.VMEM((rows,1), F32), pltpu.VMEM((rows, 1), F32), pltpu.VMEM((rows, LANES), F32),
            pltpu.VMEM((rows, LANES), F32), pltpu.VMEM((rows, slc_lanes), BF16),
        ],
    )
    return pl.pallas_call(
        functools.partial(_nsa_sample_kernel, ts=ts, past_len=past_len, slc_lanes=slc_lanes),
        grid_spec=grid_spec,
        out_shape=jax.ShapeDtypeStruct((bs * ts, 512), F32),
        compiler_params=_params(("arbitrary", "arbitrary")),
        name="nsa_sample",
    )(page_table, *([pool_k] * p_n), *([pool_v] * p_n), z, z, kc, vc, z, z, win_k, win_v, z, z,
      biasc, bpast, bnew, bwin, mslc)


def _moba_sample_kernel(pt_ref, *refs, ts, n_past_blocks):
    del pt_ref
    p_n = PAGES_PER_STEP
    k_pages, v_pages = refs[:p_n], refs[p_n:2 * p_n]
    (zq_ref, kn_ref, vn_ref, bpast_ref, bnew_ref, o_ref, q_scr, m_all, l_all, acc_all, km_scr) = refs[2 * p_n:]
    j = pl.program_id(1)
    page = k_pages[0].shape[2]
    ppb = MOBA_BLOCK // page
    bps = p_n // ppb
    rows = MOBA_KV * HPG * ts
    half = HPG * ts
    lane = lax.broadcasted_iota(jnp.int32, (rows, LANES), 1)
    kcol = lax.broadcasted_iota(jnp.int32, (LANES, LANES), 1)
    last = pl.num_programs(1) - 1

    @pl.when(j == 0)
    def _():
        _stack_queries_sample(zq_ref, q_scr, ts)
        m_all[...] = jnp.full(m_all.shape, NEG_INF, F32)
        l_all[...] = jnp.zeros(l_all.shape, F32)
        km_scr[...] = jnp.zeros(km_scr.shape, F32)

    q = q_scr[...].astype(BF16)
    for i in range(bps):
        n = j * bps + i
        k_t = jnp.concatenate([k_pages[i * ppb + r][0] for r in range(ppb)], axis=1)
        v_t = jnp.concatenate([v_pages[i * ppb + r][0].astype(BF16) for r in range(ppb)], axis=1)
        s = (jnp.dot(q, k_t.astype(BF16), preferred_element_type=F32)
             + bpast_ref[jnp.where(j == last, 1, 0), :, i * MOBA_BLOCK:(i + 1) * MOBA_BLOCK])
        mx, p, den = _softmax_tile(s)
        m_all[...] = jnp.where(lane == n, mx, m_all[...])
        l_all[...] = jnp.where(lane == n, den, l_all[...])
        acc_all[n] = _pv(p, v_t, True)
        km_scr[...] = jnp.where(kcol == n, jnp.mean(k_t, axis=1, keepdims=True), km_scr[...])

    @pl.when(j == pl.num_programs(1) - 1)
    def _():
        s_o = lax.dot_general(q, _pad_rows(kn_ref[...], LANES).astype(BF16), _NT, preferred_element_type=F32) + bnew_ref[...]
        m_o, p_o, l_o = _softmax_tile(s_o)
        acc_o = jnp.dot(p_o.astype(BF16), _pad_rows(vn_ref[...], LANES).astype(BF16), preferred_element_type=F32)
        km_hi, km_lo = _split_hi_lo(km_scr[...])
        gs = jnp.dot(q, km_hi, preferred_element_type=F32) + jnp.dot(q, km_lo, preferred_element_type=F32)
        gs = jnp.where(lane < n_past_blocks, gs, NEG_INF)
        sel = _take_topk(gs, min(MOBA_TOPK, n_past_blocks), lane)
        m_sel = jnp.where(sel > 0.5, m_all[...], NEG_INF)
        m_fin = jnp.maximum(jnp.max(m_sel, axis=-1, keepdims=True), m_o)
        wgt = jnp.where(sel > 0.5, jnp.exp(m_sel - m_fin), 0.0)
        w_o = jnp.exp(m_o - m_fin)
        den = jnp.sum(wgt * l_all[...], axis=-1, keepdims=True) + w_o * l_o
        num = w_o * acc_o
        for n in range(n_past_blocks):
            num = num + wgt[:, n:n + 1] * acc_all[n]
        res = num / den
        _unstack_heads([res[:half], res[half:]], o_ref, ts)


def _moba_sample(z, n_p, pool_k, pool_v, page_table, rel_table, ts):
    bs, n_pages = page_table.shape
    n_phys, _, page = pool_k.shape
    past_len = n_pages * page
    p_n = PAGES_PER_STEP
    n_steps = n_pages // p_n
    nk = p_n * page
    nb_past = past_len // MOBA_BLOCK
    assert (MOBA_BLOCK % page == 0 and p_n % (MOBA_BLOCK // page) == 0 and past_len % MOBA_BLOCK == 0
            and ts <= MOBA_BLOCK and nb_past <= LANES and n_pages % p_n == 0 and n_p % ts == 0)
    tbl = rel_table[:, NSA_HEADS:].reshape(REL_BUCKETS, MOBA_KV, HPG)
    rows = MOBA_HEADS * ts
    bpast = _past_bias_tiles(tbl, ts, past_len, nk)
    bnew = _sample_row_bias(tbl, ts, past_len + jnp.arange(LANES), past_len, jnp.broadcast_to(jnp.arange(LANES) < ts, (ts, LANES)))
    row_blk = n_p // ts
    zrow = lambda name, width: pl.BlockSpec((ts, width), (lambda b, j, pt, cb=_Z_OFF[name] // width: (row_blk + b, cb)))
    grid_spec = pltpu.PrefetchScalarGridSpec(
        num_scalar_prefetch=1,
        grid=(bs, n_steps),
        in_specs=_page_specs(p_n, (LANES, page)) + _page_specs(p_n, (LANES, page)) + [
            zrow("q_m", 512), zrow("k_m", LANES), zrow("v_m", LANES),
            pl.BlockSpec((2, rows, nk), lambda b, j, pt: (0, 0, 0)),
            pl.BlockSpec((rows, LANES), lambda b, j, pt: (0, 0)),
        ],
        out_specs=pl.BlockSpec((ts, 512), lambda b, j, pt: (b, 0)),
        scratch_shapes=[
            pltpu.VMEM((rows, LANES), F32),
            pltpu.VMEM((rows,LANES), F32), pltpu.VMEM((rows, LANES), F32),
            pltpu.VMEM((nb_past, rows, LANES), F32), pltpu.VMEM((LANES, LANES), F32),
        ],
    )
    return pl.pallas_call(
        functools.partial(_moba_sample_kernel, ts=ts, n_past_blocks=nb_past),
        grid_spec=grid_spec,
        out_shape=jax.ShapeDtypeStruct((bs * ts, 512), F32),
        compiler_params=_params(("arbitrary", "arbitrary")),
        name="moba_sample",
    )(page_table, *([pool_k] * p_n), *([pool_v] * p_n), z, z, z, bpast, bnew)


def _memkv_kernel(m_ref, gain_ref, w_ref, kgain_ref, seg_ref, k_ref, v_ref):
    x = m_ref[...]
    xb = (x * lax.rsqrt(jnp.mean(x * x, axis=-1, keepdims=True) + RMS_EPS) * gain_ref[...]).astype(BF16)
    kvw = k_ref.shape[1]
    k = jnp.dot(xb, w_ref[:, :kvw], preferred_element_type=F32)
    for h in range(MEM_HEADS):
        kh = k[:, h * MEM_HEAD_DIM:(h + 1) * MEM_HEAD_DIM]
        ss = _group_sumsq(kh, seg_ref[...])
        k_ref[:, h * MEM_HEAD_DIM:(h + 1) * MEM_HEAD_DIM] = kh * lax.rsqrt(ss * (1.0 / MEM_HEAD_DIM) + RMS_EPS) * kgain_ref[...]
    v_ref[...] = jnp.dot(xb, w_ref[:, kvw:], preferred_element_type=F32)


def _memory_kv_pallas(mem, norm_gain, w_mem_kv, k_gain, tm=256):
    n, d = mem.shape
    kvw = w_mem_kv.shape[1] // 2
    assert n % tm == 0
    row = lambda i: (i, 0)
    const = lambda i: (0, 0)
    return pl.pallas_call(
        _memkv_kernel,
        grid=(n // tm,),
        in_specs=[pl.BlockSpec((tm, d), row), pl.BlockSpec((1, d), const), pl.BlockSpec((d, 2 * kvw), const),
                  pl.BlockSpec((1, MEM_HEAD_DIM), const), pl.BlockSpec((LANES, LANES), const)],
        out_specs=[pl.BlockSpec((tm, kvw), row), pl.BlockSpec((tm, kvw), row)],
        out_shape=[jax.ShapeDtypeStruct((n, kvw), F32)] * 2,
        compiler_params=_params(("arbitrary",)),
        name="memory_kv",
    )(mem, norm_gain[None, :], w_mem_kv.astype(BF16), k_gain[None, :], _seg_matrix(MEM_HEAD_DIM))


def _mem_attend_kernel(zq_ref, mk_ref, mv_ref, o_ref):
    for h in range(MEM_HEADS):
        sl = slice(h * MEM_HEAD_DIM, (h + 1) * MEM_HEAD_DIM)
        q = (zq_ref[:, sl] * (MEM_HEAD_DIM ** -0.5)).astype(BF16)
        s = lax.dot_general(q, mk_ref[0, :, sl].astype(BF16), _NT, preferred_element_type=F32)
        p = jnp.exp(s - jnp.max(s, axis=-1, keepdims=True))
        o = jnp.dot(p.astype(BF16), mv_ref[0, :, sl].astype(BF16), preferred_element_type=F32)
        o_ref[:, sl] = o / jnp.sum(p, axis=-1, keepdims=True)


def _mem_attend_pallas(z, row0, n_seq, t_seq, tq, mk, mv):
    nq = t_seq // tq
    assert t_seq % tq == 0 and row0 % tq == 0
    m_len = mk.shape[1]
    width = MEM_HEADS * MEM_HEAD_DIM
    cb = _Z_OFF["q_x"] // width
    return pl.pallas_call(
        _mem_attend_kernel,
        grid=(n_seq, nq),
        in_specs=[pl.BlockSpec((tq, width), lambda b, i: (row0 // tq + b * nq + i, cb)),
                  pl.BlockSpec((1, m_len, width), lambda b, i: (b, 0, 0)),
                  pl.BlockSpec((1, m_len, width), lambda b, i: (b, 0, 0))],
        out_specs=pl.BlockSpec((tq, width), lambda b, i: (b * nq + i, 0)),
        out_shape=jax.ShapeDtypeStruct((n_seq * t_seq, width), F32),
        compiler_params=_params(("arbitrary", "arbitrary")),
        name="mem_attend",
    )(z, mk, mv)


def _compress_dense(x_raw, pe, w1, b1, w2):
    b, tk, _ = x_raw.shape
    n_sub = tk // CMP_STRIDE
    n_cmp = n_sub - 1
    xs = x_raw[:, :n_sub * CMP_STRIDE].reshape(b, n_sub, CMP_STRIDE * LANES)
    w1r = w1.reshape(2, CMP_STRIDE, HEAD_DIM, -1)
    wbig = jnp.einsum("rlde,gh->lgdrhe", w1r, jnp.eye(2, dtype=F32)).reshape(CMP_STRIDE * LANES, 2 * LANES)
    a = jnp.dot(xs, wbig, precision=HI)
    pos = jnp.einsum("rld,rlde->e", pe.reshape(2, CMP_STRIDE, HEAD_DIM), w1r, precision=HI)
    h = a[:, :n_cmp, :LANES] + a[:, 1:, LANES:] + jnp.tile(b1 + pos, 2)
    w2big = jnp.einsum("ed,gh->gehd", w2, jnp.eye(2, dtype=F32)).reshape(LANES, LANES)
    return jnp.dot(jax.nn.gelu(h), w2big, precision=HI)


def _nsa_dense(q, glog, kc_raw, vc_raw, ks, vs, kw_all, vw_all, q_start, win_start, w):
    b, t_q, _ = q.shape
    tk = kc_raw.shape[1]
    scale = HEAD_DIM ** -0.5
    qg = q.reshape(b, t_q, NSA_KV, HPG, HEAD_DIM)
    t = q_start + jnp.arange(t_q)
    tbl = w["rel_bias_table"][:, :NSA_HEADS].reshape(REL_BUCKETS, NSA_KV, HPG)

    kc = _compress_dense(kc_raw, w["cmp_pos_emb"][0], w["cmp_w1"][0], w["cmp_b1"][0], w["cmp_w2"][0])
    n_cmp = kc.shape[1]
    kc = _rms(kc.reshape(b, n_cmp, NSA_KV, HEAD_DIM), w["nsa_k_gain"][0])
    vc = _compress_dense(vc_raw, w["cmp_pos_emb"][1], w["cmp_w1"][1], w["cmp_b1"][1], w["cmp_w2"][1])
    vc = vc.reshape(b, n_cmp, NSA_KV, HEAD_DIM)
    cmp_end = jnp.arange(n_cmp) * CMP_STRIDE + CMP_LEN - 1
    dist_c = t[:, None] - cmp_end[None, :]
    bias_c = jnp.transpose(tbl[_rel_bucket(dist_c)], (2, 3, 0, 1))
    s_c = jnp.einsum("btghd,bngd->bghtn", qg, kc, precision=HI) * scale + bias_c
    p_c = _masked_softmax(s_c, dist_c >= 0)
    o_c = jnp.einsum("bghtn,bngd->btghd", p_c, vc, precision=HI)

    imp = jnp.transpose(p_c.sum(axis=2), (0, 2, 1, 3))
    n_slc = -(-tk // SLC_LEN)
    p_slc = jnp.dot(imp, _slc_matrix(n_cmp, n_slc), precision=HI)
    blk = jnp.arange(n_slc)[None, :]
    qb = (t // SLC_LEN)[:, None]
    forced = ((blk == 0) | (blk == qb) | (blk == qb - 1)).astype(F32)
    future = blk > qb
    score = jnp.where(future[None, :, None, :], NEG_INF, p_slc + FORCE_BONUS * forced[None, :, None, :])
    k_sel = min(SLC_TOPN, n_slc)
    sel_val, sel_idx = lax.top_k(score, k_sel)
    sel_ok = sel_val > HALF_NEG
    selm = (jax.nn.one_hot(sel_idx, n_slc, dtype=F32) * sel_ok[..., None].astype(F32)).sum(axis=-2) > 0.5

    kpos = jnp.arange(tk)
    d_s = t[:, None] - kpos[None, :]
    bias_s = jnp.transpose(tbl[_rel_bucket(d_s)], (2, 3, 0, 1))
    ksg = ks.reshape(b, tk, NSA_KV, HEAD_DIM)
    vsg = vs.reshape(b, tk, NSA_KV, HEAD_DIM)
    s_s = jnp.einsum("btghd,bkgd->bghtk", qg, ksg, precision=HI) * scale + bias_s
    m_s = jnp.repeat(jnp.transpose(selm, (0, 2, 1, 3)), SLC_LEN, axis=-1)[..., :tk] & (d_s >= 0)
    p_s = _masked_softmax(s_s, m_s[:, :, None])
    o_s = jnp.einsum("bghtk,bkgd->btghd", p_s, vsg, precision=HI)

    tw = kw_all.shape[1]
    wpos = win_start + jnp.arange(tw)
    d_w = t[:, None] - wpos[None, :]
    m_w = (wpos[None, :] >= 0) & (d_w >= 0) & (d_w < WINDOW)
    bias_w = jnp.transpose(tbl[_rel_bucket(d_w)], (2, 3, 0, 1))
    kwg = kw_all.reshape(b, tw, NSA_KV, HEAD_DIM)
    vwg = vw_all.reshape(b, tw, NSA_KV, HEAD_DIM)
    s_w = jnp.einsum("btghd,bkgd->bghtk", qg, kwg, precision=HI) * scale + bias_w
    p_w = _masked_softmax(s_w, m_w)
    o_w = jnp.einsum("bghtk,bkgd->btghd", p_w, vwg, precision=HI)

    g = jax.nn.sigmoid(glog).reshape(b, t_q, 3, NSA_KV, HPG)[..., None]
    o = g[:, :, 0] * o_c + g[:, :, 1] * o_s + g[:, :, 2] * o_w
    return o.reshape(b, t_q, NSA_HEADS * HEAD_DIM)


def _moba_dense(q, k, v, q_start, rel_table):
    b, t_q, _ = q.shape
    tk = k.shape[1]
    scale = HEAD_DIM ** -0.5
    qg = q.reshape(b, t_q, MOBA_KV, HPG, HEAD_DIM)
    kg = k.reshape(b, tk, MOBA_KV, HEAD_DIM)
    vg = v.reshape(b, tk, MOBA_KV, HEAD_DIM)
    t = q_start + jnp.arange(t_q)
    tbl = rel_table[:, NSA_HEADS:].reshape(REL_BUCKETS, MOBA_KV, HPG)
    nb = -(-tk // MOBA_BLOCK)
    pad = nb * MOBA_BLOCK - tk
    kp = jnp.pad(kg, ((0, 0), (0, pad), (0, 0), (0, 0)))
    n_top = min(MOBA_TOPK, nb - 1)
    kpos = jnp.arange(tk)
    kblk = kpos // MOBA_BLOCK
    d = t[:, None] - kpos[None, :]
    m_own = (kblk[None, :] == (t // MOBA_BLOCK)[:, None]) & (d >= 0)
    mask = jnp.broadcast_to(m_own[None, None, None], (b, MOBA_KV, HPG, t_q, tk))
    if n_top > 0:
        kmean = jnp.mean(kp.reshape(b, nb, MOBA_BLOCK, MOBA_KV, HEAD_DIM), axis=2)
        gs = jnp.einsum("btghd,bngd->btghn", qg, kmean, precision=HI)
        past = jnp.arange(nb)[None, :] < (t // MOBA_BLOCK)[:, None]
        gs = jnp.where(past[None, :, None, None, :], gs, NEG_INF)
        top_val, top_idx = lax.top_k(gs, n_top)
        top_ok = top_val > HALF_NEG
        selm = (jax.nn.one_hot(top_idx, nb, dtype=F32) * top_ok[..., None].astype(F32)).sum(axis=-2) > 0.5
        mask = mask | jnp.repeat(jnp.transpose(selm, (0, 2, 3, 1, 4)), MOBA_BLOCK, axis=-1)[..., :tk]
    bias = jnp.transpose(tbl[_rel_bucket(d)], (2, 3, 0, 1))
    s = jnp.einsum("btghd,bkgd->bghtk", qg, kg, precision=HI) * scale + bias
    p = _masked_softmax(s, mask)
    o = jnp.einsum("bghtk,bkgd->btghd", p, vg, precision=HI)
    return o.reshape(b, t_q, MOBA_HEADS * HEAD_DIM)


def _mem_attend(q, mk, mv):
    b, t_q, _ = q.shape
    qh = q.reshape(b, t_q, MEM_HEADS, MEM_HEAD_DIM)
    s = jnp.einsum("bthd,bmhd->bhtm", qh, mk, precision=HI) * (MEM_HEAD_DIM ** -0.5)
    p = jax.nn.softmax(s, axis=-1)
    return jnp.einsum("bhtm,bmhd->bthd", p, mv, precision=HI).reshape(b, t_q, MEM_HEADS * MEM_HEAD_DIM)


def _memory_kv(mem, w):
    b, m, _ = mem.shape
    kv = jnp.dot(_rms(mem, w["mem_norm_gain"]), w["w_mem_kv"], precision=HI)
    k, v = jnp.split(kv, 2, axis=-1)
    k = _rms(k.reshape(b, m, MEM_HEADS, MEM_HEAD_DIM), w["mem_k_gain"])
    return k, v.reshape(b, m, MEM_HEADS, MEM_HEAD_DIM)


def _gather_pages(pool, page_table):
    g = pool[page_table]
    return g.reshape(page_table.shape[0], -1, LANES)


def _zcol(z, name, width):
    return z[..., _Z_OFF[name]:_Z_OFF[name] + width]


def kernel(x_prompt, x_sample, cache_cmp_k, cache_cmp_v, cache_slc_k, cache_slc_v, cache_moba_k, cache_moba_v, cache_win_k, cache_win_v, cache_mem_k, cache_mem_v, page_table, mem_prompt, attn_norm_gain, w_in, nsa_q_gain, nsa_k_gain, cmp_pos_emb, cmp_w1, cmp_b1, cmp_w2, moba_q_gain, moba_k_gain, mem_norm_gain, w_mem_kv, mem_q_gain, mem_k_gain, rel_bias_table, w_branch, w_out, ffn_norm_gain, peer_w_q, peer_sub_keys, peer_u, peer_v):
    assert w_in.shape[0] == 1
    l = 0
    bp, tp, d = x_prompt.shape
    bs, ts, _ = x_sample.shape
    past_len = page_table.shape[1] * cache_cmp_k.shape[2]
    n_p, n_s = bp * tp, bs * ts

    x_all = jnp.concatenate([x_prompt.reshape(n_p, d), x_sample.reshape(n_s, d)], axis=0)
    hgain = _head_gain_row({"nsa_q": nsa_q_gain[l], "nsa_k1": nsa_k_gain[l, 1], "nsa_k2": nsa_k_gain[l, 2],
                            "moba_q": moba_q_gain[l], "moba_k": moba_k_gain[l], "mem_q": mem_q_gain[l]})
    z = _inproj(x_all, attn_norm_gain[l][None, :], _pack_w_in(w_in[l]), hgain)
    zp, zs = "prompt", "sample"

    def rows(group, name):
        col = _zcol(z, name, LANES)
        return col[:n_p].reshape(bp, tp, LANES) if group == zp else col[n_p:].reshape(bs, ts, LANES)

    kc_rows_p, vc_rows_p = rows(zp, "kc"), rows(zp, "vc")
    wk = _compress_weights(cmp_pos_emb[l, 0], cmp_w1[l, 0], cmp_b1[l, 0], cmp_w2[l, 0])
    wv = _compress_weights(cmp_pos_emb[l, 1], cmp_w1[l, 1], cmp_b1[l, 1], cmp_w2[l, 1])
    sub = lambda r: r.reshape(bp, tp // CMP_STRIDE, CMP_STRIDE * LANES)
    kgain = jnp.tile(nsa_k_gain[l, 0], 2)[None, :]
    kc_p = _compress(sub(kc_rows_p), wk, kgain, True)
    vc_p = _compress(sub(vc_rows_p), wv, kgain, False)
    o_nsa_p = _nsa_prompt(z, kc_p, vc_p, rel_bias_table, bp, tp)
    o_moba_p = _moba_prompt(z, rel_bias_table, bp, tp)
    m_len = mem_prompt.shape[1]
    mk_p, mv_p = _memory_kv_pallas(mem_prompt.reshape(bp * m_len, d), mem_norm_gain[l], w_mem_kv[l], mem_k_gain[l])
    mem_w = MEM_HEADS * MEM_HEAD_DIM
    o_mem_p = _mem_attend_pallas(z, 0, bp, tp, 256, mk_p.reshape(bp, m_len, mem_w), mv_p.reshape(bp, m_len, mem_w))
    mk_p = mk_p.reshape(bp, m_len, MEM_HEADS, MEM_HEAD_DIM)
    mv_p = mv_p.reshape(bp, m_len, MEM_HEADS, MEM_HEAD_DIM)

    pool = lambda c: c[l].reshape(c.shape[1], c.shape[2], LANES)
    keys_on_lanes = lambda c: jnp.transpose(c[l], (0, 2, 3, 1)).reshape(c.shape[1], LANES, c.shape[2])
    kc_s = _compress_paged(pool(cache_cmp_k), page_table, wk, kgain, True)
    vc_s = _compress_paged(pool(cache_cmp_v), page_table, wv, kgain, False)
    win_len = cache_win_k.shape[2]
    win_k = cache_win_k[l].reshape(bs, win_len, LANES)
    win_v = cache_win_v[l].reshape(bs, win_len, LANES)
    o_nsa_s = _nsa_sample(z, n_p, kc_s, vc_s, keys_on_lanes(cache_slc_k), keys_on_lanes(cache_slc_v),
                          keys_on_lanes(cache_win_k), keys_on_lanes(cache_win_v), page_table, rel_bias_table, ts)
    o_moba_s = _moba_sample(z, n_p, keys_on_lanes(cache_moba_k), keys_on_lanes(cache_moba_v), page_table,
                            rel_bias_table, ts)
    o_mem_s = _mem_attend_pallas(z, n_p, bs, ts, ts, cache_mem_k[l].reshape(bs, -1, mem_w), cache_mem_v[l].reshape(bs, -1, mem_w))
    kw_all = jnp.concatenate([win_k, rows(zs, "kw")], axis=1)
    vw_all = jnp.concatenate([win_v, rows(zs, "vw")], axis=1)

    cat = lambda a, b_: jnp.concatenate([a.reshape(n_p, 512), b_.reshape(n_s, 512)], axis=0)
    x1, xn_b = _merge(cat(o_nsa_p, o_nsa_s), cat(o_moba_p, o_moba_s), cat(o_mem_p, o_mem_s), z, x_all,
                      w_branch[l], w_out[l], ffn_norm_gain[l])
    y = _peer(x1, xn_b, peer_w_q[l], peer_sub_keys[l], peer_u[l].astype(BF16), peer_v[l].astype(BF16).T)
    y_p = y[:n_p].reshape(bp, tp, d)
    y_s = y[n_p:].reshape(bs, ts, d)

    def st(r, b_, t_):
        return r.reshape(1, b_, t_, 2, HEAD_DIM)

    keep_p = min(WINDOW, tp)
    p_win_k = st(rows(zp, "kw")[:, tp - keep_p:], bp, keep_p)
    p_win_v = st(rows(zp, "vw")[:, tp - keep_p:], bp, keep_p)
    keep_s = min(WINDOW, past_len + ts)
    s_win_k = st(kw_all[:, kw_all.shape[1] - keep_s:], bs, keep_s)
    s_win_v = st(vw_all[:, vw_all.shape[1] - keep_s:], bs, keep_s)
    return (y_p, y_s,
            st(kc_rows_p, bp, tp), st(vc_rows_p, bp, tp), st(rows(zp, "ks"), bp, tp), st(rows(zp, "vs"), bp, tp),
            st(rows(zp, "k_m"), bp, tp), st(rows(zp, "v_m"), bp, tp), p_win_k, p_win_v, mk_p[None], mv_p[None],
            st(rows(zs, "kc"), bs, ts), st(rows(zs, "vc"), bs, ts), st(rows(zs, "ks"), bs, ts), st(rows(zs, "vs"), bs, ts),
            st(rows(zs, "k_m"), bs, ts), st(rows(zs, "v_m"), bs, ts), s_win_k, s_win_v)
```

```python
import functools
import math

import jax
import jax.numpy as jnp
import numpy as np
from jax import lax
from jax.experimental import pallas as pl
from jax.experimental.pallas import tpu as pltpu

HEAD_DIM = 64
NSA_HEADS = 8
NSA_KV = 2
HPG = 4
CMP_LEN = 32
CMP_STRIDE = 16
SLC_LEN = 64
SLC_TOPN = 16
WINDOW = 512
FORCE_BONUS = 1000.0
MOBA_HEADS = 8
MOBA_KV = 2
MOBA_BLOCK = 256
MOBA_TOPK = 3
MEM_HEADS = 4
MEM_HEAD_DIM = 128
REL_BUCKETS = 32
REL_MAX_DIST = 128
PEER_HEADS = 8
PEER_NKEYS = 128
PEER_TOPK = 16
RMS_EPS = 1e-6
NEG_INF = -1e30
HALF_NEG = 0.5 * NEG_INF
KNOCKED = -3.0e38
LOG2E = 1.4426950408889634
LANES = 128
SUBLANES = 8
VMEM_LIMIT = 56 * 1024 * 1024
ATT_TQ = 128
ATT_TK = 128

F32 = jnp.float32
BF16 = jnp.bfloat16
HI = lax.Precision.HIGHEST
_NT = (((1,), (1,)), ((), ()))


def _rms(x, g):
    return x * lax.rsqrt(jnp.mean(x * x, axis=-1, keepdims=True) + RMS_EPS) * g


def _rel_bucket(dist):
    n = jnp.maximum(dist, 0)
    exact = REL_BUCKETS // 2
    nf = jnp.maximum(n, 1).astype(F32)
    large = exact + (jnp.log(nf / exact) / math.log(REL_MAX_DIST / exact) * (REL_BUCKETS - exact)).astype(jnp.int32)
    return jnp.where(n < exact, n, jnp.minimum(large, REL_BUCKETS - 1))


def _bias_lookup(tbl, dist):
    onehot = jax.nn.one_hot(_rel_bucket(dist), REL_BUCKETS, dtype=F32)
    return jnp.tensordot(onehot, tbl, axes=1, precision=HI)


def _masked_softmax(s, mask):
    s = jnp.where(mask, s, NEG_INF)
    return jnp.where(mask, jax.nn.softmax(s, axis=-1), 0.0)


def _gelu_tanh(x):
    inner = x * (0.7978845608028654 + 0.035677408136300125 * (x * x))
    return (0.5 * x) * (1.0 + jnp.tanh(inner))


def _split_hi_lo(x):
    hi = x.astype(BF16)
    lo = (x - hi.astype(F32)).astype(BF16)
    return hi, lo


def _dot3(a, b_hi, b_lo):
    a_hi, a_lo = _split_hi_lo(a)
    return (jnp.dot(a_hi, b_hi, preferred_element_type=F32) + jnp.dot(a_lo, b_hi, preferred_element_type=F32)
            + jnp.dot(a_hi, b_lo, preferred_element_type=F32))


def _seg_matrix(hd):
    i = np.arange(LANES)
    return jnp.asarray((i[:, None] // hd) == (i[None, :] // hd), BF16)


def _group_sumsq(z, seg):
    sq_hi, sq_lo = _split_hi_lo(z * z)
    return jnp.dot(sq_hi, seg, preferred_element_type=F32) + jnp.dot(sq_lo, seg, preferred_element_type=F32)


def _params(sem):
    return pltpu.CompilerParams(dimension_semantics=sem, vmem_limit_bytes=VMEM_LIMIT)


_Z_GROUPS = (
    ("g_merge", 3072, 0), ("q_n", 512, 64), ("q_m", 512, 64), ("q_x", 512, 128),
    ("kc", 128, 0), ("vc", 128, 0), ("ks", 128, 64), ("vs", 128, 0), ("kw", 128, 64), ("vw", 128, 0),
    ("k_m", 128, 64), ("v_m", 128, 0), ("g_n", 128, 0),
)
_Z_OFF = {}
_o = 0
for _n, _w, _h in _Z_GROUPS:
    _Z_OFF[_n] = _o
    _o += _w
Z_COLS = _o
IN_SPLITS = (512, 128, 128, 128, 128, 128, 128, 24, 512, 128, 128, 512, 3072)
_SRC_NAMES = ("q_n", "kc", "vc", "ks", "vs", "kw", "vw", "g_n", "q_m", "k_m", "v_m", "q_x", "g_merge")


def _pack_w_in(w_in):
    cuts = np.cumsum((0,) + IN_SPLITS)
    parts = {n: w_in[:, cuts[i]:cuts[i + 1]] for i, n in enumerate(_SRC_NAMES)}
    parts["g_n"] = jnp.pad(parts["g_n"], ((0, 0), (0, LANES - 24)))
    return jnp.concatenate([parts[n] for n, _, _ in _Z_GROUPS], axis=1).astype(BF16)


def _head_gain_row(gains):
    g = {
        "q_n": jnp.tile(gains["nsa_q"], 8), "ks": jnp.tile(gains["nsa_k1"], 2), "kw": jnp.tile(gains["nsa_k2"], 2),
        "q_m": jnp.tile(gains["moba_q"], 8), "k_m": jnp.tile(gains["moba_k"], 2), "q_x": jnp.tile(gains["mem_q"], 4),
    }
    return jnp.concatenate([g.get(n, jnp.ones((w,), F32)) for n, w, _ in _Z_GROUPS])[None, :]


def _inproj_kernel(x_ref, gain_ref, w_ref, hgain_ref, seg64_ref, seg128_ref, z_ref):
    x = x_ref[...]
    ms = jnp.mean(x * x, axis=-1, keepdims=True)
    xb = (x * lax.rsqrt(ms + RMS_EPS) * gain_ref[...]).astype(BF16)
    for name, width, hd in _Z_GROUPS:
        off = _Z_OFF[name]
        for c in range(0, width, 512):
            cw = min(512, width - c)
            lo_c, hi_c = off + c, off + c + cw
            z = jnp.dot(xb, w_ref[:, lo_c:hi_c], preferred_element_type=F32)
            if hd:
                seg = seg64_ref[...] if hd == 64 else seg128_ref[...]
                for j in range(0, cw, LANES):
                    zj = z[:, j:j + LANES]
                    ss = _group_sumsq(zj, seg)
                    z_ref[:, lo_c + j:lo_c + j + LANES] = (
                        zj * lax.rsqrt(ss * (1.0 / hd) + RMS_EPS) * hgain_ref[:, lo_c + j:lo_c + j + LANES])
            else:
                z_ref[:, lo_c:hi_c] = z


def _inproj(x, gain, w_packed, hgain, tm=256):
    n, d = x.shape
    assert n % tm == 0
    const = lambda i: (0, 0)
    return pl.pallas_call(
        _inproj_kernel,
        grid=(n // tm,),
        in_specs=[
            pl.BlockSpec((tm, d), lambda i: (i, 0)),
            pl.BlockSpec((1, d), const),
            pl.BlockSpec((d, Z_COLS), const),
            pl.BlockSpec((1, Z_COLS), const),
            pl.BlockSpec((LANES, LANES), const),
            pl.BlockSpec((LANES, LANES), const),
        ],
        out_specs=pl.BlockSpec((tm, Z_COLS), lambda i: (i, 0)),
        out_shape=jax.ShapeDtypeStruct((n, Z_COLS), F32),
        compiler_params=_params(("arbitrary",)),
        name="inproj",
    )(x, gain, w_packed, hgain, _seg_matrix(64), _seg_matrix(128))


def _compress_weights(pe, w1, b1, w2):
    w1r = w1.reshape(2, CMP_STRIDE, HEAD_DIM, -1)
    eye = jnp.eye(2, dtype=F32)
    wbig = jnp.einsum("rlde,gh->lgdrhe", w1r, eye).reshape(CMP_STRIDE * LANES, 2 * LANES)
    pos = jnp.einsum("rld,rlde->e", pe.reshape(2, CMP_STRIDE, HEAD_DIM), w1r, precision=HI)
    cvec = jnp.tile(b1 + pos, 2)[None, :]
    w2big = jnp.einsum("ed,gh->gehd", w2, eye).reshape(LANES, LANES)
    return _split_hi_lo(wbig) + (cvec,) + _split_hi_lo(w2big)


def _compress_kernel(x_ref, wh_ref, wl_ref, c_ref, w2h_ref, w2l_ref, seg_ref, gain_ref, o_ref, *, norm):
    ns = x_ref.shape[1]
    a = _dot3(x_ref[0], wh_ref[...], wl_ref[...])
    nxt = pltpu.roll(a[:, LANES:], ns - 1, axis=0)
    h = a[:, :LANES] + nxt + c_ref[...]
    o = _dot3(_gelu_tanh(h), w2h_ref[...], w2l_ref[...])
    if norm:
        o = o * lax.rsqrt(_group_sumsq(o, seg_ref[...]) * (1.0 / HEAD_DIM) + RMS_EPS) * gain_ref[...]
    o_ref[0] = o


def _compress(x_sub, weights, gain, norm):
    b, ns, kk = x_sub.shape
    wh, wl, cvec, w2h, w2l = weights
    const = lambda i: (0, 0)
    return pl.pallas_call(
        functools.partial(_compress_kernel, norm=norm),
        grid=(b,),
        in_specs=[
            pl.BlockSpec((1, ns, kk), lambda i: (i, 0, 0)),
            pl.BlockSpec((kk, 2 * LANES), const), pl.BlockSpec((kk, 2 * LANES), const),
            pl.BlockSpec((1, LANES), const),
            pl.BlockSpec((LANES, LANES), const), pl.BlockSpec((LANES, LANES), const),
            pl.BlockSpec((LANES, LANES), const), pl.BlockSpec((1, LANES), const),
        ],
        out_specs=pl.BlockSpec((1, ns, LANES), lambda i: (i, 0, 0)),
        out_shape=jax.ShapeDtypeStruct((b, ns, LANES), F32),
        compiler_params=_params(("arbitrary",)),
        name="nsa_compress",
    )(x_sub, wh, wl, cvec, w2h, w2l, _seg_matrix(HEAD_DIM), gain)


def _toeplitz_tiles(tbl, deltas, tq, tk, hi):
    i = jnp.arange(tq)[:, None]
    j = jnp.arange(tk)[None, :]
    tiles = []
    for dl in deltas:
        d = dl + i - j
        b = _bias_lookup(tbl, d)
        b = jnp.where(((d >= 0) & (d < hi))[..., None, None], b, NEG_INF)
        tiles.append(jnp.transpose(b, (2, 3, 0, 1)).reshape(tbl.shape[1], HPG * tq, tk))
    return jnp.stack(tiles, axis=1)


def _with_masked_tile(tiles):
    return jnp.concatenate([jnp.full_like(tiles[:, :1], NEG_INF), tiles], axis=1)


def _stack_queries(zq_ref, write, tq):
    lane = lax.broadcasted_iota(jnp.int32, (tq, LANES), 1)
    for g in range(2):
        keep = (lane < HEAD_DIM) if g == 0 else (lane >= HEAD_DIM)
        for h in range(HPG):
            hd = HPG * g + h
            chunk = zq_ref[:, LANES * (hd // 2):LANES * (hd // 2 + 1)] * (HEAD_DIM ** -0.5)
            if hd % 2 != g:
                chunk = pltpu.roll(chunk, HEAD_DIM, axis=1)
            write(g, h, jnp.where(keep, chunk, 0.0))


def _stack_queries_prompt(zq_ref, q_scr, tq):
    def write(g, h, val):
        q_scr[g, h * tq:(h + 1) * tq, :] = val.astype(BF16)
    _stack_queries(zq_ref, write, tq)


def _stack_queries_sample(zq_ref, q_scr, ts):
    def write(g, h, val):
        q_scr[(g * HPG + h) * ts:(g * HPG + h + 1) * ts, :] = val
    _stack_queries(zq_ref, write, ts)


def _unstack_heads(parts, o_ref, tq):
    lane = lax.broadcasted_iota(jnp.int32, (tq, LANES), 1)
    for j in range(4):
        g = j // 2
        he = (2 * j) % HPG
        left = parts[g][he * tq:(he + 1) * tq, :]
        right = parts[g][(he + 1) * tq:(he + 2) * tq, :]
        if g == 0:
            right = pltpu.roll(right, HEAD_DIM, axis=1)
        else:
            left = pltpu.roll(left, HEAD_DIM, axis=1)
        o_ref[:, LANES * j:LANES * (j + 1)] = jnp.where(lane < HEAD_DIM, left, right)


KEY_TILES_PER_STEP = 4


def _flash(q, k_ref, v_ref, bias_ref, g, tile_index, bm, blk_shift, rep, kb_lo, kb_hi, qt, m_scr, l_scr, acc_scr, tk):
    m_scr[...] = jnp.full(m_scr.shape, NEG_INF, F32)
    l_scr[...] = jnp.zeros(l_scr.shape, F32)
    acc_scr[...] = jnp.zeros(acc_scr.shape, F32)
    nk = tk * KEY_TILES_PER_STEP
    blk_row = lax.broadcasted_iota(jnp.int32, (LANES, nk), 0)
    key_lane = lax.broadcasted_iota(jnp.int32, (LANES, nk), 1)

    def body(kb, carry):
        k0 = pl.multiple_of(kb * nk, nk)
        k = k_ref[pl.ds(k0, nk), :].astype(BF16)
        v = v_ref[pl.ds(k0, nk), :].astype(BF16)
        s = lax.dot_general(q, k, _NT, preferred_element_type=F32)
        s = s + jnp.concatenate(
            [bias_ref[g, tile_index(qt - (kb * KEY_TILES_PER_STEP + c))] for c in range(KEY_TILES_PER_STEP)], axis=1)
        if bm is not None:
            expand = jnp.where(blk_row == ((k0 + key_lane) >> blk_shift), 1.0, 0.0).astype(BF16)
            vis = jnp.dot(bm, expand, preferred_element_type=F32)
            add = jnp.where(vis > 0.5, 0.0, NEG_INF)
            if rep > 1:
                add = jnp.concatenate([add] * rep, axis=0)
            s = s + add
        m_old = m_scr[...]
        m_new = jnp.maximum(m_old, jnp.max(s, axis=-1, keepdims=True))
        p = jnp.where(s > HALF_NEG, jnp.exp(s - m_new), 0.0)
        alpha = jnp.exp(m_old - m_new)
        l_scr[...] = alpha * l_scr[...] + jnp.sum(p, axis=-1, keepdims=True)
        acc_scr[...] = alpha * acc_scr[...] + jnp.dot(p.astype(BF16), v, preferred_element_type=F32)
        m_scr[...] = m_new
        return carry

    lax.fori_loop(kb_lo, kb_hi, body, 0)
    l = l_scr[...]
    return acc_scr[...] / jnp.where(l > 0.0, l, 1.0)


def _causal_tile_index(n_near):
    return lambda rel: jnp.clip(rel + 1, 0, n_near)


def _window_tile_index(n_win):
    return lambda rel: jnp.where(rel >= n_win, 0, jnp.clip(rel + 1, 0, n_win))


def _rank_select(score, count, lane, n_valid):
    rank = jnp.zeros(score.shape, F32)
    for s in range(1, n_valid):
        other = pltpu.roll(score, s, axis=1)
        rank = rank + jnp.where(other > score, 1.0, 0.0) + jnp.where(other == score, jnp.where(lane >= s, 1.0, 0.0), 0.0)
    for s in range(LANES - n_valid + 1, LANES):
        other = pltpu.roll(score, s, axis=1)
        rank = rank + jnp.where(other > score, 1.0, 0.0)
    return jnp.where((rank < count) & (score > HALF_NEG), 1.0, 0.0)


def _take_topk(score, count, lane):
    sel = jnp.zeros(score.shape, F32)
    cur = score
    lane_f = lane.astype(F32)
    for _ in range(count):
        mx = jnp.max(cur, axis=-1, keepdims=True)
        idx = jnp.min(jnp.where(cur == mx, lane_f, float(score.shape[-1])), axis=-1, keepdims=True)
        hit = lane_f == idx
        sel = jnp.where(hit, jnp.where(mx > HALF_NEG, 1.0, 0.0), sel)
        cur = jnp.where(hit, KNOCKED, cur)
    return sel


def _nsa_prompt_kernel(zq_ref, gn_ref, kc_ref, vc_ref, ks_ref, vs_ref, kw_ref, vw_ref, biasc_ref, bslc_ref, bwin_ref,
                       mslc_ref, o_ref, q_scr, m_scr, l_scr, acc_scr, comb_scr, *, tq, tk, n_slc, n_slc_tiles, n_win_tiles):
    qt = pl.program_id(1)
    q0 = qt * tq
    _stack_queries_prompt(zq_ref, q_scr, tq)
    gn = jax.nn.sigmoid(gn_ref[...])
    lane = lax.broadcasted_iota(jnp.int32, (tq, LANES), 1)
    tpos = q0 + lax.broadcasted_iota(jnp.int32, (tq, LANES), 0)
    qb = tpos >> 6

    def gate_col(c, g):
        return jnp.concatenate([gn[:, c * 8 + g * HPG + h:c * 8 + g * HPG + h + 1] for h in range(HPG)], axis=0)

    kc = kc_ref[0].astype(BF16)
    vc = vc_ref[0].astype(BF16)
    for g in range(2):
        q = q_scr[g]
        s_c = lax.dot_general(q, kc, _NT, preferred_element_type=F32) + biasc_ref[g]
        mx = jnp.max(s_c, axis=-1, keepdims=True)
        p = jnp.where(s_c > HALF_NEG, jnp.exp(s_c - mx), 0.0)
        den = jnp.sum(p, axis=-1, keepdims=True)
        p_c = p / jnp.where(den > 0.0, den, 1.0)
        comb_scr[g] = gate_col(0, g) * jnp.dot(p_c.astype(BF16), vc, preferred_element_type=F32)
        imp = p_c[0:tq] + p_c[tq:2 * tq] + p_c[2 * tq:3 * tq] + p_c[3 * tq:4 * tq]
        imp_hi, imp_lo = _split_hi_lo(imp)
        p_slc = (jnp.dot(imp_hi, mslc_ref[...], preferred_element_type=F32)
                 + jnp.dot(imp_lo, mslc_ref[...], preferred_element_type=F32))
        forced = (lane == 0) | (lane == qb) | (lane == qb - 1)
        score = jnp.where(lane > qb, NEG_INF, p_slc + jnp.where(forced, FORCE_BONUS, 0.0))
        bm = lax.cond(q0 + tq <= SLC_TOPN * SLC_LEN,
                      lambda: jnp.where(score > HALF_NEG, 1.0, 0.0),
                      lambda: _rank_select(score, SLC_TOPN, lane, n_slc)).astype(BF16)
        kb_hi = qt // KEY_TILES_PER_STEP + 1
        o_s = _flash(q, ks_ref, vs_ref, bslc_ref, g, _causal_tile_index(n_slc_tiles), bm, 6, HPG, 0, kb_hi, qt,
                     m_scr, l_scr, acc_scr, tk)
        comb_scr[g] = comb_scr[g] + gate_col(1, g) * o_s
        kb_lo = jnp.maximum(qt - (n_win_tiles - 1), 0) // KEY_TILES_PER_STEP
        o_w = _flash(q, kw_ref, vw_ref, bwin_ref, g, _window_tile_index(n_win_tiles), None, 0, 1, kb_lo, kb_hi, qt,
                     m_scr, l_scr, acc_scr, tk)
        comb_scr[g] = comb_scr[g] + gate_col(2, g) * o_w
    _unstack_heads([comb_scr[0], comb_scr[1]], o_ref, tq)


def _zspec(rows, name, width, row_map):
    cb = _Z_OFF[name] // width
    assert _Z_OFF[name] % width == 0
    return pl.BlockSpec((rows, width), lambda b, i: (row_map(b, i), cb))


def _nsa_prompt(z, kc, vc, rel_table, bp, tp):
    tq, tk = ATT_TQ, ATT_TK
    nq = tp // tq
    n_cmp = tp // CMP_STRIDE - 1
    n_slc = tp // SLC_LEN
    assert tp % tq == 0 and n_cmp < LANES and n_slc <= LANES and kc.shape[1] == LANES
    tbl = rel_table[:, :NSA_HEADS].reshape(REL_BUCKETS, NSA_KV, HPG)
    t = jnp.arange(tp)
    dist_c = t[:, None] - (jnp.arange(LANES) * CMP_STRIDE + CMP_LEN - 1)[None, :]
    valid_c = (dist_c >= 0) & (jnp.arange(LANES) < n_cmp)[None, :]
    biasc = jnp.where(valid_c[..., None, None], _bias_lookup(tbl, dist_c), NEG_INF)
    biasc = jnp.transpose(biasc.reshape(nq, tq, LANES, NSA_KV, HPG), (0, 3, 4, 1, 2)).reshape(nq, NSA_KV, HPG * tq, LANES)
    slc_deltas = (0, tq, 2 * tq)
    win_deltas = tuple(range(0, WINDOW + 1, tq))
    assert tp % (tk * KEY_TILES_PER_STEP) == 0 and n_slc <= LANES // 2
    bslc = _with_masked_tile(_toeplitz_tiles(tbl, slc_deltas, tq, tk, 1 << 30))
    bwin = _with_masked_tile(_toeplitz_tiles(tbl, win_deltas, tq, tk, WINDOW))
    mslc = jnp.pad(_slc_matrix(n_cmp, n_slc), ((0, LANES - n_cmp), (0, LANES - n_slc))).astype(BF16)
    kern = functools.partial(_nsa_prompt_kernel, tq=tq, tk=tk, n_slc=n_slc, n_slc_tiles=len(slc_deltas),
                             n_win_tiles=len(win_deltas))
    seq = lambda b, i: b
    tile = lambda b, i: b * nq + i
    full4 = lambda b, i: (0, 0, 0, 0)
    return pl.pallas_call(
        kern,
        grid=(bp, nq),
        in_specs=[
            _zspec(tq, "q_n", 512, tile), _zspec(tq, "g_n", LANES, tile),
            pl.BlockSpec((1, LANES, LANES), lambda b, i: (b, 0, 0)), pl.BlockSpec((1, LANES, LANES), lambda b, i: (b, 0, 0)),
        ] + [pl.BlockSpec((tp, LANES), (lambda b, i, cb=_Z_OFF[nm] // LANES: (b, cb))) for nm in ("ks", "vs", "kw", "vw")] + [
            pl.BlockSpec((None, NSA_KV, HPG * tq, LANES), lambda b, i: (i, 0, 0, 0)),
            pl.BlockSpec(bslc.shape, full4), pl.BlockSpec(bwin.shape, full4),
            pl.BlockSpec((LANES, LANES), lambda b, i: (0, 0)),
        ],
        out_specs=pl.BlockSpec((tq, 512), lambda b, i: (b * nq + i, 0)),
        out_shape=jax.ShapeDtypeStruct((bp * tp, 512), F32),
        scratch_shapes=[
            pltpu.VMEM((NSA_KV, HPG * tq, LANES), BF16),
            pltpu.VMEM((HPG * tq, 1), F32), pltpu.VMEM((HPG * tq, 1), F32), pltpu.VMEM((HPG * tq, LANES), F32),
            pltpu.VMEM((NSA_KV, HPG * tq, LANES), F32),
        ],
        compiler_params=_params(("arbitrary", "arbitrary")),
        name="nsa_prompt",
    )(z, z, kc, vc, z, z, z, z, biasc, bslc, bwin, mslc)


def _slc_matrix(n_cmp, n_slc):
    ratio, c_sub = SLC_LEN // CMP_STRIDE, CMP_LEN // CMP_STRIDE
    m = np.zeros((n_cmp, n_slc), np.float32)
    for j in range(n_slc):
        for o in range(-(c_sub - 1), ratio):
            k = ratio * j + o
            if 0 <= k < n_cmp:
                m[k, j] += float(min(o + c_sub, ratio) - max(o, 0))
    return jnp.asarray(m)


def _moba_prompt_kernel(zq_ref, k_ref, v_ref, bias_ref, o_ref, q_scr, m_scr, l_scr, acc_scr, res_scr, *, tq, tk, nb, n_tiles):
    qt = pl.program_id(1)
    q0 = qt * tq
    _stack_queries_prompt(zq_ref, q_scr, tq)
    rows = HPG * tq
    lane = lax.broadcasted_iota(jnp.int32, (rows, LANES), 1)
    trow = q0 + (lax.broadcasted_iota(jnp.int32, (rows, LANES), 0) & (tq - 1))
    own = trow >> 8
    kmean = jnp.concatenate(
        [jnp.mean(k_ref[n * MOBA_BLOCK:(n + 1) * MOBA_BLOCK, :], axis=0, keepdims=True) for n in range(nb)]
        + [jnp.zeros((LANES - nb, LANES), F32)], axis=0)
    km_hi, km_lo = _split_hi_lo(kmean)
    for g in range(2):
        q = q_scr[g]
        gs = (lax.dot_general(q, km_hi, _NT, preferred_element_type=F32)
              + lax.dot_general(q, km_lo, _NT, preferred_element_type=F32))
        gs = jnp.where(lane < own, gs, NEG_INF)
        sel = _take_topk(gs, min(MOBA_TOPK, nb - 1), lane)
        bm = jnp.where(lane == own, 1.0, sel).astype(BF16)
        res_scr[g] = _flash(q, k_ref, v_ref, bias_ref, g, _causal_tile_index(n_tiles), bm, 8, 1, 0,
                            qt // KEY_TILES_PER_STEP + 1, qt, m_scr, l_scr, acc_scr, tk)
    _unstack_heads([res_scr[0], res_scr[1]], o_ref, tq)


def _moba_prompt(z, rel_table, bp, tp):
    tq, tk = ATT_TQ, ATT_TK
    nq = tp // tq
    nb = tp // MOBA_BLOCK
    assert tp % MOBA_BLOCK == 0 and tq & (tq - 1) == 0
    tbl = rel_table[:, NSA_HEADS:].reshape(REL_BUCKETS, MOBA_KV, HPG)
    deltas = (0, tq, 2 * tq)
    assert tp % (tk * KEY_TILES_PER_STEP) == 0
    bias = _with_masked_tile(_toeplitz_tiles(tbl, deltas, tq, tk, 1 << 30))
    kern = functools.partial(_moba_prompt_kernel, tq=tq, tk=tk, nb=nb, n_tiles=len(deltas))
    return pl.pallas_call(
        kern,
        grid=(bp, nq),
        in_specs=[
            _zspec(tq, "q_m", 512, lambda b, i: b * nq + i),
            pl.BlockSpec((tp, LANES), lambda b, i: (b, _Z_OFF["k_m"] // LANES)),
            pl.BlockSpec((tp, LANES), lambda b, i: (b, _Z_OFF["v_m"] // LANES)),
            pl.BlockSpec(bias.shape, lambda b, i: (0, 0, 0, 0)),
        ],
        out_specs=pl.BlockSpec((tq, 512), lambda b, i: (b * nq + i, 0)),
        out_shape=jax.ShapeDtypeStruct((bp * tp, 512), F32),
        scratch_shapes=[
            pltpu.VMEM((MOBA_KV, HPG * tq, LANES), BF16),
            pltpu.VMEM((HPG * tq, 1), F32), pltpu.VMEM((HPG * tq, 1), F32), pltpu.VMEM((HPG * tq, LANES), F32),
            pltpu.VMEM((MOBA_KV, HPG * tq, LANES), F32),
        ],
        compiler_params=_params(("arbitrary", "arbitrary")),
        name="moba_prompt",
    )(z, z, z, bias)


def _merge_kernel(on_ref, om_ref, ox_ref, gm_ref, x_ref, wb_ref, wo_ref, fg_ref, x1_ref, xn_ref):
    d = x_ref.shape[1]
    merged = jnp.zeros(x_ref.shape, F32)
    for c, o_ref in enumerate((on_ref, om_ref, ox_ref)):
        proj = jnp.dot(o_ref[...].astype(BF16), wb_ref[c], preferred_element_type=F32)
        merged = merged + jax.nn.sigmoid(gm_ref[:, c * d:(c + 1) * d]) * proj
    x1 = x_ref[...] + jnp.dot(merged.astype(BF16), wo_ref[...], preferred_element_type=F32)
    x1_ref[...] = x1
    ms = jnp.mean(x1 * x1, axis=-1, keepdims=True)
    xn_ref[...] = (x1 * lax.rsqrt(ms + RMS_EPS) * fg_ref[...]).astype(BF16)


def _merge(o_nsa, o_moba, o_mem, z, x, w_branch, w_out, ffn_gain, tm=256):
    n, d = x.shape
    assert n % tm == 0 and _Z_OFF["g_merge"] == 0
    row = lambda i: (i, 0)
    return pl.pallas_call(
        _merge_kernel,
        grid=(n // tm,),
        in_specs=[
            pl.BlockSpec((tm, 512), row), pl.BlockSpec((tm, 512), row), pl.BlockSpec((tm, 512), row),
            pl.BlockSpec((tm, 3 * d), row), pl.BlockSpec((tm, d), row),
            pl.BlockSpec((3, 512, d), lambda i: (0, 0, 0)), pl.BlockSpec((d, d), lambda i: (0, 0)),
            pl.BlockSpec((1, d), lambda i: (0, 0)),
        ],
        out_specs=[pl.BlockSpec((tm, d), row), pl.BlockSpec((tm, d), row)],
        out_shape=[jax.ShapeDtypeStruct((n, d), F32), jax.ShapeDtypeStruct((n, d), BF16)],
        compiler_params=_params(("arbitrary",)),
        name="merge_outproj",
    )(o_nsa, o_moba, o_mem, z, x, w_branch.astype(BF16), w_out.astype(BF16), ffn_gain[None, :])


def _peer_route_kernel(xT_ref, wqh_ref, wql_ref, skh_ref, skl_ref, s1_ref, s2_ref, tau_ref):
    xT = xT_ref[...]
    qT = jnp.dot(wqh_ref[...], xT, preferred_element_type=F32) + jnp.dot(wql_ref[...], xT, preferred_element_type=F32)
    half = HEAD_DIM

    def top_sorted(s):
        vals, cur = [], s
        for _ in range(PEER_TOPK + 1):
            mx = jnp.max(cur, axis=0, keepdims=True)
            vals.append(mx)
            cur = jnp.where(cur == mx, KNOCKED, cur)
        return vals

    taus = []
    for h in range(PEER_HEADS):
        sc = []
        for p in range(2):
            r0 = (h * 2 + p) * half
            q_hi, q_lo = _split_hi_lo(qT[r0:r0 + half, :])
            sc.append(jnp.dot(skh_ref[h, p], q_hi, preferred_element_type=F32)
                      + jnp.dot(skl_ref[h, p], q_hi, preferred_element_type=F32)
                      + jnp.dot(skh_ref[h, p], q_lo, preferred_element_type=F32))
        a, b = top_sorted(sc[0]), top_sorted(sc[1])
        n_rank = PEER_TOPK + 1
        cand = jnp.concatenate([a[i] + b[j] for i in range(n_rank) for j in range(n_rank // (i + 1))], axis=0)
        cur = cand
        for _ in range(PEER_TOPK):
            tau = jnp.max(cur, axis=0, keepdims=True)
            cur = jnp.where(cur == tau, KNOCKED, cur)
        nxt = jnp.max(cur, axis=0, keepdims=True)
        top = a[0] + b[0]
        zsum = jnp.sum(jnp.where(cand >= tau, jnp.exp(cand - top), 0.0), axis=0, keepdims=True)
        shift = top + jnp.log(zsum)
        s1_ref[h] = (sc[0] - shift) * LOG2E
        s2_ref[h] = sc[1] * LOG2E
        taus.append((0.5 * (tau + nxt) - shift) * LOG2E)
    tau_ref[...] = jnp.concatenate(taus, axis=0)


def _peer_route(xT, peer_w_q, sub_keys, tt=256):
    d, n = xT.shape
    assert n % tt == 0
    wqh, wql = _split_hi_lo(peer_w_q.T)
    skh, skl = _split_hi_lo(sub_keys)
    tok3 = lambda i: (0, 0, i)
    big = jax.ShapeDtypeStruct((PEER_HEADS, PEER_NKEYS, n), F32)
    return pl.pallas_call(
        _peer_route_kernel,
        grid=(n // tt,),
        in_specs=[
            pl.BlockSpec((d, tt), lambda i: (0, i)),
            pl.BlockSpec((d, d), lambda i: (0, 0)), pl.BlockSpec((d, d), lambda i: (0, 0)),
            pl.BlockSpec(sub_keys.shape, lambda i: (0, 0, 0, 0)), pl.BlockSpec(sub_keys.shape, lambda i: (0, 0, 0, 0)),
        ],
        out_specs=[pl.BlockSpec((PEER_HEADS, PEER_NKEYS, tt), tok3)] * 2 + [pl.BlockSpec((PEER_HEADS, tt), lambda i: (0, i))],
        out_shape=[big] * 2 + [jax.ShapeDtypeStruct((PEER_HEADS, n), F32)],
        compiler_params=_params(("arbitrary",)),
        name="peer_route",
    )(xT, wqh, wql, skh, skl)


def _peer_dense_kernel(xT_ref, u_ref, vT_ref, s1_ref, s2_ref, tau_ref, o_ref, a_ref, *, ec, tt):
    c = pl.program_id(1)

    @pl.when(c == 0)
    def _():
        o_ref[...] = jnp.zeros_like(o_ref)

    h_t = jnp.dot(u_ref[...], xT_ref[...], preferred_element_type=F32)
    for k in range(ec // LANES):
        for lc in range(0, tt, LANES):
            wk = jnp.zeros((LANES, LANES), F32)
            for h in range(PEER_HEADS):
                csum = s2_ref[h, :, lc:lc + LANES] + s1_ref[h, k:k + 1, lc:lc + LANES]
                wk = wk + jnp.where(csum >= tau_ref[h:h + 1, lc:lc + LANES], jnp.exp2(csum), 0.0)
            g = _gelu_tanh(h_t[k * LANES:(k + 1) * LANES, lc:lc + LANES])
            a_ref[k * LANES:(k + 1) * LANES, lc:lc + LANES] = (wk * g).astype(BF16)
    o_ref[...] += jnp.dot(vT_ref[...], a_ref[...], preferred_element_type=F32)


def _peer_dense(xT, u_b, vT_b, s1T, s2T, tauT, ec=1024):
    d, n = xT.shape
    tt = 512 if n % 512 == 0 else 256
    assert n % tt == 0 and ec == 8 * PEER_NKEYS
    ne = u_b.shape[0]
    kern = functools.partial(_peer_dense_kernel, ec=ec, tt=tt)
    allk = pl.BlockSpec((PEER_HEADS, PEER_NKEYS, tt), lambda i, c: (0, 0, i))
    chunk = pl.BlockSpec((PEER_HEADS, ec // PEER_NKEYS, tt), lambda i, c: (0, c, i))
    return pl.pallas_call(
        kern,
        grid=(n // tt, ne // ec),
        in_specs=[
            pl.BlockSpec((d, tt), lambda i, c: (0, i)),
            pl.BlockSpec((ec, d), lambda i, c: (c, 0)),
            pl.BlockSpec((d, ec), lambda i, c: (0, c)),
            chunk, allk,
            pl.BlockSpec((PEER_HEADS, tt), lambda i, c: (0, i)),
        ],
        out_specs=pl.BlockSpec((d, tt), lambda i, c: (0, i)),
        out_shape=jax.ShapeDtypeStruct((d, n), F32),
        scratch_shapes=[pltpu.VMEM((ec, tt), BF16)],
        compiler_params=_params(("arbitrary", "arbitrary")),
        name="peer_dense",
    )(xT, u_b, vT_b, s1T, s2T, tauT)


def _peer(x1, xn_b, peer_w_q, sub_keys, u_b, vT_b):
    xT = xn_b.T
    s1T, s2T, tauT = _peer_route(xT, peer_w_q, sub_keys)
    out_t = _peer_dense(xT, u_b, vT_b, s1T, s2T, tauT)
    return x1 + out_t.T


PAGES_PER_STEP = 16


def _page_specs(n_lead, inner):
    zeros = (0,) * len(inner)
    return [pl.BlockSpec((1,) + inner, (lambda b, j, pt, p=p: (pt[b, j * PAGES_PER_STEP + p],) + zeros))
            for p in range(n_lead)]


def _compress_paged_kernel(pt_ref, *refs, norm):
    del pt_ref
    p_n = PAGES_PER_STEP
    x_refs = refs[:p_n]
    perm_ref, wh_ref, c_ref, w2h_ref, w2l_ref, seg_ref, gain_ref, o_ref, x_scr = refs[p_n:]
    j = pl.program_id(1)
    pair_sub = 2 * x_refs[0].shape[2] // CMP_STRIDE
    for pp in range(p_n // 2):
        x_t = jnp.concatenate([x_refs[2 * pp][0], x_refs[2 * pp + 1][0]], axis=1).astype(BF16)
        rows = lax.dot_general(perm_ref[...], x_t, _NT, preferred_element_type=F32).astype(BF16)
        base = pl.multiple_of((j * (p_n // 2) + pp) * pair_sub, pair_sub)
        for l in range(CMP_STRIDE):
            x_scr[pl.ds(base, pair_sub), l * LANES:(l + 1) * LANES] = rows[l * pair_sub:(l + 1) * pair_sub, :]

    @pl.when(j == pl.num_programs(1) - 1)
    def _():
        ns = x_scr.shape[0]
        a = jnp.dot(x_scr[...], wh_ref[...], preferred_element_type=F32)
        nxt = pltpu.roll(a[:, LANES:], ns - 1, axis=0)
        h = a[:, :LANES] + nxt + c_ref[...]
        o = _dot3(_gelu_tanh(h), w2h_ref[...], w2l_ref[...])
        if norm:
            o = o * lax.rsqrt(_group_sumsq(o, seg_ref[...]) * (1.0 / HEAD_DIM) + RMS_EPS) * gain_ref[...]
        o_ref[0] = o


def _compress_paged(pool, page_table, weights, gain, norm):
    n_phys, _, page = pool.shape
    b, n_pages = page_table.shape
    sub = page // CMP_STRIDE
    kk = CMP_STRIDE * LANES
    ns = n_pages * sub
    assert n_pages % PAGES_PER_STEP == 0 and page % CMP_STRIDE == 0 and PAGES_PER_STEP % 2 == 0 and (2 * sub) % 16 == 0
    wh, _, cvec, w2h, w2l = weights
    const = lambda b_, j, pt: (0, 0)
    r = np.arange(2 * page)
    perm = jnp.asarray(((r[:, None] // (2 * sub)) + CMP_STRIDE * (r[:, None] % (2 * sub))) == r[None, :], BF16)
    grid_spec = pltpu.PrefetchScalarGridSpec(
        num_scalar_prefetch=1,
        grid=(b, n_pages // PAGES_PER_STEP),
        in_specs=_page_specs(PAGES_PER_STEP, (LANES, page)) + [
            pl.BlockSpec((2 * page, 2 * page), const), pl.BlockSpec((kk, 2 * LANES), const),
            pl.BlockSpec((1, LANES), const),
            pl.BlockSpec((LANES, LANES), const), pl.BlockSpec((LANES, LANES), const),
            pl.BlockSpec((LANES, LANES), const), pl.BlockSpec((1, LANES), const),
        ],
        out_specs=pl.BlockSpec((1, ns, LANES), lambda b_, j, pt: (b_, 0, 0)),
        scratch_shapes=[pltpu.VMEM((ns, kk), BF16)],
    )
    return pl.pallas_call(
        functools.partial(_compress_paged_kernel, norm=norm),
        grid_spec=grid_spec,
        out_shape=jax.ShapeDtypeStruct((b, ns, LANES), F32),
        compiler_params=_params(("arbitrary", "arbitrary")),
        name="nsa_compress_paged",
    )(page_table, *([pool] * PAGES_PER_STEP), perm, wh, cvec, w2h, w2l, _seg_matrix(HEAD_DIM), gain)


def _softmax_tile(s):
    mx = jnp.max(s, axis=-1, keepdims=True)
    p = jnp.where(s > HALF_NEG, jnp.exp(s - mx), 0.0)
    return mx, p, jnp.sum(p, axis=-1, keepdims=True)


def _pv(p, v, v_t):
    if v_t:
        return lax.dot_general(p.astype(BF16), v, _NT, preferred_element_type=F32)
    return jnp.dot(p.astype(BF16), v, preferred_element_type=F32)


def _online_update(s, v, m_scr, l_scr, acc_scr, v_t=False):
    m_old = m_scr[...]
    m_new = jnp.maximum(m_old, jnp.max(s, axis=-1, keepdims=True))
    p = jnp.where(s > HALF_NEG, jnp.exp(s - m_new), 0.0)
    alpha = jnp.exp(m_old - m_new)
    l_scr[...] = alpha * l_scr[...] + jnp.sum(p, axis=-1, keepdims=True)
    acc_scr[...] = alpha * acc_scr[...] + _pv(p, v, v_t)
    m_scr[...] = m_new


def _pad_rows(x, rows):
    return jnp.concatenate([x, jnp.zeros((rows - x.shape[0], x.shape[1]), x.dtype)], axis=0)


def _sample_row_bias(tbl, ts, kpos, qpos0, lo_ok):
    d = (qpos0 + jnp.arange(ts))[:, None] - kpos[None, :]
    b = jnp.where(((d >= 0) & lo_ok)[..., None, None], _bias_lookup(tbl, d), NEG_INF)
    return jnp.transpose(b, (2, 3, 0, 1)).reshape(-1, kpos.shape[0])


def _past_bias_tiles(tbl, ts, past_len, nk):
    assert nk >= LANES
    far = jnp.repeat(tbl[REL_BUCKETS - 1].reshape(-1), ts)[:, None]
    far_tile = jnp.broadcast_to(far, (far.shape[0], nk))
    near = _sample_row_bias(tbl, ts, past_len - LANES + jnp.arange(LANES), past_len, jnp.ones((ts, LANES), bool))
    return jnp.stack([far_tile, jnp.concatenate([far_tile[:, :nk - LANES], near], axis=1)])


def _nsa_sample_kernel(pt_ref, *refs, ts, past_len, slc_lanes):
    del pt_ref
    p_n = PAGES_PER_STEP
    k_pages, v_pages = refs[:p_n], refs[p_n:2 * p_n]
    (zq_ref, gn_ref, kc_ref, vc_ref, ksn_ref, vsn_ref, wk_ref, wv_ref, kwn_ref, vwn_ref,
     biasc_ref, bpast_ref, bnew_ref, bwin_ref, mslc_ref, o_ref,
     q_scr, m_scr, l_scr, acc_scr, comb_scr, bm_scr) = refs[2 * p_n:]
    j = pl.program_id(1)
    page = k_pages[0].shape[2]
    rows = NSA_KV * HPG * ts
    half = HPG * ts

    def gate_col(gn, c):
        return jnp.concatenate([gn[:, c * 8 + hd:c * 8 + hd + 1] for hd in range(NSA_HEADS)], axis=0)

    @pl.when(j == 0)
    def _():
        _stack_queries_sample(zq_ref, q_scr, ts)
        q = q_scr[...].astype(BF16)
        gn = jax.nn.sigmoid(gn_ref[...])
        s_c = lax.dot_general(q, kc_ref[0].astype(BF16), _NT, preferred_element_type=F32) + biasc_ref[...]
        mx, p, den = _softmax_tile(s_c)
        p_c = p / jnp.where(den > 0.0, den, 1.0)
        comb_scr[...] = gate_col(gn, 0) * jnp.dot(p_c.astype(BF16), vc_ref[0].astype(BF16), preferred_element_type=F32)
        imp = jnp.concatenate(
            [sum(p_c[g * half + h * ts:g * half + (h + 1) * ts] for h in range(HPG)) for g in range(NSA_KV)], axis=0)
        imp_hi, imp_lo = _split_hi_lo(imp)
        p_slc = (jnp.dot(imp_hi, mslc_ref[...], preferred_element_type=F32)
                 + jnp.dot(imp_lo, mslc_ref[...], preferred_element_type=F32))
        lane = lax.broadcasted_iota(jnp.int32, p_slc.shape, 1)
        tpos = past_len + (lax.broadcasted_iota(jnp.int32, p_slc.shape, 0) & (ts - 1))
        qb = tpos >> 6
        forced = (lane == 0) | (lane == qb) | (lane == qb - 1)
        score = jnp.where(lane > qb, NEG_INF, p_slc + jnp.where(forced, FORCE_BONUS, 0.0))
        sel = _take_topk(score, SLC_TOPN, lane)
        bm_scr[...] = jnp.concatenate(
            [sel[g * ts:(g + 1) * ts] for g in range(NSA_KV) for _ in range(HPG)], axis=0).astype(BF16)
        m_scr[...] = jnp.full(m_scr.shape, NEG_INF, F32)
        l_scr[...] = jnp.zeros(l_scr.shape, F32)
        acc_scr[...] = jnp.zeros(acc_scr.shape, F32)
        s_w = jnp.dot(q, wk_ref[0].astype(BF16), preferred_element_type=F32) + bwin_ref[...]
        _online_update(s_w, wv_ref[0].astype(BF16), m_scr, l_scr, acc_scr, v_t=True)
        s_n = lax.dot_general(q, _pad_rows(kwn_ref[...], LANES).astype(BF16), _NT, preferred_element_type=F32) + bnew_ref[...]
        _online_update(s_n, _pad_rows(vwn_ref[...], LANES).astype(BF16), m_scr, l_scr, acc_scr)
        l = l_scr[...]
        comb_scr[...] = comb_scr[...] + gate_col(gn, 2) * (acc_scr[...] / jnp.where(l > 0.0, l, 1.0))
        m_scr[...] = jnp.full(m_scr.shape, NEG_INF, F32)
        l_scr[...] = jnp.zeros(l_scr.shape, F32)
        acc_scr[...] = jnp.zeros(acc_scr.shape, F32)

    q = q_scr[...].astype(BF16)
    nk = p_n * page
    k_t = jnp.concatenate([r[0].astype(BF16) for r in k_pages], axis=1)
    v_t = jnp.concatenate([r[0].astype(BF16) for r in v_pages], axis=1)
    blk_row = lax.broadcasted_iota(jnp.int32, (slc_lanes, nk), 0)
    key_pos = j * nk + lax.broadcasted_iota(jnp.int32, (slc_lanes, nk), 1)
    expand = jnp.where(blk_row == (key_pos >> 6), 1.0, 0.0).astype(BF16)
    vis = jnp.dot(bm_scr[...], expand, preferred_element_type=F32)
    last = pl.num_programs(1) - 1
    s = (jnp.dot(q, k_t, preferred_element_type=F32) + bpast_ref[jnp.where(j == last, 1, 0)]
         + jnp.where(vis > 0.5, 0.0, NEG_INF))
    _online_update(s, v_t, m_scr, l_scr, acc_scr, v_t=True)

    @pl.when(j == pl.num_programs(1) - 1)
    def _():
        gn = jax.nn.sigmoid(gn_ref[...])
        s_n = lax.dot_general(q, _pad_rows(ksn_ref[...], LANES).astype(BF16), _NT, preferred_element_type=F32) + bnew_ref[...]
        _online_update(s_n, _pad_rows(vsn_ref[...], LANES).astype(BF16), m_scr, l_scr, acc_scr)
        l = l_scr[...]
        comb = comb_scr[...] + gate_col(gn, 1) * (acc_scr[...] / jnp.where(l > 0.0, l, 1.0))
        _unstack_heads([comb[:half], comb[half:]], o_ref, ts)


def _nsa_sample(z, n_p, kc, vc, pool_k, pool_v, win_k, win_v, page_table, rel_table, ts):
    bs, n_pages = page_table.shape
    n_phys, _, page = pool_k.shape
    past_len = n_pages * page
    p_n = PAGES_PER_STEP
    n_steps = n_pages // p_n
    nk = p_n * page
    n_sub = kc.shape[1]
    n_cmp = n_sub - 1
    n_slc = -(-(past_len + ts) // SLC_LEN)
    slc_lanes = -(-n_slc // LANES) * LANES
    win_len = win_k.shape[2]
    assert n_pages % p_n == 0 and ts & (ts - 1) == 0 and ts <= SLC_LEN and past_len % SLC_LEN == 0 and n_p % ts == 0
    assert n_sub % LANES == 0 and win_len + ts >= WINDOW
    tbl = rel_table[:, :NSA_HEADS].reshape(REL_BUCKETS, NSA_KV, HPG)
    rows = NSA_HEADS * ts
    cmp_end = jnp.arange(n_sub) * CMP_STRIDE + CMP_LEN - 1
    biasc = _sample_row_bias(tbl, ts, cmp_end, past_len, jnp.broadcast_to(jnp.arange(n_sub) < n_cmp, (ts, n_sub)))
    bpast = _past_bias_tiles(tbl, ts, past_len, nk)
    bnew = _sample_row_bias(tbl, ts, past_len + jnp.arange(LANES), past_len, jnp.broadcast_to(jnp.arange(LANES) < ts, (ts, LANES)))
    wpos = past_len - win_len + jnp.arange(win_len)
    in_win = ((past_len + jnp.arange(ts))[:, None] - wpos[None, :]) < WINDOW
    bwin = _sample_row_bias(tbl, ts, wpos, past_len, in_win)
    mslc = jnp.pad(_slc_matrix(n_cmp, n_slc), ((0, n_sub - n_cmp), (0, slc_lanes - n_slc))).astype(BF16)
    row_blk = n_p // ts
    zrow = lambda name, width: pl.BlockSpec((ts, width), (lambda b, j, pt, cb=_Z_OFF[name] // width: (row_blk + b, cb)))
    seq3 = lambda shape: pl.BlockSpec((1,) + shape, lambda b, j, pt: (b, 0, 0))
    const2 = lambda shape: pl.BlockSpec(shape, lambda b, j, pt: (0, 0))
    grid_spec = pltpu.PrefetchScalarGridSpec(
        num_scalar_prefetch=1,
        grid=(bs, n_steps),
        in_specs=_page_specs(p_n, (LANES, page)) + _page_specs(p_n, (LANES, page)) + [
            zrow("q_n", 512), zrow("g_n", LANES),
            seq3((n_sub, LANES)), seq3((n_sub, LANES)),
            zrow("ks", LANES), zrow("vs", LANES),
            seq3((LANES, win_len)), seq3((LANES, win_len)),
            zrow("kw", LANES), zrow("vw", LANES),
            const2((rows, n_sub)),
            pl.BlockSpec((2, rows, nk), lambda b, j, pt: (0, 0, 0)),
            const2((rows, LANES)), const2((rows, win_len)), const2((n_sub, slc_lanes)),
        ],
        out_specs=pl.BlockSpec((ts, 512), lambda b, j, pt: (b, 0)),
        scratch_shapes=[
            pltpu.VMEM((rows, LANES), F32),
            pltpu.VMEM((rows,1), F32), pltpu.VMEM((rows, 1), F32), pltpu.VMEM((rows, LANES), F32),
            pltpu.VMEM((rows, LANES), F32), pltpu.VMEM((rows, slc_lanes), BF16),
        ],
    )
    return pl.pallas_call(
        functools.partial(_nsa_sample_kernel, ts=ts, past_len=past_len, slc_lanes=slc_lanes),
        grid_spec=grid_spec,
        out_shape=jax.ShapeDtypeStruct((bs * ts, 512), F32),
        compiler_params=_params(("arbitrary", "arbitrary")),
        name="nsa_sample",
    )(page_table, *([pool_k] * p_n), *([pool_v] * p_n), z, z, kc, vc, z, z, win_k, win_v, z, z,
      biasc, bpast, bnew, bwin, mslc)


def _moba_sample_kernel(pt_ref, *refs, ts, n_past_blocks):
    del pt_ref
    p_n = PAGES_PER_STEP
    k_pages, v_pages = refs[:p_n], refs[p_n:2 * p_n]
    (zq_ref, kn_ref, vn_ref, bpast_ref, bnew_ref, o_ref, q_scr, m_all, l_all, acc_all, km_scr) = refs[2 * p_n:]
    j = pl.program_id(1)
    page = k_pages[0].shape[2]
    ppb = MOBA_BLOCK // page
    bps = p_n // ppb
    rows = MOBA_KV * HPG * ts
    half = HPG * ts
    lane = lax.broadcasted_iota(jnp.int32, (rows, LANES), 1)
    kcol = lax.broadcasted_iota(jnp.int32, (LANES, LANES), 1)
    last = pl.num_programs(1) - 1

    @pl.when(j == 0)
    def _():
        _stack_queries_sample(zq_ref, q_scr, ts)
        m_all[...] = jnp.full(m_all.shape, NEG_INF, F32)
        l_all[...] = jnp.zeros(l_all.shape, F32)
        km_scr[...] = jnp.zeros(km_scr.shape, F32)

    q = q_scr[...].astype(BF16)
    for i in range(bps):
        n = j * bps + i
        k_t = jnp.concatenate([k_pages[i * ppb + r][0] for r in range(ppb)], axis=1)
        v_t = jnp.concatenate([v_pages[i * ppb + r][0].astype(BF16) for r in range(ppb)], axis=1)
        s = (jnp.dot(q, k_t.astype(BF16), preferred_element_type=F32)
             + bpast_ref[jnp.where(j == last, 1, 0), :, i * MOBA_BLOCK:(i + 1) * MOBA_BLOCK])
        mx, p, den = _softmax_tile(s)
        m_all[...] = jnp.where(lane == n, mx, m_all[...])
        l_all[...] = jnp.where(lane == n, den, l_all[...])
        acc_all[n] = _pv(p, v_t, True)
        km_scr[...] = jnp.where(kcol == n, jnp.mean(k_t, axis=1, keepdims=True), km_scr[...])

    @pl.when(j == pl.num_programs(1) - 1)
    def _():
        s_o = lax.dot_general(q, _pad_rows(kn_ref[...], LANES).astype(BF16), _NT, preferred_element_type=F32) + bnew_ref[...]
        m_o, p_o, l_o = _softmax_tile(s_o)
        acc_o = jnp.dot(p_o.astype(BF16), _pad_rows(vn_ref[...], LANES).astype(BF16), preferred_element_type=F32)
        km_hi, km_lo = _split_hi_lo(km_scr[...])
        gs = jnp.dot(q, km_hi, preferred_element_type=F32) + jnp.dot(q, km_lo, preferred_element_type=F32)
        gs = jnp.where(lane < n_past_blocks, gs, NEG_INF)
        sel = _take_topk(gs, min(MOBA_TOPK, n_past_blocks), lane)
        m_sel = jnp.where(sel > 0.5, m_all[...], NEG_INF)
        m_fin = jnp.maximum(jnp.max(m_sel, axis=-1, keepdims=True), m_o)
        wgt = jnp.where(sel > 0.5, jnp.exp(m_sel - m_fin), 0.0)
        w_o = jnp.exp(m_o - m_fin)
        den = jnp.sum(wgt * l_all[...], axis=-1, keepdims=True) + w_o * l_o
        num = w_o * acc_o
        for n in range(n_past_blocks):
            num = num + wgt[:, n:n + 1] * acc_all[n]
        res = num / den
        _unstack_heads([res[:half], res[half:]], o_ref, ts)


def _moba_sample(z, n_p, pool_k, pool_v, page_table, rel_table, ts):
    bs, n_pages = page_table.shape
    n_phys, _, page = pool_k.shape
    past_len = n_pages * page
    p_n = PAGES_PER_STEP
    n_steps = n_pages // p_n
    nk = p_n * page
    nb_past = past_len // MOBA_BLOCK
    assert (MOBA_BLOCK % page == 0 and p_n % (MOBA_BLOCK // page) == 0 and past_len % MOBA_BLOCK == 0
            and ts <= MOBA_BLOCK and nb_past <= LANES and n_pages % p_n == 0 and n_p % ts == 0)
    tbl = rel_table[:, NSA_HEADS:].reshape(REL_BUCKETS, MOBA_KV, HPG)
    rows = MOBA_HEADS * ts
    bpast = _past_bias_tiles(tbl, ts, past_len, nk)
    bnew = _sample_row_bias(tbl, ts, past_len + jnp.arange(LANES), past_len, jnp.broadcast_to(jnp.arange(LANES) < ts, (ts, LANES)))
    row_blk = n_p // ts
    zrow = lambda name, width: pl.BlockSpec((ts, width), (lambda b, j, pt, cb=_Z_OFF[name] // width: (row_blk + b, cb)))
    grid_spec = pltpu.PrefetchScalarGridSpec(
        num_scalar_prefetch=1,
        grid=(bs, n_steps),
        in_specs=_page_specs(p_n, (LANES, page)) + _page_specs(p_n, (LANES, page)) + [
            zrow("q_m", 512), zrow("k_m", LANES), zrow("v_m", LANES),
            pl.BlockSpec((2, rows, nk), lambda b, j, pt: (0, 0, 0)),
            pl.BlockSpec((rows, LANES), lambda b, j, pt: (0, 0)),
        ],
        out_specs=pl.BlockSpec((ts, 512), lambda b, j, pt: (b, 0)),
        scratch_shapes=[
            pltpu.VMEM((rows, LANES), F32),
            pltpu.VMEM((rows,LANES), F32), pltpu.VMEM((rows, LANES), F32),
            pltpu.VMEM((nb_past, rows, LANES), F32), pltpu.VMEM((LANES, LANES), F32),
        ],
    )
    return pl.pallas_call(
        functools.partial(_moba_sample_kernel, ts=ts, n_past_blocks=nb_past),
        grid_spec=grid_spec,
        out_shape=jax.ShapeDtypeStruct((bs * ts, 512), F32),
        compiler_params=_params(("arbitrary", "arbitrary")),
        name="moba_sample",
    )(page_table, *([pool_k] * p_n), *([pool_v] * p_n), z, z, z, bpast, bnew)


def _memkv_kernel(m_ref, gain_ref, w_ref, kgain_ref, seg_ref, k_ref, v_ref):
    x = m_ref[...]
    xb = (x * lax.rsqrt(jnp.mean(x * x, axis=-1, keepdims=True) + RMS_EPS) * gain_ref[...]).astype(BF16)
    kvw = k_ref.shape[1]
    k = jnp.dot(xb, w_ref[:, :kvw], preferred_element_type=F32)
    for h in range(MEM_HEADS):
        kh = k[:, h * MEM_HEAD_DIM:(h + 1) * MEM_HEAD_DIM]
        ss = _group_sumsq(kh, seg_ref[...])
        k_ref[:, h * MEM_HEAD_DIM:(h + 1) * MEM_HEAD_DIM] = kh * lax.rsqrt(ss * (1.0 / MEM_HEAD_DIM) + RMS_EPS) * kgain_ref[...]
    v_ref[...] = jnp.dot(xb, w_ref[:, kvw:], preferred_element_type=F32)


def _memory_kv_pallas(mem, norm_gain, w_mem_kv, k_gain, tm=256):
    n, d = mem.shape
    kvw = w_mem_kv.shape[1] // 2
    assert n % tm == 0
    row = lambda i: (i, 0)
    const = lambda i: (0, 0)
    return pl.pallas_call(
        _memkv_kernel,
        grid=(n // tm,),
        in_specs=[pl.BlockSpec((tm, d), row), pl.BlockSpec((1, d), const), pl.BlockSpec((d, 2 * kvw), const),
                  pl.BlockSpec((1, MEM_HEAD_DIM), const), pl.BlockSpec((LANES, LANES), const)],
        out_specs=[pl.BlockSpec((tm, kvw), row), pl.BlockSpec((tm, kvw), row)],
        out_shape=[jax.ShapeDtypeStruct((n, kvw), F32)] * 2,
        compiler_params=_params(("arbitrary",)),
        name="memory_kv",
    )(mem, norm_gain[None, :], w_mem_kv.astype(BF16), k_gain[None, :], _seg_matrix(MEM_HEAD_DIM))


def _mem_attend_kernel(zq_ref, mk_ref, mv_ref, o_ref):
    for h in range(MEM_HEADS):
        sl = slice(h * MEM_HEAD_DIM, (h + 1) * MEM_HEAD_DIM)
        q = (zq_ref[:, sl] * (MEM_HEAD_DIM ** -0.5)).astype(BF16)
        s = lax.dot_general(q, mk_ref[0, :, sl].astype(BF16), _NT, preferred_element_type=F32)
        p = jnp.exp(s - jnp.max(s, axis=-1, keepdims=True))
        o = jnp.dot(p.astype(BF16), mv_ref[0, :, sl].astype(BF16), preferred_element_type=F32)
        o_ref[:, sl] = o / jnp.sum(p, axis=-1, keepdims=True)


def _mem_attend_pallas(z, row0, n_seq, t_seq, tq, mk, mv):
    nq = t_seq // tq
    assert t_seq % tq == 0 and row0 % tq == 0
    m_len = mk.shape[1]
    width = MEM_HEADS * MEM_HEAD_DIM
    cb = _Z_OFF["q_x"] // width
    return pl.pallas_call(
        _mem_attend_kernel,
        grid=(n_seq, nq),
        in_specs=[pl.BlockSpec((tq, width), lambda b, i: (row0 // tq + b * nq + i, cb)),
                  pl.BlockSpec((1, m_len, width), lambda b, i: (b, 0, 0)),
                  pl.BlockSpec((1, m_len, width), lambda b, i: (b, 0, 0))],
        out_specs=pl.BlockSpec((tq, width), lambda b, i: (b * nq + i, 0)),
        out_shape=jax.ShapeDtypeStruct((n_seq * t_seq, width), F32),
        compiler_params=_params(("arbitrary", "arbitrary")),
        name="mem_attend",
    )(z, mk, mv)


def _compress_dense(x_raw, pe, w1, b1, w2):
    b, tk, _ = x_raw.shape
    n_sub = tk // CMP_STRIDE
    n_cmp = n_sub - 1
    xs = x_raw[:, :n_sub * CMP_STRIDE].reshape(b, n_sub, CMP_STRIDE * LANES)
    w1r = w1.reshape(2, CMP_STRIDE, HEAD_DIM, -1)
    wbig = jnp.einsum("rlde,gh->lgdrhe", w1r, jnp.eye(2, dtype=F32)).reshape(CMP_STRIDE * LANES, 2 * LANES)
    a = jnp.dot(xs, wbig, precision=HI)
    pos = jnp.einsum("rld,rlde->e", pe.reshape(2, CMP_STRIDE, HEAD_DIM), w1r, precision=HI)
    h = a[:, :n_cmp, :LANES] + a[:, 1:, LANES:] + jnp.tile(b1 + pos, 2)
    w2big = jnp.einsum("ed,gh->gehd", w2, jnp.eye(2, dtype=F32)).reshape(LANES, LANES)
    return jnp.dot(jax.nn.gelu(h), w2big, precision=HI)


def _nsa_dense(q, glog, kc_raw, vc_raw, ks, vs, kw_all, vw_all, q_start, win_start, w):
    b, t_q, _ = q.shape
    tk = kc_raw.shape[1]
    scale = HEAD_DIM ** -0.5
    qg = q.reshape(b, t_q, NSA_KV, HPG, HEAD_DIM)
    t = q_start + jnp.arange(t_q)
    tbl = w["rel_bias_table"][:, :NSA_HEADS].reshape(REL_BUCKETS, NSA_KV, HPG)

    kc = _compress_dense(kc_raw, w["cmp_pos_emb"][0], w["cmp_w1"][0], w["cmp_b1"][0], w["cmp_w2"][0])
    n_cmp = kc.shape[1]
    kc = _rms(kc.reshape(b, n_cmp, NSA_KV, HEAD_DIM), w["nsa_k_gain"][0])
    vc = _compress_dense(vc_raw, w["cmp_pos_emb"][1], w["cmp_w1"][1], w["cmp_b1"][1], w["cmp_w2"][1])
    vc = vc.reshape(b, n_cmp, NSA_KV, HEAD_DIM)
    cmp_end = jnp.arange(n_cmp) * CMP_STRIDE + CMP_LEN - 1
    dist_c = t[:, None] - cmp_end[None, :]
    bias_c = jnp.transpose(tbl[_rel_bucket(dist_c)], (2, 3, 0, 1))
    s_c = jnp.einsum("btghd,bngd->bghtn", qg, kc, precision=HI) * scale + bias_c
    p_c = _masked_softmax(s_c, dist_c >= 0)
    o_c = jnp.einsum("bghtn,bngd->btghd", p_c, vc, precision=HI)

    imp = jnp.transpose(p_c.sum(axis=2), (0, 2, 1, 3))
    n_slc = -(-tk // SLC_LEN)
    p_slc = jnp.dot(imp, _slc_matrix(n_cmp, n_slc), precision=HI)
    blk = jnp.arange(n_slc)[None, :]
    qb = (t // SLC_LEN)[:, None]
    forced = ((blk == 0) | (blk == qb) | (blk == qb - 1)).astype(F32)
    future = blk > qb
    score = jnp.where(future[None, :, None, :], NEG_INF, p_slc + FORCE_BONUS * forced[None, :, None, :])
    k_sel = min(SLC_TOPN, n_slc)
    sel_val, sel_idx = lax.top_k(score, k_sel)
    sel_ok = sel_val > HALF_NEG
    selm = (jax.nn.one_hot(sel_idx, n_slc, dtype=F32) * sel_ok[..., None].astype(F32)).sum(axis=-2) > 0.5

    kpos = jnp.arange(tk)
    d_s = t[:, None] - kpos[None, :]
    bias_s = jnp.transpose(tbl[_rel_bucket(d_s)], (2, 3, 0, 1))
    ksg = ks.reshape(b, tk, NSA_KV, HEAD_DIM)
    vsg = vs.reshape(b, tk, NSA_KV, HEAD_DIM)
    s_s = jnp.einsum("btghd,bkgd->bghtk", qg, ksg, precision=HI) * scale + bias_s
    m_s = jnp.repeat(jnp.transpose(selm, (0, 2, 1, 3)), SLC_LEN, axis=-1)[..., :tk] & (d_s >= 0)
    p_s = _masked_softmax(s_s, m_s[:, :, None])
    o_s = jnp.einsum("bghtk,bkgd->btghd", p_s, vsg, precision=HI)

    tw = kw_all.shape[1]
    wpos = win_start + jnp.arange(tw)
    d_w = t[:, None] - wpos[None, :]
    m_w = (wpos[None, :] >= 0) & (d_w >= 0) & (d_w < WINDOW)
    bias_w = jnp.transpose(tbl[_rel_bucket(d_w)], (2, 3, 0, 1))
    kwg = kw_all.reshape(b, tw, NSA_KV, HEAD_DIM)
    vwg = vw_all.reshape(b, tw, NSA_KV, HEAD_DIM)
    s_w = jnp.einsum("btghd,bkgd->bghtk", qg, kwg, precision=HI) * scale + bias_w
    p_w = _masked_softmax(s_w, m_w)
    o_w = jnp.einsum("bghtk,bkgd->btghd", p_w, vwg, precision=HI)

    g = jax.nn.sigmoid(glog).reshape(b, t_q, 3, NSA_KV, HPG)[..., None]
    o = g[:, :, 0] * o_c + g[:, :, 1] * o_s + g[:, :, 2] * o_w
    return o.reshape(b, t_q, NSA_HEADS * HEAD_DIM)


def _moba_dense(q, k, v, q_start, rel_table):
    b, t_q, _ = q.shape
    tk = k.shape[1]
    scale = HEAD_DIM ** -0.5
    qg = q.reshape(b, t_q, MOBA_KV, HPG, HEAD_DIM)
    kg = k.reshape(b, tk, MOBA_KV, HEAD_DIM)
    vg = v.reshape(b, tk, MOBA_KV, HEAD_DIM)
    t = q_start + jnp.arange(t_q)
    tbl = rel_table[:, NSA_HEADS:].reshape(REL_BUCKETS, MOBA_KV, HPG)
    nb = -(-tk // MOBA_BLOCK)
    pad = nb * MOBA_BLOCK - tk
    kp = jnp.pad(kg, ((0, 0), (0, pad), (0, 0), (0, 0)))
    n_top = min(MOBA_TOPK, nb - 1)
    kpos = jnp.arange(tk)
    kblk = kpos // MOBA_BLOCK
    d = t[:, None] - kpos[None, :]
    m_own = (kblk[None, :] == (t // MOBA_BLOCK)[:, None]) & (d >= 0)
    mask = jnp.broadcast_to(m_own[None, None, None], (b, MOBA_KV, HPG, t_q, tk))
    if n_top > 0:
        kmean = jnp.mean(kp.reshape(b, nb, MOBA_BLOCK, MOBA_KV, HEAD_DIM), axis=2)
        gs = jnp.einsum("btghd,bngd->btghn", qg, kmean, precision=HI)
        past = jnp.arange(nb)[None, :] < (t // MOBA_BLOCK)[:, None]
        gs = jnp.where(past[None, :, None, None, :], gs, NEG_INF)
        top_val, top_idx = lax.top_k(gs, n_top)
        top_ok = top_val > HALF_NEG
        selm = (jax.nn.one_hot(top_idx, nb, dtype=F32) * top_ok[..., None].astype(F32)).sum(axis=-2) > 0.5
        mask = mask | jnp.repeat(jnp.transpose(selm, (0, 2, 3, 1, 4)), MOBA_BLOCK, axis=-1)[..., :tk]
    bias = jnp.transpose(tbl[_rel_bucket(d)], (2, 3, 0, 1))
    s = jnp.einsum("btghd,bkgd->bghtk", qg, kg, precision=HI) * scale + bias
    p = _masked_softmax(s, mask)
    o = jnp.einsum("bghtk,bkgd->btghd", p, vg, precision=HI)
    return o.reshape(b, t_q, MOBA_HEADS * HEAD_DIM)


def _mem_attend(q, mk, mv):
    b, t_q, _ = q.shape
    qh = q.reshape(b, t_q, MEM_HEADS, MEM_HEAD_DIM)
    s = jnp.einsum("bthd,bmhd->bhtm", qh, mk, precision=HI) * (MEM_HEAD_DIM ** -0.5)
    p = jax.nn.softmax(s, axis=-1)
    return jnp.einsum("bhtm,bmhd->bthd", p, mv, precision=HI).reshape(b, t_q, MEM_HEADS * MEM_HEAD_DIM)


def _memory_kv(mem, w):
    b, m, _ = mem.shape
    kv = jnp.dot(_rms(mem, w["mem_norm_gain"]), w["w_mem_kv"], precision=HI)
    k, v = jnp.split(kv, 2, axis=-1)
    k = _rms(k.reshape(b, m, MEM_HEADS, MEM_HEAD_DIM), w["mem_k_gain"])
    return k, v.reshape(b, m, MEM_HEADS, MEM_HEAD_DIM)


def _gather_pages(pool, page_table):
    g = pool[page_table]
    return g.reshape(page_table.shape[0], -1, LANES)


def _zcol(z, name, width):
    return z[..., _Z_OFF[name]:_Z_OFF[name] + width]


def kernel(x_prompt, x_sample, cache_cmp_k, cache_cmp_v, cache_slc_k, cache_slc_v, cache_moba_k, cache_moba_v, cache_win_k, cache_win_v, cache_mem_k, cache_mem_v, page_table, mem_prompt, attn_norm_gain, w_in, nsa_q_gain, nsa_k_gain, cmp_pos_emb, cmp_w1, cmp_b1, cmp_w2, moba_q_gain, moba_k_gain, mem_norm_gain, w_mem_kv, mem_q_gain, mem_k_gain, rel_bias_table, w_branch, w_out, ffn_norm_gain, peer_w_q, peer_sub_keys, peer_u, peer_v):
    assert w_in.shape[0] == 1
    l = 0
    bp, tp, d = x_prompt.shape
    bs, ts, _ = x_sample.shape
    past_len = page_table.shape[1] * cache_cmp_k.shape[2]
    n_p, n_s = bp * tp, bs * ts

    x_all = jnp.concatenate([x_prompt.reshape(n_p, d), x_sample.reshape(n_s, d)], axis=0)
    hgain = _head_gain_row({"nsa_q": nsa_q_gain[l], "nsa_k1": nsa_k_gain[l, 1], "nsa_k2": nsa_k_gain[l, 2],
                            "moba_q": moba_q_gain[l], "moba_k": moba_k_gain[l], "mem_q": mem_q_gain[l]})
    z = _inproj(x_all, attn_norm_gain[l][None, :], _pack_w_in(w_in[l]), hgain)
    zp, zs = "prompt", "sample"

    def rows(group, name):
        col = _zcol(z, name, LANES)
        return col[:n_p].reshape(bp, tp, LANES) if group == zp else col[n_p:].reshape(bs, ts, LANES)

    kc_rows_p, vc_rows_p = rows(zp, "kc"), rows(zp, "vc")
    wk = _compress_weights(cmp_pos_emb[l, 0], cmp_w1[l, 0], cmp_b1[l, 0], cmp_w2[l, 0])
    wv = _compress_weights(cmp_pos_emb[l, 1], cmp_w1[l, 1], cmp_b1[l, 1], cmp_w2[l, 1])
    sub = lambda r: r.reshape(bp, tp // CMP_STRIDE, CMP_STRIDE * LANES)
    kgain = jnp.tile(nsa_k_gain[l, 0], 2)[None, :]
    kc_p = _compress(sub(kc_rows_p), wk, kgain, True)
    vc_p = _compress(sub(vc_rows_p), wv, kgain, False)
    o_nsa_p = _nsa_prompt(z, kc_p, vc_p, rel_bias_table, bp, tp)
    o_moba_p = _moba_prompt(z, rel_bias_table, bp, tp)
    m_len = mem_prompt.shape[1]
    mk_p, mv_p = _memory_kv_pallas(mem_prompt.reshape(bp * m_len, d), mem_norm_gain[l], w_mem_kv[l], mem_k_gain[l])
    mem_w = MEM_HEADS * MEM_HEAD_DIM
    o_mem_p = _mem_attend_pallas(z, 0, bp, tp, 256, mk_p.reshape(bp, m_len, mem_w), mv_p.reshape(bp, m_len, mem_w))
    mk_p = mk_p.reshape(bp, m_len, MEM_HEADS, MEM_HEAD_DIM)
    mv_p = mv_p.reshape(bp, m_len, MEM_HEADS, MEM_HEAD_DIM)

    keys_on_lanes = lambda c: jnp.transpose(c[l], (0, 2, 3, 1)).reshape(c.shape[1], LANES, c.shape[2])
    kc_s = _compress_paged(keys_on_lanes(cache_cmp_k), page_table, wk, kgain, True)
    vc_s = _compress_paged(keys_on_lanes(cache_cmp_v), page_table, wv, kgain, False)
    win_len = cache_win_k.shape[2]
    win_k = cache_win_k[l].reshape(bs, win_len, LANES)
    win_v = cache_win_v[l].reshape(bs, win_len, LANES)
    o_nsa_s = _nsa_sample(z, n_p, kc_s, vc_s, keys_on_lanes(cache_slc_k), keys_on_lanes(cache_slc_v),
                          keys_on_lanes(cache_win_k), keys_on_lanes(cache_win_v), page_table, rel_bias_table, ts)
    o_moba_s = _moba_sample(z, n_p, keys_on_lanes(cache_moba_k), keys_on_lanes(cache_moba_v), page_table,
                            rel_bias_table, ts)
    o_mem_s = _mem_attend_pallas(z, n_p, bs, ts, ts, cache_mem_k[l].reshape(bs, -1, mem_w), cache_mem_v[l].reshape(bs, -1, mem_w))
    kw_all = jnp.concatenate([win_k, rows(zs, "kw")], axis=1)
    vw_all = jnp.concatenate([win_v, rows(zs, "vw")], axis=1)

    cat = lambda a, b_: jnp.concatenate([a.reshape(n_p, 512), b_.reshape(n_s, 512)], axis=0)
    x1, xn_b = _merge(cat(o_nsa_p, o_nsa_s), cat(o_moba_p, o_moba_s), cat(o_mem_p, o_mem_s), z, x_all,
                      w_branch[l], w_out[l], ffn_norm_gain[l])
    y = _peer(x1, xn_b, peer_w_q[l], peer_sub_keys[l], peer_u[l].astype(BF16), peer_v[l].astype(BF16).T)
    y_p = y[:n_p].reshape(bp, tp, d)
    y_s = y[n_p:].reshape(bs, ts, d)

    def st(r, b_, t_):
        return r.reshape(1, b_, t_, 2, HEAD_DIM)

    keep_p = min(WINDOW, tp)
    p_win_k = st(rows(zp, "kw")[:, tp - keep_p:], bp, keep_p)
    p_win_v = st(rows(zp, "vw")[:, tp - keep_p:], bp, keep_p)
    keep_s = min(WINDOW, past_len + ts)
    s_win_k = st(kw_all[:, kw_all.shape[1] - keep_s:], bs, keep_s)
    s_win_v = st(vw_all[:, vw_all.shape[1] - keep_s:], bs, keep_s)
    return (y_p, y_s,
            st(kc_rows_p, bp, tp), st(vc_rows_p, bp, tp), st(rows(zp, "ks"), bp, tp), st(rows(zp, "vs"), bp, tp),
            st(rows(zp, "k_m"), bp, tp), st(rows(zp, "v_m"), bp, tp), p_win_k, p_win_v, mk_p[None], mv_p[None],
            st(rows(zs, "kc"), bs, ts), st(rows(zs, "vc"), bs, ts), st(rows(zs, "ks"), bs, ts), st(rows(zs, "vs"), bs, ts),
            st(rows(zs, "k_m"), bs, ts), st(rows(zs, "v_m"), bs, ts), s_win_k, s_win_v)
```

```python
import functools
import math

import jax
import jax.numpy as jnp
import numpy as np
from jax import lax
from jax.experimental import pallas as pl
from jax.experimental.pallas import tpu as pltpu

HEAD_DIM = 64
NSA_HEADS = 8
NSA_KV = 2
HPG = 4
CMP_LEN = 32
CMP_STRIDE = 16
SLC_LEN = 64
SLC_TOPN = 16
WINDOW = 512
FORCE_BONUS = 1000.0
MOBA_HEADS = 8
MOBA_KV = 2
MOBA_BLOCK = 256
MOBA_TOPK = 3
MEM_HEADS = 4
MEM_HEAD_DIM = 128
REL_BUCKETS = 32
REL_MAX_DIST = 128
PEER_HEADS = 8
PEER_NKEYS = 128
PEER_TOPK = 16
RMS_EPS = 1e-6
NEG_INF = -1e30
HALF_NEG = 0.5 * NEG_INF
KNOCKED = -3.0e38
LOG2E = 1.4426950408889634
LANES = 128
SUBLANES = 8
VMEM_LIMIT = 56 * 1024 * 1024
ATT_TQ = 128
ATT_TK = 128

F32 = jnp.float32
BF16 = jnp.bfloat16
HI = lax.Precision.HIGHEST
_NT = (((1,), (1,)), ((), ()))


def _rms(x, g):
    return x * lax.rsqrt(jnp.mean(x * x, axis=-1, keepdims=True) + RMS_EPS) * g


def _rel_bucket(dist):
    n = jnp.maximum(dist, 0)
    exact = REL_BUCKETS // 2
    nf = jnp.maximum(n, 1).astype(F32)
    large = exact + (jnp.log(nf / exact) / math.log(REL_MAX_DIST / exact) * (REL_BUCKETS - exact)).astype(jnp.int32)
    return jnp.where(n < exact, n, jnp.minimum(large, REL_BUCKETS - 1))


def _bias_lookup(tbl, dist):
    onehot = jax.nn.one_hot(_rel_bucket(dist), REL_BUCKETS, dtype=F32)
    return jnp.tensordot(onehot, tbl, axes=1, precision=HI)


def _masked_softmax(s, mask):
    s = jnp.where(mask, s, NEG_INF)
    return jnp.where(mask, jax.nn.softmax(s, axis=-1), 0.0)


def _gelu_tanh(x):
    inner = x * (0.7978845608028654 + 0.035677408136300125 * (x * x))
    return (0.5 * x) * (1.0 + jnp.tanh(inner))


def _split_hi_lo(x):
    hi = x.astype(BF16)
    lo = (x - hi.astype(F32)).astype(BF16)
    return hi, lo


def _dot3(a, b_hi, b_lo):
    a_hi, a_lo = _split_hi_lo(a)
    return (jnp.dot(a_hi, b_hi, preferred_element_type=F32) + jnp.dot(a_lo, b_hi, preferred_element_type=F32)
            + jnp.dot(a_hi, b_lo, preferred_element_type=F32))


def _seg_matrix(hd):
    i = np.arange(LANES)
    return jnp.asarray((i[:, None] // hd) == (i[None, :] // hd), BF16)


def _group_sumsq(z, seg):
    sq_hi, sq_lo = _split_hi_lo(z * z)
    return jnp.dot(sq_hi, seg, preferred_element_type=F32) + jnp.dot(sq_lo, seg, preferred_element_type=F32)


def _params(sem):
    return pltpu.CompilerParams(dimension_semantics=sem, vmem_limit_bytes=VMEM_LIMIT)


_Z_GROUPS = (
    ("g_merge", 3072, 0), ("q_n", 512, 64), ("q_m", 512, 64), ("q_x", 512, 128),
    ("kc", 128, 0), ("vc", 128, 0), ("ks", 128, 64), ("vs", 128, 0), ("kw", 128, 64), ("vw", 128, 0),
    ("k_m", 128, 64), ("v_m", 128, 0), ("g_n", 128, 0),
)
_Z_OFF = {}
_o = 0
for _n, _w, _h in _Z_GROUPS:
    _Z_OFF[_n] = _o
    _o += _w
Z_COLS = _o
IN_SPLITS = (512, 128, 128, 128, 128, 128, 128, 24, 512, 128, 128, 512, 3072)
_SRC_NAMES = ("q_n", "kc", "vc", "ks", "vs", "kw", "vw", "g_n", "q_m", "k_m", "v_m", "q_x", "g_merge")


def _pack_w_in(w_in):
    cuts = np.cumsum((0,) + IN_SPLITS)
    parts = {n: w_in[:, cuts[i]:cuts[i + 1]] for i, n in enumerate(_SRC_NAMES)}
    parts["g_n"] = jnp.pad(parts["g_n"], ((0, 0), (0, LANES - 24)))
    return jnp.concatenate([parts[n] for n, _, _ in _Z_GROUPS], axis=1).astype(BF16)


def _head_gain_row(gains):
    g = {
        "q_n": jnp.tile(gains["nsa_q"], 8), "ks": jnp.tile(gains["nsa_k1"], 2), "kw": jnp.tile(gains["nsa_k2"], 2),
        "q_m": jnp.tile(gains["moba_q"], 8), "k_m": jnp.tile(gains["moba_k"], 2), "q_x": jnp.tile(gains["mem_q"], 4),
    }
    return jnp.concatenate([g.get(n, jnp.ones((w,), F32)) for n, w, _ in _Z_GROUPS])[None, :]


def _inproj_kernel(x_ref, gain_ref, w_ref, hgain_ref, seg64_ref, seg128_ref, z_ref):
    x = x_ref[...]
    ms = jnp.mean(x * x, axis=-1, keepdims=True)
    xb = (x * lax.rsqrt(ms + RMS_EPS) * gain_ref[...]).astype(BF16)
    for name, width, hd in _Z_GROUPS:
        off = _Z_OFF[name]
        for c in range(0, width, 512):
            cw = min(512, width - c)
            lo_c, hi_c = off + c, off + c + cw
            z = jnp.dot(xb, w_ref[:, lo_c:hi_c], preferred_element_type=F32)
            if hd:
                seg = seg64_ref[...] if hd == 64 else seg128_ref[...]
                for j in range(0, cw, LANES):
                    zj = z[:, j:j + LANES]
                    ss = _group_sumsq(zj, seg)
                    z_ref[:, lo_c + j:lo_c + j + LANES] = (
                        zj * lax.rsqrt(ss * (1.0 / hd) + RMS_EPS) * hgain_ref[:, lo_c + j:lo_c + j + LANES])
            else:
                z_ref[:, lo_c:hi_c] = z


def _inproj(x, gain, w_packed, hgain, tm=256):
    n, d = x.shape
    assert n % tm == 0
    const = lambda i: (0, 0)
    return pl.pallas_call(
        _inproj_kernel,
        grid=(n // tm,),
        in_specs=[
            pl.BlockSpec((tm, d), lambda i: (i, 0)),
            pl.BlockSpec((1, d), const),
            pl.BlockSpec((d, Z_COLS), const),
            pl.BlockSpec((1, Z_COLS), const),
            pl.BlockSpec((LANES, LANES), const),
            pl.BlockSpec((LANES, LANES), const),
        ],
        out_specs=pl.BlockSpec((tm, Z_COLS), lambda i: (i, 0)),
        out_shape=jax.ShapeDtypeStruct((n, Z_COLS), F32),
        compiler_params=_params(("arbitrary",)),
        name="inproj",
    )(x, gain, w_packed, hgain, _seg_matrix(64), _seg_matrix(128))


def _compress_weights(pe, w1, b1, w2):
    w1r = w1.reshape(2, CMP_STRIDE, HEAD_DIM, -1)
    eye = jnp.eye(2, dtype=F32)
    wbig = jnp.einsum("rlde,gh->lgdrhe", w1r, eye).reshape(CMP_STRIDE * LANES, 2 * LANES)
    pos = jnp.einsum("rld,rlde->e", pe.reshape(2, CMP_STRIDE, HEAD_DIM), w1r, precision=HI)
    cvec = jnp.tile(b1 + pos, 2)[None, :]
    w2big = jnp.einsum("ed,gh->gehd", w2, eye).reshape(LANES, LANES)
    return _split_hi_lo(wbig) + (cvec,) + _split_hi_lo(w2big)


def _compress_kernel(x_ref, wh_ref, wl_ref, c_ref, w2h_ref, w2l_ref, seg_ref, gain_ref, o_ref, *, norm):
    ns = x_ref.shape[1]
    a = _dot3(x_ref[0], wh_ref[...], wl_ref[...])
    nxt = pltpu.roll(a[:, LANES:], ns - 1, axis=0)
    h = a[:, :LANES] + nxt + c_ref[...]
    o = _dot3(_gelu_tanh(h), w2h_ref[...], w2l_ref[...])
    if norm:
        o = o * lax.rsqrt(_group_sumsq(o, seg_ref[...]) * (1.0 / HEAD_DIM) + RMS_EPS) * gain_ref[...]
    o_ref[0] = o


def _compress(x_sub, weights, gain, norm):
    b, ns, kk = x_sub.shape
    wh, wl, cvec, w2h, w2l = weights
    const = lambda i: (0, 0)
    return pl.pallas_call(
        functools.partial(_compress_kernel, norm=norm),
        grid=(b,),
        in_specs=[
            pl.BlockSpec((1, ns, kk), lambda i: (i, 0, 0)),
            pl.BlockSpec((kk, 2 * LANES), const), pl.BlockSpec((kk, 2 * LANES), const),
            pl.BlockSpec((1, LANES), const),
            pl.BlockSpec((LANES, LANES), const), pl.BlockSpec((LANES, LANES), const),
            pl.BlockSpec((LANES, LANES), const), pl.BlockSpec((1, LANES), const),
        ],
        out_specs=pl.BlockSpec((1, ns, LANES), lambda i: (i, 0, 0)),
        out_shape=jax.ShapeDtypeStruct((b, ns, LANES), F32),
        compiler_params=_params(("arbitrary",)),
        name="nsa_compress",
    )(x_sub, wh, wl, cvec, w2h, w2l, _seg_matrix(HEAD_DIM), gain)


def _toeplitz_tiles(tbl, deltas, tq, tk, hi):
    d = (jnp.asarray(deltas, jnp.int32)[:, None, None] + jnp.arange(tq)[None, :, None] - jnp.arange(tk)[None, None, :])
    b = jnp.where(((d >= 0) & (d < hi))[..., None, None], _bias_lookup(tbl, d), NEG_INF)
    return jnp.transpose(b, (3, 0, 4, 1, 2)).reshape(tbl.shape[1], len(deltas), HPG * tq, tk)


def _with_masked_tile(tiles):
    return jnp.concatenate([jnp.full_like(tiles[:, :1], NEG_INF), tiles], axis=1)


def _stack_queries(zq_ref, write, tq):
    lane = lax.broadcasted_iota(jnp.int32, (tq, LANES), 1)
    for g in range(2):
        keep = (lane < HEAD_DIM) if g == 0 else (lane >= HEAD_DIM)
        for h in range(HPG):
            hd = HPG * g + h
            chunk = zq_ref[:, LANES * (hd // 2):LANES * (hd // 2 + 1)] * (HEAD_DIM ** -0.5)
            if hd % 2 != g:
                chunk = pltpu.roll(chunk, HEAD_DIM, axis=1)
            write(g, h, jnp.where(keep, chunk, 0.0))


def _stack_queries_prompt(zq_ref, q_scr, tq):
    def write(g, h, val):
        q_scr[g, h * tq:(h + 1) * tq, :] = val.astype(BF16)
    _stack_queries(zq_ref, write, tq)


def _stack_queries_sample(zq_ref, q_scr, ts):
    def write(g, h, val):
        q_scr[(g * HPG + h) * ts:(g * HPG + h + 1) * ts, :] = val
    _stack_queries(zq_ref, write, ts)


def _unstack_heads(parts, o_ref, tq):
    lane = lax.broadcasted_iota(jnp.int32, (tq, LANES), 1)
    for j in range(4):
        g = j // 2
        he = (2 * j) % HPG
        left = parts[g][he * tq:(he + 1) * tq, :]
        right = parts[g][(he + 1) * tq:(he + 2) * tq, :]
        if g == 0:
            right = pltpu.roll(right, HEAD_DIM, axis=1)
        else:
            left = pltpu.roll(left, HEAD_DIM, axis=1)
        o_ref[:, LANES * j:LANES * (j + 1)] = jnp.where(lane < HEAD_DIM, left, right)


KEY_TILES_PER_STEP = 4


def _flash(q, k_ref, v_ref, bias_ref, g, tile_index, bm, blk_shift, rep, kb_lo, kb_hi, qt, m_scr, l_scr, acc_scr, tk):
    m_scr[...] = jnp.full(m_scr.shape, NEG_INF, F32)
    l_scr[...] = jnp.zeros(l_scr.shape, F32)
    acc_scr[...] = jnp.zeros(acc_scr.shape, F32)
    nk = tk * KEY_TILES_PER_STEP
    blk_row = lax.broadcasted_iota(jnp.int32, (LANES, nk), 0)
    key_lane = lax.broadcasted_iota(jnp.int32, (LANES, nk), 1)

    def body(kb, carry):
        k0 = pl.multiple_of(kb * nk, nk)
        k = k_ref[pl.ds(k0, nk), :].astype(BF16)
        v = v_ref[pl.ds(k0, nk), :].astype(BF16)
        s = lax.dot_general(q, k, _NT, preferred_element_type=F32)
        s = s + jnp.concatenate(
            [bias_ref[g, tile_index(qt - (kb * KEY_TILES_PER_STEP + c))] for c in range(KEY_TILES_PER_STEP)], axis=1)
        if bm is not None:
            expand = jnp.where(blk_row == ((k0 + key_lane) >> blk_shift), 1.0, 0.0).astype(BF16)
            vis = jnp.dot(bm, expand, preferred_element_type=F32)
            add = jnp.where(vis > 0.5, 0.0, NEG_INF)
            if rep > 1:
                add = jnp.concatenate([add] * rep, axis=0)
            s = s + add
        m_old = m_scr[...]
        m_new = jnp.maximum(m_old, jnp.max(s, axis=-1, keepdims=True))
        p = jnp.where(s > HALF_NEG, jnp.exp(s - m_new), 0.0)
        alpha = jnp.exp(m_old - m_new)
        l_scr[...] = alpha * l_scr[...] + jnp.sum(p, axis=-1, keepdims=True)
        acc_scr[...] = alpha * acc_scr[...] + jnp.dot(p.astype(BF16), v, preferred_element_type=F32)
        m_scr[...] = m_new
        return carry

    lax.fori_loop(kb_lo, kb_hi, body, 0)
    l = l_scr[...]
    return acc_scr[...] / jnp.where(l > 0.0, l, 1.0)


def _causal_tile_index(n_near):
    return lambda rel: jnp.clip(rel + 1, 0, n_near)


def _window_tile_index(n_win):
    return lambda rel: jnp.where(rel >= n_win, 0, jnp.clip(rel + 1, 0, n_win))


def _rank_select(score, count, lane, n_valid):
    width = score.shape[-1]
    rank = jnp.zeros(score.shape, F32)
    for s in range(1, width):
        if n_valid <= s <= width - n_valid:
            continue
        other = pltpu.roll(score, s, axis=1)
        rank = rank + jnp.where(other > score, 1.0, 0.0)
        if s < n_valid:
            rank = rank + jnp.where(other == score, jnp.where(lane >= s, 1.0, 0.0), 0.0)
    return jnp.where((rank < count) & (score > HALF_NEG), 1.0, 0.0)


def _take_topk(score, count, lane):
    sel = jnp.zeros(score.shape, F32)
    cur = score
    lane_f = lane.astype(F32)
    for _ in range(count):
        mx = jnp.max(cur, axis=-1, keepdims=True)
        idx = jnp.min(jnp.where(cur == mx, lane_f, float(score.shape[-1])), axis=-1, keepdims=True)
        hit = lane_f == idx
        sel = jnp.where(hit, jnp.where(mx > HALF_NEG, 1.0, 0.0), sel)
        cur = jnp.where(hit, KNOCKED, cur)
    return sel


def _nsa_prompt_kernel(zq_ref, gn_ref, kc_ref, vc_ref, ks_ref, vs_ref, kw_ref, vw_ref, biasc_ref, bslc_ref, bwin_ref,
                       mslc_ref, o_ref, q_scr, m_scr, l_scr, acc_scr, comb_scr, *, tq, tk, n_slc, n_slc_tiles, n_win_tiles):
    qt = pl.program_id(1)
    q0 = qt * tq
    _stack_queries_prompt(zq_ref, q_scr, tq)
    gn = jax.nn.sigmoid(gn_ref[...])
    lane = lax.broadcasted_iota(jnp.int32, (tq, LANES), 1)
    tpos = q0 + lax.broadcasted_iota(jnp.int32, (tq, LANES), 0)
    qb = tpos >> 6

    def gate_col(c, g):
        return jnp.concatenate([gn[:, c * 8 + g * HPG + h:c * 8 + g * HPG + h + 1] for h in range(HPG)], axis=0)

    kc = kc_ref[0].astype(BF16)
    vc = vc_ref[0].astype(BF16)
    for g in range(2):
        q = q_scr[g]
        s_c = lax.dot_general(q, kc, _NT, preferred_element_type=F32) + biasc_ref[g]
        mx = jnp.max(s_c, axis=-1, keepdims=True)
        p = jnp.where(s_c > HALF_NEG, jnp.exp(s_c - mx), 0.0)
        den = jnp.sum(p, axis=-1, keepdims=True)
        p_c = p / jnp.where(den > 0.0, den, 1.0)
        comb_scr[g] = gate_col(0, g) * jnp.dot(p_c.astype(BF16), vc, preferred_element_type=F32)
        imp = p_c[0:tq] + p_c[tq:2 * tq] + p_c[2 * tq:3 * tq] + p_c[3 * tq:4 * tq]
        imp_hi, imp_lo = _split_hi_lo(imp)
        p_slc = (jnp.dot(imp_hi, mslc_ref[...], preferred_element_type=F32)
                 + jnp.dot(imp_lo, mslc_ref[...], preferred_element_type=F32))
        forced = (lane == 0) | (lane == qb) | (lane == qb - 1)
        score = jnp.where(lane > qb, NEG_INF, p_slc + jnp.where(forced, FORCE_BONUS, 0.0))
        bm = lax.cond(q0 + tq <= SLC_TOPN * SLC_LEN,
                      lambda: jnp.where(score > HALF_NEG, 1.0, 0.0),
                      lambda: _rank_select(score, SLC_TOPN, lane, n_slc)).astype(BF16)
        kb_hi = qt // KEY_TILES_PER_STEP + 1
        o_s = _flash(q, ks_ref, vs_ref, bslc_ref, g, _causal_tile_index(n_slc_tiles), bm, 6, HPG, 0, kb_hi, qt,
                     m_scr, l_scr, acc_scr, tk)
        comb_scr[g] = comb_scr[g] + gate_col(1, g) * o_s
        kb_lo = jnp.maximum(qt - (n_win_tiles - 1), 0) // KEY_TILES_PER_STEP
        o_w = _flash(q, kw_ref, vw_ref, bwin_ref, g, _window_tile_index(n_win_tiles), None, 0, 1, kb_lo, kb_hi, qt,
                     m_scr, l_scr, acc_scr, tk)
        comb_scr[g] = comb_scr[g] + gate_col(2, g) * o_w
    _unstack_heads([comb_scr[0], comb_scr[1]], o_ref, tq)


def _zspec(rows, name, width, row_map):
    cb = _Z_OFF[name] // width
    assert _Z_OFF[name] % width == 0
    return pl.BlockSpec((rows, width), lambda b, i: (row_map(b, i), cb))


def _nsa_prompt(z, kc, vc, rel_table, bp, tp):
    tq, tk = ATT_TQ, ATT_TK
    nq = tp // tq
    n_cmp = tp // CMP_STRIDE - 1
    n_slc = tp // SLC_LEN
    assert tp % tq == 0 and n_cmp < LANES and n_slc <= LANES and kc.shape[1] == LANES
    tbl = rel_table[:, :NSA_HEADS].reshape(REL_BUCKETS, NSA_KV, HPG)
    t = jnp.arange(tp)
    dist_c = t[:, None] - (jnp.arange(LANES) * CMP_STRIDE + CMP_LEN - 1)[None, :]
    valid_c = (dist_c >= 0) & (jnp.arange(LANES) < n_cmp)[None, :]
    biasc = jnp.where(valid_c[..., None, None], _bias_lookup(tbl, dist_c), NEG_INF)
    biasc = jnp.transpose(biasc.reshape(nq, tq, LANES, NSA_KV, HPG), (0, 3, 4, 1, 2)).reshape(nq, NSA_KV, HPG * tq, LANES)
    slc_deltas = (0, tq, 2 * tq)
    win_deltas = tuple(range(0, WINDOW + 1, tq))
    assert tp % (tk * KEY_TILES_PER_STEP) == 0 and n_slc <= LANES // 2
    bslc = _with_masked_tile(_toeplitz_tiles(tbl, slc_deltas, tq, tk, 1 << 30))
    bwin = _with_masked_tile(_toeplitz_tiles(tbl, win_deltas, tq, tk, WINDOW))
    mslc = jnp.pad(_slc_matrix(n_cmp, n_slc), ((0, LANES - n_cmp), (0, LANES - n_slc))).astype(BF16)
    kern = functools.partial(_nsa_prompt_kernel, tq=tq, tk=tk, n_slc=n_slc, n_slc_tiles=len(slc_deltas),
                             n_win_tiles=len(win_deltas))
    seq = lambda b, i: b
    tile = lambda b, i: b * nq + i
    full4 = lambda b, i: (0, 0, 0, 0)
    return pl.pallas_call(
        kern,
        grid=(bp, nq),
        in_specs=[
            _zspec(tq, "q_n", 512, tile), _zspec(tq, "g_n", LANES, tile),
            pl.BlockSpec((1, LANES, LANES), lambda b, i: (b, 0, 0)), pl.BlockSpec((1, LANES, LANES), lambda b, i: (b, 0, 0)),
        ] + [pl.BlockSpec((tp, LANES), (lambda b, i, cb=_Z_OFF[nm] // LANES: (b, cb))) for nm in ("ks", "vs", "kw", "vw")] + [
            pl.BlockSpec((None, NSA_KV, HPG * tq, LANES), lambda b, i: (i, 0, 0, 0)),
            pl.BlockSpec(bslc.shape, full4), pl.BlockSpec(bwin.shape, full4),
            pl.BlockSpec((LANES, LANES), lambda b, i: (0, 0)),
        ],
        out_specs=pl.BlockSpec((tq, 512), lambda b, i: (b * nq + i, 0)),
        out_shape=jax.ShapeDtypeStruct((bp * tp, 512), F32),
        scratch_shapes=[
            pltpu.VMEM((NSA_KV, HPG * tq, LANES), BF16),
            pltpu.VMEM((HPG * tq, 1), F32), pltpu.VMEM((HPG * tq, 1), F32), pltpu.VMEM((HPG * tq, LANES), F32),
            pltpu.VMEM((NSA_KV, HPG * tq, LANES), F32),
        ],
        compiler_params=_params(("arbitrary", "arbitrary")),
        name="nsa_prompt",
    )(z, z, kc, vc, z, z, z, z, biasc, bslc, bwin, mslc)


def _slc_matrix(n_cmp, n_slc):
    ratio, c_sub = SLC_LEN // CMP_STRIDE, CMP_LEN // CMP_STRIDE
    m = np.zeros((n_cmp, n_slc), np.float32)
    for j in range(n_slc):
        for o in range(-(c_sub - 1), ratio):
            k = ratio * j + o
            if 0 <= k < n_cmp:
                m[k, j] += float(min(o + c_sub, ratio) - max(o, 0))
    return jnp.asarray(m)


def _moba_prompt_kernel(zq_ref, k_ref, v_ref, bias_ref, o_ref, q_scr, m_scr, l_scr, acc_scr, res_scr, *, tq, tk, nb, n_tiles):
    qt = pl.program_id(1)
    q0 = qt * tq
    _stack_queries_prompt(zq_ref, q_scr, tq)
    rows = HPG * tq
    lane = lax.broadcasted_iota(jnp.int32, (rows, LANES), 1)
    trow = q0 + (lax.broadcasted_iota(jnp.int32, (rows, LANES), 0) & (tq - 1))
    own = trow >> 8
    kmean = jnp.concatenate(
        [jnp.mean(k_ref[n * MOBA_BLOCK:(n + 1) * MOBA_BLOCK, :], axis=0, keepdims=True) for n in range(nb)]
        + [jnp.zeros((LANES - nb, LANES), F32)], axis=0)
    km_hi, km_lo = _split_hi_lo(kmean)
    for g in range(2):
        q = q_scr[g]
        gs = (lax.dot_general(q, km_hi, _NT, preferred_element_type=F32)
              + lax.dot_general(q, km_lo, _NT, preferred_element_type=F32))
        gs = jnp.where(lane < own, gs, NEG_INF)
        sel = _take_topk(gs, min(MOBA_TOPK, nb - 1), lane)
        bm = jnp.where(lane == own, 1.0, sel).astype(BF16)
        res_scr[g] = _flash(q, k_ref, v_ref, bias_ref, g, _causal_tile_index(n_tiles), bm, 8, 1, 0,
                            qt // KEY_TILES_PER_STEP + 1, qt, m_scr, l_scr, acc_scr, tk)
    _unstack_heads([res_scr[0], res_scr[1]], o_ref, tq)


def _moba_prompt(z, rel_table, bp, tp):
    tq, tk = ATT_TQ, ATT_TK
    nq = tp // tq
    nb = tp // MOBA_BLOCK
    assert tp % MOBA_BLOCK == 0 and tq & (tq - 1) == 0
    tbl = rel_table[:, NSA_HEADS:].reshape(REL_BUCKETS, MOBA_KV, HPG)
    deltas = (0, tq, 2 * tq)
    assert tp % (tk * KEY_TILES_PER_STEP) == 0
    bias = _with_masked_tile(_toeplitz_tiles(tbl, deltas, tq, tk, 1 << 30))
    kern = functools.partial(_moba_prompt_kernel, tq=tq, tk=tk, nb=nb, n_tiles=len(deltas))
    return pl.pallas_call(
        kern,
        grid=(bp, nq),
        in_specs=[
            _zspec(tq, "q_m", 512, lambda b, i: b * nq + i),
            pl.BlockSpec((tp, LANES), lambda b, i: (b, _Z_OFF["k_m"] // LANES)),
            pl.BlockSpec((tp, LANES), lambda b, i: (b, _Z_OFF["v_m"] // LANES)),
            pl.BlockSpec(bias.shape, lambda b, i: (0, 0, 0, 0)),
        ],
        out_specs=pl.BlockSpec((tq, 512), lambda b, i: (b * nq + i, 0)),
        out_shape=jax.ShapeDtypeStruct((bp * tp, 512), F32),
        scratch_shapes=[
            pltpu.VMEM((MOBA_KV, HPG * tq, LANES), BF16),
            pltpu.VMEM((HPG * tq, 1), F32), pltpu.VMEM((HPG * tq, 1), F32), pltpu.VMEM((HPG * tq, LANES), F32),
            pltpu.VMEM((MOBA_KV, HPG * tq, LANES), F32),
        ],
        compiler_params=_params(("arbitrary", "arbitrary")),
        name="moba_prompt",
    )(z, z, z, bias)


def _merge_kernel(on_ref, om_ref, ox_ref, gm_ref, x_ref, wb_ref, wo_ref, fg_ref, x1_ref, xn_ref):
    d = x_ref.shape[1]
    merged = jnp.zeros(x_ref.shape, F32)
    for c, o_ref in enumerate((on_ref, om_ref, ox_ref)):
        proj = jnp.dot(o_ref[...].astype(BF16), wb_ref[c], preferred_element_type=F32)
        merged = merged + jax.nn.sigmoid(gm_ref[:, c * d:(c + 1) * d]) * proj
    x1 = x_ref[...] + jnp.dot(merged.astype(BF16), wo_ref[...], preferred_element_type=F32)
    x1_ref[...] = x1
    ms = jnp.mean(x1 * x1, axis=-1, keepdims=True)
    xn_ref[...] = (x1 * lax.rsqrt(ms + RMS_EPS) * fg_ref[...]).astype(BF16)


def _merge(o_nsa, o_moba, o_mem, z, x, w_branch, w_out, ffn_gain, tm=256):
    n, d = x.shape
    assert n % tm == 0 and _Z_OFF["g_merge"] == 0
    row = lambda i: (i, 0)
    return pl.pallas_call(
        _merge_kernel,
        grid=(n // tm,),
        in_specs=[
            pl.BlockSpec((tm, 512), row), pl.BlockSpec((tm, 512), row), pl.BlockSpec((tm, 512), row),
            pl.BlockSpec((tm, 3 * d), row), pl.BlockSpec((tm, d), row),
            pl.BlockSpec((3, 512, d), lambda i: (0, 0, 0)), pl.BlockSpec((d, d), lambda i: (0, 0)),
            pl.BlockSpec((1, d), lambda i: (0, 0)),
        ],
        out_specs=[pl.BlockSpec((tm, d), row), pl.BlockSpec((tm, d), row)],
        out_shape=[jax.ShapeDtypeStruct((n, d), F32), jax.ShapeDtypeStruct((n, d), BF16)],
        compiler_params=_params(("arbitrary",)),
        name="merge_outproj",
    )(o_nsa, o_moba, o_mem, z, x, w_branch.astype(BF16), w_out.astype(BF16), ffn_gain[None, :])


def _peer_route_kernel(xT_ref, wqh_ref, wql_ref, skh_ref, skl_ref, s1_ref, s2_ref, tau_ref):
    xT = xT_ref[...]
    qT = jnp.dot(wqh_ref[...], xT, preferred_element_type=F32) + jnp.dot(wql_ref[...], xT, preferred_element_type=F32)
    half = HEAD_DIM

    def top_sorted(s):
        vals, cur = [], s
        for _ in range(PEER_TOPK + 1):
            mx = jnp.max(cur, axis=0, keepdims=True)
            vals.append(mx)
            cur = jnp.where(cur == mx, KNOCKED, cur)
        return vals

    taus = []
    for h in range(PEER_HEADS):
        sc = []
        for p in range(2):
            r0 = (h * 2 + p) * half
            q_hi, q_lo = _split_hi_lo(qT[r0:r0 + half, :])
            sc.append(jnp.dot(skh_ref[h, p], q_hi, preferred_element_type=F32)
                      + jnp.dot(skl_ref[h, p], q_hi, preferred_element_type=F32)
                      + jnp.dot(skh_ref[h, p], q_lo, preferred_element_type=F32))
        a, b = top_sorted(sc[0]), top_sorted(sc[1])
        n_rank = PEER_TOPK + 1
        cand = jnp.concatenate([a[i] + b[j] for i in range(n_rank) for j in range(n_rank // (i + 1))], axis=0)
        cur = cand
        for _ in range(PEER_TOPK):
            tau = jnp.max(cur, axis=0, keepdims=True)
            cur = jnp.where(cur == tau, KNOCKED, cur)
        nxt = jnp.max(cur, axis=0, keepdims=True)
        top = a[0] + b[0]
        zsum = jnp.sum(jnp.where(cand >= tau, jnp.exp(cand - top), 0.0), axis=0, keepdims=True)
        shift = top + jnp.log(zsum)
        s1_ref[h] = (sc[0] - shift) * LOG2E
        s2_ref[h] = sc[1] * LOG2E
        taus.append((0.5 * (tau + nxt) - shift) * LOG2E)
    tau_ref[...] = jnp.concatenate(taus, axis=0)


def _peer_route(xT, peer_w_q, sub_keys, tt=256):
    d, n = xT.shape
    assert n % tt == 0
    wqh, wql = _split_hi_lo(peer_w_q.T)
    skh, skl = _split_hi_lo(sub_keys)
    tok3 = lambda i: (0, 0, i)
    big = jax.ShapeDtypeStruct((PEER_HEADS, PEER_NKEYS, n), F32)
    return pl.pallas_call(
        _peer_route_kernel,
        grid=(n // tt,),
        in_specs=[
            pl.BlockSpec((d, tt), lambda i: (0, i)),
            pl.BlockSpec((d, d), lambda i: (0, 0)), pl.BlockSpec((d, d), lambda i: (0, 0)),
            pl.BlockSpec(sub_keys.shape, lambda i: (0, 0, 0, 0)), pl.BlockSpec(sub_keys.shape, lambda i: (0, 0, 0, 0)),
        ],
        out_specs=[pl.BlockSpec((PEER_HEADS, PEER_NKEYS, tt), tok3)] * 2 + [pl.BlockSpec((PEER_HEADS, tt), lambda i: (0, i))],
        out_shape=[big] * 2 + [jax.ShapeDtypeStruct((PEER_HEADS, n), F32)],
        compiler_params=_params(("arbitrary",)),
        name="peer_route",
    )(xT, wqh, wql, skh, skl)


def _peer_dense_kernel(xT_ref, u_ref, vT_ref, s1_ref, s2_ref, tau_ref, o_ref, a_ref, *, ec, tt):
    c = pl.program_id(1)

    @pl.when(c == 0)
    def _():
        o_ref[...] = jnp.zeros_like(o_ref)

    h_t = jnp.dot(u_ref[...], xT_ref[...], preferred_element_type=F32)
    for k in range(ec // LANES):
        for lc in range(0, tt, LANES):
            wk = jnp.zeros((LANES, LANES), F32)
            for h in range(PEER_HEADS):
                csum = s2_ref[h, :, lc:lc + LANES] + s1_ref[h, k:k + 1, lc:lc + LANES]
                wk = wk + jnp.where(csum >= tau_ref[h:h + 1, lc:lc + LANES], jnp.exp2(csum), 0.0)
            g = _gelu_tanh(h_t[k * LANES:(k + 1) * LANES, lc:lc + LANES])
            a_ref[k * LANES:(k + 1) * LANES, lc:lc + LANES] = (wk * g).astype(BF16)
    o_ref[...] += jnp.dot(vT_ref[...], a_ref[...], preferred_element_type=F32)


def _peer_dense(xT, u_b, vT_b, s1T, s2T, tauT, ec=1024):
    d, n = xT.shape
    tt = 512 if n % 512 == 0 else 256
    assert n % tt == 0 and ec == 8 * PEER_NKEYS
    ne = u_b.shape[0]
    kern = functools.partial(_peer_dense_kernel, ec=ec, tt=tt)
    allk = pl.BlockSpec((PEER_HEADS, PEER_NKEYS, tt), lambda i, c: (0, 0, i))
    chunk = pl.BlockSpec((PEER_HEADS, ec // PEER_NKEYS, tt), lambda i, c: (0, c, i))
    return pl.pallas_call(
        kern,
        grid=(n // tt, ne // ec),
        in_specs=[
            pl.BlockSpec((d, tt), lambda i, c: (0, i)),
            pl.BlockSpec((ec, d), lambda i, c: (c, 0)),
            pl.BlockSpec((d, ec), lambda i, c: (0, c)),
            chunk, allk,
            pl.BlockSpec((PEER_HEADS, tt), lambda i, c: (0, i)),
        ],
        out_specs=pl.BlockSpec((d, tt), lambda i, c: (0, i)),
        out_shape=jax.ShapeDtypeStruct((d, n), F32),
        scratch_shapes=[pltpu.VMEM((ec, tt), BF16)],
        compiler_params=_params(("arbitrary", "arbitrary")),
        name="peer_dense",
    )(xT, u_b, vT_b, s1T, s2T, tauT)


def _peer(x1, xn_b, peer_w_q, sub_keys, u_b, vT_b):
    xT = xn_b.T
    s1T, s2T, tauT = _peer_route(xT, peer_w_q, sub_keys)
    out_t = _peer_dense(xT, u_b, vT_b, s1T, s2T, tauT)
    return x1 + out_t.T


PAGES_PER_STEP = 32


def _page_specs(n_lead, inner):
    zeros = (0,) * len(inner)
    return [pl.BlockSpec((1,) + inner, (lambda b, j, pt, p=p: (pt[b, j * PAGES_PER_STEP + p],) + zeros))
            for p in range(n_lead)]


def _compress_paged_kernel(pt_ref, *refs, norm):
    del pt_ref
    p_n = PAGES_PER_STEP
    x_refs = refs[:p_n]
    perm_ref, wh_ref, c_ref, w2h_ref, w2l_ref, seg_ref, gain_ref, o_ref, x_scr = refs[p_n:]
    j = pl.program_id(1)
    pair_sub = 2 * x_refs[0].shape[2] // CMP_STRIDE
    for pp in range(p_n // 2):
        x_t = jnp.concatenate([x_refs[2 * pp][0], x_refs[2 * pp + 1][0]], axis=1).astype(BF16)
        rows = lax.dot_general(perm_ref[...], x_t, _NT, preferred_element_type=F32).astype(BF16)
        base = pl.multiple_of((j * (p_n // 2) + pp) * pair_sub, pair_sub)
        for l in range(CMP_STRIDE):
            x_scr[pl.ds(base, pair_sub), l * LANES:(l + 1) * LANES] = rows[l * pair_sub:(l + 1) * pair_sub, :]

    @pl.when(j == pl.num_programs(1) - 1)
    def _():
        ns = x_scr.shape[0]
        a = jnp.dot(x_scr[...], wh_ref[...], preferred_element_type=F32)
        nxt = pltpu.roll(a[:, LANES:], ns - 1, axis=0)
        h = a[:, :LANES] + nxt + c_ref[...]
        o = _dot3(_gelu_tanh(h), w2h_ref[...], w2l_ref[...])
        if norm:
            o = o * lax.rsqrt(_group_sumsq(o, seg_ref[...]) * (1.0 / HEAD_DIM) + RMS_EPS) * gain_ref[...]
        o_ref[0] = o


def _compress_paged(pool, page_table, weights, gain, norm):
    n_phys, _, page = pool.shape
    b, n_pages = page_table.shape
    sub = page // CMP_STRIDE
    kk = CMP_STRIDE * LANES
    ns = n_pages * sub
    assert n_pages % PAGES_PER_STEP == 0 and page % CMP_STRIDE == 0 and PAGES_PER_STEP % 2 == 0 and (2 * sub) % 16 == 0
    wh, _, cvec, w2h, w2l = weights
    const = lambda b_, j, pt: (0, 0)
    r = np.arange(2 * page)
    perm = jnp.asarray(((r[:, None] // (2 * sub)) + CMP_STRIDE * (r[:, None] % (2 * sub))) == r[None, :], BF16)
    grid_spec = pltpu.PrefetchScalarGridSpec(
        num_scalar_prefetch=1,
        grid=(b, n_pages // PAGES_PER_STEP),
        in_specs=_page_specs(PAGES_PER_STEP, (LANES, page)) + [
            pl.BlockSpec((2 * page, 2 * page), const), pl.BlockSpec((kk, 2 * LANES), const),
            pl.BlockSpec((1, LANES), const),
            pl.BlockSpec((LANES, LANES), const), pl.BlockSpec((LANES, LANES), const),
            pl.BlockSpec((LANES, LANES), const), pl.BlockSpec((1, LANES), const),
        ],
        out_specs=pl.BlockSpec((1, ns, LANES), lambda b_, j, pt: (b_, 0, 0)),
        scratch_shapes=[pltpu.VMEM((ns, kk), BF16)],
    )
    return pl.pallas_call(
        functools.partial(_compress_paged_kernel, norm=norm),
        grid_spec=grid_spec,
        out_shape=jax.ShapeDtypeStruct((b, ns, LANES), F32),
        compiler_params=_params(("arbitrary", "arbitrary")),
        name="nsa_compress_paged",
    )(page_table, *([pool] * PAGES_PER_STEP), perm, wh, cvec, w2h, w2l, _seg_matrix(HEAD_DIM), gain)


def _softmax_tile(s):
    mx = jnp.max(s, axis=-1, keepdims=True)
    p = jnp.where(s > HALF_NEG, jnp.exp(s - mx), 0.0)
    return mx, p, jnp.sum(p, axis=-1, keepdims=True)


def _pv(p, v, v_t):
    if v_t:
        return lax.dot_general(p.astype(BF16), v, _NT, preferred_element_type=F32)
    return jnp.dot(p.astype(BF16), v, preferred_element_type=F32)


def _online_update(s, v, m_scr, l_scr, acc_scr, v_t=False):
    m_old = m_scr[...]
    m_new = jnp.maximum(m_old, jnp.max(s, axis=-1, keepdims=True))
    p = jnp.where(s > HALF_NEG, jnp.exp(s - m_new), 0.0)
    alpha = jnp.exp(m_old - m_new)
    l_scr[...] = alpha * l_scr[...] + jnp.sum(p, axis=-1, keepdims=True)
    acc_scr[...] = alpha * acc_scr[...] + _pv(p, v, v_t)
    m_scr[...] = m_new


def _pad_rows(x, rows):
    return jnp.concatenate([x, jnp.zeros((rows - x.shape[0], x.shape[1]), x.dtype)], axis=0)


def _sample_row_bias(tbl, ts, kpos, qpos0, lo_ok):
    d = (qpos0 + jnp.arange(ts))[:, None] - kpos[None, :]
    b = jnp.where(((d >= 0) & lo_ok)[..., None, None], _bias_lookup(tbl, d), NEG_INF)
    return jnp.transpose(b, (2, 3, 0, 1)).reshape(-1, kpos.shape[0])


def _past_bias_tiles(tbl, ts, past_len, nk):
    assert nk >= LANES
    far = jnp.repeat(tbl[REL_BUCKETS - 1].reshape(-1), ts)[:, None]
    far_tile = jnp.broadcast_to(far, (far.shape[0], nk))
    near = _sample_row_bias(tbl, ts, past_len - LANES + jnp.arange(LANES), past_len, jnp.ones((ts, LANES), bool))
    return jnp.stack([far_tile, jnp.concatenate([far_tile[:, :nk - LANES], near], axis=1)])


def _block_expand_matrix(n_blk_lanes, nk, n_steps):
    blk_step = nk // SLC_LEN
    assert nk % SLC_LEN == 0 and blk_step % 16 == 0
    r = np.arange(n_blk_lanes + (n_steps - 1) * blk_step)[:, None]
    key_blk = (np.arange(nk) // SLC_LEN)[None, :]
    return jnp.asarray(r == (n_steps - 1) * blk_step + key_blk, BF16)


def _nsa_sample_kernel(pt_ref, *refs, ts, past_len, n_slc, slc_lanes):
    del pt_ref
    p_n = PAGES_PER_STEP
    k_pages, v_pages = refs[:p_n], refs[p_n:2 * p_n]
    (zq_ref, gn_ref, kc_ref, vc_ref, ksn_ref, vsn_ref, wk_ref, wv_ref, kwn_ref, vwn_ref,
     biasc_ref, bpast_ref, bnew_ref, bwin_ref, mslc_ref, expand_ref, o_ref,
     q_scr, m_scr, l_scr, acc_scr, comb_scr, bm_scr) = refs[2 * p_n:]
    j = pl.program_id(1)
    page = k_pages[0].shape[2]
    rows = NSA_KV * HPG * ts
    half = HPG * ts

    def gate_col(gn, c):
        return jnp.concatenate([gn[:, c * 8 + hd:c * 8 + hd + 1] for hd in range(NSA_HEADS)], axis=0)

    @pl.when(j == 0)
    def _():
        _stack_queries_sample(zq_ref, q_scr, ts)
        q = q_scr[...].astype(BF16)
        gn = jax.nn.sigmoid(gn_ref[...])
        s_c = lax.dot_general(q, kc_ref[0].astype(BF16), _NT, preferred_element_type=F32) + biasc_ref[...]
        mx, p, den = _softmax_tile(s_c)
        p_c = p / jnp.where(den > 0.0, den, 1.0)
        comb_scr[...] = gate_col(gn, 0) * jnp.dot(p_c.astype(BF16), vc_ref[0].astype(BF16), preferred_element_type=F32)
        imp = jnp.concatenate(
            [sum(p_c[g * half + h * ts:g * half + (h + 1) * ts] for h in range(HPG)) for g in range(NSA_KV)], axis=0)
        imp_hi, imp_lo = _split_hi_lo(imp)
        p_slc = (jnp.dot(imp_hi, mslc_ref[...], preferred_element_type=F32)
                 + jnp.dot(imp_lo, mslc_ref[...], preferred_element_type=F32))
        lane = lax.broadcasted_iota(jnp.int32, p_slc.shape, 1)
        tpos = past_len + (lax.broadcasted_iota(jnp.int32, p_slc.shape, 0) & (ts - 1))
        qb = tpos >> 6
        forced = (lane == 0) | (lane == qb) | (lane == qb - 1)
        score = jnp.where(lane > qb, NEG_INF, p_slc + jnp.where(forced, FORCE_BONUS, 0.0))
        sel = _rank_select(score, SLC_TOPN, lane, n_slc)
        bm_scr[...] = jnp.concatenate(
            [sel[g * ts:(g + 1) * ts] for g in range(NSA_KV) for _ in range(HPG)], axis=0).astype(BF16)
        m_scr[...] = jnp.full(m_scr.shape, NEG_INF, F32)
        l_scr[...] = jnp.zeros(l_scr.shape, F32)
        acc_scr[...] = jnp.zeros(acc_scr.shape, F32)
        s_w = jnp.dot(q, wk_ref[0].astype(BF16), preferred_element_type=F32) + bwin_ref[...]
        _online_update(s_w, wv_ref[0].astype(BF16), m_scr, l_scr, acc_scr, v_t=True)
        s_n = lax.dot_general(q, _pad_rows(kwn_ref[...], LANES).astype(BF16), _NT, preferred_element_type=F32) + bnew_ref[...]
        _online_update(s_n, _pad_rows(vwn_ref[...], LANES).astype(BF16), m_scr, l_scr, acc_scr)
        l = l_scr[...]
        comb_scr[...] = comb_scr[...] + gate_col(gn, 2) * (acc_scr[...] / jnp.where(l > 0.0, l, 1.0))
        m_scr[...] = jnp.full(m_scr.shape, NEG_INF, F32)
        l_scr[...] = jnp.zeros(l_scr.shape, F32)
        acc_scr[...] = jnp.zeros(acc_scr.shape, F32)

    q = q_scr[...].astype(BF16)
    nk = p_n * page
    k_t = jnp.concatenate([r[0].astype(BF16) for r in k_pages], axis=1)
    v_t = jnp.concatenate([r[0].astype(BF16) for r in v_pages], axis=1)
    last = pl.num_programs(1) - 1
    blk_step = nk // SLC_LEN
    expand = expand_ref[pl.ds(pl.multiple_of((last - j) * blk_step, blk_step), slc_lanes), :]
    vis = jnp.dot(bm_scr[...], expand, preferred_element_type=F32)
    s = (jnp.dot(q, k_t, preferred_element_type=F32) + bpast_ref[jnp.where(j == last, 1, 0)]
         + jnp.where(vis > 0.5, 0.0, NEG_INF))
    _online_update(s, v_t, m_scr, l_scr, acc_scr, v_t=True)

    @pl.when(j == pl.num_programs(1) - 1)
    def _():
        gn = jax.nn.sigmoid(gn_ref[...])
        s_n = lax.dot_general(q, _pad_rows(ksn_ref[...], LANES).astype(BF16), _NT, preferred_element_type=F32) + bnew_ref[...]
        _online_update(s_n, _pad_rows(vsn_ref[...], LANES).astype(BF16), m_scr, l_scr, acc_scr)
        l = l_scr[...]
        comb = comb_scr[...] + gate_col(gn, 1) * (acc_scr[...] / jnp.where(l > 0.0, l, 1.0))
        _unstack_heads([comb[:half], comb[half:]], o_ref, ts)


def _nsa_sample(z, n_p, kc, vc, pool_k, pool_v, win_k, win_v, page_table, rel_table, ts):
    bs, n_pages = page_table.shape
    n_phys, _, page = pool_k.shape
    past_len = n_pages * page
    p_n = PAGES_PER_STEP
    n_steps = n_pages // p_n
    nk = p_n * page
    n_sub = kc.shape[1]
    n_cmp = n_sub - 1
    n_slc = -(-(past_len + ts) // SLC_LEN)
    slc_lanes = -(-n_slc // LANES) * LANES
    win_len = win_k.shape[2]
    assert n_pages % p_n == 0 and ts & (ts - 1) == 0 and ts <= SLC_LEN and past_len % SLC_LEN == 0 and n_p % ts == 0
    assert n_sub % LANES == 0 and win_len + ts >= WINDOW
    tbl = rel_table[:, :NSA_HEADS].reshape(REL_BUCKETS, NSA_KV, HPG)
    rows = NSA_HEADS * ts
    cmp_end = jnp.arange(n_sub) * CMP_STRIDE + CMP_LEN - 1
    biasc = _sample_row_bias(tbl, ts, cmp_end, past_len, jnp.broadcast_to(jnp.arange(n_sub) < n_cmp, (ts, n_sub)))
    bpast = _past_bias_tiles(tbl, ts, past_len, nk)
    bnew = _sample_row_bias(tbl, ts, past_len + jnp.arange(LANES), past_len, jnp.broadcast_to(jnp.arange(LANES) < ts, (ts, LANES)))
    wpos = past_len - win_len + jnp.arange(win_len)
    in_win = ((past_len + jnp.arange(ts))[:, None] - wpos[None, :]) < WINDOW
    bwin = _sample_row_bias(tbl, ts, wpos, past_len, in_win)
    mslc = jnp.pad(_slc_matrix(n_cmp, n_slc), ((0, n_sub - n_cmp), (0, slc_lanes - n_slc))).astype(BF16)
    expand = _block_expand_matrix(slc_lanes, nk, n_steps)
    row_blk = n_p // ts
    zrow = lambda name, width: pl.BlockSpec((ts, width), (lambda b, j, pt, cb=_Z_OFF[name] // width: (row_blk + b, cb)))
    seq3 = lambda shape: pl.BlockSpec((1,) + shape, lambda b, j, pt: (b, 0, 0))
    const2 = lambda shape: pl.BlockSpec(shape, lambda b, j, pt: (0, 0))
    grid_spec = pltpu.PrefetchScalarGridSpec(
        num_scalar_prefetch=1,
        grid=(bs, n_steps),
        in_specs=_page_specs(p_n, (LANES, page)) + _page_specs(p_n, (LANES, page)) + [
            zrow("q_n", 512), zrow("g_n", LANES),
            seq3((n_sub, LANES)), seq3((n_sub, LANES)),
            zrow("ks", LANES), zrow("vs", LANES),
            seq3((LANES, win_len)), seq3((LANES, win_len)),
            zrow("kw", LANES), zrow("vw", LANES),
            const2((rows, n_sub)),
            pl.BlockSpec((2, rows, nk), lambda b, j, pt: (0, 0, 0)),
            const2((rows, LANES)), const2((rows, win_len)), const2((n_sub, slc_lanes)), const2(expand.shape),
        ],
        out_specs=pl.BlockSpec((ts, 512), lambda b, j, pt: (b, 0)),
        scratch_shapes=[
            pltpu.VMEM((rows, LANES), F32),
            pltpu.VMEM((rows,1), F32), pltpu.VMEM((rows, 1), F32), pltpu.VMEM((rows, LANES), F32),
            pltpu.VMEM((rows, LANES), F32), pltpu.VMEM((rows, slc_lanes), BF16),
        ],
    )
    return pl.pallas_call(
        functools.partial(_nsa_sample_kernel, ts=ts, past_len=past_len, n_slc=n_slc, slc_lanes=slc_lanes),
        grid_spec=grid_spec,
        out_shape=jax.ShapeDtypeStruct((bs * ts, 512), F32),
        compiler_params=_params(("arbitrary", "arbitrary")),
        name="nsa_sample",
    )(page_table, *([pool_k] * p_n), *([pool_v] * p_n), z, z, kc, vc, z, z, win_k, win_v, z, z,
      biasc, bpast, bnew, bwin, mslc, expand)


def _moba_sample_kernel(pt_ref, *refs, ts, n_past_blocks):
    del pt_ref
    p_n = PAGES_PER_STEP
    k_pages, v_pages = refs[:p_n], refs[p_n:2 * p_n]
    (zq_ref, kn_ref, vn_ref, bpast_ref, bnew_ref, o_ref, q_scr, m_all, l_all, acc_all, km_scr) = refs[2 * p_n:]
    j = pl.program_id(1)
    page = k_pages[0].shape[2]
    ppb = MOBA_BLOCK // page
    bps = p_n // ppb
    rows = MOBA_KV * HPG * ts
    half = HPG * ts
    lane = lax.broadcasted_iota(jnp.int32, (rows, LANES), 1)
    kcol = lax.broadcasted_iota(jnp.int32, (LANES, LANES), 1)
    last = pl.num_programs(1) - 1

    @pl.when(j == 0)
    def _():
        _stack_queries_sample(zq_ref, q_scr, ts)
        m_all[...] = jnp.full(m_all.shape, NEG_INF, F32)
        l_all[...] = jnp.zeros(l_all.shape, F32)
        km_scr[...] = jnp.zeros(km_scr.shape, F32)

    q = q_scr[...].astype(BF16)
    for i in range(bps):
        n = j * bps + i
        k_t = jnp.concatenate([k_pages[i * ppb + r][0] for r in range(ppb)], axis=1)
        v_t = jnp.concatenate([v_pages[i * ppb + r][0].astype(BF16) for r in range(ppb)], axis=1)
        s = (jnp.dot(q, k_t.astype(BF16), preferred_element_type=F32)
             + bpast_ref[jnp.where(j == last, 1, 0), :, i * MOBA_BLOCK:(i + 1) * MOBA_BLOCK])
        mx, p, den = _softmax_tile(s)
        m_all[...] = jnp.where(lane == n, mx, m_all[...])
        l_all[...] = jnp.where(lane == n, den, l_all[...])
        acc_all[n] = _pv(p, v_t, True)
        km_scr[...] = jnp.where(kcol == n, jnp.mean(k_t, axis=1, keepdims=True), km_scr[...])

    @pl.when(j == pl.num_programs(1) - 1)
    def _():
        s_o = lax.dot_general(q, _pad_rows(kn_ref[...], LANES).astype(BF16), _NT, preferred_element_type=F32) + bnew_ref[...]
        m_o, p_o, l_o = _softmax_tile(s_o)
        acc_o = jnp.dot(p_o.astype(BF16), _pad_rows(vn_ref[...], LANES).astype(BF16), preferred_element_type=F32)
        km_hi, km_lo = _split_hi_lo(km_scr[...])
        gs = jnp.dot(q, km_hi, preferred_element_type=F32) + jnp.dot(q, km_lo, preferred_element_type=F32)
        gs = jnp.where(lane < n_past_blocks, gs, NEG_INF)
        sel = _take_topk(gs, min(MOBA_TOPK, n_past_blocks), lane)
        m_sel = jnp.where(sel > 0.5, m_all[...], NEG_INF)
        m_fin = jnp.maximum(jnp.max(m_sel, axis=-1, keepdims=True), m_o)
        wgt = jnp.where(sel > 0.5, jnp.exp(m_sel - m_fin), 0.0)
        w_o = jnp.exp(m_o - m_fin)
        den = jnp.sum(wgt * l_all[...], axis=-1, keepdims=True) + w_o * l_o
        num = w_o * acc_o
        for n in range(n_past_blocks):
            num = num + wgt[:, n:n + 1] * acc_all[n]
        res = num / den
        _unstack_heads([res[:half], res[half:]], o_ref, ts)


def _moba_sample(z, n_p, pool_k, pool_v, page_table, rel_table, ts):
    bs, n_pages = page_table.shape
    n_phys, _, page = pool_k.shape
    past_len = n_pages * page
    p_n = PAGES_PER_STEP
    n_steps = n_pages // p_n
    nk = p_n * page
    nb_past = past_len // MOBA_BLOCK
    assert (MOBA_BLOCK % page == 0 and p_n % (MOBA_BLOCK // page) == 0 and past_len % MOBA_BLOCK == 0
            and ts <= MOBA_BLOCK and nb_past <= LANES and n_pages % p_n == 0 and n_p % ts == 0)
    tbl = rel_table[:, NSA_HEADS:].reshape(REL_BUCKETS, MOBA_KV, HPG)
    rows = MOBA_HEADS * ts
    bpast = _past_bias_tiles(tbl, ts, past_len, nk)
    bnew = _sample_row_bias(tbl, ts, past_len + jnp.arange(LANES), past_len, jnp.broadcast_to(jnp.arange(LANES) < ts, (ts, LANES)))
    row_blk = n_p // ts
    zrow = lambda name, width: pl.BlockSpec((ts, width), (lambda b, j, pt, cb=_Z_OFF[name] // width: (row_blk + b, cb)))
    grid_spec = pltpu.PrefetchScalarGridSpec(
        num_scalar_prefetch=1,
        grid=(bs, n_steps),
        in_specs=_page_specs(p_n, (LANES, page)) + _page_specs(p_n, (LANES, page)) + [
            zrow("q_m", 512), zrow("k_m", LANES), zrow("v_m", LANES),
            pl.BlockSpec((2, rows, nk), lambda b, j, pt: (0, 0, 0)),
            pl.BlockSpec((rows, LANES), lambda b, j, pt: (0, 0)),
        ],
        out_specs=pl.BlockSpec((ts, 512), lambda b, j, pt: (b, 0)),
        scratch_shapes=[
            pltpu.VMEM((rows, LANES), F32),
            pltpu.VMEM((rows,LANES), F32), pltpu.VMEM((rows, LANES), F32),
            pltpu.VMEM((nb_past, rows, LANES), F32), pltpu.VMEM((LANES, LANES), F32),
        ],
    )
    return pl.pallas_call(
        functools.partial(_moba_sample_kernel, ts=ts, n_past_blocks=nb_past),
        grid_spec=grid_spec,
        out_shape=jax.ShapeDtypeStruct((bs * ts, 512), F32),
        compiler_params=_params(("arbitrary", "arbitrary")),
        name="moba_sample",
    )(page_table, *([pool_k] * p_n), *([pool_v] * p_n), z, z, z, bpast, bnew)


def _memkv_kernel(m_ref, gain_ref, w_ref, kgain_ref, seg_ref, k_ref, v_ref):
    x = m_ref[...]
    xb = (x * lax.rsqrt(jnp.mean(x * x, axis=-1, keepdims=True) + RMS_EPS) * gain_ref[...]).astype(BF16)
    kvw = k_ref.shape[1]
    k = jnp.dot(xb, w_ref[:, :kvw], preferred_element_type=F32)
    for h in range(MEM_HEADS):
        kh = k[:, h * MEM_HEAD_DIM:(h + 1) * MEM_HEAD_DIM]
        ss = _group_sumsq(kh, seg_ref[...])
        k_ref[:, h * MEM_HEAD_DIM:(h + 1) * MEM_HEAD_DIM] = kh * lax.rsqrt(ss * (1.0 / MEM_HEAD_DIM) + RMS_EPS) * kgain_ref[...]
    v_ref[...] = jnp.dot(xb, w_ref[:, kvw:], preferred_element_type=F32)


def _memory_kv_pallas(mem, norm_gain, w_mem_kv, k_gain, tm=256):
    n, d = mem.shape
    kvw = w_mem_kv.shape[1] // 2
    assert n % tm == 0
    row = lambda i: (i, 0)
    const = lambda i: (0, 0)
    return pl.pallas_call(
        _memkv_kernel,
        grid=(n // tm,),
        in_specs=[pl.BlockSpec((tm, d), row), pl.BlockSpec((1, d), const), pl.BlockSpec((d, 2 * kvw), const),
                  pl.BlockSpec((1, MEM_HEAD_DIM), const), pl.BlockSpec((LANES, LANES), const)],
        out_specs=[pl.BlockSpec((tm, kvw), row), pl.BlockSpec((tm, kvw), row)],
        out_shape=[jax.ShapeDtypeStruct((n, kvw), F32)] * 2,
        compiler_params=_params(("arbitrary",)),
        name="memory_kv",
    )(mem, norm_gain[None, :], w_mem_kv.astype(BF16), k_gain[None, :], _seg_matrix(MEM_HEAD_DIM))


def _mem_attend_kernel(zq_ref, mk_ref, mv_ref, o_ref):
    for h in range(MEM_HEADS):
        sl = slice(h * MEM_HEAD_DIM, (h + 1) * MEM_HEAD_DIM)
        q = (zq_ref[:, sl] * (MEM_HEAD_DIM ** -0.5)).astype(BF16)
        s = lax.dot_general(q, mk_ref[0, :, sl].astype(BF16), _NT, preferred_element_type=F32)
        p = jnp.exp(s - jnp.max(s, axis=-1, keepdims=True))
        o = jnp.dot(p.astype(BF16), mv_ref[0, :, sl].astype(BF16), preferred_element_type=F32)
        o_ref[:, sl] = o / jnp.sum(p, axis=-1, keepdims=True)


def _mem_attend_pallas(z, row0, n_seq, t_seq, tq, mk, mv):
    nq = t_seq // tq
    assert t_seq % tq == 0 and row0 % tq == 0
    m_len = mk.shape[1]
    width = MEM_HEADS * MEM_HEAD_DIM
    cb = _Z_OFF["q_x"] // width
    return pl.pallas_call(
        _mem_attend_kernel,
        grid=(n_seq, nq),
        in_specs=[pl.BlockSpec((tq, width), lambda b, i: (row0 // tq + b * nq + i, cb)),
                  pl.BlockSpec((1, m_len, width), lambda b, i: (b, 0, 0)),
                  pl.BlockSpec((1, m_len, width), lambda b, i: (b, 0, 0))],
        out_specs=pl.BlockSpec((tq, width), lambda b, i: (b * nq + i, 0)),
        out_shape=jax.ShapeDtypeStruct((n_seq * t_seq, width), F32),
        compiler_params=_params(("arbitrary", "arbitrary")),
        name="mem_attend",
    )(z, mk, mv)


def _compress_dense(x_raw, pe, w1, b1, w2):
    b, tk, _ = x_raw.shape
    n_sub = tk // CMP_STRIDE
    n_cmp = n_sub - 1
    xs = x_raw[:, :n_sub * CMP_STRIDE].reshape(b, n_sub, CMP_STRIDE * LANES)
    w1r = w1.reshape(2, CMP_STRIDE, HEAD_DIM, -1)
    wbig = jnp.einsum("rlde,gh->lgdrhe", w1r, jnp.eye(2, dtype=F32)).reshape(CMP_STRIDE * LANES, 2 * LANES)
    a = jnp.dot(xs, wbig, precision=HI)
    pos = jnp.einsum("rld,rlde->e", pe.reshape(2, CMP_STRIDE, HEAD_DIM), w1r, precision=HI)
    h = a[:, :n_cmp, :LANES] + a[:, 1:, LANES:] + jnp.tile(b1 + pos, 2)
    w2big = jnp.einsum("ed,gh->gehd", w2, jnp.eye(2, dtype=F32)).reshape(LANES, LANES)
    return jnp.dot(jax.nn.gelu(h), w2big, precision=HI)


def _nsa_dense(q, glog, kc_raw, vc_raw, ks, vs, kw_all, vw_all, q_start, win_start, w):
    b, t_q, _ = q.shape
    tk = kc_raw.shape[1]
    scale = HEAD_DIM ** -0.5
    qg = q.reshape(b, t_q, NSA_KV, HPG, HEAD_DIM)
    t = q_start + jnp.arange(t_q)
    tbl = w["rel_bias_table"][:, :NSA_HEADS].reshape(REL_BUCKETS, NSA_KV, HPG)

    kc = _compress_dense(kc_raw, w["cmp_pos_emb"][0], w["cmp_w1"][0], w["cmp_b1"][0], w["cmp_w2"][0])
    n_cmp = kc.shape[1]
    kc = _rms(kc.reshape(b, n_cmp, NSA_KV, HEAD_DIM), w["nsa_k_gain"][0])
    vc = _compress_dense(vc_raw, w["cmp_pos_emb"][1], w["cmp_w1"][1], w["cmp_b1"][1], w["cmp_w2"][1])
    vc = vc.reshape(b, n_cmp, NSA_KV, HEAD_DIM)
    cmp_end = jnp.arange(n_cmp) * CMP_STRIDE + CMP_LEN - 1
    dist_c = t[:, None] - cmp_end[None, :]
    bias_c = jnp.transpose(tbl[_rel_bucket(dist_c)], (2, 3, 0, 1))
    s_c = jnp.einsum("btghd,bngd->bghtn", qg, kc, precision=HI) * scale + bias_c
    p_c = _masked_softmax(s_c, dist_c >= 0)
    o_c = jnp.einsum("bghtn,bngd->btghd", p_c, vc, precision=HI)

    imp = jnp.transpose(p_c.sum(axis=2), (0, 2, 1, 3))
    n_slc = -(-tk // SLC_LEN)
    p_slc = jnp.dot(imp, _slc_matrix(n_cmp, n_slc), precision=HI)
    blk = jnp.arange(n_slc)[None, :]
    qb = (t // SLC_LEN)[:, None]
    forced = ((blk == 0) | (blk == qb) | (blk == qb - 1)).astype(F32)
    future = blk > qb
    score = jnp.where(future[None, :, None, :], NEG_INF, p_slc + FORCE_BONUS * forced[None, :, None, :])
    k_sel = min(SLC_TOPN, n_slc)
    sel_val, sel_idx = lax.top_k(score, k_sel)
    sel_ok = sel_val > HALF_NEG
    selm = (jax.nn.one_hot(sel_idx, n_slc, dtype=F32) * sel_ok[..., None].astype(F32)).sum(axis=-2) > 0.5

    kpos = jnp.arange(tk)
    d_s = t[:, None] - kpos[None, :]
    bias_s = jnp.transpose(tbl[_rel_bucket(d_s)], (2, 3, 0, 1))
    ksg = ks.reshape(b, tk, NSA_KV, HEAD_DIM)
    vsg = vs.reshape(b, tk, NSA_KV, HEAD_DIM)
    s_s = jnp.einsum("btghd,bkgd->bghtk", qg, ksg, precision=HI) * scale + bias_s
    m_s = jnp.repeat(jnp.transpose(selm, (0, 2, 1, 3)), SLC_LEN, axis=-1)[..., :tk] & (d_s >= 0)
    p_s = _masked_softmax(s_s, m_s[:, :, None])
    o_s = jnp.einsum("bghtk,bkgd->btghd", p_s, vsg, precision=HI)

    tw = kw_all.shape[1]
    wpos = win_start + jnp.arange(tw)
    d_w = t[:, None] - wpos[None, :]
    m_w = (wpos[None, :] >= 0) & (d_w >= 0) & (d_w < WINDOW)
    bias_w = jnp.transpose(tbl[_rel_bucket(d_w)], (2, 3, 0, 1))
    kwg = kw_all.reshape(b, tw, NSA_KV, HEAD_DIM)
    vwg = vw_all.reshape(b, tw, NSA_KV, HEAD_DIM)
    s_w = jnp.einsum("btghd,bkgd->bghtk", qg, kwg, precision=HI) * scale + bias_w
    p_w = _masked_softmax(s_w, m_w)
    o_w = jnp.einsum("bghtk,bkgd->btghd", p_w, vwg, precision=HI)

    g = jax.nn.sigmoid(glog).reshape(b, t_q, 3, NSA_KV, HPG)[..., None]
    o = g[:, :, 0] * o_c + g[:, :, 1] * o_s + g[:, :, 2] * o_w
    return o.reshape(b, t_q, NSA_HEADS * HEAD_DIM)


def _moba_dense(q, k, v, q_start, rel_table):
    b, t_q, _ = q.shape
    tk = k.shape[1]
    scale = HEAD_DIM ** -0.5
    qg = q.reshape(b, t_q, MOBA_KV, HPG, HEAD_DIM)
    kg = k.reshape(b, tk, MOBA_KV, HEAD_DIM)
    vg = v.reshape(b, tk, MOBA_KV, HEAD_DIM)
    t = q_start + jnp.arange(t_q)
    tbl = rel_table[:, NSA_HEADS:].reshape(REL_BUCKETS, MOBA_KV, HPG)
    nb = -(-tk // MOBA_BLOCK)
    pad = nb * MOBA_BLOCK - tk
    kp = jnp.pad(kg, ((0, 0), (0, pad), (0, 0), (0, 0)))
    n_top = min(MOBA_TOPK, nb - 1)
    kpos = jnp.arange(tk)
    kblk = kpos // MOBA_BLOCK
    d = t[:, None] - kpos[None, :]
    m_own = (kblk[None, :] == (t // MOBA_BLOCK)[:, None]) & (d >= 0)
    mask = jnp.broadcast_to(m_own[None, None, None], (b, MOBA_KV, HPG, t_q, tk))
    if n_top > 0:
        kmean = jnp.mean(kp.reshape(b, nb, MOBA_BLOCK, MOBA_KV, HEAD_DIM), axis=2)
        gs = jnp.einsum("btghd,bngd->btghn", qg, kmean, precision=HI)
        past = jnp.arange(nb)[None, :] < (t // MOBA_BLOCK)[:, None]
        gs = jnp.where(past[None, :, None, None, :], gs, NEG_INF)
        top_val, top_idx = lax.top_k(gs, n_top)
        top_ok = top_val > HALF_NEG
        selm = (jax.nn.one_hot(top_idx, nb, dtype=F32) * top_ok[..., None].astype(F32)).sum(axis=-2) > 0.5
        mask = mask | jnp.repeat(jnp.transpose(selm, (0, 2, 3, 1, 4)), MOBA_BLOCK, axis=-1)[..., :tk]
    bias = jnp.transpose(tbl[_rel_bucket(d)], (2, 3, 0, 1))
    s = jnp.einsum("btghd,bkgd->bghtk", qg, kg, precision=HI) * scale + bias
    p = _masked_softmax(s, mask)
    o = jnp.einsum("bghtk,bkgd->btghd", p, vg, precision=HI)
    return o.reshape(b, t_q, MOBA_HEADS * HEAD_DIM)


def _mem_attend(q, mk, mv):
    b, t_q, _ = q.shape
    qh = q.reshape(b, t_q, MEM_HEADS, MEM_HEAD_DIM)
    s = jnp.einsum("bthd,bmhd->bhtm", qh, mk, precision=HI) * (MEM_HEAD_DIM ** -0.5)
    p = jax.nn.softmax(s, axis=-1)
    return jnp.einsum("bhtm,bmhd->bthd", p, mv, precision=HI).reshape(b, t_q, MEM_HEADS * MEM_HEAD_DIM)


def _memory_kv(mem, w):
    b, m, _ = mem.shape
    kv = jnp.dot(_rms(mem, w["mem_norm_gain"]), w["w_mem_kv"], precision=HI)
    k, v = jnp.split(kv, 2, axis=-1)
    k = _rms(k.reshape(b, m, MEM_HEADS, MEM_HEAD_DIM), w["mem_k_gain"])
    return k, v.reshape(b, m, MEM_HEADS, MEM_HEAD_DIM)


def _gather_pages(pool, page_table):
    g = pool[page_table]
    return g.reshape(page_table.shape[0], -1, LANES)


def _zcol(z, name, width):
    return z[..., _Z_OFF[name]:_Z_OFF[name] + width]


def kernel(x_prompt, x_sample, cache_cmp_k, cache_cmp_v, cache_slc_k, cache_slc_v, cache_moba_k, cache_moba_v, cache_win_k, cache_win_v, cache_mem_k, cache_mem_v, page_table, mem_prompt, attn_norm_gain, w_in, nsa_q_gain, nsa_k_gain, cmp_pos_emb, cmp_w1, cmp_b1, cmp_w2, moba_q_gain, moba_k_gain, mem_norm_gain, w_mem_kv, mem_q_gain, mem_k_gain, rel_bias_table, w_branch, w_out, ffn_norm_gain, peer_w_q, peer_sub_keys, peer_u, peer_v):
    assert w_in.shape[0] == 1
    l = 0
    bp, tp, d = x_prompt.shape
    bs, ts, _ = x_sample.shape
    past_len = page_table.shape[1] * cache_cmp_k.shape[2]
    n_p, n_s = bp * tp, bs * ts

    x_all = jnp.concatenate([x_prompt.reshape(n_p, d), x_sample.reshape(n_s, d)], axis=0)
    hgain = _head_gain_row({"nsa_q": nsa_q_gain[l], "nsa_k1": nsa_k_gain[l, 1], "nsa_k2": nsa_k_gain[l, 2],
                            "moba_q": moba_q_gain[l], "moba_k": moba_k_gain[l], "mem_q": mem_q_gain[l]})
    z = _inproj(x_all, attn_norm_gain[l][None, :], _pack_w_in(w_in[l]), hgain)
    zp, zs = "prompt", "sample"

    def rows(group, name):
        col = _zcol(z, name, LANES)
        return col[:n_p].reshape(bp, tp, LANES) if group == zp else col[n_p:].reshape(bs, ts, LANES)

    kc_rows_p, vc_rows_p = rows(zp, "kc"), rows(zp, "vc")
    wk = _compress_weights(cmp_pos_emb[l, 0], cmp_w1[l, 0], cmp_b1[l, 0], cmp_w2[l, 0])
    wv = _compress_weights(cmp_pos_emb[l, 1], cmp_w1[l, 1], cmp_b1[l, 1], cmp_w2[l, 1])
    sub = lambda r: r.reshape(bp, tp // CMP_STRIDE, CMP_STRIDE * LANES)
    kgain = jnp.tile(nsa_k_gain[l, 0], 2)[None, :]
    kc_p = _compress(sub(kc_rows_p), wk, kgain, True)
    vc_p = _compress(sub(vc_rows_p), wv, kgain, False)
    o_nsa_p = _nsa_prompt(z, kc_p, vc_p, rel_bias_table, bp, tp)
    o_moba_p = _moba_prompt(z, rel_bias_table, bp, tp)
    m_len = mem_prompt.shape[1]
    mk_p, mv_p = _memory_kv_pallas(mem_prompt.reshape(bp * m_len, d), mem_norm_gain[l], w_mem_kv[l], mem_k_gain[l])
    mem_w = MEM_HEADS * MEM_HEAD_DIM
    o_mem_p = _mem_attend_pallas(z, 0, bp, tp, 256, mk_p.reshape(bp, m_len, mem_w), mv_p.reshape(bp, m_len, mem_w))
    mk_p = mk_p.reshape(bp, m_len, MEM_HEADS, MEM_HEAD_DIM)
    mv_p = mv_p.reshape(bp, m_len, MEM_HEADS, MEM_HEAD_DIM)

    keys_on_lanes = lambda c: jnp.transpose(c[l], (0, 2, 3, 1)).reshape(c.shape[1], LANES, c.shape[2])
    kc_s = _compress_paged(keys_on_lanes(cache_cmp_k), page_table, wk, kgain, True)
    vc_s = _compress_paged(keys_on_lanes(cache_cmp_v), page_table, wv, kgain, False)
    win_len = cache_win_k.shape[2]
    win_k = cache_win_k[l].reshape(bs, win_len, LANES)
    win_v = cache_win_v[l].reshape(bs, win_len, LANES)
    o_nsa_s = _nsa_sample(z, n_p, kc_s, vc_s, keys_on_lanes(cache_slc_k), keys_on_lanes(cache_slc_v),
                          keys_on_lanes(cache_win_k), keys_on_lanes(cache_win_v), page_table, rel_bias_table, ts)
    o_moba_s = _moba_sample(z, n_p, keys_on_lanes(cache_moba_k), keys_on_lanes(cache_moba_v), page_table,
                            rel_bias_table, ts)
    o_mem_s = _mem_attend_pallas(z, n_p, bs, ts, ts, cache_mem_k[l].reshape(bs, -1, mem_w), cache_mem_v[l].reshape(bs, -1, mem_w))
    kw_all = jnp.concatenate([win_k, rows(zs, "kw")], axis=1)
    vw_all = jnp.concatenate([win_v, rows(zs, "vw")], axis=1)

    cat = lambda a, b_: jnp.concatenate([a.reshape(n_p, 512), b_.reshape(n_s, 512)], axis=0)
    x1, xn_b = _merge(cat(o_nsa_p, o_nsa_s), cat(o_moba_p, o_moba_s), cat(o_mem_p, o_mem_s), z, x_all,
                      w_branch[l], w_out[l], ffn_norm_gain[l])
    y = _peer(x1, xn_b, peer_w_q[l], peer_sub_keys[l], peer_u[l].astype(BF16), peer_v[l].astype(BF16).T)
    y_p = y[:n_p].reshape(bp, tp, d)
    y_s = y[n_p:].reshape(bs, ts, d)

    def st(r, b_, t_):
        return r.reshape(1, b_, t_, 2, HEAD_DIM)

    keep_p = min(WINDOW, tp)
    p_win_k = st(rows(zp, "kw")[:, tp - keep_p:], bp, keep_p)
    p_win_v = st(rows(zp, "vw")[:, tp - keep_p:], bp, keep_p)
    keep_s = min(WINDOW, past_len + ts)
    s_win_k = st(kw_all[:, kw_all.shape[1] - keep_s:], bs, keep_s)
    s_win_v = st(vw_all[:, vw_all.shape[1] - keep_s:], bs, keep_s)
    return (y_p, y_s,
            st(kc_rows_p, bp, tp), st(vc_rows_p, bp, tp), st(rows(zp, "ks"), bp, tp), st(rows(zp, "vs"), bp, tp),
            st(rows(zp, "k_m"), bp, tp), st(rows(zp, "v_m"), bp, tp), p_win_k, p_win_v, mk_p[None], mv_p[None],
            st(rows(zs, "kc"), bs, ts), st(rows(zs, "vc"), bs, ts), st(rows(zs, "ks"), bs, ts), st(rows(zs, "vs"), bs, ts),
            st(rows(zs, "k_m"), bs, ts), st(rows(zs, "v_m"), bs, ts), s_win_k, s_win_v)
```

```python
import functools
import math

import jax
import jax.numpy as jnp
import numpy as np
from jax import lax
from jax.experimental import pallas as pl
from jax.experimental.pallas import tpu as pltpu

HEAD_DIM = 64
NSA_HEADS = 8
NSA_KV = 2
HPG = 4
CMP_LEN = 32
CMP_STRIDE = 16
SLC_LEN = 64
SLC_TOPN = 16
WINDOW = 512
FORCE_BONUS = 1000.0
MOBA_HEADS = 8
MOBA_KV = 2
MOBA_BLOCK = 256
MOBA_TOPK = 3
MEM_HEADS = 4
MEM_HEAD_DIM = 128
REL_BUCKETS = 32
REL_MAX_DIST = 128
PEER_HEADS = 8
PEER_NKEYS = 128
PEER_TOPK = 16
RMS_EPS = 1e-6
NEG_INF = -1e30
HALF_NEG = 0.5 * NEG_INF
M_FLOOR = 0.1 * NEG_INF
KNOCKED = -3.0e38
LOG2E = 1.4426950408889634
LANES = 128
VMEM_LIMIT = 56 * 1024 * 1024
ATT_TQ = 128
ATT_TK = 128

F32 = jnp.float32
BF16 = jnp.bfloat16
HI = lax.Precision.HIGHEST
_NT = (((1,), (1,)), ((), ()))


def _rel_bucket(dist):
    n = jnp.maximum(dist, 0)
    exact = REL_BUCKETS // 2
    nf = jnp.maximum(n, 1).astype(F32)
    large = exact + (jnp.log(nf / exact) / math.log(REL_MAX_DIST / exact) * (REL_BUCKETS - exact)).astype(jnp.int32)
    return jnp.where(n < exact, n, jnp.minimum(large, REL_BUCKETS - 1))


def _bias_lookup(tbl, dist):
    onehot = jax.nn.one_hot(_rel_bucket(dist), REL_BUCKETS, dtype=F32)
    return jnp.tensordot(onehot, tbl, axes=1, precision=HI)


def _gelu_tanh(x):
    inner = x * (0.7978845608028654 + 0.035677408136300125 * (x * x))
    return (0.5 * x) * (1.0 + jnp.tanh(inner))


def _split_hi_lo(x):
    hi = x.astype(BF16)
    lo = (x - hi.astype(F32)).astype(BF16)
    return hi, lo


def _dot3(a, b_hi, b_lo):
    a_hi, a_lo = _split_hi_lo(a)
    return (jnp.dot(a_hi, b_hi, preferred_element_type=F32) + jnp.dot(a_lo, b_hi, preferred_element_type=F32)
            + jnp.dot(a_hi, b_lo, preferred_element_type=F32))


def _seg_matrix(hd):
    i = np.arange(LANES)
    return jnp.asarray((i[:, None] // hd) == (i[None, :] // hd), BF16)


def _group_sumsq(z, seg):
    sq_hi, sq_lo = _split_hi_lo(z * z)
    return jnp.dot(sq_hi, seg, preferred_element_type=F32) + jnp.dot(sq_lo, seg, preferred_element_type=F32)


def _params(sem):
    return pltpu.CompilerParams(dimension_semantics=sem, vmem_limit_bytes=VMEM_LIMIT)


_Z_GROUPS = (
    ("g_merge", 3072, 0), ("q_n", 512, 64), ("q_m", 512, 64), ("q_x", 512, 128),
    ("kc", 128, 0), ("vc", 128, 0), ("ks", 128, 64), ("vs", 128, 0), ("kw", 128, 64), ("vw", 128, 0),
    ("k_m", 128, 64), ("v_m", 128, 0), ("g_n", 128, 0),
)
_Z_OFF = {}
_o = 0
for _n, _w, _h in _Z_GROUPS:
    _Z_OFF[_n] = _o
    _o += _w
Z_COLS = _o
IN_SPLITS = (512, 128, 128, 128, 128, 128, 128, 24, 512, 128, 128, 512, 3072)
_SRC_NAMES = ("q_n", "kc", "vc", "ks", "vs", "kw", "vw", "g_n", "q_m", "k_m", "v_m", "q_x", "g_merge")


def _pack_w_in(w_in):
    cuts = np.cumsum((0,) + IN_SPLITS)
    parts = {n: w_in[:, cuts[i]:cuts[i + 1]] for i, n in enumerate(_SRC_NAMES)}
    parts["g_n"] = jnp.pad(parts["g_n"], ((0, 0), (0, LANES - 24)))
    return jnp.concatenate([parts[n] for n, _, _ in _Z_GROUPS], axis=1).astype(BF16)


def _head_gain_row(gains):
    g = {
        "q_n": jnp.tile(gains["nsa_q"], 8), "ks": jnp.tile(gains["nsa_k1"], 2), "kw": jnp.tile(gains["nsa_k2"], 2),
        "q_m": jnp.tile(gains["moba_q"], 8), "k_m": jnp.tile(gains["moba_k"], 2), "q_x": jnp.tile(gains["mem_q"], 4),
    }
    return jnp.concatenate([g.get(n, jnp.ones((w,), F32)) for n, w, _ in _Z_GROUPS])[None, :]


def _inproj_kernel(x_ref, gain_ref, w_ref, hgain_ref, seg64_ref, seg128_ref, z_ref):
    x = x_ref[...]
    ms = jnp.mean(x * x, axis=-1, keepdims=True)
    xb = (x * lax.rsqrt(ms + RMS_EPS) * gain_ref[...]).astype(BF16)
    for name, width, hd in _Z_GROUPS:
        off = _Z_OFF[name]
        for c in range(0, width, 512):
            cw = min(512, width - c)
            lo_c, hi_c = off + c, off + c + cw
            z = jnp.dot(xb, w_ref[:, lo_c:hi_c], preferred_element_type=F32)
            if hd:
                seg = seg64_ref[...] if hd == 64 else seg128_ref[...]
                for j in range(0, cw, LANES):
                    zj = z[:, j:j + LANES]
                    ss = _group_sumsq(zj, seg)
                    z_ref[:, lo_c + j:lo_c + j + LANES] = (
                        zj * lax.rsqrt(ss * (1.0 / hd) + RMS_EPS) * hgain_ref[:, lo_c + j:lo_c + j + LANES])
            else:
                z_ref[:, lo_c:hi_c] = z


def _inproj(x, gain, w_packed, hgain, tm=256):
    n, d = x.shape
    assert n % tm == 0
    const = lambda i: (0, 0)
    return pl.pallas_call(
        _inproj_kernel,
        grid=(n // tm,),
        in_specs=[
            pl.BlockSpec((tm, d), lambda i: (i, 0)),
            pl.BlockSpec((1, d), const),
            pl.BlockSpec((d, Z_COLS), const),
            pl.BlockSpec((1, Z_COLS), const),
            pl.BlockSpec((LANES, LANES), const),
            pl.BlockSpec((LANES, LANES), const),
        ],
        out_specs=pl.BlockSpec((tm, Z_COLS), lambda i: (i, 0)),
        out_shape=jax.ShapeDtypeStruct((n, Z_COLS), F32),
        compiler_params=_params(("arbitrary",)),
        name="inproj",
    )(x, gain, w_packed, hgain, _seg_matrix(64), _seg_matrix(128))


def _compress_weights(pe, w1, b1, w2):
    w1r = w1.reshape(2, CMP_STRIDE, HEAD_DIM, -1)
    eye = jnp.eye(2, dtype=F32)
    wbig = jnp.einsum("rlde,gh->lgdrhe", w1r, eye).reshape(CMP_STRIDE * LANES, 2 * LANES)
    pos = jnp.einsum("rld,rlde->e", pe.reshape(2, CMP_STRIDE, HEAD_DIM), w1r, precision=HI)
    cvec = jnp.tile(b1 + pos, 2)[None, :]
    w2big = jnp.einsum("ed,gh->gehd", w2, eye).reshape(LANES, LANES)
    return _split_hi_lo(wbig) + (cvec,) + _split_hi_lo(w2big)


def _compress_kernel(x_ref, wh_ref, wl_ref, c_ref, w2h_ref, w2l_ref, seg_ref, gain_ref, o_ref, *, norm):
    ns = x_ref.shape[1]
    a = _dot3(x_ref[0], wh_ref[...], wl_ref[...])
    nxt = pltpu.roll(a[:, LANES:], ns - 1, axis=0)
    h = a[:, :LANES] + nxt + c_ref[...]
    o = _dot3(_gelu_tanh(h), w2h_ref[...], w2l_ref[...])
    if norm:
        o = o * lax.rsqrt(_group_sumsq(o, seg_ref[...]) * (1.0 / HEAD_DIM) + RMS_EPS) * gain_ref[...]
    o_ref[0] = o


def _compress(x_sub, weights, gain, norm):
    b, ns, kk = x_sub.shape
    wh, wl, cvec, w2h, w2l = weights
    const = lambda i: (0, 0)
    return pl.pallas_call(
        functools.partial(_compress_kernel, norm=norm),
        grid=(b,),
        in_specs=[
            pl.BlockSpec((1, ns, kk), lambda i: (i, 0, 0)),
            pl.BlockSpec((kk, 2 * LANES), const), pl.BlockSpec((kk, 2 * LANES), const),
            pl.BlockSpec((1, LANES), const),
            pl.BlockSpec((LANES, LANES), const), pl.BlockSpec((LANES, LANES), const),
            pl.BlockSpec((LANES, LANES), const), pl.BlockSpec((1, LANES), const),
        ],
        out_specs=pl.BlockSpec((1, ns, LANES), lambda i: (i, 0, 0)),
        out_shape=jax.ShapeDtypeStruct((b, ns, LANES), F32),
        compiler_params=_params(("arbitrary",)),
        name="nsa_compress",
    )(x_sub, wh, wl, cvec, w2h, w2l, _seg_matrix(HEAD_DIM), gain)


def _toeplitz_tiles(tbl, deltas, tq, tk, hi):
    d = (jnp.asarray(deltas, jnp.int32)[:, None, None] + jnp.arange(tq)[None, :, None] - jnp.arange(tk)[None, None, :])
    b = jnp.where(((d >= 0) & (d < hi))[..., None, None], _bias_lookup(tbl, d), NEG_INF)
    return jnp.transpose(b, (3, 0, 4, 1, 2)).reshape(tbl.shape[1], len(deltas), HPG * tq, tk)


def _with_masked_tile(tiles):
    return jnp.concatenate([jnp.full_like(tiles[:, :1], NEG_INF), tiles], axis=1)


def _stack_queries(zq_ref, write, tq):
    lane = lax.broadcasted_iota(jnp.int32, (tq, LANES), 1)
    for g in range(2):
        keep = (lane < HEAD_DIM) if g == 0 else (lane >= HEAD_DIM)
        for h in range(HPG):
            hd = HPG * g + h
            chunk = zq_ref[:, LANES * (hd // 2):LANES * (hd // 2 + 1)] * (HEAD_DIM ** -0.5)
            if hd % 2 != g:
                chunk = pltpu.roll(chunk, HEAD_DIM, axis=1)
            write(g, h, jnp.where(keep, chunk, 0.0))


def _stack_queries_prompt(zq_ref, q_scr, tq):
    def write(g, h, val):
        q_scr[g, h * tq:(h + 1) * tq, :] = val.astype(BF16)
    _stack_queries(zq_ref, write, tq)


def _stack_queries_sample(zq_ref, q_scr, ts):
    def write(g, h, val):
        q_scr[(g * HPG + h) * ts:(g * HPG + h + 1) * ts, :] = val
    _stack_queries(zq_ref, write, ts)


def _unstack_heads(parts, o_ref, tq):
    lane = lax.broadcasted_iota(jnp.int32, (tq, LANES), 1)
    for j in range(4):
        g = j // 2
        he = (2 * j) % HPG
        left = parts[g][he * tq:(he + 1) * tq, :]
        right = parts[g][(he + 1) * tq:(he + 2) * tq, :]
        if g == 0:
            right = pltpu.roll(right, HEAD_DIM, axis=1)
        else:
            left = pltpu.roll(left, HEAD_DIM, axis=1)
        o_ref[:, LANES * j:LANES * (j + 1)] = jnp.where(lane < HEAD_DIM, left, right)


KEY_TILES_PER_STEP = 4


def _flash(qs, k_ref, v_ref, bias_ref, tile_index, bms, blk_shift, rep, kb_lo, kb_hi, qt, m_scr, l_scr, acc_scr, tk):
    m_scr[...] = jnp.full(m_scr.shape, M_FLOOR, F32)
    l_scr[...] = jnp.zeros(l_scr.shape, F32)
    acc_scr[...] = jnp.zeros(acc_scr.shape, F32)
    nk = tk * KEY_TILES_PER_STEP
    blk_row = lax.broadcasted_iota(jnp.int32, (LANES, nk), 0)
    key_lane = lax.broadcasted_iota(jnp.int32, (LANES, nk), 1)

    def body(kb, carry):
        k0 = pl.multiple_of(kb * nk, nk)
        k = k_ref[pl.ds(k0, nk), :].astype(BF16)
        v = v_ref[pl.ds(k0, nk), :].astype(BF16)
        tiles = [tile_index(qt - (kb * KEY_TILES_PER_STEP + c)) for c in range(KEY_TILES_PER_STEP)]
        if bms is not None:
            expand = jnp.where(blk_row == ((k0 + key_lane) >> blk_shift), 1.0, 0.0).astype(BF16)
        for g, q in enumerate(qs):
            s = lax.dot_general(q, k, _NT, preferred_element_type=F32)
            s = s + jnp.concatenate([bias_ref[g, t] for t in tiles], axis=1)
            if bms is not None:
                vis = jnp.dot(bms[g], expand, preferred_element_type=F32)
                if rep > 1:
                    s = s + jnp.concatenate([jnp.where(vis > 0.5, 0.0, NEG_INF)] * rep, axis=0)
                else:
                    s = jnp.where(vis > 0.5, s, NEG_INF)
            m_old = m_scr[g]
            m_new = jnp.maximum(m_old, jnp.max(s, axis=-1, keepdims=True))
            p = jnp.exp(s - m_new)
            alpha = jnp.exp(m_old - m_new)
            l_scr[g] = alpha * l_scr[g] + jnp.sum(p, axis=-1, keepdims=True)
            acc_scr[g] = alpha * acc_scr[g] + jnp.dot(p.astype(BF16), v, preferred_element_type=F32)
            m_scr[g] = m_new
        return carry

    lax.fori_loop(kb_lo, kb_hi, body, 0)
    outs = []
    for g in range(len(qs)):
        l = l_scr[g]
        outs.append(acc_scr[g] / jnp.where(l > 0.0, l, 1.0))
    return outs


def _causal_tile_index(n_near):
    return lambda rel: jnp.clip(rel + 1, 0, n_near)


def _window_tile_index(n_win):
    return lambda rel: jnp.where(rel >= n_win, 0, jnp.clip(rel + 1, 0, n_win))


def _rank_select(score, count, lane, n_valid):
    width = score.shape[-1]
    rank = jnp.zeros(score.shape, F32)
    for s in range(1, width):
        if n_valid <= s <= width - n_valid:
            continue
        other = pltpu.roll(score, s, axis=1)
        rank = rank + jnp.where(other > score, 1.0, 0.0)
        if s < n_valid:
            rank = rank + jnp.where(other == score, jnp.where(lane >= s, 1.0, 0.0), 0.0)
    return jnp.where((rank < count) & (score > HALF_NEG), 1.0, 0.0)


def _take_topk(score, count, lane):
    sel = jnp.zeros(score.shape, F32)
    cur = score
    lane_f = lane.astype(F32)
    for _ in range(count):
        mx = jnp.max(cur, axis=-1, keepdims=True)
        idx = jnp.min(jnp.where(cur == mx, lane_f, float(score.shape[-1])), axis=-1, keepdims=True)
        hit = lane_f == idx
        sel = jnp.where(hit, jnp.where(mx > HALF_NEG, 1.0, 0.0), sel)
        cur = jnp.where(hit, KNOCKED, cur)
    return sel


def _nsa_prompt_kernel(zq_ref, gn_ref, kc_ref, vc_ref, ks_ref, vs_ref, kw_ref, vw_ref, biasc_ref, bslc_ref, bwin_ref,
                       mslc_ref, o_ref, q_scr, m_scr, l_scr, acc_scr, comb_scr, *, tq, tk, n_slc, n_slc_tiles, n_win_tiles):
    qt = pl.program_id(1)
    q0 = qt * tq
    _stack_queries_prompt(zq_ref, q_scr, tq)
    gn = jax.nn.sigmoid(gn_ref[...])
    lane = lax.broadcasted_iota(jnp.int32, (tq, LANES), 1)
    tpos = q0 + lax.broadcasted_iota(jnp.int32, (tq, LANES), 0)
    qb = tpos >> 6

    def gate_col(c, g):
        return jnp.concatenate([gn[:, c * 8 + g * HPG + h:c * 8 + g * HPG + h + 1] for h in range(HPG)], axis=0)

    kc = kc_ref[0].astype(BF16)
    vc = vc_ref[0].astype(BF16)
    qs = [q_scr[g] for g in range(NSA_KV)]
    bms = []
    for g, q in enumerate(qs):
        s_c = lax.dot_general(q, kc, _NT, preferred_element_type=F32) + biasc_ref[g]
        mx = jnp.maximum(jnp.max(s_c, axis=-1, keepdims=True), M_FLOOR)
        p = jnp.exp(s_c - mx)
        den = jnp.sum(p, axis=-1, keepdims=True)
        p_c = p / jnp.where(den > 0.0, den, 1.0)
        comb_scr[g] = gate_col(0, g) * jnp.dot(p_c.astype(BF16), vc, preferred_element_type=F32)
        imp = p_c[0:tq] + p_c[tq:2 * tq] + p_c[2 * tq:3 * tq] + p_c[3 * tq:4 * tq]
        imp_hi, imp_lo = _split_hi_lo(imp)
        p_slc = (jnp.dot(imp_hi, mslc_ref[...], preferred_element_type=F32)
                 + jnp.dot(imp_lo, mslc_ref[...], preferred_element_type=F32))
        forced = (lane == 0) | (lane == qb) | (lane == qb - 1)
        score = jnp.where(lane > qb, NEG_INF, p_slc + jnp.where(forced, FORCE_BONUS, 0.0))
        bms.append(lax.cond(q0 + tq <= SLC_TOPN * SLC_LEN,
                            lambda score=score: jnp.where(score > HALF_NEG, 1.0, 0.0),
                            lambda score=score: _rank_select(score, SLC_TOPN, lane, n_slc)).astype(BF16))
    kb_hi = qt // KEY_TILES_PER_STEP + 1
    o_s = _flash(qs, ks_ref, vs_ref, bslc_ref, _causal_tile_index(n_slc_tiles), bms, 6, HPG, 0, kb_hi, qt,
                 m_scr, l_scr, acc_scr, tk)
    kb_lo = jnp.maximum(qt - (n_win_tiles - 1), 0) // KEY_TILES_PER_STEP
    o_w = _flash(qs, kw_ref, vw_ref, bwin_ref, _window_tile_index(n_win_tiles), None, 0, 1, kb_lo, kb_hi, qt,
                 m_scr, l_scr, acc_scr, tk)
    for g in range(NSA_KV):
        comb_scr[g] = comb_scr[g] + gate_col(1, g) * o_s[g] + gate_col(2, g) * o_w[g]
    _unstack_heads([comb_scr[0], comb_scr[1]], o_ref, tq)


def _zspec(rows, name, width, row_map):
    cb = _Z_OFF[name] // width
    assert _Z_OFF[name] % width == 0
    return pl.BlockSpec((rows, width), lambda b, i: (row_map(b, i), cb))


def _nsa_prompt(z, kc, vc, rel_table, bp, tp):
    tq, tk = ATT_TQ, ATT_TK
    nq = tp // tq
    n_cmp = tp // CMP_STRIDE - 1
    n_slc = tp // SLC_LEN
    assert tp % tq == 0 and n_cmp < LANES and n_slc <= LANES and kc.shape[1] == LANES
    tbl = rel_table[:, :NSA_HEADS].reshape(REL_BUCKETS, NSA_KV, HPG)
    t = jnp.arange(tp)
    dist_c = t[:, None] - (jnp.arange(LANES) * CMP_STRIDE + CMP_LEN - 1)[None, :]
    valid_c = (dist_c >= 0) & (jnp.arange(LANES) < n_cmp)[None, :]
    biasc = jnp.where(valid_c[..., None, None], _bias_lookup(tbl, dist_c), NEG_INF)
    biasc = jnp.transpose(biasc.reshape(nq, tq, LANES, NSA_KV, HPG), (0, 3, 4, 1, 2)).reshape(nq, NSA_KV, HPG * tq, LANES)
    slc_deltas = (0, tq, 2 * tq)
    win_deltas = tuple(range(0, WINDOW + 1, tq))
    assert tp % (tk * KEY_TILES_PER_STEP) == 0 and n_slc <= LANES // 2
    bslc = _with_masked_tile(_toeplitz_tiles(tbl, slc_deltas, tq, tk, 1 << 30))
    bwin = _with_masked_tile(_toeplitz_tiles(tbl, win_deltas, tq, tk, WINDOW))
    mslc = jnp.pad(_slc_matrix(n_cmp, n_slc), ((0, LANES - n_cmp), (0, LANES - n_slc))).astype(BF16)
    kern = functools.partial(_nsa_prompt_kernel, tq=tq, tk=tk, n_slc=n_slc, n_slc_tiles=len(slc_deltas),
                             n_win_tiles=len(win_deltas))
    seq = lambda b, i: b
    tile = lambda b, i: b * nq + i
    full4 = lambda b, i: (0, 0, 0, 0)
    return pl.pallas_call(
        kern,
        grid=(bp, nq),
        in_specs=[
            _zspec(tq, "q_n", 512, tile), _zspec(tq, "g_n", LANES, tile),
            pl.BlockSpec((1, LANES, LANES), lambda b, i: (b, 0, 0)), pl.BlockSpec((1, LANES, LANES), lambda b, i: (b, 0, 0)),
        ] + [pl.BlockSpec((tp, LANES), (lambda b, i, cb=_Z_OFF[nm] // LANES: (b, cb))) for nm in ("ks", "vs", "kw", "vw")] + [
            pl.BlockSpec((None, NSA_KV, HPG * tq, LANES), lambda b, i: (i, 0, 0, 0)),
            pl.BlockSpec(bslc.shape, full4), pl.BlockSpec(bwin.shape, full4),
            pl.BlockSpec((LANES, LANES), lambda b, i: (0, 0)),
        ],
        out_specs=pl.BlockSpec((tq, 512), lambda b, i: (b * nq + i, 0)),
        out_shape=jax.ShapeDtypeStruct((bp * tp, 512), F32),
        scratch_shapes=[
            pltpu.VMEM((NSA_KV, HPG * tq, LANES), BF16),
            pltpu.VMEM((2, HPG * tq, 1), F32), pltpu.VMEM((2, HPG * tq, 1), F32), pltpu.VMEM((2, HPG * tq, LANES), F32),
            pltpu.VMEM((NSA_KV, HPG * tq, LANES), F32),
        ],
        compiler_params=_params(("arbitrary", "arbitrary")),
        name="nsa_prompt",
    )(z, z, kc, vc, z, z, z, z, biasc, bslc, bwin, mslc)


def _slc_matrix(n_cmp, n_slc):
    ratio, c_sub = SLC_LEN // CMP_STRIDE, CMP_LEN // CMP_STRIDE
    m = np.zeros((n_cmp, n_slc), np.float32)
    for j in range(n_slc):
        for o in range(-(c_sub - 1), ratio):
            k = ratio * j + o
            if 0 <= k < n_cmp:
                m[k, j] += float(min(o + c_sub, ratio) - max(o, 0))
    return jnp.asarray(m)


def _moba_prompt_kernel(zq_ref, k_ref, v_ref, bias_ref, o_ref, q_scr, m_scr, l_scr, acc_scr, *, tq, tk, nb, n_tiles):
    qt = pl.program_id(1)
    q0 = qt * tq
    _stack_queries_prompt(zq_ref, q_scr, tq)
    rows = HPG * tq
    lane = lax.broadcasted_iota(jnp.int32, (rows, LANES), 1)
    trow = q0 + (lax.broadcasted_iota(jnp.int32, (rows, LANES), 0) & (tq - 1))
    own = trow >> 8
    kmean = jnp.concatenate(
        [jnp.mean(k_ref[n * MOBA_BLOCK:(n + 1) * MOBA_BLOCK, :], axis=0, keepdims=True) for n in range(nb)]
        + [jnp.zeros((LANES - nb, LANES), F32)], axis=0)
    km_hi, km_lo = _split_hi_lo(kmean)
    qs = [q_scr[g] for g in range(MOBA_KV)]
    bms = []
    for q in qs:
        gs = (lax.dot_general(q, km_hi, _NT, preferred_element_type=F32)
              + lax.dot_general(q, km_lo, _NT, preferred_element_type=F32))
        gs = jnp.where(lane < own, gs, NEG_INF)
        sel = _take_topk(gs, min(MOBA_TOPK, nb - 1), lane)
        bms.append(jnp.where(lane == own, 1.0, sel).astype(BF16))
    res = _flash(qs, k_ref, v_ref, bias_ref, _causal_tile_index(n_tiles), bms, 8, 1, 0,
                 qt // KEY_TILES_PER_STEP + 1, qt, m_scr, l_scr, acc_scr, tk)
    _unstack_heads(res, o_ref, tq)


def _moba_prompt(z, rel_table, bp, tp):
    tq, tk = ATT_TQ, ATT_TK
    nq = tp // tq
    nb = tp // MOBA_BLOCK
    assert tp % MOBA_BLOCK == 0 and tq & (tq - 1) == 0
    tbl = rel_table[:, NSA_HEADS:].reshape(REL_BUCKETS, MOBA_KV, HPG)
    deltas = (0, tq, 2 * tq)
    assert tp % (tk * KEY_TILES_PER_STEP) == 0
    bias = _with_masked_tile(_toeplitz_tiles(tbl, deltas, tq, tk, 1 << 30))
    kern = functools.partial(_moba_prompt_kernel, tq=tq, tk=tk, nb=nb, n_tiles=len(deltas))
    return pl.pallas_call(
        kern,
        grid=(bp, nq),
        in_specs=[
            _zspec(tq, "q_m", 512, lambda b, i: b * nq + i),
            pl.BlockSpec((tp, LANES), lambda b, i: (b, _Z_OFF["k_m"] // LANES)),
            pl.BlockSpec((tp, LANES), lambda b, i: (b, _Z_OFF["v_m"] // LANES)),
            pl.BlockSpec(bias.shape, lambda b, i: (0, 0, 0, 0)),
        ],
        out_specs=pl.BlockSpec((tq, 512), lambda b, i: (b * nq + i, 0)),
        out_shape=jax.ShapeDtypeStruct((bp * tp, 512), F32),
        scratch_shapes=[
            pltpu.VMEM((MOBA_KV, HPG * tq, LANES), BF16),
            pltpu.VMEM((2, HPG * tq, 1), F32), pltpu.VMEM((2, HPG * tq, 1), F32), pltpu.VMEM((2, HPG * tq, LANES), F32),
        ],
        compiler_params=_params(("arbitrary", "arbitrary")),
        name="moba_prompt",
    )(z, z, z, bias)


def _merge_kernel(on_ref, om_ref, ox_ref, gm_ref, x_ref, wb_ref, wo_ref, fg_ref, x1_ref, xn_ref):
    d = x_ref.shape[1]
    merged = jnp.zeros(x_ref.shape, F32)
    for c, o_ref in enumerate((on_ref, om_ref, ox_ref)):
        proj = jnp.dot(o_ref[...].astype(BF16), wb_ref[c], preferred_element_type=F32)
        merged = merged + jax.nn.sigmoid(gm_ref[:, c * d:(c + 1) * d]) * proj
    x1 = x_ref[...] + jnp.dot(merged.astype(BF16), wo_ref[...], preferred_element_type=F32)
    x1_ref[...] = x1
    ms = jnp.mean(x1 * x1, axis=-1, keepdims=True)
    xn_ref[...] = (x1 * lax.rsqrt(ms + RMS_EPS) * fg_ref[...]).astype(BF16)


def _merge(o_nsa, o_moba, o_mem, z, x, w_branch, w_out, ffn_gain, tm=256):
    n, d = x.shape
    assert n % tm == 0 and _Z_OFF["g_merge"] == 0
    row = lambda i: (i, 0)
    return pl.pallas_call(
        _merge_kernel,
        grid=(n // tm,),
        in_specs=[
            pl.BlockSpec((tm, 512), row), pl.BlockSpec((tm, 512), row), pl.BlockSpec((tm, 512), row),
            pl.BlockSpec((tm, 3 * d), row), pl.BlockSpec((tm, d), row),
            pl.BlockSpec((3, 512, d), lambda i: (0, 0, 0)), pl.BlockSpec((d, d), lambda i: (0, 0)),
            pl.BlockSpec((1, d), lambda i: (0, 0)),
        ],
        out_specs=[pl.BlockSpec((tm, d), row), pl.BlockSpec((tm, d), row)],
        out_shape=[jax.ShapeDtypeStruct((n, d), F32), jax.ShapeDtypeStruct((n, d), BF16)],
        compiler_params=_params(("arbitrary",)),
        name="merge_outproj",
    )(o_nsa, o_moba, o_mem, z, x, w_branch.astype(BF16), w_out.astype(BF16), ffn_gain[None, :])


def _peer_route_kernel(xT_ref, wqh_ref, wql_ref, skh_ref, skl_ref, s1_ref, s2_ref, tau_ref):
    xT = xT_ref[...]
    qT = jnp.dot(wqh_ref[...], xT, preferred_element_type=F32) + jnp.dot(wql_ref[...], xT, preferred_element_type=F32)
    half = HEAD_DIM

    def top_sorted(s):
        vals, cur = [], s
        for _ in range(PEER_TOPK + 1):
            mx = jnp.max(cur, axis=0, keepdims=True)
            vals.append(mx)
            cur = jnp.where(cur == mx, KNOCKED, cur)
        return vals

    taus = []
    for h in range(PEER_HEADS):
        sc = []
        for p in range(2):
            r0 = (h * 2 + p) * half
            q_hi, q_lo = _split_hi_lo(qT[r0:r0 + half, :])
            sc.append(jnp.dot(skh_ref[h, p], q_hi, preferred_element_type=F32)
                      + jnp.dot(skl_ref[h, p], q_hi, preferred_element_type=F32)
                      + jnp.dot(skh_ref[h, p], q_lo, preferred_element_type=F32))
        a, b = top_sorted(sc[0]), top_sorted(sc[1])
        n_rank = PEER_TOPK + 1
        cand = jnp.concatenate([a[i] + b[j] for i in range(n_rank) for j in range(n_rank // (i + 1))], axis=0)
        cur = cand
        for _ in range(PEER_TOPK):
            tau = jnp.max(cur, axis=0, keepdims=True)
            cur = jnp.where(cur == tau, KNOCKED, cur)
        nxt = jnp.max(cur, axis=0, keepdims=True)
        top = a[0] + b[0]
        zsum = jnp.sum(jnp.where(cand >= tau, jnp.exp(cand - top), 0.0), axis=0, keepdims=True)
        shift = top + jnp.log(zsum)
        s1_ref[h] = (sc[0] - shift) * LOG2E
        s2_ref[h] = sc[1] * LOG2E
        taus.append((0.5 * (tau + nxt) - shift) * LOG2E)
    tau_ref[...] = jnp.concatenate(taus, axis=0)


def _peer_route(xT, peer_w_q, sub_keys, tt=256):
    d, n = xT.shape
    assert n % tt == 0
    wqh, wql = _split_hi_lo(peer_w_q.T)
    skh, skl = _split_hi_lo(sub_keys)
    tok3 = lambda i: (0, 0, i)
    big = jax.ShapeDtypeStruct((PEER_HEADS, PEER_NKEYS, n), F32)
    return pl.pallas_call(
        _peer_route_kernel,
        grid=(n // tt,),
        in_specs=[
            pl.BlockSpec((d, tt), lambda i: (0, i)),
            pl.BlockSpec((d, d), lambda i: (0, 0)), pl.BlockSpec((d, d), lambda i: (0, 0)),
            pl.BlockSpec(sub_keys.shape, lambda i: (0, 0, 0, 0)), pl.BlockSpec(sub_keys.shape, lambda i: (0, 0, 0, 0)),
        ],
        out_specs=[pl.BlockSpec((PEER_HEADS, PEER_NKEYS, tt), tok3)] * 2 + [pl.BlockSpec((PEER_HEADS, tt), lambda i: (0, i))],
        out_shape=[big] * 2 + [jax.ShapeDtypeStruct((PEER_HEADS, n), F32)],
        compiler_params=_params(("arbitrary",)),
        name="peer_route",
    )(xT, wqh, wql, skh, skl)


def _peer_dense_kernel(xT_ref, u_ref, vT_ref, s1_ref, s2_ref, tau_ref, o_ref, a_ref, *, ec, tt):
    c = pl.program_id(1)

    @pl.when(c == 0)
    def _():
        o_ref[...] = jnp.zeros_like(o_ref)

    h_t = jnp.dot(u_ref[...], xT_ref[...], preferred_element_type=F32)
    for k in range(ec // LANES):
        for lc in range(0, tt, LANES):
            wk = jnp.zeros((LANES, LANES), F32)
            for h in range(PEER_HEADS):
                csum = s2_ref[h, :, lc:lc + LANES] + s1_ref[h, k:k + 1, lc:lc + LANES]
                wk = wk + jnp.where(csum >= tau_ref[h:h + 1, lc:lc + LANES], jnp.exp2(csum), 0.0)
            g = _gelu_tanh(h_t[k * LANES:(k + 1) * LANES, lc:lc + LANES])
            a_ref[k * LANES:(k + 1) * LANES, lc:lc + LANES] = (wk * g).astype(BF16)
    o_ref[...] += jnp.dot(vT_ref[...], a_ref[...], preferred_element_type=F32)


def _peer_dense(xT, u_b, vT_b, s1T, s2T, tauT, ec=1024):
    d, n = xT.shape
    tt = 512 if n % 512 == 0 else 256
    assert n % tt == 0 and ec == 8 * PEER_NKEYS
    ne = u_b.shape[0]
    kern = functools.partial(_peer_dense_kernel, ec=ec, tt=tt)
    allk = pl.BlockSpec((PEER_HEADS, PEER_NKEYS, tt), lambda i, c: (0, 0, i))
    chunk = pl.BlockSpec((PEER_HEADS, ec // PEER_NKEYS, tt), lambda i, c: (0, c, i))
    return pl.pallas_call(
        kern,
        grid=(n // tt, ne // ec),
        in_specs=[
            pl.BlockSpec((d, tt), lambda i, c: (0, i)),
            pl.BlockSpec((ec, d), lambda i, c: (c, 0)),
            pl.BlockSpec((d, ec), lambda i, c: (0, c)),
            chunk, allk,
            pl.BlockSpec((PEER_HEADS, tt), lambda i, c: (0, i)),
        ],
        out_specs=pl.BlockSpec((d, tt), lambda i, c: (0, i)),
        out_shape=jax.ShapeDtypeStruct((d, n), F32),
        scratch_shapes=[pltpu.VMEM((ec, tt), BF16)],
        compiler_params=_params(("arbitrary", "arbitrary")),
        name="peer_dense",
    )(xT, u_b, vT_b, s1T, s2T, tauT)


def _peer(x1, xn_b, peer_w_q, sub_keys, u_b, vT_b):
    xT = xn_b.T
    s1T, s2T, tauT = _peer_route(xT, peer_w_q, sub_keys)
    out_t = _peer_dense(xT, u_b, vT_b, s1T, s2T, tauT)
    return x1 + out_t.T


PAGES_PER_STEP = 32


def _page_specs(n_lead, inner):
    zeros = (0,) * len(inner)
    return [pl.BlockSpec((1,) + inner, (lambda b, j, pt, p=p: (pt[b, j * PAGES_PER_STEP + p],) + zeros))
            for p in range(n_lead)]


def _compress_paged_kernel(pt_ref, *refs, norm):
    del pt_ref
    p_n = PAGES_PER_STEP
    x_refs = refs[:p_n]
    perm_ref, wh_ref, c_ref, w2h_ref, w2l_ref, seg_ref, gain_ref, o_ref, x_scr = refs[p_n:]
    j = pl.program_id(1)
    pair_sub = 2 * x_refs[0].shape[2] // CMP_STRIDE
    for pp in range(p_n // 2):
        x_t = jnp.concatenate([x_refs[2 * pp][0], x_refs[2 * pp + 1][0]], axis=1).astype(BF16)
        rows = lax.dot_general(perm_ref[...], x_t, _NT, preferred_element_type=F32).astype(BF16)
        base = pl.multiple_of((j * (p_n // 2) + pp) * pair_sub, pair_sub)
        for l in range(CMP_STRIDE):
            x_scr[pl.ds(base, pair_sub), l * LANES:(l + 1) * LANES] = rows[l * pair_sub:(l + 1) * pair_sub, :]

    @pl.when(j == pl.num_programs(1) - 1)
    def _():
        ns = x_scr.shape[0]
        a = jnp.dot(x_scr[...], wh_ref[...], preferred_element_type=F32)
        nxt = pltpu.roll(a[:, LANES:], ns - 1, axis=0)
        h = a[:, :LANES] + nxt + c_ref[...]
        o = _dot3(_gelu_tanh(h), w2h_ref[...], w2l_ref[...])
        if norm:
            o = o * lax.rsqrt(_group_sumsq(o, seg_ref[...]) * (1.0 / HEAD_DIM) + RMS_EPS) * gain_ref[...]
        o_ref[0] = o


def _compress_paged(pool, page_table, weights, gain, norm):
    n_phys, _, page = pool.shape
    b, n_pages = page_table.shape
    sub = page // CMP_STRIDE
    kk = CMP_STRIDE * LANES
    ns = n_pages * sub
    assert n_pages % PAGES_PER_STEP == 0 and page % CMP_STRIDE == 0 and PAGES_PER_STEP % 2 == 0 and (2 * sub) % 16 == 0
    wh, _, cvec, w2h, w2l = weights
    const = lambda b_, j, pt: (0, 0)
    r = np.arange(2 * page)
    perm = jnp.asarray(((r[:, None] // (2 * sub)) + CMP_STRIDE * (r[:, None] % (2 * sub))) == r[None, :], BF16)
    grid_spec = pltpu.PrefetchScalarGridSpec(
        num_scalar_prefetch=1,
        grid=(b, n_pages // PAGES_PER_STEP),
        in_specs=_page_specs(PAGES_PER_STEP, (LANES, page)) + [
            pl.BlockSpec((2 * page, 2 * page), const), pl.BlockSpec((kk, 2 * LANES), const),
            pl.BlockSpec((1, LANES), const),
            pl.BlockSpec((LANES, LANES), const), pl.BlockSpec((LANES, LANES), const),
            pl.BlockSpec((LANES, LANES), const), pl.BlockSpec((1, LANES), const),
        ],
        out_specs=pl.BlockSpec((1, ns, LANES), lambda b_, j, pt: (b_, 0, 0)),
        scratch_shapes=[pltpu.VMEM((ns, kk), BF16)],
    )
    return pl.pallas_call(
        functools.partial(_compress_paged_kernel, norm=norm),
        grid_spec=grid_spec,
        out_shape=jax.ShapeDtypeStruct((b, ns, LANES), F32),
        compiler_params=_params(("arbitrary", "arbitrary")),
        name="nsa_compress_paged",
    )(page_table, *([pool] * PAGES_PER_STEP), perm, wh, cvec, w2h, w2l, _seg_matrix(HEAD_DIM), gain)


def _softmax_tile(s):
    mx = jnp.maximum(jnp.max(s, axis=-1, keepdims=True), M_FLOOR)
    p = jnp.exp(s - mx)
    return mx, p, jnp.sum(p, axis=-1, keepdims=True)


def _pv(p, v, v_t):
    if v_t:
        return lax.dot_general(p.astype(BF16), v, _NT, preferred_element_type=F32)
    return jnp.dot(p.astype(BF16), v, preferred_element_type=F32)


def _online_update(s, v, m_scr, l_scr, acc_scr, v_t=False):
    m_old = m_scr[...]
    m_new = jnp.maximum(m_old, jnp.max(s, axis=-1, keepdims=True))
    p = jnp.exp(s - m_new)
    alpha = jnp.exp(m_old - m_new)
    l_scr[...] = alpha * l_scr[...] + jnp.sum(p, axis=-1, keepdims=True)
    acc_scr[...] = alpha * acc_scr[...] + _pv(p, v, v_t)
    m_scr[...] = m_new


def _pad_rows(x, rows):
    return jnp.concatenate([x, jnp.zeros((rows - x.shape[0], x.shape[1]), x.dtype)], axis=0)


def _sample_row_bias(tbl, ts, kpos, qpos0, lo_ok):
    d = (qpos0 + jnp.arange(ts))[:, None] - kpos[None, :]
    b = jnp.where(((d >= 0) & lo_ok)[..., None, None], _bias_lookup(tbl, d), NEG_INF)
    return jnp.transpose(b, (2, 3, 0, 1)).reshape(-1, kpos.shape[0])


def _past_bias_tiles(tbl, ts, past_len, nk):
    assert nk >= LANES
    far = jnp.repeat(tbl[REL_BUCKETS - 1].reshape(-1), ts)[:, None]
    far_tile = jnp.broadcast_to(far, (far.shape[0], nk))
    near = _sample_row_bias(tbl, ts, past_len - LANES + jnp.arange(LANES), past_len, jnp.ones((ts, LANES), bool))
    return jnp.stack([far_tile, jnp.concatenate([far_tile[:, :nk - LANES], near], axis=1)])


def _block_expand_matrix(n_blk_lanes, nk, n_steps):
    blk_step = nk // SLC_LEN
    assert nk % SLC_LEN == 0 and blk_step % 16 == 0
    r = np.arange(n_blk_lanes + (n_steps - 1) * blk_step)[:, None]
    key_blk = (np.arange(nk) // SLC_LEN)[None, :]
    return jnp.asarray(r == (n_steps - 1) * blk_step + key_blk, BF16)


def _nsa_sample_kernel(pt_ref, *refs, ts, past_len, n_slc, slc_lanes):
    del pt_ref
    p_n = PAGES_PER_STEP
    k_pages, v_pages = refs[:p_n], refs[p_n:2 * p_n]
    (zq_ref, gn_ref, kc_ref, vc_ref, ksn_ref, vsn_ref, wk_ref, wv_ref, kwn_ref, vwn_ref,
     biasc_ref, bpast_ref, bnew_ref, bwin_ref, mslc_ref, expand_ref, o_ref,
     q_scr, m_scr, l_scr, acc_scr, comb_scr, bm_scr) = refs[2 * p_n:]
    j = pl.program_id(1)
    page = k_pages[0].shape[2]
    rows = NSA_KV * HPG * ts
    half = HPG * ts

    def gate_col(gn, c):
        return jnp.concatenate([gn[:, c * 8 + hd:c * 8 + hd + 1] for hd in range(NSA_HEADS)], axis=0)

    @pl.when(j == 0)
    def _():
        _stack_queries_sample(zq_ref, q_scr, ts)
        q = q_scr[...].astype(BF16)
        gn = jax.nn.sigmoid(gn_ref[...])
        s_c = lax.dot_general(q, kc_ref[0].astype(BF16), _NT, preferred_element_type=F32) + biasc_ref[...]
        mx, p, den = _softmax_tile(s_c)
        p_c = p / jnp.where(den > 0.0, den, 1.0)
        comb_scr[...] = gate_col(gn, 0) * jnp.dot(p_c.astype(BF16), vc_ref[0].astype(BF16), preferred_element_type=F32)
        imp = jnp.concatenate(
            [sum(p_c[g * half + h * ts:g * half + (h + 1) * ts] for h in range(HPG)) for g in range(NSA_KV)], axis=0)
        imp_hi, imp_lo = _split_hi_lo(imp)
        p_slc = (jnp.dot(imp_hi, mslc_ref[...], preferred_element_type=F32)
                 + jnp.dot(imp_lo, mslc_ref[...], preferred_element_type=F32))
        lane = lax.broadcasted_iota(jnp.int32, p_slc.shape, 1)
        tpos = past_len + (lax.broadcasted_iota(jnp.int32, p_slc.shape, 0) & (ts - 1))
        qb = tpos >> 6
        forced = (lane == 0) | (lane == qb) | (lane == qb - 1)
        score = jnp.where(lane > qb, NEG_INF, p_slc + jnp.where(forced, FORCE_BONUS, 0.0))
        sel = _rank_select(score, SLC_TOPN, lane, n_slc)
        bm_scr[...] = jnp.concatenate(
            [sel[g * ts:(g + 1) * ts] for g in range(NSA_KV) for _ in range(HPG)], axis=0).astype(BF16)
        m_scr[...] = jnp.full(m_scr.shape, M_FLOOR, F32)
        l_scr[...] = jnp.zeros(l_scr.shape, F32)
        acc_scr[...] = jnp.zeros(acc_scr.shape, F32)
        s_w = jnp.dot(q, wk_ref[0].astype(BF16), preferred_element_type=F32) + bwin_ref[...]
        _online_update(s_w, wv_ref[0].astype(BF16), m_scr, l_scr, acc_scr, v_t=True)
        s_n = lax.dot_general(q, _pad_rows(kwn_ref[...], LANES).astype(BF16), _NT, preferred_element_type=F32) + bnew_ref[...]
        _online_update(s_n, _pad_rows(vwn_ref[...], LANES).astype(BF16), m_scr, l_scr, acc_scr)
        l = l_scr[...]
        comb_scr[...] = comb_scr[...] + gate_col(gn, 2) * (acc_scr[...] / jnp.where(l > 0.0, l, 1.0))
        m_scr[...] = jnp.full(m_scr.shape, M_FLOOR, F32)
        l_scr[...] = jnp.zeros(l_scr.shape, F32)
        acc_scr[...] = jnp.zeros(acc_scr.shape, F32)

    q = q_scr[...].astype(BF16)
    nk = p_n * page
    k_t = jnp.concatenate([r[0].astype(BF16) for r in k_pages], axis=1)
    v_t = jnp.concatenate([r[0].astype(BF16) for r in v_pages], axis=1)
    last = pl.num_programs(1) - 1
    blk_step = nk // SLC_LEN
    expand = expand_ref[pl.ds(pl.multiple_of((last - j) * blk_step, blk_step), slc_lanes), :]
    vis = jnp.dot(bm_scr[...], expand, preferred_element_type=F32)
    s = (jnp.dot(q, k_t, preferred_element_type=F32) + bpast_ref[jnp.where(j == last, 1, 0)]
         + jnp.where(vis > 0.5, 0.0, NEG_INF))
    _online_update(s, v_t, m_scr, l_scr, acc_scr, v_t=True)

    @pl.when(j == pl.num_programs(1) - 1)
    def _():
        gn = jax.nn.sigmoid(gn_ref[...])
        s_n = lax.dot_general(q, _pad_rows(ksn_ref[...], LANES).astype(BF16), _NT, preferred_element_type=F32) + bnew_ref[...]
        _online_update(s_n, _pad_rows(vsn_ref[...], LANES).astype(BF16), m_scr, l_scr, acc_scr)
        l = l_scr[...]
        comb = comb_scr[...] + gate_col(gn, 1) * (acc_scr[...] / jnp.where(l > 0.0, l, 1.0))
        _unstack_heads([comb[:half], comb[half:]], o_ref, ts)


def _nsa_sample(z, n_p, kc, vc, pool_k, pool_v, win_k, win_v, page_table, rel_table, ts):
    bs, n_pages = page_table.shape
    n_phys, _, page = pool_k.shape
    past_len = n_pages * page
    p_n = PAGES_PER_STEP
    n_steps = n_pages // p_n
    nk = p_n * page
    n_sub = kc.shape[1]
    n_cmp = n_sub - 1
    n_slc = -(-(past_len + ts) // SLC_LEN)
    slc_lanes = -(-n_slc // LANES) * LANES
    win_len = win_k.shape[2]
    assert n_pages % p_n == 0 and ts & (ts - 1) == 0 and ts <= SLC_LEN and past_len % SLC_LEN == 0 and n_p % ts == 0
    assert n_sub % LANES == 0 and win_len + ts >= WINDOW
    tbl = rel_table[:, :NSA_HEADS].reshape(REL_BUCKETS, NSA_KV, HPG)
    rows = NSA_HEADS * ts
    cmp_end = jnp.arange(n_sub) * CMP_STRIDE + CMP_LEN - 1
    biasc = _sample_row_bias(tbl, ts, cmp_end, past_len, jnp.broadcast_to(jnp.arange(n_sub) < n_cmp, (ts, n_sub)))
    bpast = _past_bias_tiles(tbl, ts, past_len, nk)
    bnew = _sample_row_bias(tbl, ts, past_len + jnp.arange(LANES), past_len, jnp.broadcast_to(jnp.arange(LANES) < ts, (ts, LANES)))
    wpos = past_len - win_len + jnp.arange(win_len)
    in_win = ((past_len + jnp.arange(ts))[:, None] - wpos[None, :]) < WINDOW
    bwin = _sample_row_bias(tbl, ts, wpos, past_len, in_win)
    mslc = jnp.pad(_slc_matrix(n_cmp, n_slc), ((0, n_sub - n_cmp), (0, slc_lanes - n_slc))).astype(BF16)
    expand = _block_expand_matrix(slc_lanes, nk, n_steps)
    row_blk = n_p // ts
    zrow = lambda name, width: pl.BlockSpec((ts, width), (lambda b, j, pt, cb=_Z_OFF[name] // width: (row_blk + b, cb)))
    seq3 = lambda shape: pl.BlockSpec((1,) + shape, lambda b, j, pt: (b, 0, 0))
    const2 = lambda shape: pl.BlockSpec(shape, lambda b, j, pt: (0, 0))
    grid_spec = pltpu.PrefetchScalarGridSpec(
        num_scalar_prefetch=1,
        grid=(bs, n_steps),
        in_specs=_page_specs(p_n, (LANES, page)) + _page_specs(p_n, (LANES, page)) + [
            zrow("q_n", 512), zrow("g_n", LANES),
            seq3((n_sub, LANES)), seq3((n_sub, LANES)),
            zrow("ks", LANES), zrow("vs", LANES),
            seq3((LANES, win_len)), seq3((LANES, win_len)),
            zrow("kw", LANES), zrow("vw", LANES),
            const2((rows, n_sub)),
            pl.BlockSpec((2, rows, nk), lambda b, j, pt: (0, 0, 0)),
            const2((rows, LANES)), const2((rows, win_len)), const2((n_sub, slc_lanes)), const2(expand.shape),
        ],
        out_specs=pl.BlockSpec((ts, 512), lambda b, j, pt: (b, 0)),
        scratch_shapes=[
            pltpu.VMEM((rows, LANES), F32),
            pltpu.VMEM((rows,1), F32), pltpu.VMEM((rows, 1), F32), pltpu.VMEM((rows, LANES), F32),
            pltpu.VMEM((rows, LANES), F32), pltpu.VMEM((rows, slc_lanes), BF16),
        ],
    )
    return pl.pallas_call(
        functools.partial(_nsa_sample_kernel, ts=ts, past_len=past_len, n_slc=n_slc, slc_lanes=slc_lanes),
        grid_spec=grid_spec,
        out_shape=jax.ShapeDtypeStruct((bs * ts, 512), F32),
        compiler_params=_params(("arbitrary", "arbitrary")),
        name="nsa_sample",
    )(page_table, *([pool_k] * p_n), *([pool_v] * p_n), z, z, kc, vc, z, z, win_k, win_v, z, z,
      biasc, bpast, bnew, bwin, mslc, expand)


def _moba_sample_kernel(pt_ref, *refs, ts, n_past_blocks):
    del pt_ref
    p_n = PAGES_PER_STEP
    k_pages, v_pages = refs[:p_n], refs[p_n:2 * p_n]
    (zq_ref, kn_ref, vn_ref, bpast_ref, bnew_ref, o_ref, q_scr, m_all, l_all, acc_all, km_scr) = refs[2 * p_n:]
    j = pl.program_id(1)
    page = k_pages[0].shape[2]
    ppb = MOBA_BLOCK // page
    bps = p_n // ppb
    rows = MOBA_KV * HPG * ts
    half = HPG * ts
    lane = lax.broadcasted_iota(jnp.int32, (rows, LANES), 1)
    kcol = lax.broadcasted_iota(jnp.int32, (LANES, LANES), 1)
    last = pl.num_programs(1) - 1

    @pl.when(j == 0)
    def _():
        _stack_queries_sample(zq_ref, q_scr, ts)
        m_all[...] = jnp.full(m_all.shape, NEG_INF, F32)
        l_all[...] = jnp.zeros(l_all.shape, F32)
        km_scr[...] = jnp.zeros(km_scr.shape, F32)

    q = q_scr[...].astype(BF16)
    for i in range(bps):
        n = j * bps + i
        k_t = jnp.concatenate([k_pages[i * ppb + r][0] for r in range(ppb)], axis=1)
        v_t = jnp.concatenate([v_pages[i * ppb + r][0].astype(BF16) for r in range(ppb)], axis=1)
        s = (jnp.dot(q, k_t.astype(BF16), preferred_element_type=F32)
             + bpast_ref[jnp.where(j == last, 1, 0), :, i * MOBA_BLOCK:(i + 1) * MOBA_BLOCK])
        mx, p, den = _softmax_tile(s)
        m_all[...] = jnp.where(lane == n, mx, m_all[...])
        l_all[...] = jnp.where(lane == n, den, l_all[...])
        acc_all[n] = _pv(p, v_t, True)
        km_scr[...] = jnp.where(kcol == n, jnp.mean(k_t, axis=1, keepdims=True), km_scr[...])

    @pl.when(j == pl.num_programs(1) - 1)
    def _():
        s_o = lax.dot_general(q, _pad_rows(kn_ref[...], LANES).astype(BF16), _NT, preferred_element_type=F32) + bnew_ref[...]
        m_o, p_o, l_o = _softmax_tile(s_o)
        acc_o = jnp.dot(p_o.astype(BF16), _pad_rows(vn_ref[...], LANES).astype(BF16), preferred_element_type=F32)
        km_hi, km_lo = _split_hi_lo(km_scr[...])
        gs = jnp.dot(q, km_hi, preferred_element_type=F32) + jnp.dot(q, km_lo, preferred_element_type=F32)
        gs = jnp.where(lane < n_past_blocks, gs, NEG_INF)
        sel = _take_topk(gs, min(MOBA_TOPK, n_past_blocks), lane)
        m_sel = jnp.where(sel > 0.5, m_all[...], NEG_INF)
        m_fin = jnp.maximum(jnp.max(m_sel, axis=-1, keepdims=True), m_o)
        wgt = jnp.where(sel > 0.5, jnp.exp(m_sel - m_fin), 0.0)
        w_o = jnp.exp(m_o - m_fin)
        den = jnp.sum(wgt * l_all[...], axis=-1, keepdims=True) + w_o * l_o
        num = w_o * acc_o
        for n in range(n_past_blocks):
            num = num + wgt[:, n:n + 1] * acc_all[n]
        res = num / den
        _unstack_heads([res[:half], res[half:]], o_ref, ts)


def _moba_sample(z, n_p, pool_k, pool_v, page_table, rel_table, ts):
    bs, n_pages = page_table.shape
    n_phys, _, page = pool_k.shape
    past_len = n_pages * page
    p_n = PAGES_PER_STEP
    n_steps = n_pages // p_n
    nk = p_n * page
    nb_past = past_len // MOBA_BLOCK
    assert (MOBA_BLOCK % page == 0 and p_n % (MOBA_BLOCK // page) == 0 and past_len % MOBA_BLOCK == 0
            and ts <= MOBA_BLOCK and nb_past <= LANES and n_pages % p_n == 0 and n_p % ts == 0)
    tbl = rel_table[:, NSA_HEADS:].reshape(REL_BUCKETS, MOBA_KV, HPG)
    rows = MOBA_HEADS * ts
    bpast = _past_bias_tiles(tbl, ts, past_len, nk)
    bnew = _sample_row_bias(tbl, ts, past_len + jnp.arange(LANES), past_len, jnp.broadcast_to(jnp.arange(LANES) < ts, (ts, LANES)))
    row_blk = n_p // ts
    zrow = lambda name, width: pl.BlockSpec((ts, width), (lambda b, j, pt, cb=_Z_OFF[name] // width: (row_blk + b, cb)))
    grid_spec = pltpu.PrefetchScalarGridSpec(
        num_scalar_prefetch=1,
        grid=(bs, n_steps),
        in_specs=_page_specs(p_n, (LANES, page)) + _page_specs(p_n, (LANES, page)) + [
            zrow("q_m", 512), zrow("k_m", LANES), zrow("v_m", LANES),
            pl.BlockSpec((2, rows, nk), lambda b, j, pt: (0, 0, 0)),
            pl.BlockSpec((rows, LANES), lambda b, j, pt: (0, 0)),
        ],
        out_specs=pl.BlockSpec((ts, 512), lambda b, j, pt: (b, 0)),
        scratch_shapes=[
            pltpu.VMEM((rows, LANES), F32),
            pltpu.VMEM((rows,LANES), F32), pltpu.VMEM((rows, LANES), F32),
            pltpu.VMEM((nb_past, rows, LANES), F32), pltpu.VMEM((LANES, LANES), F32),
        ],
    )
    return pl.pallas_call(
        functools.partial(_moba_sample_kernel, ts=ts, n_past_blocks=nb_past),
        grid_spec=grid_spec,
        out_shape=jax.ShapeDtypeStruct((bs * ts, 512), F32),
        compiler_params=_params(("arbitrary", "arbitrary")),
        name="moba_sample",
    )(page_table, *([pool_k] * p_n), *([pool_v] * p_n), z, z, z, bpast, bnew)


def _memkv_kernel(m_ref, gain_ref, w_ref, kgain_ref, seg_ref, k_ref, v_ref):
    x = m_ref[...]
    xb = (x * lax.rsqrt(jnp.mean(x * x, axis=-1, keepdims=True) + RMS_EPS) * gain_ref[...]).astype(BF16)
    kvw = k_ref.shape[1]
    k = jnp.dot(xb, w_ref[:, :kvw], preferred_element_type=F32)
    for h in range(MEM_HEADS):
        kh = k[:, h * MEM_HEAD_DIM:(h + 1) * MEM_HEAD_DIM]
        ss = _group_sumsq(kh, seg_ref[...])
        k_ref[:, h * MEM_HEAD_DIM:(h + 1) * MEM_HEAD_DIM] = kh * lax.rsqrt(ss * (1.0 / MEM_HEAD_DIM) + RMS_EPS) * kgain_ref[...]
    v_ref[...] = jnp.dot(xb, w_ref[:, kvw:], preferred_element_type=F32)


def _memory_kv_pallas(mem, norm_gain, w_mem_kv, k_gain, tm=256):
    n, d = mem.shape
    kvw = w_mem_kv.shape[1] // 2
    assert n % tm == 0
    row = lambda i: (i, 0)
    const = lambda i: (0, 0)
    return pl.pallas_call(
        _memkv_kernel,
        grid=(n // tm,),
        in_specs=[pl.BlockSpec((tm, d), row), pl.BlockSpec((1, d), const), pl.BlockSpec((d, 2 * kvw), const),
                  pl.BlockSpec((1, MEM_HEAD_DIM), const), pl.BlockSpec((LANES, LANES), const)],
        out_specs=[pl.BlockSpec((tm, kvw), row), pl.BlockSpec((tm, kvw), row)],
        out_shape=[jax.ShapeDtypeStruct((n, kvw), F32)] * 2,
        compiler_params=_params(("arbitrary",)),
        name="memory_kv",
    )(mem, norm_gain[None, :], w_mem_kv.astype(BF16), k_gain[None, :], _seg_matrix(MEM_HEAD_DIM))


def _mem_attend_kernel(zq_ref, mk_ref, mv_ref, o_ref):
    for h in range(MEM_HEADS):
        sl = slice(h * MEM_HEAD_DIM, (h + 1) * MEM_HEAD_DIM)
        q = (zq_ref[:, sl] * (MEM_HEAD_DIM ** -0.5)).astype(BF16)
        s = lax.dot_general(q, mk_ref[0, :, sl].astype(BF16), _NT, preferred_element_type=F32)
        p = jnp.exp(s - jnp.max(s, axis=-1, keepdims=True))
        o = jnp.dot(p.astype(BF16), mv_ref[0, :, sl].astype(BF16), preferred_element_type=F32)
        o_ref[:, sl] = o / jnp.sum(p, axis=-1, keepdims=True)


def _mem_attend_pallas(z, row0, n_seq, t_seq, tq, mk, mv):
    nq = t_seq // tq
    assert t_seq % tq == 0 and row0 % tq == 0
    m_len = mk.shape[1]
    width = MEM_HEADS * MEM_HEAD_DIM
    cb = _Z_OFF["q_x"] // width
    return pl.pallas_call(
        _mem_attend_kernel,
        grid=(n_seq, nq),
        in_specs=[pl.BlockSpec((tq, width), lambda b, i: (row0 // tq + b * nq + i, cb)),
                  pl.BlockSpec((1, m_len, width), lambda b, i: (b, 0, 0)),
                  pl.BlockSpec((1, m_len, width), lambda b, i: (b, 0, 0))],
        out_specs=pl.BlockSpec((tq, width), lambda b, i: (b * nq + i, 0)),
        out_shape=jax.ShapeDtypeStruct((n_seq * t_seq, width), F32),
        compiler_params=_params(("arbitrary", "arbitrary")),
        name="mem_attend",
    )(z, mk, mv)


def _zcol(z, name, width):
    return z[..., _Z_OFF[name]:_Z_OFF[name] + width]


def kernel(x_prompt, x_sample, cache_cmp_k, cache_cmp_v, cache_slc_k, cache_slc_v, cache_moba_k, cache_moba_v, cache_win_k, cache_win_v, cache_mem_k, cache_mem_v, page_table, mem_prompt, attn_norm_gain, w_in, nsa_q_gain, nsa_k_gain, cmp_pos_emb, cmp_w1, cmp_b1, cmp_w2, moba_q_gain, moba_k_gain, mem_norm_gain, w_mem_kv, mem_q_gain, mem_k_gain, rel_bias_table, w_branch, w_out, ffn_norm_gain, peer_w_q, peer_sub_keys, peer_u, peer_v):
    assert w_in.shape[0] == 1
    l = 0
    bp, tp, d = x_prompt.shape
    bs, ts, _ = x_sample.shape
    past_len = page_table.shape[1] * cache_cmp_k.shape[2]
    n_p, n_s = bp * tp, bs * ts

    x_all = jnp.concatenate([x_prompt.reshape(n_p, d), x_sample.reshape(n_s, d)], axis=0)
    hgain = _head_gain_row({"nsa_q": nsa_q_gain[l], "nsa_k1": nsa_k_gain[l, 1], "nsa_k2": nsa_k_gain[l, 2],
                            "moba_q": moba_q_gain[l], "moba_k": moba_k_gain[l], "mem_q": mem_q_gain[l]})
    z = _inproj(x_all, attn_norm_gain[l][None, :], _pack_w_in(w_in[l]), hgain)
    zp, zs = "prompt", "sample"

    def rows(group, name):
        col = _zcol(z, name, LANES)
        return col[:n_p].reshape(bp, tp, LANES) if group == zp else col[n_p:].reshape(bs, ts, LANES)

    kc_rows_p, vc_rows_p = rows(zp, "kc"), rows(zp, "vc")
    wk = _compress_weights(cmp_pos_emb[l, 0], cmp_w1[l, 0], cmp_b1[l, 0], cmp_w2[l, 0])
    wv = _compress_weights(cmp_pos_emb[l, 1], cmp_w1[l, 1], cmp_b1[l, 1], cmp_w2[l, 1])
    sub = lambda r: r.reshape(bp, tp // CMP_STRIDE, CMP_STRIDE * LANES)
    kgain = jnp.tile(nsa_k_gain[l, 0], 2)[None, :]
    kc_p = _compress(sub(kc_rows_p), wk, kgain, True)
    vc_p = _compress(sub(vc_rows_p), wv, kgain, False)
    o_nsa_p = _nsa_prompt(z, kc_p, vc_p, rel_bias_table, bp, tp)
    o_moba_p = _moba_prompt(z, rel_bias_table, bp, tp)
    m_len = mem_prompt.shape[1]
    mk_p, mv_p = _memory_kv_pallas(mem_prompt.reshape(bp * m_len, d), mem_norm_gain[l], w_mem_kv[l], mem_k_gain[l])
    mem_w = MEM_HEADS * MEM_HEAD_DIM
    o_mem_p = _mem_attend_pallas(z, 0, bp, tp, 256, mk_p.reshape(bp, m_len, mem_w), mv_p.reshape(bp, m_len, mem_w))
    mk_p = mk_p.reshape(bp, m_len, MEM_HEADS, MEM_HEAD_DIM)
    mv_p = mv_p.reshape(bp, m_len, MEM_HEADS, MEM_HEAD_DIM)

    keys_on_lanes = lambda c: jnp.transpose(c[l], (0, 2, 3, 1)).reshape(c.shape[1], LANES, c.shape[2])
    kc_s = _compress_paged(keys_on_lanes(cache_cmp_k), page_table, wk, kgain, True)
    vc_s = _compress_paged(keys_on_lanes(cache_cmp_v), page_table, wv, kgain, False)
    win_len = cache_win_k.shape[2]
    win_k = cache_win_k[l].reshape(bs, win_len, LANES)
    win_v = cache_win_v[l].reshape(bs, win_len, LANES)
    o_nsa_s = _nsa_sample(z, n_p, kc_s, vc_s, keys_on_lanes(cache_slc_k), keys_on_lanes(cache_slc_v),
                          keys_on_lanes(cache_win_k), keys_on_lanes(cache_win_v), page_table, rel_bias_table, ts)
    o_moba_s = _moba_sample(z, n_p, keys_on_lanes(cache_moba_k), keys_on_lanes(cache_moba_v), page_table,
                            rel_bias_table, ts)
    o_mem_s = _mem_attend_pallas(z, n_p, bs, ts, ts, cache_mem_k[l].reshape(bs, -1, mem_w), cache_mem_v[l].reshape(bs, -1, mem_w))
    kw_all = jnp.concatenate([win_k, rows(zs, "kw")], axis=1)
    vw_all = jnp.concatenate([win_v, rows(zs, "vw")], axis=1)

    cat = lambda a, b_: jnp.concatenate([a.reshape(n_p, 512), b_.reshape(n_s, 512)], axis=0)
    x1, xn_b = _merge(cat(o_nsa_p, o_nsa_s), cat(o_moba_p, o_moba_s), cat(o_mem_p, o_mem_s), z, x_all,
                      w_branch[l], w_out[l], ffn_norm_gain[l])
    y = _peer(x1, xn_b, peer_w_q[l], peer_sub_keys[l], peer_u[l].astype(BF16), peer_v[l].astype(BF16).T)
    y_p = y[:n_p].reshape(bp, tp, d)
    y_s = y[n_p:].reshape(bs, ts, d)

    def st(r, b_, t_):
        return r.reshape(1, b_, t_, 2, HEAD_DIM)

    keep_p = min(WINDOW, tp)
    p_win_k = st(rows(zp, "kw")[:, tp - keep_p:], bp, keep_p)
    p_win_v = st(rows(zp, "vw")[:, tp - keep_p:], bp, keep_p)
    keep_s = min(WINDOW, past_len + ts)
    s_win_k = st(kw_all[:, kw_all.shape[1] - keep_s:], bs, keep_s)
    s_win_v = st(vw_all[:, vw_all.shape[1] - keep_s:], bs, keep_s)
    return (y_p, y_s,
            st(kc_rows_p, bp, tp), st(vc_rows_p, bp, tp), st(rows(zp, "ks"), bp, tp), st(rows(zp, "vs"), bp, tp),
            st(rows(zp, "k_m"), bp, tp), st(rows(zp, "v_m"), bp, tp), p_win_k, p_win_v, mk_p[None], mv_p[None],
            st(rows(zs, "kc"), bs, ts), st(rows(zs, "vc"), bs, ts), st(rows(zs, "ks"), bs, ts), st(rows(zs, "vs"), bs, ts),
            st(rows(zs, "k_m"), bs, ts), st(rows(zs, "v_m"), bs, ts), s_win_k, s_win_v)
```

```python
import functools
import math

import jax
import jax.numpy as jnp
import numpy as np
from jax import lax
from jax.experimental import pallas as pl
from jax.experimental.pallas import tpu as pltpu

HEAD_DIM = 64
NSA_HEADS = 8
NSA_KV = 2
HPG = 4
CMP_LEN = 32
CMP_STRIDE = 16
SLC_LEN = 64
SLC_TOPN = 16
WINDOW = 512
FORCE_BONUS = 1000.0
MOBA_HEADS = 8
MOBA_KV = 2
MOBA_BLOCK = 256
MOBA_TOPK = 3
MEM_HEADS = 4
MEM_HEAD_DIM = 128
REL_BUCKETS = 32
REL_MAX_DIST = 128
PEER_HEADS = 8
PEER_NKEYS = 128
PEER_TOPK = 16
RMS_EPS = 1e-6
NEG_INF = -1e30
HALF_NEG = 0.5 * NEG_INF
M_FLOOR = 0.1 * NEG_INF
KNOCKED = -3.0e38
LOG2E = 1.4426950408889634
LANES = 128
VMEM_LIMIT = 56 * 1024 * 1024
ATT_TQ = 128
ATT_TK = 128

F32 = jnp.float32
BF16 = jnp.bfloat16
HI = lax.Precision.HIGHEST
_NT = (((1,), (1,)), ((), ()))


def _rel_bucket(dist):
    n = jnp.maximum(dist, 0)
    exact = REL_BUCKETS // 2
    nf = jnp.maximum(n, 1).astype(F32)
    large = exact + (jnp.log(nf / exact) / math.log(REL_MAX_DIST / exact) * (REL_BUCKETS - exact)).astype(jnp.int32)
    return jnp.where(n < exact, n, jnp.minimum(large, REL_BUCKETS - 1))


def _bias_lookup(tbl, dist):
    onehot = jax.nn.one_hot(_rel_bucket(dist), REL_BUCKETS, dtype=F32)
    return jnp.tensordot(onehot, tbl, axes=1, precision=HI)


def _gelu_tanh(x):
    inner = x * (0.7978845608028654 + 0.035677408136300125 * (x * x))
    return (0.5 * x) * (1.0 + jnp.tanh(inner))


def _split_hi_lo(x):
    hi = x.astype(BF16)
    lo = (x - hi.astype(F32)).astype(BF16)
    return hi, lo


def _dot3(a, b_hi, b_lo):
    a_hi, a_lo = _split_hi_lo(a)
    return (jnp.dot(a_hi, b_hi, preferred_element_type=F32) + jnp.dot(a_lo, b_hi, preferred_element_type=F32)
            + jnp.dot(a_hi, b_lo, preferred_element_type=F32))


def _seg_matrix(hd):
    i = np.arange(LANES)
    return jnp.asarray((i[:, None] // hd) == (i[None, :] // hd), BF16)


def _group_sumsq(z, seg):
    sq_hi, sq_lo = _split_hi_lo(z * z)
    return jnp.dot(sq_hi, seg, preferred_element_type=F32) + jnp.dot(sq_lo, seg, preferred_element_type=F32)


def _params(sem):
    return pltpu.CompilerParams(dimension_semantics=sem, vmem_limit_bytes=VMEM_LIMIT)


_Z_GROUPS = (
    ("g_merge", 3072, 0), ("q_n", 512, 64), ("q_m", 512, 64), ("q_x", 512, 128),
    ("kc", 128, 0), ("vc", 128, 0), ("ks", 128, 64), ("vs", 128, 0), ("kw", 128, 64), ("vw", 128, 0),
    ("k_m", 128, 64), ("v_m", 128, 0), ("g_n", 128, 0),
)
_Z_OFF = {}
_o = 0
for _n, _w, _h in _Z_GROUPS:
    _Z_OFF[_n] = _o
    _o += _w
Z_COLS = _o
IN_SPLITS = (512, 128, 128, 128, 128, 128, 128, 24, 512, 128, 128, 512, 3072)
_SRC_NAMES = ("q_n", "kc", "vc", "ks", "vs", "kw", "vw", "g_n", "q_m", "k_m", "v_m", "q_x", "g_merge")


def _pack_w_in(w_in):
    cuts = np.cumsum((0,) + IN_SPLITS)
    parts = {n: w_in[:, cuts[i]:cuts[i + 1]] for i, n in enumerate(_SRC_NAMES)}
    parts["g_n"] = jnp.pad(parts["g_n"], ((0, 0), (0, LANES - 24)))
    return jnp.concatenate([parts[n] for n, _, _ in _Z_GROUPS], axis=1).astype(BF16)


def _head_gain_row(gains):
    g = {
        "q_n": jnp.tile(gains["nsa_q"], 8), "ks": jnp.tile(gains["nsa_k1"], 2), "kw": jnp.tile(gains["nsa_k2"], 2),
        "q_m": jnp.tile(gains["moba_q"], 8), "k_m": jnp.tile(gains["moba_k"], 2), "q_x": jnp.tile(gains["mem_q"], 4),
    }
    return jnp.concatenate([g.get(n, jnp.ones((w,), F32)) for n, w, _ in _Z_GROUPS])[None, :]


def _inproj_kernel(x_ref, gain_ref, w_ref, hgain_ref, seg64_ref, seg128_ref, z_ref):
    x = x_ref[...]
    ms = jnp.mean(x * x, axis=-1, keepdims=True)
    xb = (x * lax.rsqrt(ms + RMS_EPS) * gain_ref[...]).astype(BF16)
    for name, width, hd in _Z_GROUPS:
        off = _Z_OFF[name]
        for c in range(0, width, 512):
            cw = min(512, width - c)
            lo_c, hi_c = off + c, off + c + cw
            z = jnp.dot(xb, w_ref[:, lo_c:hi_c], preferred_element_type=F32)
            if hd:
                seg = seg64_ref[...] if hd == 64 else seg128_ref[...]
                for j in range(0, cw, LANES):
                    zj = z[:, j:j + LANES]
                    ss = _group_sumsq(zj, seg)
                    z_ref[:, lo_c + j:lo_c + j + LANES] = (
                        zj * lax.rsqrt(ss * (1.0 / hd) + RMS_EPS) * hgain_ref[:, lo_c + j:lo_c + j + LANES])
            else:
                z_ref[:, lo_c:hi_c] = z


def _inproj(x, gain, w_packed, hgain, tm=256):
    n, d = x.shape
    assert n % tm == 0
    const = lambda i: (0, 0)
    return pl.pallas_call(
        _inproj_kernel,
        grid=(n // tm,),
        in_specs=[
            pl.BlockSpec((tm, d), lambda i: (i, 0)),
            pl.BlockSpec((1, d), const),
            pl.BlockSpec((d, Z_COLS), const),
            pl.BlockSpec((1, Z_COLS), const),
            pl.BlockSpec((LANES, LANES), const),
            pl.BlockSpec((LANES, LANES), const),
        ],
        out_specs=pl.BlockSpec((tm, Z_COLS), lambda i: (i, 0)),
        out_shape=jax.ShapeDtypeStruct((n, Z_COLS), F32),
        compiler_params=_params(("arbitrary",)),
        name="inproj",
    )(x, gain, w_packed, hgain, _seg_matrix(64), _seg_matrix(128))


def _compress_weights(pe, w1, b1, w2):
    w1r = w1.reshape(2, CMP_STRIDE, HEAD_DIM, -1)
    eye = jnp.eye(2, dtype=F32)
    wbig = jnp.einsum("rlde,gh->lgdrhe", w1r, eye).reshape(CMP_STRIDE * LANES, 2 * LANES)
    pos = jnp.einsum("rld,rlde->e", pe.reshape(2, CMP_STRIDE, HEAD_DIM), w1r, precision=HI)
    cvec = jnp.tile(b1 + pos, 2)[None, :]
    w2big = jnp.einsum("ed,gh->gehd", w2, eye).reshape(LANES, LANES)
    return _split_hi_lo(wbig) + (cvec,) + _split_hi_lo(w2big)


def _compress_kernel(x_ref, wh_ref, wl_ref, c_ref, w2h_ref, w2l_ref, seg_ref, gain_ref, o_ref, *, norm):
    ns = x_ref.shape[1]
    a = _dot3(x_ref[0], wh_ref[...], wl_ref[...])
    nxt = pltpu.roll(a[:, LANES:], ns - 1, axis=0)
    h = a[:, :LANES] + nxt + c_ref[...]
    o = _dot3(_gelu_tanh(h), w2h_ref[...], w2l_ref[...])
    if norm:
        o = o * lax.rsqrt(_group_sumsq(o, seg_ref[...]) * (1.0 / HEAD_DIM) + RMS_EPS) * gain_ref[...]
    o_ref[0] = o


def _compress(x_sub, weights, gain, norm):
    b, ns, kk = x_sub.shape
    wh, wl, cvec, w2h, w2l = weights
    const = lambda i: (0, 0)
    return pl.pallas_call(
        functools.partial(_compress_kernel, norm=norm),
        grid=(b,),
        in_specs=[
            pl.BlockSpec((1, ns, kk), lambda i: (i, 0, 0)),
            pl.BlockSpec((kk, 2 * LANES), const), pl.BlockSpec((kk, 2 * LANES), const),
            pl.BlockSpec((1, LANES), const),
            pl.BlockSpec((LANES, LANES), const), pl.BlockSpec((LANES, LANES), const),
            pl.BlockSpec((LANES, LANES), const), pl.BlockSpec((1, LANES), const),
        ],
        out_specs=pl.BlockSpec((1, ns, LANES), lambda i: (i, 0, 0)),
        out_shape=jax.ShapeDtypeStruct((b, ns, LANES), F32),
        compiler_params=_params(("arbitrary",)),
        name="nsa_compress",
    )(x_sub, wh, wl, cvec, w2h, w2l, _seg_matrix(HEAD_DIM), gain)


def _toeplitz_tiles(tbl, deltas, tq, tk, hi):
    d = (jnp.asarray(deltas, jnp.int32)[:, None, None] + jnp.arange(tq)[None, :, None] - jnp.arange(tk)[None, None, :])
    b = jnp.where(((d >= 0) & (d < hi))[..., None, None], _bias_lookup(tbl, d), NEG_INF)
    return jnp.transpose(b, (3, 0, 4, 1, 2)).reshape(tbl.shape[1], len(deltas), HPG * tq, tk)


def _with_masked_tile(tiles):
    return jnp.concatenate([jnp.full_like(tiles[:, :1], NEG_INF), tiles], axis=1)


def _stack_queries(zq_ref, write, tq):
    lane = lax.broadcasted_iota(jnp.int32, (tq, LANES), 1)
    for g in range(2):
        keep = (lane < HEAD_DIM) if g == 0 else (lane >= HEAD_DIM)
        for h in range(HPG):
            hd = HPG * g + h
            chunk = zq_ref[:, LANES * (hd // 2):LANES * (hd // 2 + 1)] * (HEAD_DIM ** -0.5)
            if hd % 2 != g:
                chunk = pltpu.roll(chunk, HEAD_DIM, axis=1)
            write(g, h, jnp.where(keep, chunk, 0.0))


def _stack_queries_prompt(zq_ref, q_scr, tq):
    def write(g, h, val):
        q_scr[g, h * tq:(h + 1) * tq, :] = val.astype(BF16)
    _stack_queries(zq_ref, write, tq)


def _stack_queries_sample(zq_ref, q_scr, ts):
    def write(g, h, val):
        q_scr[(g * HPG + h) * ts:(g * HPG + h + 1) * ts, :] = val
    _stack_queries(zq_ref, write, ts)


def _unstack_heads(parts, o_ref, tq):
    lane = lax.broadcasted_iota(jnp.int32, (tq, LANES), 1)
    for j in range(4):
        g = j // 2
        he = (2 * j) % HPG
        left = parts[g][he * tq:(he + 1) * tq, :]
        right = parts[g][(he + 1) * tq:(he + 2) * tq, :]
        if g == 0:
            right = pltpu.roll(right, HEAD_DIM, axis=1)
        else:
            left = pltpu.roll(left, HEAD_DIM, axis=1)
        o_ref[:, LANES * j:LANES * (j + 1)] = jnp.where(lane < HEAD_DIM, left, right)


KEY_TILES_PER_STEP = 4


def _flash(qs, k_ref, v_ref, bias_ref, tile_index, bms, blk_shift, rep, kb_lo, kb_hi, qt, m_scr, l_scr, acc_scr, tk):
    m_scr[...] = jnp.full(m_scr.shape, M_FLOOR, F32)
    l_scr[...] = jnp.zeros(l_scr.shape, F32)
    acc_scr[...] = jnp.zeros(acc_scr.shape, F32)
    nk = tk * KEY_TILES_PER_STEP
    blk_row = lax.broadcasted_iota(jnp.int32, (LANES, nk), 0)
    key_lane = lax.broadcasted_iota(jnp.int32, (LANES, nk), 1)

    def body(kb, carry):
        k0 = pl.multiple_of(kb * nk, nk)
        k = k_ref[pl.ds(k0, nk), :].astype(BF16)
        v = v_ref[pl.ds(k0, nk), :].astype(BF16)
        tiles = [tile_index(qt - (kb * KEY_TILES_PER_STEP + c)) for c in range(KEY_TILES_PER_STEP)]
        if bms is not None:
            expand = jnp.where(blk_row == ((k0 + key_lane) >> blk_shift), 1.0, 0.0).astype(BF16)
        for g, q in enumerate(qs):
            s = lax.dot_general(q, k, _NT, preferred_element_type=F32)
            s = s + jnp.concatenate([bias_ref[g, t] for t in tiles], axis=1)
            if bms is not None:
                vis = jnp.dot(bms[g], expand, preferred_element_type=F32)
                if rep > 1:
                    s = s + jnp.concatenate([jnp.where(vis > 0.5, 0.0, NEG_INF)] * rep, axis=0)
                else:
                    s = jnp.where(vis > 0.5, s, NEG_INF)
            m_old = m_scr[g]
            m_new = jnp.maximum(m_old, jnp.max(s, axis=-1, keepdims=True))
            p = jnp.exp(s - m_new)
            alpha = jnp.exp(m_old - m_new)
            l_scr[g] = alpha * l_scr[g] + jnp.sum(p, axis=-1, keepdims=True)
            acc_scr[g] = alpha * acc_scr[g] + jnp.dot(p.astype(BF16), v, preferred_element_type=F32)
            m_scr[g] = m_new
        return carry

    lax.fori_loop(kb_lo, kb_hi, body, 0)
    outs = []
    for g in range(len(qs)):
        l = l_scr[g]
        outs.append(acc_scr[g] / jnp.where(l > 0.0, l, 1.0))
    return outs


def _causal_tile_index(n_near):
    return lambda rel: jnp.clip(rel + 1, 0, n_near)


def _window_tile_index(n_win):
    return lambda rel: jnp.where(rel >= n_win, 0, jnp.clip(rel + 1, 0, n_win))


def _rank_select(score, count, lane, n_valid):
    width = score.shape[-1]
    rank = jnp.zeros(score.shape, F32)
    for s in range(1, width):
        if n_valid <= s <= width - n_valid:
            continue
        other = pltpu.roll(score, s, axis=1)
        rank = rank + jnp.where(other > score, 1.0, 0.0)
        if s < n_valid:
            rank = rank + jnp.where(other == score, jnp.where(lane >= s, 1.0, 0.0), 0.0)
    return jnp.where((rank < count) & (score > HALF_NEG), 1.0, 0.0)


def _take_topk(score, count, lane):
    sel = jnp.zeros(score.shape, F32)
    cur = score
    lane_f = lane.astype(F32)
    for _ in range(count):
        mx = jnp.max(cur, axis=-1, keepdims=True)
        idx = jnp.min(jnp.where(cur == mx, lane_f, float(score.shape[-1])), axis=-1, keepdims=True)
        hit = lane_f == idx
        sel = jnp.where(hit, jnp.where(mx > HALF_NEG, 1.0, 0.0), sel)
        cur = jnp.where(hit, KNOCKED, cur)
    return sel


def _nsa_prompt_kernel(zq_ref, gn_ref, kc_ref, vc_ref, ks_ref, vs_ref, kw_ref, vw_ref, biasc_ref, bslc_ref, bwin_ref,
                       mslc_ref, o_ref, q_scr, m_scr, l_scr, acc_scr, comb_scr, *, tq, tk, n_slc, n_slc_tiles, n_win_tiles):
    qt = pl.program_id(1)
    q0 = qt * tq
    _stack_queries_prompt(zq_ref, q_scr, tq)
    gn = jax.nn.sigmoid(gn_ref[...])
    lane = lax.broadcasted_iota(jnp.int32, (tq, LANES), 1)
    tpos = q0 + lax.broadcasted_iota(jnp.int32, (tq, LANES), 0)
    qb = tpos >> 6

    def gate_col(c, g):
        return jnp.concatenate([gn[:, c * 8 + g * HPG + h:c * 8 + g * HPG + h + 1] for h in range(HPG)], axis=0)

    kc = kc_ref[0].astype(BF16)
    vc = vc_ref[0].astype(BF16)
    qs = [q_scr[g] for g in range(NSA_KV)]
    bms = []
    for g, q in enumerate(qs):
        s_c = lax.dot_general(q, kc, _NT, preferred_element_type=F32) + biasc_ref[g]
        mx = jnp.maximum(jnp.max(s_c, axis=-1, keepdims=True), M_FLOOR)
        p = jnp.exp(s_c - mx)
        den = jnp.sum(p, axis=-1, keepdims=True)
        p_c = p / jnp.where(den > 0.0, den, 1.0)
        comb_scr[g] = gate_col(0, g) * jnp.dot(p_c.astype(BF16), vc, preferred_element_type=F32)
        imp = p_c[0:tq] + p_c[tq:2 * tq] + p_c[2 * tq:3 * tq] + p_c[3 * tq:4 * tq]
        imp_hi, imp_lo = _split_hi_lo(imp)
        p_slc = (jnp.dot(imp_hi, mslc_ref[...], preferred_element_type=F32)
                 + jnp.dot(imp_lo, mslc_ref[...], preferred_element_type=F32))
        forced = (lane == 0) | (lane == qb) | (lane == qb - 1)
        score = jnp.where(lane > qb, NEG_INF, p_slc + jnp.where(forced, FORCE_BONUS, 0.0))
        bms.append(lax.cond(q0 + tq <= SLC_TOPN * SLC_LEN,
                            lambda score=score: jnp.where(score > HALF_NEG, 1.0, 0.0),
                            lambda score=score: _rank_select(score, SLC_TOPN, lane, n_slc)).astype(BF16))
    kb_hi = qt // KEY_TILES_PER_STEP + 1
    o_s = _flash(qs, ks_ref, vs_ref, bslc_ref, _causal_tile_index(n_slc_tiles), bms, 6, HPG, 0, kb_hi, qt,
                 m_scr, l_scr, acc_scr, tk)
    kb_lo = jnp.maximum(qt - (n_win_tiles - 1), 0) // KEY_TILES_PER_STEP
    o_w = _flash(qs, kw_ref, vw_ref, bwin_ref, _window_tile_index(n_win_tiles), None, 0, 1, kb_lo, kb_hi, qt,
                 m_scr, l_scr, acc_scr, tk)
    for g in range(NSA_KV):
        comb_scr[g] = comb_scr[g] + gate_col(1, g) * o_s[g] + gate_col(2, g) * o_w[g]
    _unstack_heads([comb_scr[0], comb_scr[1]], o_ref, tq)


def _zspec(rows, name, width, row_map):
    cb = _Z_OFF[name] // width
    assert _Z_OFF[name] % width == 0
    return pl.BlockSpec((rows, width), lambda b, i: (row_map(b, i), cb))


def _nsa_prompt(z, kc, vc, rel_table, bp, tp):
    tq, tk = ATT_TQ, ATT_TK
    nq = tp // tq
    n_cmp = tp // CMP_STRIDE - 1
    n_slc = tp // SLC_LEN
    assert tp % tq == 0 and n_cmp < LANES and n_slc <= LANES and kc.shape[1] == LANES
    tbl = rel_table[:, :NSA_HEADS].reshape(REL_BUCKETS, NSA_KV, HPG)
    t = jnp.arange(tp)
    dist_c = t[:, None] - (jnp.arange(LANES) * CMP_STRIDE + CMP_LEN - 1)[None, :]
    valid_c = (dist_c >= 0) & (jnp.arange(LANES) < n_cmp)[None, :]
    biasc = jnp.where(valid_c[..., None, None], _bias_lookup(tbl, dist_c), NEG_INF)
    biasc = jnp.transpose(biasc.reshape(nq, tq, LANES, NSA_KV, HPG), (0, 3, 4, 1, 2)).reshape(nq, NSA_KV, HPG * tq, LANES)
    slc_deltas = (0, tq, 2 * tq)
    win_deltas = tuple(range(0, WINDOW + 1, tq))
    assert tp % (tk * KEY_TILES_PER_STEP) == 0 and n_slc <= LANES // 2
    bslc = _with_masked_tile(_toeplitz_tiles(tbl, slc_deltas, tq, tk, 1 << 30))
    bwin = _with_masked_tile(_toeplitz_tiles(tbl, win_deltas, tq, tk, WINDOW))
    mslc = jnp.pad(_slc_matrix(n_cmp, n_slc), ((0, LANES - n_cmp), (0, LANES - n_slc))).astype(BF16)
    kern = functools.partial(_nsa_prompt_kernel, tq=tq, tk=tk, n_slc=n_slc, n_slc_tiles=len(slc_deltas),
                             n_win_tiles=len(win_deltas))
    seq = lambda b, i: b
    tile = lambda b, i: b * nq + i
    full4 = lambda b, i: (0, 0, 0, 0)
    return pl.pallas_call(
        kern,
        grid=(bp, nq),
        in_specs=[
            _zspec(tq, "q_n", 512, tile), _zspec(tq, "g_n", LANES, tile),
            pl.BlockSpec((1, LANES, LANES), lambda b, i: (b, 0, 0)), pl.BlockSpec((1, LANES, LANES), lambda b, i: (b, 0, 0)),
        ] + [pl.BlockSpec((tp, LANES), (lambda b, i, cb=_Z_OFF[nm] // LANES: (b, cb))) for nm in ("ks", "vs", "kw", "vw")] + [
            pl.BlockSpec((None, NSA_KV, HPG * tq, LANES), lambda b, i: (i, 0, 0, 0)),
            pl.BlockSpec(bslc.shape, full4), pl.BlockSpec(bwin.shape, full4),
            pl.BlockSpec((LANES, LANES), lambda b, i: (0, 0)),
        ],
        out_specs=pl.BlockSpec((tq, 512), lambda b, i: (b * nq + i, 0)),
        out_shape=jax.ShapeDtypeStruct((bp * tp, 512), F32),
        scratch_shapes=[
            pltpu.VMEM((NSA_KV, HPG * tq, LANES), BF16),
            pltpu.VMEM((2, HPG * tq, 1), F32), pltpu.VMEM((2, HPG * tq, 1), F32), pltpu.VMEM((2, HPG * tq, LANES), F32),
            pltpu.VMEM((NSA_KV, HPG * tq, LANES), F32),
        ],
        compiler_params=_params(("arbitrary", "arbitrary")),
        name="nsa_prompt",
    )(z, z, kc, vc, z, z, z, z, biasc, bslc, bwin, mslc)


def _slc_matrix(n_cmp, n_slc):
    ratio, c_sub = SLC_LEN // CMP_STRIDE, CMP_LEN // CMP_STRIDE
    m = np.zeros((n_cmp, n_slc), np.float32)
    for j in range(n_slc):
        for o in range(-(c_sub - 1), ratio):
            k = ratio * j + o
            if 0 <= k < n_cmp:
                m[k, j] += float(min(o + c_sub, ratio) - max(o, 0))
    return jnp.asarray(m)


def _moba_prompt_kernel(zq_ref, k_ref, v_ref, bias_ref, o_ref, q_scr, m_scr, l_scr, acc_scr, *, tq, tk, nb, n_tiles):
    qt = pl.program_id(1)
    q0 = qt * tq
    _stack_queries_prompt(zq_ref, q_scr, tq)
    rows = HPG * tq
    lane = lax.broadcasted_iota(jnp.int32, (rows, LANES), 1)
    trow = q0 + (lax.broadcasted_iota(jnp.int32, (rows, LANES), 0) & (tq - 1))
    own = trow >> 8
    kmean = jnp.concatenate(
        [jnp.mean(k_ref[n * MOBA_BLOCK:(n + 1) * MOBA_BLOCK, :], axis=0, keepdims=True) for n in range(nb)]
        + [jnp.zeros((LANES - nb, LANES), F32)], axis=0)
    km_hi, km_lo = _split_hi_lo(kmean)
    qs = [q_scr[g] for g in range(MOBA_KV)]
    bms = []
    for q in qs:
        gs = (lax.dot_general(q, km_hi, _NT, preferred_element_type=F32)
              + lax.dot_general(q, km_lo, _NT, preferred_element_type=F32))
        gs = jnp.where(lane < own, gs, NEG_INF)
        sel = _take_topk(gs, min(MOBA_TOPK, nb - 1), lane)
        bms.append(jnp.where(lane == own, 1.0, sel).astype(BF16))
    res = _flash(qs, k_ref, v_ref, bias_ref, _causal_tile_index(n_tiles), bms, 8, 1, 0,
                 qt // KEY_TILES_PER_STEP + 1, qt, m_scr, l_scr, acc_scr, tk)
    _unstack_heads(res, o_ref, tq)


def _moba_prompt(z, rel_table, bp, tp):
    tq, tk = ATT_TQ, ATT_TK
    nq = tp // tq
    nb = tp // MOBA_BLOCK
    assert tp % MOBA_BLOCK == 0 and tq & (tq - 1) == 0
    tbl = rel_table[:, NSA_HEADS:].reshape(REL_BUCKETS, MOBA_KV, HPG)
    deltas = (0, tq, 2 * tq)
    assert tp % (tk * KEY_TILES_PER_STEP) == 0
    bias = _with_masked_tile(_toeplitz_tiles(tbl, deltas, tq, tk, 1 << 30))
    kern = functools.partial(_moba_prompt_kernel, tq=tq, tk=tk, nb=nb, n_tiles=len(deltas))
    return pl.pallas_call(
        kern,
        grid=(bp, nq),
        in_specs=[
            _zspec(tq, "q_m", 512, lambda b, i: b * nq + i),
            pl.BlockSpec((tp, LANES), lambda b, i: (b, _Z_OFF["k_m"] // LANES)),
            pl.BlockSpec((tp, LANES), lambda b, i: (b, _Z_OFF["v_m"] // LANES)),
            pl.BlockSpec(bias.shape, lambda b, i: (0, 0, 0, 0)),
        ],
        out_specs=pl.BlockSpec((tq, 512), lambda b, i: (b * nq + i, 0)),
        out_shape=jax.ShapeDtypeStruct((bp * tp, 512), F32),
        scratch_shapes=[
            pltpu.VMEM((MOBA_KV, HPG * tq, LANES), BF16),
            pltpu.VMEM((2, HPG * tq, 1), F32), pltpu.VMEM((2, HPG * tq, 1), F32), pltpu.VMEM((2, HPG * tq, LANES), F32),
        ],
        compiler_params=_params(("arbitrary", "arbitrary")),
        name="moba_prompt",
    )(z, z, z, bias)


def _merge_kernel(on_ref, om_ref, ox_ref, gm_ref, x_ref, wb_ref, wo_ref, fg_ref, x1_ref, xn_ref):
    d = x_ref.shape[1]
    merged = jnp.zeros(x_ref.shape, F32)
    for c, o_ref in enumerate((on_ref, om_ref, ox_ref)):
        proj = jnp.dot(o_ref[...].astype(BF16), wb_ref[c], preferred_element_type=F32)
        merged = merged + jax.nn.sigmoid(gm_ref[:, c * d:(c + 1) * d]) * proj
    x1 = x_ref[...] + jnp.dot(merged.astype(BF16), wo_ref[...], preferred_element_type=F32)
    x1_ref[...] = x1
    ms = jnp.mean(x1 * x1, axis=-1, keepdims=True)
    xn_ref[...] = (x1 * lax.rsqrt(ms + RMS_EPS) * fg_ref[...]).astype(BF16)


def _merge(o_nsa, o_moba, o_mem, z, x, w_branch, w_out, ffn_gain, tm=256):
    n, d = x.shape
    assert n % tm == 0 and _Z_OFF["g_merge"] == 0
    row = lambda i: (i, 0)
    return pl.pallas_call(
        _merge_kernel,
        grid=(n // tm,),
        in_specs=[
            pl.BlockSpec((tm, 512), row), pl.BlockSpec((tm, 512), row), pl.BlockSpec((tm, 512), row),
            pl.BlockSpec((tm, 3 * d), row), pl.BlockSpec((tm, d), row),
            pl.BlockSpec((3, 512, d), lambda i: (0, 0, 0)), pl.BlockSpec((d, d), lambda i: (0, 0)),
            pl.BlockSpec((1, d), lambda i: (0, 0)),
        ],
        out_specs=[pl.BlockSpec((tm, d), row), pl.BlockSpec((tm, d), row)],
        out_shape=[jax.ShapeDtypeStruct((n, d), F32), jax.ShapeDtypeStruct((n, d), BF16)],
        compiler_params=_params(("arbitrary",)),
        name="merge_outproj",
    )(o_nsa, o_moba, o_mem, z, x, w_branch.astype(BF16), w_out.astype(BF16), ffn_gain[None, :])


def _peer_route_kernel(xT_ref, wqh_ref, wql_ref, skh_ref, skl_ref, s1_ref, s2_ref, tau_ref):
    xT = xT_ref[...]
    qT = jnp.dot(wqh_ref[...], xT, preferred_element_type=F32) + jnp.dot(wql_ref[...], xT, preferred_element_type=F32)
    half = HEAD_DIM

    def top_sorted(s):
        vals, cur = [], s
        for _ in range(PEER_TOPK + 1):
            mx = jnp.max(cur, axis=0, keepdims=True)
            vals.append(mx)
            cur = jnp.where(cur == mx, KNOCKED, cur)
        return vals

    taus = []
    for h in range(PEER_HEADS):
        sc = []
        for p in range(2):
            r0 = (h * 2 + p) * half
            q_hi, q_lo = _split_hi_lo(qT[r0:r0 + half, :])
            sc.append(jnp.dot(skh_ref[h, p], q_hi, preferred_element_type=F32)
                      + jnp.dot(skl_ref[h, p], q_hi, preferred_element_type=F32)
                      + jnp.dot(skh_ref[h, p], q_lo, preferred_element_type=F32))
        a, b = top_sorted(sc[0]), top_sorted(sc[1])
        n_rank = PEER_TOPK + 1
        cand = jnp.concatenate([a[i] + b[j] for i in range(n_rank) for j in range(n_rank // (i + 1))], axis=0)
        cur = cand
        for _ in range(PEER_TOPK):
            tau = jnp.max(cur, axis=0, keepdims=True)
            cur = jnp.where(cur == tau, KNOCKED, cur)
        nxt = jnp.max(cur, axis=0, keepdims=True)
        top = a[0] + b[0]
        zsum = jnp.sum(jnp.where(cand >= tau, jnp.exp(cand - top), 0.0), axis=0, keepdims=True)
        shift = top + jnp.log(zsum)
        s1_ref[h] = (sc[0] - shift) * LOG2E
        s2_ref[h] = sc[1] * LOG2E
        taus.append((0.5 * (tau + nxt) - shift) * LOG2E)
    tau_ref[...] = jnp.concatenate(taus, axis=0)


def _peer_route(xT, peer_w_q, sub_keys, tt=256):
    d, n = xT.shape
    assert n % tt == 0
    wqh, wql = _split_hi_lo(peer_w_q.T)
    skh, skl = _split_hi_lo(sub_keys)
    tok3 = lambda i: (0, 0, i)
    big = jax.ShapeDtypeStruct((PEER_HEADS, PEER_NKEYS, n), F32)
    return pl.pallas_call(
        _peer_route_kernel,
        grid=(n // tt,),
        in_specs=[
            pl.BlockSpec((d, tt), lambda i: (0, i)),
            pl.BlockSpec((d, d), lambda i: (0, 0)), pl.BlockSpec((d, d), lambda i: (0, 0)),
            pl.BlockSpec(sub_keys.shape, lambda i: (0, 0, 0, 0)), pl.BlockSpec(sub_keys.shape, lambda i: (0, 0, 0, 0)),
        ],
        out_specs=[pl.BlockSpec((PEER_HEADS, PEER_NKEYS, tt), tok3)] * 2 + [pl.BlockSpec((PEER_HEADS, tt), lambda i: (0, i))],
        out_shape=[big] * 2 + [jax.ShapeDtypeStruct((PEER_HEADS, n), F32)],
        compiler_params=_params(("arbitrary",)),
        name="peer_route",
    )(xT, wqh, wql, skh, skl)


def _peer_dense_kernel(xT_ref, u_ref, vT_ref, s1_ref, s2_ref, tau_ref, o_ref, a_ref, *, ec, tt):
    c = pl.program_id(1)

    @pl.when(c == 0)
    def _():
        o_ref[...] = jnp.zeros_like(o_ref)

    h_t = jnp.dot(u_ref[...], xT_ref[...], preferred_element_type=F32)
    for k in range(ec // LANES):
        for lc in range(0, tt, LANES):
            wk = jnp.zeros((LANES, LANES), F32)
            for h in range(PEER_HEADS):
                csum = s2_ref[h, :, lc:lc + LANES] + s1_ref[h, k:k + 1, lc:lc + LANES]
                wk = wk + jnp.where(csum >= tau_ref[h:h + 1, lc:lc + LANES], jnp.exp2(csum), 0.0)
            g = _gelu_tanh(h_t[k * LANES:(k + 1) * LANES, lc:lc + LANES])
            a_ref[k * LANES:(k + 1) * LANES, lc:lc + LANES] = (wk * g).astype(BF16)
    o_ref[...] += jnp.dot(vT_ref[...], a_ref[...], preferred_element_type=F32)


def _peer_dense(xT, u_b, vT_b, s1T, s2T, tauT, ec=1024):
    d, n = xT.shape
    tt = 512 if n % 512 == 0 else 256
    assert n % tt == 0 and ec == 8 * PEER_NKEYS
    ne = u_b.shape[0]
    kern = functools.partial(_peer_dense_kernel, ec=ec, tt=tt)
    allk = pl.BlockSpec((PEER_HEADS, PEER_NKEYS, tt), lambda i, c: (0, 0, i))
    chunk = pl.BlockSpec((PEER_HEADS, ec // PEER_NKEYS, tt), lambda i, c: (0, c, i))
    return pl.pallas_call(
        kern,
        grid=(n // tt, ne // ec),
        in_specs=[
            pl.BlockSpec((d, tt), lambda i, c: (0, i)),
            pl.BlockSpec((ec, d), lambda i, c: (c, 0)),
            pl.BlockSpec((d, ec), lambda i, c: (0, c)),
            chunk, allk,
            pl.BlockSpec((PEER_HEADS, tt), lambda i, c: (0, i)),
        ],
        out_specs=pl.BlockSpec((d, tt), lambda i, c: (0, i)),
        out_shape=jax.ShapeDtypeStruct((d, n), F32),
        scratch_shapes=[pltpu.VMEM((ec, tt), BF16)],
        compiler_params=_params(("arbitrary", "arbitrary")),
        name="peer_dense",
    )(xT, u_b, vT_b, s1T, s2T, tauT)


def _peer(x1, xn_b, peer_w_q, sub_keys, u_b, vT_b):
    xT = xn_b.T
    s1T, s2T, tauT = _peer_route(xT, peer_w_q, sub_keys)
    out_t = _peer_dense(xT, u_b, vT_b, s1T, s2T, tauT)
    return x1 + out_t.T


PAGES_PER_STEP = 64


def _page_specs(n_lead, inner):
    zeros = (0,) * len(inner)
    return [pl.BlockSpec((1,) + inner, (lambda b, j, pt, p=p: (pt[b, j * PAGES_PER_STEP + p],) + zeros))
            for p in range(n_lead)]


def _compress_paged_kernel(pt_ref, *refs, norm):
    del pt_ref
    p_n = PAGES_PER_STEP
    x_refs = refs[:p_n]
    perm_ref, wh_ref, c_ref, w2h_ref, w2l_ref, seg_ref, gain_ref, o_ref, x_scr = refs[p_n:]
    j = pl.program_id(1)
    pair_sub = 2 * x_refs[0].shape[2] // CMP_STRIDE
    for pp in range(p_n // 2):
        x_t = jnp.concatenate([x_refs[2 * pp][0], x_refs[2 * pp + 1][0]], axis=1).astype(BF16)
        rows = lax.dot_general(perm_ref[...], x_t, _NT, preferred_element_type=F32).astype(BF16)
        base = pl.multiple_of((j * (p_n // 2) + pp) * pair_sub, pair_sub)
        for l in range(CMP_STRIDE):
            x_scr[pl.ds(base, pair_sub), l * LANES:(l + 1) * LANES] = rows[l * pair_sub:(l + 1) * pair_sub, :]

    @pl.when(j == pl.num_programs(1) - 1)
    def _():
        ns = x_scr.shape[0]
        a = jnp.dot(x_scr[...], wh_ref[...], preferred_element_type=F32)
        nxt = pltpu.roll(a[:, LANES:], ns - 1, axis=0)
        h = a[:, :LANES] + nxt + c_ref[...]
        o = _dot3(_gelu_tanh(h), w2h_ref[...], w2l_ref[...])
        if norm:
            o = o * lax.rsqrt(_group_sumsq(o, seg_ref[...]) * (1.0 / HEAD_DIM) + RMS_EPS) * gain_ref[...]
        o_ref[0] = o


def _compress_paged(pool, page_table, weights, gain, norm):
    n_phys, _, page = pool.shape
    b, n_pages = page_table.shape
    sub = page // CMP_STRIDE
    kk = CMP_STRIDE * LANES
    ns = n_pages * sub
    assert n_pages % PAGES_PER_STEP == 0 and page % CMP_STRIDE == 0 and PAGES_PER_STEP % 2 == 0 and (2 * sub) % 16 == 0
    wh, _, cvec, w2h, w2l = weights
    const = lambda b_, j, pt: (0, 0)
    r = np.arange(2 * page)
    perm = jnp.asarray(((r[:, None] // (2 * sub)) + CMP_STRIDE * (r[:, None] % (2 * sub))) == r[None, :], BF16)
    grid_spec = pltpu.PrefetchScalarGridSpec(
        num_scalar_prefetch=1,
        grid=(b, n_pages // PAGES_PER_STEP),
        in_specs=_page_specs(PAGES_PER_STEP, (LANES, page)) + [
            pl.BlockSpec((2 * page, 2 * page), const), pl.BlockSpec((kk, 2 * LANES), const),
            pl.BlockSpec((1, LANES), const),
            pl.BlockSpec((LANES, LANES), const), pl.BlockSpec((LANES, LANES), const),
            pl.BlockSpec((LANES, LANES), const), pl.BlockSpec((1, LANES), const),
        ],
        out_specs=pl.BlockSpec((1, ns, LANES), lambda b_, j, pt: (b_, 0, 0)),
        scratch_shapes=[pltpu.VMEM((ns, kk), BF16)],
    )
    return pl.pallas_call(
        functools.partial(_compress_paged_kernel, norm=norm),
        grid_spec=grid_spec,
        out_shape=jax.ShapeDtypeStruct((b, ns, LANES), F32),
        compiler_params=_params(("arbitrary", "arbitrary")),
        name="nsa_compress_paged",
    )(page_table, *([pool] * PAGES_PER_STEP), perm, wh, cvec, w2h, w2l, _seg_matrix(HEAD_DIM), gain)


def _softmax_tile(s):
    mx = jnp.maximum(jnp.max(s, axis=-1, keepdims=True), M_FLOOR)
    p = jnp.exp(s - mx)
    return mx, p, jnp.sum(p, axis=-1, keepdims=True)


def _pv(p, v, v_t):
    if v_t:
        return lax.dot_general(p.astype(BF16), v, _NT, preferred_element_type=F32)
    return jnp.dot(p.astype(BF16), v, preferred_element_type=F32)


def _online_update(s, v, m_scr, l_scr, acc_scr, v_t=False):
    m_old = m_scr[...]
    m_new = jnp.maximum(m_old, jnp.max(s, axis=-1, keepdims=True))
    p = jnp.exp(s - m_new)
    alpha = jnp.exp(m_old - m_new)
    l_scr[...] = alpha * l_scr[...] + jnp.sum(p, axis=-1, keepdims=True)
    acc_scr[...] = alpha * acc_scr[...] + _pv(p, v, v_t)
    m_scr[...] = m_new


def _pad_rows(x, rows):
    return jnp.concatenate([x, jnp.zeros((rows - x.shape[0], x.shape[1]), x.dtype)], axis=0)


def _sample_row_bias(tbl, ts, kpos, qpos0, lo_ok):
    d = (qpos0 + jnp.arange(ts))[:, None] - kpos[None, :]
    b = jnp.where(((d >= 0) & lo_ok)[..., None, None], _bias_lookup(tbl, d), NEG_INF)
    return jnp.transpose(b, (2, 3, 0, 1)).reshape(-1, kpos.shape[0])


def _past_bias_tiles(tbl, ts, past_len, nk):
    assert nk >= LANES
    far = jnp.repeat(tbl[REL_BUCKETS - 1].reshape(-1), ts)[:, None]
    far_tile = jnp.broadcast_to(far, (far.shape[0], nk))
    near = _sample_row_bias(tbl, ts, past_len - LANES + jnp.arange(LANES), past_len, jnp.ones((ts, LANES), bool))
    return jnp.stack([far_tile, jnp.concatenate([far_tile[:, :nk - LANES], near], axis=1)])


def _block_expand_matrix(n_blk_lanes, nk, n_steps):
    blk_step = nk // SLC_LEN
    assert nk % SLC_LEN == 0 and blk_step % 16 == 0
    r = np.arange(n_blk_lanes + (n_steps - 1) * blk_step)[:, None]
    key_blk = (np.arange(nk) // SLC_LEN)[None, :]
    return jnp.asarray(r == (n_steps - 1) * blk_step + key_blk, BF16)


def _nsa_sample_kernel(pt_ref, *refs, ts, past_len, n_slc, slc_lanes):
    del pt_ref
    p_n = PAGES_PER_STEP
    k_pages, v_pages = refs[:p_n], refs[p_n:2 * p_n]
    (zq_ref, gn_ref, kc_ref, vc_ref, ksn_ref, vsn_ref, wk_ref, wv_ref, kwn_ref, vwn_ref,
     biasc_ref, bpast_ref, bnew_ref, bwin_ref, mslc_ref, expand_ref, o_ref,
     q_scr, m_scr, l_scr, acc_scr, comb_scr, bm_scr) = refs[2 * p_n:]
    j = pl.program_id(1)
    page = k_pages[0].shape[2]
    rows = NSA_KV * HPG * ts
    half = HPG * ts

    def gate_col(gn, c):
        return jnp.concatenate([gn[:, c * 8 + hd:c * 8 + hd + 1] for hd in range(NSA_HEADS)], axis=0)

    @pl.when(j == 0)
    def _():
        _stack_queries_sample(zq_ref, q_scr, ts)
        q = q_scr[...].astype(BF16)
        gn = jax.nn.sigmoid(gn_ref[...])
        s_c = lax.dot_general(q, kc_ref[0].astype(BF16), _NT, preferred_element_type=F32) + biasc_ref[...]
        mx, p, den = _softmax_tile(s_c)
        p_c = p / jnp.where(den > 0.0, den, 1.0)
        comb_scr[...] = gate_col(gn, 0) * jnp.dot(p_c.astype(BF16), vc_ref[0].astype(BF16), preferred_element_type=F32)
        imp = jnp.concatenate(
            [sum(p_c[g * half + h * ts:g * half + (h + 1) * ts] for h in range(HPG)) for g in range(NSA_KV)], axis=0)
        imp_hi, imp_lo = _split_hi_lo(imp)
        p_slc = (jnp.dot(imp_hi, mslc_ref[...], preferred_element_type=F32)
                 + jnp.dot(imp_lo, mslc_ref[...], preferred_element_type=F32))
        lane = lax.broadcasted_iota(jnp.int32, p_slc.shape, 1)
        tpos = past_len + (lax.broadcasted_iota(jnp.int32, p_slc.shape, 0) & (ts - 1))
        qb = tpos >> 6
        forced = (lane == 0) | (lane == qb) | (lane == qb - 1)
        score = jnp.where(lane > qb, NEG_INF, p_slc + jnp.where(forced, FORCE_BONUS, 0.0))
        sel = _rank_select(score, SLC_TOPN, lane, n_slc)
        bm_scr[...] = jnp.concatenate(
            [sel[g * ts:(g + 1) * ts] for g in range(NSA_KV) for _ in range(HPG)], axis=0).astype(BF16)
        m_scr[...] = jnp.full(m_scr.shape, M_FLOOR, F32)
        l_scr[...] = jnp.zeros(l_scr.shape, F32)
        acc_scr[...] = jnp.zeros(acc_scr.shape, F32)
        s_w = jnp.dot(q, wk_ref[0].astype(BF16), preferred_element_type=F32) + bwin_ref[...]
        _online_update(s_w, wv_ref[0].astype(BF16), m_scr, l_scr, acc_scr, v_t=True)
        s_n = lax.dot_general(q, _pad_rows(kwn_ref[...], LANES).astype(BF16), _NT, preferred_element_type=F32) + bnew_ref[...]
        _online_update(s_n, _pad_rows(vwn_ref[...], LANES).astype(BF16), m_scr, l_scr, acc_scr)
        l = l_scr[...]
        comb_scr[...] = comb_scr[...] + gate_col(gn, 2) * (acc_scr[...] / jnp.where(l > 0.0, l, 1.0))
        m_scr[...] = jnp.full(m_scr.shape, M_FLOOR, F32)
        l_scr[...] = jnp.zeros(l_scr.shape, F32)
        acc_scr[...] = jnp.zeros(acc_scr.shape, F32)

    q = q_scr[...].astype(BF16)
    nk = p_n * page
    k_t = jnp.concatenate([r[0].astype(BF16) for r in k_pages], axis=1)
    v_t = jnp.concatenate([r[0].astype(BF16) for r in v_pages], axis=1)
    last = pl.num_programs(1) - 1
    blk_step = nk // SLC_LEN
    expand = expand_ref[pl.ds(pl.multiple_of((last - j) * blk_step, blk_step), slc_lanes), :]
    vis = jnp.dot(bm_scr[...], expand, preferred_element_type=F32)
    s = (jnp.dot(q, k_t, preferred_element_type=F32) + bpast_ref[jnp.where(j == last, 1, 0)]
         + jnp.where(vis > 0.5, 0.0, NEG_INF))
    _online_update(s, v_t, m_scr, l_scr, acc_scr, v_t=True)

    @pl.when(j == pl.num_programs(1) - 1)
    def _():
        gn = jax.nn.sigmoid(gn_ref[...])
        s_n = lax.dot_general(q, _pad_rows(ksn_ref[...], LANES).astype(BF16), _NT, preferred_element_type=F32) + bnew_ref[...]
        _online_update(s_n, _pad_rows(vsn_ref[...], LANES).astype(BF16), m_scr, l_scr, acc_scr)
        l = l_scr[...]
        comb = comb_scr[...] + gate_col(gn, 1) * (acc_scr[...] / jnp.where(l > 0.0, l, 1.0))
        _unstack_heads([comb[:half], comb[half:]], o_ref, ts)


def _nsa_sample(z, n_p, kc, vc, pool_k, pool_v, win_k, win_v, page_table, rel_table, ts):
    bs, n_pages = page_table.shape
    n_phys, _, page = pool_k.shape
    past_len = n_pages * page
    p_n = PAGES_PER_STEP
    n_steps = n_pages // p_n
    nk = p_n * page
    n_sub = kc.shape[1]
    n_cmp = n_sub - 1
    n_slc = -(-(past_len + ts) // SLC_LEN)
    slc_lanes = -(-n_slc // LANES) * LANES
    win_len = win_k.shape[2]
    assert n_pages % p_n == 0 and ts & (ts - 1) == 0 and ts <= SLC_LEN and past_len % SLC_LEN == 0 and n_p % ts == 0
    assert n_sub % LANES == 0 and win_len + ts >= WINDOW
    tbl = rel_table[:, :NSA_HEADS].reshape(REL_BUCKETS, NSA_KV, HPG)
    rows = NSA_HEADS * ts
    cmp_end = jnp.arange(n_sub) * CMP_STRIDE + CMP_LEN - 1
    biasc = _sample_row_bias(tbl, ts, cmp_end, past_len, jnp.broadcast_to(jnp.arange(n_sub) < n_cmp, (ts, n_sub)))
    bpast = _past_bias_tiles(tbl, ts, past_len, nk)
    bnew = _sample_row_bias(tbl, ts, past_len + jnp.arange(LANES), past_len, jnp.broadcast_to(jnp.arange(LANES) < ts, (ts, LANES)))
    wpos = past_len - win_len + jnp.arange(win_len)
    in_win = ((past_len + jnp.arange(ts))[:, None] - wpos[None, :]) < WINDOW
    bwin = _sample_row_bias(tbl, ts, wpos, past_len, in_win)
    mslc = jnp.pad(_slc_matrix(n_cmp, n_slc), ((0, n_sub - n_cmp), (0, slc_lanes - n_slc))).astype(BF16)
    expand = _block_expand_matrix(slc_lanes, nk, n_steps)
    row_blk = n_p // ts
    zrow = lambda name, width: pl.BlockSpec((ts, width), (lambda b, j, pt, cb=_Z_OFF[name] // width: (row_blk + b, cb)))
    seq3 = lambda shape: pl.BlockSpec((1,) + shape, lambda b, j, pt: (b, 0, 0))
    const2 = lambda shape: pl.BlockSpec(shape, lambda b, j, pt: (0, 0))
    grid_spec = pltpu.PrefetchScalarGridSpec(
        num_scalar_prefetch=1,
        grid=(bs, n_steps),
        in_specs=_page_specs(p_n, (LANES, page)) + _page_specs(p_n, (LANES, page)) + [
            zrow("q_n", 512), zrow("g_n", LANES),
            seq3((n_sub, LANES)), seq3((n_sub, LANES)),
            zrow("ks", LANES), zrow("vs", LANES),
            seq3((LANES, win_len)), seq3((LANES, win_len)),
            zrow("kw", LANES), zrow("vw", LANES),
            const2((rows, n_sub)),
            pl.BlockSpec((2, rows, nk), lambda b, j, pt: (0, 0, 0)),
            const2((rows, LANES)), const2((rows, win_len)), const2((n_sub, slc_lanes)), const2(expand.shape),
        ],
        out_specs=pl.BlockSpec((ts, 512), lambda b, j, pt: (b, 0)),
        scratch_shapes=[
            pltpu.VMEM((rows, LANES), F32),
            pltpu.VMEM((rows,1), F32), pltpu.VMEM((rows, 1), F32), pltpu.VMEM((rows, LANES), F32),
            pltpu.VMEM((rows, LANES), F32), pltpu.VMEM((rows, slc_lanes), BF16),
        ],
    )
    return pl.pallas_call(
        functools.partial(_nsa_sample_kernel, ts=ts, past_len=past_len, n_slc=n_slc, slc_lanes=slc_lanes),
        grid_spec=grid_spec,
        out_shape=jax.ShapeDtypeStruct((bs * ts, 512), F32),
        compiler_params=_params(("arbitrary", "arbitrary")),
        name="nsa_sample",
    )(page_table, *([pool_k] * p_n), *([pool_v] * p_n), z, z, kc, vc, z, z, win_k, win_v, z, z,
      biasc, bpast, bnew, bwin, mslc, expand)


def _moba_sample_kernel(pt_ref, *refs, ts, n_past_blocks):
    del pt_ref
    p_n = PAGES_PER_STEP
    k_pages, v_pages = refs[:p_n], refs[p_n:2 * p_n]
    (zq_ref, kn_ref, vn_ref, bpast_ref, bnew_ref, o_ref, q_scr, m_all, l_all, acc_all, km_scr) = refs[2 * p_n:]
    j = pl.program_id(1)
    page = k_pages[0].shape[2]
    ppb = MOBA_BLOCK // page
    bps = p_n // ppb
    rows = MOBA_KV * HPG * ts
    half = HPG * ts
    lane = lax.broadcasted_iota(jnp.int32, (rows, LANES), 1)
    kcol = lax.broadcasted_iota(jnp.int32, (LANES, LANES), 1)
    last = pl.num_programs(1) - 1

    @pl.when(j == 0)
    def _():
        _stack_queries_sample(zq_ref, q_scr, ts)
        m_all[...] = jnp.full(m_all.shape, NEG_INF, F32)
        l_all[...] = jnp.zeros(l_all.shape, F32)
        km_scr[...] = jnp.zeros(km_scr.shape, F32)

    q = q_scr[...].astype(BF16)
    for i in range(bps):
        n = j * bps + i
        k_t = jnp.concatenate([k_pages[i * ppb + r][0] for r in range(ppb)], axis=1)
        v_t = jnp.concatenate([v_pages[i * ppb + r][0].astype(BF16) for r in range(ppb)], axis=1)
        s = (jnp.dot(q, k_t.astype(BF16), preferred_element_type=F32)
             + bpast_ref[jnp.where(j == last, 1, 0), :, i * MOBA_BLOCK:(i + 1) * MOBA_BLOCK])
        mx, p, den = _softmax_tile(s)
        m_all[...] = jnp.where(lane == n, mx, m_all[...])
        l_all[...] = jnp.where(lane == n, den, l_all[...])
        acc_all[n] = _pv(p, v_t, True)
        km_scr[...] = jnp.where(kcol == n, jnp.mean(k_t, axis=1, keepdims=True), km_scr[...])

    @pl.when(j == pl.num_programs(1) - 1)
    def _():
        s_o = lax.dot_general(q, _pad_rows(kn_ref[...], LANES).astype(BF16), _NT, preferred_element_type=F32) + bnew_ref[...]
        m_o, p_o, l_o = _softmax_tile(s_o)
        acc_o = jnp.dot(p_o.astype(BF16), _pad_rows(vn_ref[...], LANES).astype(BF16), preferred_element_type=F32)
        km_hi, km_lo = _split_hi_lo(km_scr[...])
        gs = jnp.dot(q, km_hi, preferred_element_type=F32) + jnp.dot(q, km_lo, preferred_element_type=F32)
        gs = jnp.where(lane < n_past_blocks, gs, NEG_INF)
        sel = _take_topk(gs, min(MOBA_TOPK, n_past_blocks), lane)
        m_sel = jnp.where(sel > 0.5, m_all[...], NEG_INF)
        m_fin = jnp.maximum(jnp.max(m_sel, axis=-1, keepdims=True), m_o)
        wgt = jnp.where(sel > 0.5, jnp.exp(m_sel - m_fin), 0.0)
        w_o = jnp.exp(m_o - m_fin)
        den = jnp.sum(wgt * l_all[...], axis=-1, keepdims=True) + w_o * l_o
        num = w_o * acc_o
        for n in range(n_past_blocks):
            num = num + wgt[:, n:n + 1] * acc_all[n]
        res = num / den
        _unstack_heads([res[:half], res[half:]], o_ref, ts)


def _moba_sample(z, n_p, pool_k, pool_v, page_table, rel_table, ts):
    bs, n_pages = page_table.shape
    n_phys, _, page = pool_k.shape
    past_len = n_pages * page
    p_n = PAGES_PER_STEP
    n_steps = n_pages // p_n
    nk = p_n * page
    nb_past = past_len // MOBA_BLOCK
    assert (MOBA_BLOCK % page == 0 and p_n % (MOBA_BLOCK // page) == 0 and past_len % MOBA_BLOCK == 0
            and ts <= MOBA_BLOCK and nb_past <= LANES and n_pages % p_n == 0 and n_p % ts == 0)
    tbl = rel_table[:, NSA_HEADS:].reshape(REL_BUCKETS, MOBA_KV, HPG)
    rows = MOBA_HEADS * ts
    bpast = _past_bias_tiles(tbl, ts, past_len, nk)
    bnew = _sample_row_bias(tbl, ts, past_len + jnp.arange(LANES), past_len, jnp.broadcast_to(jnp.arange(LANES) < ts, (ts, LANES)))
    row_blk = n_p // ts
    zrow = lambda name, width: pl.BlockSpec((ts, width), (lambda b, j, pt, cb=_Z_OFF[name] // width: (row_blk + b, cb)))
    grid_spec = pltpu.PrefetchScalarGridSpec(
        num_scalar_prefetch=1,
        grid=(bs, n_steps),
        in_specs=_page_specs(p_n, (LANES, page)) + _page_specs(p_n, (LANES, page)) + [
            zrow("q_m", 512), zrow("k_m", LANES), zrow("v_m", LANES),
            pl.BlockSpec((2, rows, nk), lambda b, j, pt: (0, 0, 0)),
            pl.BlockSpec((rows, LANES), lambda b, j, pt: (0, 0)),
        ],
        out_specs=pl.BlockSpec((ts, 512), lambda b, j, pt: (b, 0)),
        scratch_shapes=[
            pltpu.VMEM((rows, LANES), F32),
            pltpu.VMEM((rows,LANES), F32), pltpu.VMEM((rows, LANES), F32),
            pltpu.VMEM((nb_past, rows, LANES), F32), pltpu.VMEM((LANES, LANES), F32),
        ],
    )
    return pl.pallas_call(
        functools.partial(_moba_sample_kernel, ts=ts, n_past_blocks=nb_past),
        grid_spec=grid_spec,
        out_shape=jax.ShapeDtypeStruct((bs * ts, 512), F32),
        compiler_params=_params(("arbitrary", "arbitrary")),
        name="moba_sample",
    )(page_table, *([pool_k] * p_n), *([pool_v] * p_n), z, z, z, bpast, bnew)


def _memkv_kernel(m_ref, gain_ref, w_ref, kgain_ref, seg_ref, k_ref, v_ref):
    x = m_ref[...]
    xb = (x * lax.rsqrt(jnp.mean(x * x, axis=-1, keepdims=True) + RMS_EPS) * gain_ref[...]).astype(BF16)
    kvw = k_ref.shape[1]
    k = jnp.dot(xb, w_ref[:, :kvw], preferred_element_type=F32)
    for h in range(MEM_HEADS):
        kh = k[:, h * MEM_HEAD_DIM:(h + 1) * MEM_HEAD_DIM]
        ss = _group_sumsq(kh, seg_ref[...])
        k_ref[:, h * MEM_HEAD_DIM:(h + 1) * MEM_HEAD_DIM] = kh * lax.rsqrt(ss * (1.0 / MEM_HEAD_DIM) + RMS_EPS) * kgain_ref[...]
    v_ref[...] = jnp.dot(xb, w_ref[:, kvw:], preferred_element_type=F32)


def _memory_kv_pallas(mem, norm_gain, w_mem_kv, k_gain, tm=256):
    n, d = mem.shape
    kvw = w_mem_kv.shape[1] // 2
    assert n % tm == 0
    row = lambda i: (i, 0)
    const = lambda i: (0, 0)
    return pl.pallas_call(
        _memkv_kernel,
        grid=(n // tm,),
        in_specs=[pl.BlockSpec((tm, d), row), pl.BlockSpec((1, d), const), pl.BlockSpec((d, 2 * kvw), const),
                  pl.BlockSpec((1, MEM_HEAD_DIM), const), pl.BlockSpec((LANES, LANES), const)],
        out_specs=[pl.BlockSpec((tm, kvw), row), pl.BlockSpec((tm, kvw), row)],
        out_shape=[jax.ShapeDtypeStruct((n, kvw), F32)] * 2,
        compiler_params=_params(("arbitrary",)),
        name="memory_kv",
    )(mem, norm_gain[None, :], w_mem_kv.astype(BF16), k_gain[None, :], _seg_matrix(MEM_HEAD_DIM))


def _mem_attend_kernel(zq_ref, mk_ref, mv_ref, o_ref):
    for h in range(MEM_HEADS):
        sl = slice(h * MEM_HEAD_DIM, (h + 1) * MEM_HEAD_DIM)
        q = (zq_ref[:, sl] * (MEM_HEAD_DIM ** -0.5)).astype(BF16)
        s = lax.dot_general(q, mk_ref[0, :, sl].astype(BF16), _NT, preferred_element_type=F32)
        p = jnp.exp(s - jnp.max(s, axis=-1, keepdims=True))
        o = jnp.dot(p.astype(BF16), mv_ref[0, :, sl].astype(BF16), preferred_element_type=F32)
        o_ref[:, sl] = o / jnp.sum(p, axis=-1, keepdims=True)


def _mem_attend_pallas(z, row0, n_seq, t_seq, tq, mk, mv):
    nq = t_seq // tq
    assert t_seq % tq == 0 and row0 % tq == 0
    m_len = mk.shape[1]
    width = MEM_HEADS * MEM_HEAD_DIM
    cb = _Z_OFF["q_x"] // width
    return pl.pallas_call(
        _mem_attend_kernel,
        grid=(n_seq, nq),
        in_specs=[pl.BlockSpec((tq, width), lambda b, i: (row0 // tq + b * nq + i, cb)),
                  pl.BlockSpec((1, m_len, width), lambda b, i: (b, 0, 0)),
                  pl.BlockSpec((1, m_len, width), lambda b, i: (b, 0, 0))],
        out_specs=pl.BlockSpec((tq, width), lambda b, i: (b * nq + i, 0)),
        out_shape=jax.ShapeDtypeStruct((n_seq * t_seq, width), F32),
        compiler_params=_params(("arbitrary", "arbitrary")),
        name="mem_attend",
    )(z, mk, mv)


def _zcol(z, name, width):
    return z[..., _Z_OFF[name]:_Z_OFF[name] + width]


def kernel(x_prompt, x_sample, cache_cmp_k, cache_cmp_v, cache_slc_k, cache_slc_v, cache_moba_k, cache_moba_v, cache_win_k, cache_win_v, cache_mem_k, cache_mem_v, page_table, mem_prompt, attn_norm_gain, w_in, nsa_q_gain, nsa_k_gain, cmp_pos_emb, cmp_w1, cmp_b1, cmp_w2, moba_q_gain, moba_k_gain, mem_norm_gain, w_mem_kv, mem_q_gain, mem_k_gain, rel_bias_table, w_branch, w_out, ffn_norm_gain, peer_w_q, peer_sub_keys, peer_u, peer_v):
    assert w_in.shape[0] == 1
    l = 0
    bp, tp, d = x_prompt.shape
    bs, ts, _ = x_sample.shape
    past_len = page_table.shape[1] * cache_cmp_k.shape[2]
    n_p, n_s = bp * tp, bs * ts

    x_all = jnp.concatenate([x_prompt.reshape(n_p, d), x_sample.reshape(n_s, d)], axis=0)
    hgain = _head_gain_row({"nsa_q": nsa_q_gain[l], "nsa_k1": nsa_k_gain[l, 1], "nsa_k2": nsa_k_gain[l, 2],
                            "moba_q": moba_q_gain[l], "moba_k": moba_k_gain[l], "mem_q": mem_q_gain[l]})
    z = _inproj(x_all, attn_norm_gain[l][None, :], _pack_w_in(w_in[l]), hgain)
    zp, zs = "prompt", "sample"

    def rows(group, name):
        col = _zcol(z, name, LANES)
        return col[:n_p].reshape(bp, tp, LANES) if group == zp else col[n_p:].reshape(bs, ts, LANES)

    kc_rows_p, vc_rows_p = rows(zp, "kc"), rows(zp, "vc")
    wk = _compress_weights(cmp_pos_emb[l, 0], cmp_w1[l, 0], cmp_b1[l, 0], cmp_w2[l, 0])
    wv = _compress_weights(cmp_pos_emb[l, 1], cmp_w1[l, 1], cmp_b1[l, 1], cmp_w2[l, 1])
    sub = lambda r: r.reshape(bp, tp // CMP_STRIDE, CMP_STRIDE * LANES)
    kgain = jnp.tile(nsa_k_gain[l, 0], 2)[None, :]
    kc_p = _compress(sub(kc_rows_p), wk, kgain, True)
    vc_p = _compress(sub(vc_rows_p), wv, kgain, False)
    o_nsa_p = _nsa_prompt(z, kc_p, vc_p, rel_bias_table, bp, tp)
    o_moba_p = _moba_prompt(z, rel_bias_table, bp, tp)
    m_len = mem_prompt.shape[1]
    mk_p, mv_p = _memory_kv_pallas(mem_prompt.reshape(bp * m_len, d), mem_norm_gain[l], w_mem_kv[l], mem_k_gain[l])
    mem_w = MEM_HEADS * MEM_HEAD_DIM
    o_mem_p = _mem_attend_pallas(z, 0, bp, tp, 256, mk_p.reshape(bp, m_len, mem_w), mv_p.reshape(bp, m_len, mem_w))
    mk_p = mk_p.reshape(bp, m_len, MEM_HEADS, MEM_HEAD_DIM)
    mv_p = mv_p.reshape(bp, m_len, MEM_HEADS, MEM_HEAD_DIM)

    keys_on_lanes = lambda c: jnp.transpose(c[l], (0, 2, 3, 1)).reshape(c.shape[1], LANES, c.shape[2])
    kc_s = _compress_paged(keys_on_lanes(cache_cmp_k), page_table, wk, kgain, True)
    vc_s = _compress_paged(keys_on_lanes(cache_cmp_v), page_table, wv, kgain, False)
    win_len = cache_win_k.shape[2]
    win_k = cache_win_k[l].reshape(bs, win_len, LANES)
    win_v = cache_win_v[l].reshape(bs, win_len, LANES)
    o_nsa_s = _nsa_sample(z, n_p, kc_s, vc_s, keys_on_lanes(cache_slc_k), keys_on_lanes(cache_slc_v),
                          keys_on_lanes(cache_win_k), keys_on_lanes(cache_win_v), page_table, rel_bias_table, ts)
    o_moba_s = _moba_sample(z, n_p, keys_on_lanes(cache_moba_k), keys_on_lanes(cache_moba_v), page_table,
                            rel_bias_table, ts)
    o_mem_s = _mem_attend_pallas(z, n_p, bs, ts, ts, cache_mem_k[l].reshape(bs, -1, mem_w), cache_mem_v[l].reshape(bs, -1, mem_w))
    kw_all = jnp.concatenate([win_k, rows(zs, "kw")], axis=1)
    vw_all = jnp.concatenate([win_v, rows(zs, "vw")], axis=1)

    cat = lambda a, b_: jnp.concatenate([a.reshape(n_p, 512), b_.reshape(n_s, 512)], axis=0)
    x1, xn_b = _merge(cat(o_nsa_p, o_nsa_s), cat(o_moba_p, o_moba_s), cat(o_mem_p, o_mem_s), z, x_all,
                      w_branch[l], w_out[l], ffn_norm_gain[l])
    y = _peer(x1, xn_b, peer_w_q[l], peer_sub_keys[l], peer_u[l].astype(BF16), peer_v[l].astype(BF16).T)
    y_p = y[:n_p].reshape(bp, tp, d)
    y_s = y[n_p:].reshape(bs, ts, d)

    def st(r, b_, t_):
        return r.reshape(1, b_, t_, 2, HEAD_DIM)

    keep_p = min(WINDOW, tp)
    p_win_k = st(rows(zp, "kw")[:, tp - keep_p:], bp, keep_p)
    p_win_v = st(rows(zp, "vw")[:, tp - keep_p:], bp, keep_p)
    keep_s = min(WINDOW, past_len + ts)
    s_win_k = st(kw_all[:, kw_all.shape[1] - keep_s:], bs, keep_s)
    s_win_v = st(vw_all[:, vw_all.shape[1] - keep_s:], bs, keep_s)
    return (y_p, y_s,
            st(kc_rows_p, bp, tp), st(vc_rows_p, bp, tp), st(rows(zp, "ks"), bp, tp), st(rows(zp, "vs"), bp, tp),
            st(rows(zp, "k_m"), bp, tp), st(rows(zp, "v_m"), bp, tp), p_win_k, p_win_v, mk_p[None], mv_p[None],
            st(rows(zs, "kc"), bs, ts), st(rows(zs, "vc"), bs, ts), st(rows(zs, "ks"), bs, ts), st(rows(zs, "vs"), bs, ts),
            st(rows(zs, "k_m"), bs, ts), st(rows(zs, "v_m"), bs, ts), s_win_k, s_win_v)
```

```python
import functools
import math

import jax
import jax.numpy as jnp
import numpy as np
from jax import lax
from jax.experimental import pallas as pl
from jax.experimental.pallas import tpu as pltpu

HEAD_DIM = 64
NSA_HEADS = 8
NSA_KV = 2
HPG = 4
CMP_LEN = 32
CMP_STRIDE = 16
SLC_LEN = 64
SLC_TOPN = 16
WINDOW = 512
FORCE_BONUS = 1000.0
MOBA_HEADS = 8
MOBA_KV = 2
MOBA_BLOCK = 256
MOBA_TOPK = 3
MEM_HEADS = 4
MEM_HEAD_DIM = 128
REL_BUCKETS = 32
REL_MAX_DIST = 128
PEER_HEADS = 8
PEER_NKEYS = 128
PEER_TOPK = 16
RMS_EPS = 1e-6
NEG_INF = -1e30
HALF_NEG = 0.5 * NEG_INF
M_FLOOR = 0.1 * NEG_INF
KNOCKED = -3.0e38
LOG2E = 1.4426950408889634
LANES = 128
VMEM_LIMIT = 56 * 1024 * 1024
ATT_TQ = 128
ATT_TK = 128

F32 = jnp.float32
BF16 = jnp.bfloat16
HI = lax.Precision.HIGHEST
_NT = (((1,), (1,)), ((), ()))


def _rel_bucket(dist):
    n = jnp.maximum(dist, 0)
    exact = REL_BUCKETS // 2
    nf = jnp.maximum(n, 1).astype(F32)
    large = exact + (jnp.log(nf / exact) / math.log(REL_MAX_DIST / exact) * (REL_BUCKETS - exact)).astype(jnp.int32)
    return jnp.where(n < exact, n, jnp.minimum(large, REL_BUCKETS - 1))


def _bias_lookup(tbl, dist):
    onehot = jax.nn.one_hot(_rel_bucket(dist), REL_BUCKETS, dtype=F32)
    return jnp.tensordot(onehot, tbl, axes=1, precision=HI)


def _gelu_tanh(x):
    inner = x * (0.7978845608028654 + 0.035677408136300125 * (x * x))
    return (0.5 * x) * (1.0 + jnp.tanh(inner))


def _split_hi_lo(x):
    hi = x.astype(BF16)
    lo = (x - hi.astype(F32)).astype(BF16)
    return hi, lo


def _dot3(a, b_hi, b_lo):
    a_hi, a_lo = _split_hi_lo(a)
    return (jnp.dot(a_hi, b_hi, preferred_element_type=F32) + jnp.dot(a_lo, b_hi, preferred_element_type=F32)
            + jnp.dot(a_hi, b_lo, preferred_element_type=F32))


def _seg_matrix(hd):
    i = np.arange(LANES)
    return jnp.asarray((i[:, None] // hd) == (i[None, :] // hd), BF16)


def _group_sumsq(z, seg):
    sq_hi, sq_lo = _split_hi_lo(z * z)
    return jnp.dot(sq_hi, seg, preferred_element_type=F32) + jnp.dot(sq_lo, seg, preferred_element_type=F32)


def _params(sem):
    return pltpu.CompilerParams(dimension_semantics=sem, vmem_limit_bytes=VMEM_LIMIT)


_Z_GROUPS = (
    ("g_merge", 3072, 0), ("q_n", 512, 64), ("q_m", 512, 64), ("q_x", 512, 128),
    ("kc", 128, 0), ("vc", 128, 0), ("ks", 128, 64), ("vs", 128, 0), ("kw", 128, 64), ("vw", 128, 0),
    ("k_m", 128, 64), ("v_m", 128, 0), ("g_n", 128, 0),
)
_Z_OFF = {}
_o = 0
for _n, _w, _h in _Z_GROUPS:
    _Z_OFF[_n] = _o
    _o += _w
Z_COLS = _o
IN_SPLITS = (512, 128, 128, 128, 128, 128, 128, 24, 512, 128, 128, 512, 3072)
_SRC_NAMES = ("q_n", "kc", "vc", "ks", "vs", "kw", "vw", "g_n", "q_m", "k_m", "v_m", "q_x", "g_merge")


def _pack_w_in(w_in):
    cuts = np.cumsum((0,) + IN_SPLITS)
    parts = {n: w_in[:, cuts[i]:cuts[i + 1]] for i, n in enumerate(_SRC_NAMES)}
    parts["g_n"] = jnp.pad(parts["g_n"], ((0, 0), (0, LANES - 24)))
    return jnp.concatenate([parts[n] for n, _, _ in _Z_GROUPS], axis=1).astype(BF16)


def _head_gain_row(gains):
    g = {
        "q_n": jnp.tile(gains["nsa_q"], 8), "ks": jnp.tile(gains["nsa_k1"], 2), "kw": jnp.tile(gains["nsa_k2"], 2),
        "q_m": jnp.tile(gains["moba_q"], 8), "k_m": jnp.tile(gains["moba_k"], 2), "q_x": jnp.tile(gains["mem_q"], 4),
    }
    return jnp.concatenate([g.get(n, jnp.ones((w,), F32)) for n, w, _ in _Z_GROUPS])[None, :]


def _inproj_kernel(x_ref, gain_ref, w_ref, hgain_ref, seg64_ref, seg128_ref, z_ref):
    x = x_ref[...]
    ms = jnp.mean(x * x, axis=-1, keepdims=True)
    xb = (x * lax.rsqrt(ms + RMS_EPS) * gain_ref[...]).astype(BF16)
    for name, width, hd in _Z_GROUPS:
        off = _Z_OFF[name]
        for c in range(0, width, 512):
            cw = min(512, width - c)
            lo_c, hi_c = off + c, off + c + cw
            z = jnp.dot(xb, w_ref[:, lo_c:hi_c], preferred_element_type=F32)
            if hd:
                seg = seg64_ref[...] if hd == 64 else seg128_ref[...]
                for j in range(0, cw, LANES):
                    zj = z[:, j:j + LANES]
                    ss = _group_sumsq(zj, seg)
                    z_ref[:, lo_c + j:lo_c + j + LANES] = (
                        zj * lax.rsqrt(ss * (1.0 / hd) + RMS_EPS) * hgain_ref[:, lo_c + j:lo_c + j + LANES])
            else:
                z_ref[:, lo_c:hi_c] = z


def _inproj(x, gain, w_packed, hgain, tm=256):
    n, d = x.shape
    assert n % tm == 0
    const = lambda i: (0, 0)
    return pl.pallas_call(
        _inproj_kernel,
        grid=(n // tm,),
        in_specs=[
            pl.BlockSpec((tm, d), lambda i: (i, 0)),
            pl.BlockSpec((1, d), const),
            pl.BlockSpec((d, Z_COLS), const),
            pl.BlockSpec((1, Z_COLS), const),
            pl.BlockSpec((LANES, LANES), const),
            pl.BlockSpec((LANES, LANES), const),
        ],
        out_specs=pl.BlockSpec((tm, Z_COLS), lambda i: (i, 0)),
        out_shape=jax.ShapeDtypeStruct((n, Z_COLS), F32),
        compiler_params=_params(("arbitrary",)),
        name="inproj",
    )(x, gain, w_packed, hgain, _seg_matrix(64), _seg_matrix(128))


def _compress_weights(pe, w1, b1, w2):
    w1r = w1.reshape(2, CMP_STRIDE, HEAD_DIM, -1)
    eye = jnp.eye(2, dtype=F32)
    wbig = jnp.einsum("rlde,gh->lgdrhe", w1r, eye).reshape(CMP_STRIDE * LANES, 2 * LANES)
    pos = jnp.einsum("rld,rlde->e", pe.reshape(2, CMP_STRIDE, HEAD_DIM), w1r, precision=HI)
    cvec = jnp.tile(b1 + pos, 2)[None, :]
    w2big = jnp.einsum("ed,gh->gehd", w2, eye).reshape(LANES, LANES)
    return _split_hi_lo(wbig) + (cvec,) + _split_hi_lo(w2big)


def _compress_kernel(x_ref, wh_ref, wl_ref, c_ref, w2h_ref, w2l_ref, seg_ref, gain_ref, o_ref, *, norm):
    ns = x_ref.shape[1]
    a = _dot3(x_ref[0], wh_ref[...], wl_ref[...])
    nxt = pltpu.roll(a[:, LANES:], ns - 1, axis=0)
    h = a[:, :LANES] + nxt + c_ref[...]
    o = _dot3(_gelu_tanh(h), w2h_ref[...], w2l_ref[...])
    if norm:
        o = o * lax.rsqrt(_group_sumsq(o, seg_ref[...]) * (1.0 / HEAD_DIM) + RMS_EPS) * gain_ref[...]
    o_ref[0] = o


def _compress(x_sub, weights, gain, norm):
    b, ns, kk = x_sub.shape
    wh, wl, cvec, w2h, w2l = weights
    const = lambda i: (0, 0)
    return pl.pallas_call(
        functools.partial(_compress_kernel, norm=norm),
        grid=(b,),
        in_specs=[
            pl.BlockSpec((1, ns, kk), lambda i: (i, 0, 0)),
            pl.BlockSpec((kk, 2 * LANES), const), pl.BlockSpec((kk, 2 * LANES), const),
            pl.BlockSpec((1, LANES), const),
            pl.BlockSpec((LANES, LANES), const), pl.BlockSpec((LANES, LANES), const),
            pl.BlockSpec((LANES, LANES), const), pl.BlockSpec((1, LANES), const),
        ],
        out_specs=pl.BlockSpec((1, ns, LANES), lambda i: (i, 0, 0)),
        out_shape=jax.ShapeDtypeStruct((b, ns, LANES), F32),
        compiler_params=_params(("arbitrary",)),
        name="nsa_compress",
    )(x_sub, wh, wl, cvec, w2h, w2l, _seg_matrix(HEAD_DIM), gain)


def _toeplitz_tiles(tbl, deltas, tq, tk, hi):
    d = (jnp.asarray(deltas, jnp.int32)[:, None, None] + jnp.arange(tq)[None, :, None] - jnp.arange(tk)[None, None, :])
    b = jnp.where(((d >= 0) & (d < hi))[..., None, None], _bias_lookup(tbl, d), NEG_INF)
    return jnp.transpose(b, (3, 0, 4, 1, 2)).reshape(tbl.shape[1], len(deltas), HPG * tq, tk)


def _with_masked_tile(tiles):
    return jnp.concatenate([jnp.full_like(tiles[:, :1], NEG_INF), tiles], axis=1)


def _stack_queries(zq_ref, write, tq):
    lane = lax.broadcasted_iota(jnp.int32, (tq, LANES), 1)
    for g in range(2):
        keep = (lane < HEAD_DIM) if g == 0 else (lane >= HEAD_DIM)
        for h in range(HPG):
            hd = HPG * g + h
            chunk = zq_ref[:, LANES * (hd // 2):LANES * (hd // 2 + 1)] * (HEAD_DIM ** -0.5)
            if hd % 2 != g:
                chunk = pltpu.roll(chunk, HEAD_DIM, axis=1)
            write(g, h, jnp.where(keep, chunk, 0.0))


def _stack_queries_prompt(zq_ref, q_scr, tq):
    def write(g, h, val):
        q_scr[g, h * tq:(h + 1) * tq, :] = val.astype(BF16)
    _stack_queries(zq_ref, write, tq)


def _stack_queries_sample(zq_ref, q_scr, ts):
    def write(g, h, val):
        q_scr[(g * HPG + h) * ts:(g * HPG + h + 1) * ts, :] = val
    _stack_queries(zq_ref, write, ts)


def _unstack_heads(parts, o_ref, tq):
    lane = lax.broadcasted_iota(jnp.int32, (tq, LANES), 1)
    for j in range(4):
        g = j // 2
        he = (2 * j) % HPG
        left = parts[g][he * tq:(he + 1) * tq, :]
        right = parts[g][(he + 1) * tq:(he + 2) * tq, :]
        if g == 0:
            right = pltpu.roll(right, HEAD_DIM, axis=1)
        else:
            left = pltpu.roll(left, HEAD_DIM, axis=1)
        o_ref[:, LANES * j:LANES * (j + 1)] = jnp.where(lane < HEAD_DIM, left, right)


KEY_TILES_PER_STEP = 4


def _flash(qs, k_ref, v_ref, bias_ref, tile_index, bms, blk_shift, rep, kb_lo, kb_hi, qt, m_scr, l_scr, acc_scr, tk):
    m_scr[...] = jnp.full(m_scr.shape, M_FLOOR, F32)
    l_scr[...] = jnp.zeros(l_scr.shape, F32)
    acc_scr[...] = jnp.zeros(acc_scr.shape, F32)
    nk = tk * KEY_TILES_PER_STEP
    blk_row = lax.broadcasted_iota(jnp.int32, (LANES, nk), 0)
    key_lane = lax.broadcasted_iota(jnp.int32, (LANES, nk), 1)

    def body(kb, carry):
        k0 = pl.multiple_of(kb * nk, nk)
        k = k_ref[pl.ds(k0, nk), :].astype(BF16)
        v = v_ref[pl.ds(k0, nk), :].astype(BF16)
        tiles = [tile_index(qt - (kb * KEY_TILES_PER_STEP + c)) for c in range(KEY_TILES_PER_STEP)]
        if bms is not None:
            expand = jnp.where(blk_row == ((k0 + key_lane) >> blk_shift), 1.0, 0.0).astype(BF16)
        for g, q in enumerate(qs):
            s = lax.dot_general(q, k, _NT, preferred_element_type=F32)
            s = s + jnp.concatenate([bias_ref[g, t] for t in tiles], axis=1)
            if bms is not None:
                vis = jnp.dot(bms[g], expand, preferred_element_type=F32)
                if rep > 1:
                    s = s + jnp.concatenate([jnp.where(vis > 0.5, 0.0, NEG_INF)] * rep, axis=0)
                else:
                    s = jnp.where(vis > 0.5, s, NEG_INF)
            m_old = m_scr[g]
            m_new = jnp.maximum(m_old, jnp.max(s, axis=-1, keepdims=True))
            p = jnp.exp(s - m_new)
            alpha = jnp.exp(m_old - m_new)
            l_scr[g] = alpha * l_scr[g] + jnp.sum(p, axis=-1, keepdims=True)
            acc_scr[g] = alpha * acc_scr[g] + jnp.dot(p.astype(BF16), v, preferred_element_type=F32)
            m_scr[g] = m_new
        return carry

    lax.fori_loop(kb_lo, kb_hi, body, 0)
    outs = []
    for g in range(len(qs)):
        l = l_scr[g]
        outs.append(acc_scr[g] / jnp.where(l > 0.0, l, 1.0))
    return outs


def _causal_tile_index(n_near):
    return lambda rel: jnp.clip(rel + 1, 0, n_near)


def _window_tile_index(n_win):
    return lambda rel: jnp.where(rel >= n_win, 0, jnp.clip(rel + 1, 0, n_win))


def _rank_select(score, count, lane, n_valid):
    width = score.shape[-1]
    rank = jnp.zeros(score.shape, F32)
    for s in range(1, width):
        if n_valid <= s <= width - n_valid:
            continue
        other = pltpu.roll(score, s, axis=1)
        rank = rank + jnp.where(other > score, 1.0, 0.0)
        if s < n_valid:
            rank = rank + jnp.where(other == score, jnp.where(lane >= s, 1.0, 0.0), 0.0)
    return jnp.where((rank < count) & (score > HALF_NEG), 1.0, 0.0)


def _take_topk(score, count, lane):
    sel = jnp.zeros(score.shape, F32)
    cur = score
    lane_f = lane.astype(F32)
    for _ in range(count):
        mx = jnp.max(cur, axis=-1, keepdims=True)
        idx = jnp.min(jnp.where(cur == mx, lane_f, float(score.shape[-1])), axis=-1, keepdims=True)
        hit = lane_f == idx
        sel = jnp.where(hit, jnp.where(mx > HALF_NEG, 1.0, 0.0), sel)
        cur = jnp.where(hit, KNOCKED, cur)
    return sel


def _nsa_prompt_kernel(zq_ref, gn_ref, kc_ref, vc_ref, ks_ref, vs_ref, kw_ref, vw_ref, biasc_ref, bslc_ref, bwin_ref,
                       mslc_ref, o_ref, q_scr, m_scr, l_scr, acc_scr, comb_scr, *, tq, tk, n_slc, n_slc_tiles, n_win_tiles):
    qt = pl.program_id(1)
    q0 = qt * tq
    _stack_queries_prompt(zq_ref, q_scr, tq)
    gn = jax.nn.sigmoid(gn_ref[...])
    lane = lax.broadcasted_iota(jnp.int32, (tq, LANES), 1)
    tpos = q0 + lax.broadcasted_iota(jnp.int32, (tq, LANES), 0)
    qb = tpos >> 6

    def gate_col(c, g):
        return jnp.concatenate([gn[:, c * 8 + g * HPG + h:c * 8 + g * HPG + h + 1] for h in range(HPG)], axis=0)

    kc = kc_ref[0].astype(BF16)
    vc = vc_ref[0].astype(BF16)
    qs = [q_scr[g] for g in range(NSA_KV)]
    bms = []
    for g, q in enumerate(qs):
        s_c = lax.dot_general(q, kc, _NT, preferred_element_type=F32) + biasc_ref[g]
        mx = jnp.maximum(jnp.max(s_c, axis=-1, keepdims=True), M_FLOOR)
        p = jnp.exp(s_c - mx)
        den = jnp.sum(p, axis=-1, keepdims=True)
        p_c = p / jnp.where(den > 0.0, den, 1.0)
        comb_scr[g] = gate_col(0, g) * jnp.dot(p_c.astype(BF16), vc, preferred_element_type=F32)
        imp = p_c[0:tq] + p_c[tq:2 * tq] + p_c[2 * tq:3 * tq] + p_c[3 * tq:4 * tq]
        imp_hi, imp_lo = _split_hi_lo(imp)
        p_slc = (jnp.dot(imp_hi, mslc_ref[...], preferred_element_type=F32)
                 + jnp.dot(imp_lo, mslc_ref[...], preferred_element_type=F32))
        forced = (lane == 0) | (lane == qb) | (lane == qb - 1)
        score = jnp.where(lane > qb, NEG_INF, p_slc + jnp.where(forced, FORCE_BONUS, 0.0))
        bms.append(lax.cond(q0 + tq <= SLC_TOPN * SLC_LEN,
                            lambda score=score: jnp.where(score > HALF_NEG, 1.0, 0.0),
                            lambda score=score: _rank_select(score, SLC_TOPN, lane, n_slc)).astype(BF16))
    kb_hi = qt // KEY_TILES_PER_STEP + 1
    o_s = _flash(qs, ks_ref, vs_ref, bslc_ref, _causal_tile_index(n_slc_tiles), bms, 6, HPG, 0, kb_hi, qt,
                 m_scr, l_scr, acc_scr, tk)
    kb_lo = jnp.maximum(qt - (n_win_tiles - 1), 0) // KEY_TILES_PER_STEP
    o_w = _flash(qs, kw_ref, vw_ref, bwin_ref, _window_tile_index(n_win_tiles), None, 0, 1, kb_lo, kb_hi, qt,
                 m_scr, l_scr, acc_scr, tk)
    for g in range(NSA_KV):
        comb_scr[g] = comb_scr[g] + gate_col(1, g) * o_s[g] + gate_col(2, g) * o_w[g]
    _unstack_heads([comb_scr[0], comb_scr[1]], o_ref, tq)


def _zspec(rows, name, width, row_map):
    cb = _Z_OFF[name] // width
    assert _Z_OFF[name] % width == 0
    return pl.BlockSpec((rows, width), lambda b, i: (row_map(b, i), cb))


def _nsa_prompt(z, kc, vc, rel_table, bp, tp):
    tq, tk = ATT_TQ, ATT_TK
    nq = tp // tq
    n_cmp = tp // CMP_STRIDE - 1
    n_slc = tp // SLC_LEN
    assert tp % tq == 0 and n_cmp < LANES and n_slc <= LANES and kc.shape[1] == LANES
    tbl = rel_table[:, :NSA_HEADS].reshape(REL_BUCKETS, NSA_KV, HPG)
    t = jnp.arange(tp)
    dist_c = t[:, None] - (jnp.arange(LANES) * CMP_STRIDE + CMP_LEN - 1)[None, :]
    valid_c = (dist_c >= 0) & (jnp.arange(LANES) < n_cmp)[None, :]
    biasc = jnp.where(valid_c[..., None, None], _bias_lookup(tbl, dist_c), NEG_INF)
    biasc = jnp.transpose(biasc.reshape(nq, tq, LANES, NSA_KV, HPG), (0, 3, 4, 1, 2)).reshape(nq, NSA_KV, HPG * tq, LANES)
    slc_deltas = (0, tq, 2 * tq)
    win_deltas = tuple(range(0, WINDOW + 1, tq))
    assert tp % (tk * KEY_TILES_PER_STEP) == 0 and n_slc <= LANES // 2
    bslc = _with_masked_tile(_toeplitz_tiles(tbl, slc_deltas, tq, tk, 1 << 30))
    bwin = _with_masked_tile(_toeplitz_tiles(tbl, win_deltas, tq, tk, WINDOW))
    mslc = jnp.pad(_slc_matrix(n_cmp, n_slc), ((0, LANES - n_cmp), (0, LANES - n_slc))).astype(BF16)
    kern = functools.partial(_nsa_prompt_kernel, tq=tq, tk=tk, n_slc=n_slc, n_slc_tiles=len(slc_deltas),
                             n_win_tiles=len(win_deltas))
    seq = lambda b, i: b
    tile = lambda b, i: b * nq + i
    full4 = lambda b, i: (0, 0, 0, 0)
    return pl.pallas_call(
        kern,
        grid=(bp, nq),
        in_specs=[
            _zspec(tq, "q_n", 512, tile), _zspec(tq, "g_n", LANES, tile),
            pl.BlockSpec((1, LANES, LANES), lambda b, i: (b, 0, 0)), pl.BlockSpec((1, LANES, LANES), lambda b, i: (b, 0, 0)),
        ] + [pl.BlockSpec((tp, LANES), (lambda b, i, cb=_Z_OFF[nm] // LANES: (b, cb))) for nm in ("ks", "vs", "kw", "vw")] + [
            pl.BlockSpec((None, NSA_KV, HPG * tq, LANES), lambda b, i: (i, 0, 0, 0)),
            pl.BlockSpec(bslc.shape, full4), pl.BlockSpec(bwin.shape, full4),
            pl.BlockSpec((LANES, LANES), lambda b, i: (0, 0)),
        ],
        out_specs=pl.BlockSpec((tq, 512), lambda b, i: (b * nq + i, 0)),
        out_shape=jax.ShapeDtypeStruct((bp * tp, 512), F32),
        scratch_shapes=[
            pltpu.VMEM((NSA_KV, HPG * tq, LANES), BF16),
            pltpu.VMEM((2, HPG * tq, 1), F32), pltpu.VMEM((2, HPG * tq, 1), F32), pltpu.VMEM((2, HPG * tq, LANES), F32),
            pltpu.VMEM((NSA_KV, HPG * tq, LANES), F32),
        ],
        compiler_params=_params(("arbitrary", "arbitrary")),
        name="nsa_prompt",
    )(z, z, kc, vc, z, z, z, z, biasc, bslc, bwin, mslc)


def _slc_matrix(n_cmp, n_slc):
    ratio, c_sub = SLC_LEN // CMP_STRIDE, CMP_LEN // CMP_STRIDE
    m = np.zeros((n_cmp, n_slc), np.float32)
    for j in range(n_slc):
        for o in range(-(c_sub - 1), ratio):
            k = ratio * j + o
            if 0 <= k < n_cmp:
                m[k, j] += float(min(o + c_sub, ratio) - max(o, 0))
    return jnp.asarray(m)


def _moba_prompt_kernel(zq_ref, k_ref, v_ref, bias_ref, o_ref, q_scr, m_scr, l_scr, acc_scr, *, tq, tk, nb, n_tiles):
    qt = pl.program_id(1)
    q0 = qt * tq
    _stack_queries_prompt(zq_ref, q_scr, tq)
    rows = HPG * tq
    lane = lax.broadcasted_iota(jnp.int32, (rows, LANES), 1)
    trow = q0 + (lax.broadcasted_iota(jnp.int32, (rows, LANES), 0) & (tq - 1))
    own = trow >> 8
    kmean = jnp.concatenate(
        [jnp.mean(k_ref[n * MOBA_BLOCK:(n + 1) * MOBA_BLOCK, :], axis=0, keepdims=True) for n in range(nb)]
        + [jnp.zeros((LANES - nb, LANES), F32)], axis=0)
    km_hi, km_lo = _split_hi_lo(kmean)
    qs = [q_scr[g] for g in range(MOBA_KV)]
    bms = []
    for q in qs:
        gs = (lax.dot_general(q, km_hi, _NT, preferred_element_type=F32)
              + lax.dot_general(q, km_lo, _NT, preferred_element_type=F32))
        gs = jnp.where(lane < own, gs, NEG_INF)
        sel = _take_topk(gs, min(MOBA_TOPK, nb - 1), lane)
        bms.append(jnp.where(lane == own, 1.0, sel).astype(BF16))
    res = _flash(qs, k_ref, v_ref, bias_ref, _causal_tile_index(n_tiles), bms, 8, 1, 0,
                 qt // KEY_TILES_PER_STEP + 1, qt, m_scr, l_scr, acc_scr, tk)
    _unstack_heads(res, o_ref, tq)


def _moba_prompt(z, rel_table, bp, tp):
    tq, tk = ATT_TQ, ATT_TK
    nq = tp // tq
    nb = tp // MOBA_BLOCK
    assert tp % MOBA_BLOCK == 0 and tq & (tq - 1) == 0
    tbl = rel_table[:, NSA_HEADS:].reshape(REL_BUCKETS, MOBA_KV, HPG)
    deltas = (0, tq, 2 * tq)
    assert tp % (tk * KEY_TILES_PER_STEP) == 0
    bias = _with_masked_tile(_toeplitz_tiles(tbl, deltas, tq, tk, 1 << 30))
    kern = functools.partial(_moba_prompt_kernel, tq=tq, tk=tk, nb=nb, n_tiles=len(deltas))
    return pl.pallas_call(
        kern,
        grid=(bp, nq),
        in_specs=[
            _zspec(tq, "q_m", 512, lambda b, i: b * nq + i),
            pl.BlockSpec((tp, LANES), lambda b, i: (b, _Z_OFF["k_m"] // LANES)),
            pl.BlockSpec((tp, LANES), lambda b, i: (b, _Z_OFF["v_m"] // LANES)),
            pl.BlockSpec(bias.shape, lambda b, i: (0, 0, 0, 0)),
        ],
        out_specs=pl.BlockSpec((tq, 512), lambda b, i: (b * nq + i, 0)),
        out_shape=jax.ShapeDtypeStruct((bp * tp, 512), F32),
        scratch_shapes=[
            pltpu.VMEM((MOBA_KV, HPG * tq, LANES), BF16),
            pltpu.VMEM((2, HPG * tq, 1), F32), pltpu.VMEM((2, HPG * tq, 1), F32), pltpu.VMEM((2, HPG * tq, LANES), F32),
        ],
        compiler_params=_params(("arbitrary", "arbitrary")),
        name="moba_prompt",
    )(z, z, z, bias)


def _merge_kernel(on_ref, om_ref, ox_ref, gm_ref, x_ref, wb_ref, wo_ref, fg_ref, x1_ref, xn_ref):
    d = x_ref.shape[1]
    merged = jnp.zeros(x_ref.shape, F32)
    for c, o_ref in enumerate((on_ref, om_ref, ox_ref)):
        proj = jnp.dot(o_ref[...].astype(BF16), wb_ref[c], preferred_element_type=F32)
        merged = merged + jax.nn.sigmoid(gm_ref[:, c * d:(c + 1) * d]) * proj
    x1 = x_ref[...] + jnp.dot(merged.astype(BF16), wo_ref[...], preferred_element_type=F32)
    x1_ref[...] = x1
    ms = jnp.mean(x1 * x1, axis=-1, keepdims=True)
    xn_ref[...] = (x1 * lax.rsqrt(ms + RMS_EPS) * fg_ref[...]).astype(BF16)


def _merge(o_nsa, o_moba, o_mem, z, x, w_branch, w_out, ffn_gain, tm=256):
    n, d = x.shape
    assert n % tm == 0 and _Z_OFF["g_merge"] == 0
    row = lambda i: (i, 0)
    return pl.pallas_call(
        _merge_kernel,
        grid=(n // tm,),
        in_specs=[
            pl.BlockSpec((tm, 512), row), pl.BlockSpec((tm, 512), row), pl.BlockSpec((tm, 512), row),
            pl.BlockSpec((tm, 3 * d), row), pl.BlockSpec((tm, d), row),
            pl.BlockSpec((3, 512, d), lambda i: (0, 0, 0)), pl.BlockSpec((d, d), lambda i: (0, 0)),
            pl.BlockSpec((1, d), lambda i: (0, 0)),
        ],
        out_specs=[pl.BlockSpec((tm, d), row), pl.BlockSpec((tm, d), row)],
        out_shape=[jax.ShapeDtypeStruct((n, d), F32), jax.ShapeDtypeStruct((n, d), BF16)],
        compiler_params=_params(("arbitrary",)),
        name="merge_outproj",
    )(o_nsa, o_moba, o_mem, z, x, w_branch.astype(BF16), w_out.astype(BF16), ffn_gain[None, :])


def _peer_route_kernel(xT_ref, wqh_ref, wql_ref, skh_ref, skl_ref, s1_ref, s2_ref, tau_ref):
    xT = xT_ref[...]
    qT = jnp.dot(wqh_ref[...], xT, preferred_element_type=F32) + jnp.dot(wql_ref[...], xT, preferred_element_type=F32)
    half = HEAD_DIM

    def top_sorted(s):
        vals, cur = [], s
        for _ in range(PEER_TOPK + 1):
            mx = jnp.max(cur, axis=0, keepdims=True)
            vals.append(mx)
            cur = jnp.where(cur == mx, KNOCKED, cur)
        return vals

    taus = []
    for h in range(PEER_HEADS):
        sc = []
        for p in range(2):
            r0 = (h * 2 + p) * half
            q_hi, q_lo = _split_hi_lo(qT[r0:r0 + half, :])
            sc.append(jnp.dot(skh_ref[h, p], q_hi, preferred_element_type=F32)
                      + jnp.dot(skl_ref[h, p], q_hi, preferred_element_type=F32)
                      + jnp.dot(skh_ref[h, p], q_lo, preferred_element_type=F32))
        a, b = top_sorted(sc[0]), top_sorted(sc[1])
        n_rank = PEER_TOPK + 1
        cand = jnp.concatenate([a[i] + b[j] for i in range(n_rank) for j in range(n_rank // (i + 1))], axis=0)
        cur = cand
        for _ in range(PEER_TOPK):
            tau = jnp.max(cur, axis=0, keepdims=True)
            cur = jnp.where(cur == tau, KNOCKED, cur)
        nxt = jnp.max(cur, axis=0, keepdims=True)
        top = a[0] + b[0]
        zsum = jnp.sum(jnp.where(cand >= tau, jnp.exp(cand - top), 0.0), axis=0, keepdims=True)
        shift = top + jnp.log(zsum)
        s1_ref[h] = (sc[0] - shift) * LOG2E
        s2_ref[h] = sc[1] * LOG2E
        taus.append((0.5 * (tau + nxt) - shift) * LOG2E)
    tau_ref[...] = jnp.concatenate(taus, axis=0)


def _peer_route(xT, peer_w_q, sub_keys, tt=256):
    d, n = xT.shape
    assert n % tt == 0
    wqh, wql = _split_hi_lo(peer_w_q.T)
    skh, skl = _split_hi_lo(sub_keys)
    tok3 = lambda i: (0, 0, i)
    big = jax.ShapeDtypeStruct((PEER_HEADS, PEER_NKEYS, n), F32)
    return pl.pallas_call(
        _peer_route_kernel,
        grid=(n // tt,),
        in_specs=[
            pl.BlockSpec((d, tt), lambda i: (0, i)),
            pl.BlockSpec((d, d), lambda i: (0, 0)), pl.BlockSpec((d, d), lambda i: (0, 0)),
            pl.BlockSpec(sub_keys.shape, lambda i: (0, 0, 0, 0)), pl.BlockSpec(sub_keys.shape, lambda i: (0, 0, 0, 0)),
        ],
        out_specs=[pl.BlockSpec((PEER_HEADS, PEER_NKEYS, tt), tok3)] * 2 + [pl.BlockSpec((PEER_HEADS, tt), lambda i: (0, i))],
        out_shape=[big] * 2 + [jax.ShapeDtypeStruct((PEER_HEADS, n), F32)],
        compiler_params=_params(("arbitrary",)),
        name="peer_route",
    )(xT, wqh, wql, skh, skl)


def _peer_dense_kernel(xT_ref, u_ref, vT_ref, s1_ref, s2_ref, tau_ref, o_ref, a_ref, *, ec, tt):
    c = pl.program_id(1)

    @pl.when(c == 0)
    def _():
        o_ref[...] = jnp.zeros_like(o_ref)

    h_t = jnp.dot(u_ref[...], xT_ref[...], preferred_element_type=F32)
    for k in range(ec // LANES):
        for lc in range(0, tt, LANES):
            wk = jnp.zeros((LANES, LANES), F32)
            for h in range(PEER_HEADS):
                csum = s2_ref[h, :, lc:lc + LANES] + s1_ref[h, k:k + 1, lc:lc + LANES]
                wk = wk + jnp.where(csum >= tau_ref[h:h + 1, lc:lc + LANES], jnp.exp2(csum), 0.0)
            g = _gelu_tanh(h_t[k * LANES:(k + 1) * LANES, lc:lc + LANES])
            a_ref[k * LANES:(k + 1) * LANES, lc:lc + LANES] = (wk * g).astype(BF16)
    o_ref[...] += jnp.dot(vT_ref[...], a_ref[...], preferred_element_type=F32)


def _peer_dense(xT, u_b, vT_b, s1T, s2T, tauT, ec=1024):
    d, n = xT.shape
    tt = next(t for t in (1024, 512, 256) if n % t == 0)
    assert n % tt == 0 and ec == 8 * PEER_NKEYS
    ne = u_b.shape[0]
    kern = functools.partial(_peer_dense_kernel, ec=ec, tt=tt)
    allk = pl.BlockSpec((PEER_HEADS, PEER_NKEYS, tt), lambda i, c: (0, 0, i))
    chunk = pl.BlockSpec((PEER_HEADS, ec // PEER_NKEYS, tt), lambda i, c: (0, c, i))
    return pl.pallas_call(
        kern,
        grid=(n // tt, ne // ec),
        in_specs=[
            pl.BlockSpec((d, tt), lambda i, c: (0, i)),
            pl.BlockSpec((ec, d), lambda i, c: (c, 0)),
            pl.BlockSpec((d, ec), lambda i, c: (0, c)),
            chunk, allk,
            pl.BlockSpec((PEER_HEADS, tt), lambda i, c: (0, i)),
        ],
        out_specs=pl.BlockSpec((d, tt), lambda i, c: (0, i)),
        out_shape=jax.ShapeDtypeStruct((d, n), F32),
        scratch_shapes=[pltpu.VMEM((ec, tt), BF16)],
        compiler_params=_params(("arbitrary", "arbitrary")),
        name="peer_dense",
    )(xT, u_b, vT_b, s1T, s2T, tauT)


def _peer(x1, xn_b, peer_w_q, sub_keys, u_b, vT_b):
    xT = xn_b.T
    s1T, s2T, tauT = _peer_route(xT, peer_w_q, sub_keys)
    out_t = _peer_dense(xT, u_b, vT_b, s1T, s2T, tauT)
    return x1 + out_t.T


PAGES_PER_STEP = 64


def _page_specs(n_lead, inner):
    zeros = (0,) * len(inner)
    return [pl.BlockSpec((1,) + inner, (lambda b, j, pt, p=p: (pt[b, j * PAGES_PER_STEP + p],) + zeros))
            for p in range(n_lead)]


def _compress_paged_kernel(pt_ref, *refs, norm):
    del pt_ref
    p_n = PAGES_PER_STEP
    x_refs = refs[:p_n]
    perm_ref, wh_ref, c_ref, w2h_ref, w2l_ref, seg_ref, gain_ref, o_ref, x_scr = refs[p_n:]
    j = pl.program_id(1)
    pair_sub = 2 * x_refs[0].shape[2] // CMP_STRIDE
    for pp in range(p_n // 2):
        x_t = jnp.concatenate([x_refs[2 * pp][0], x_refs[2 * pp + 1][0]], axis=1).astype(BF16)
        rows = lax.dot_general(perm_ref[...], x_t, _NT, preferred_element_type=F32).astype(BF16)
        base = pl.multiple_of((j * (p_n // 2) + pp) * pair_sub, pair_sub)
        for l in range(CMP_STRIDE):
            x_scr[pl.ds(base, pair_sub), l * LANES:(l + 1) * LANES] = rows[l * pair_sub:(l + 1) * pair_sub, :]

    @pl.when(j == pl.num_programs(1) - 1)
    def _():
        ns = x_scr.shape[0]
        a = jnp.dot(x_scr[...], wh_ref[...], preferred_element_type=F32)
        nxt = pltpu.roll(a[:, LANES:], ns - 1, axis=0)
        h = a[:, :LANES] + nxt + c_ref[...]
        o = _dot3(_gelu_tanh(h), w2h_ref[...], w2l_ref[...])
        if norm:
            o = o * lax.rsqrt(_group_sumsq(o, seg_ref[...]) * (1.0 / HEAD_DIM) + RMS_EPS) * gain_ref[...]
        o_ref[0] = o


def _compress_paged(pool, page_table, weights, gain, norm):
    n_phys, _, page = pool.shape
    b, n_pages = page_table.shape
    sub = page // CMP_STRIDE
    kk = CMP_STRIDE * LANES
    ns = n_pages * sub
    assert n_pages % PAGES_PER_STEP == 0 and page % CMP_STRIDE == 0 and PAGES_PER_STEP % 2 == 0 and (2 * sub) % 16 == 0
    wh, _, cvec, w2h, w2l = weights
    const = lambda b_, j, pt: (0, 0)
    r = np.arange(2 * page)
    perm = jnp.asarray(((r[:, None] // (2 * sub)) + CMP_STRIDE * (r[:, None] % (2 * sub))) == r[None, :], BF16)
    grid_spec = pltpu.PrefetchScalarGridSpec(
        num_scalar_prefetch=1,
        grid=(b, n_pages // PAGES_PER_STEP),
        in_specs=_page_specs(PAGES_PER_STEP, (LANES, page)) + [
            pl.BlockSpec((2 * page, 2 * page), const), pl.BlockSpec((kk, 2 * LANES), const),
            pl.BlockSpec((1, LANES), const),
            pl.BlockSpec((LANES, LANES), const), pl.BlockSpec((LANES, LANES), const),
            pl.BlockSpec((LANES, LANES), const), pl.BlockSpec((1, LANES), const),
        ],
        out_specs=pl.BlockSpec((1, ns, LANES), lambda b_, j, pt: (b_, 0, 0)),
        scratch_shapes=[pltpu.VMEM((ns, kk), BF16)],
    )
    return pl.pallas_call(
        functools.partial(_compress_paged_kernel, norm=norm),
        grid_spec=grid_spec,
        out_shape=jax.ShapeDtypeStruct((b, ns, LANES), F32),
        compiler_params=_params(("arbitrary", "arbitrary")),
        name="nsa_compress_paged",
    )(page_table, *([pool] * PAGES_PER_STEP), perm, wh, cvec, w2h, w2l, _seg_matrix(HEAD_DIM), gain)


def _softmax_tile(s):
    mx = jnp.maximum(jnp.max(s, axis=-1, keepdims=True), M_FLOOR)
    p = jnp.exp(s - mx)
    return mx, p, jnp.sum(p, axis=-1, keepdims=True)


def _pv(p, v, v_t):
    if v_t:
        return lax.dot_general(p.astype(BF16), v, _NT, preferred_element_type=F32)
    return jnp.dot(p.astype(BF16), v, preferred_element_type=F32)


def _online_update(s, v, m_scr, l_scr, acc_scr, v_t=False):
    m_old = m_scr[...]
    m_new = jnp.maximum(m_old, jnp.max(s, axis=-1, keepdims=True))
    p = jnp.exp(s - m_new)
    alpha = jnp.exp(m_old - m_new)
    l_scr[...] = alpha * l_scr[...] + jnp.sum(p, axis=-1, keepdims=True)
    acc_scr[...] = alpha * acc_scr[...] + _pv(p, v, v_t)
    m_scr[...] = m_new


def _pad_rows(x, rows):
    return jnp.concatenate([x, jnp.zeros((rows - x.shape[0], x.shape[1]), x.dtype)], axis=0)


def _sample_row_bias(tbl, ts, kpos, qpos0, lo_ok):
    d = (qpos0 + jnp.arange(ts))[:, None] - kpos[None, :]
    b = jnp.where(((d >= 0) & lo_ok)[..., None, None], _bias_lookup(tbl, d), NEG_INF)
    return jnp.transpose(b, (2, 3, 0, 1)).reshape(-1, kpos.shape[0])


def _past_bias_tiles(tbl, ts, past_len, nk):
    assert nk >= LANES
    far = jnp.repeat(tbl[REL_BUCKETS - 1].reshape(-1), ts)[:, None]
    far_tile = jnp.broadcast_to(far, (far.shape[0], nk))
    near = _sample_row_bias(tbl, ts, past_len - LANES + jnp.arange(LANES), past_len, jnp.ones((ts, LANES), bool))
    return jnp.stack([far_tile, jnp.concatenate([far_tile[:, :nk - LANES], near], axis=1)])


def _block_expand_matrix(n_blk_lanes, nk, n_steps):
    blk_step = nk // SLC_LEN
    assert nk % SLC_LEN == 0 and blk_step % 16 == 0
    r = np.arange(n_blk_lanes + (n_steps - 1) * blk_step)[:, None]
    key_blk = (np.arange(nk) // SLC_LEN)[None, :]
    return jnp.asarray(r == (n_steps - 1) * blk_step + key_blk, BF16)


def _nsa_sample_kernel(pt_ref, *refs, ts, past_len, n_slc, slc_lanes):
    del pt_ref
    p_n = PAGES_PER_STEP
    k_pages, v_pages = refs[:p_n], refs[p_n:2 * p_n]
    (zq_ref, gn_ref, kc_ref, vc_ref, ksn_ref, vsn_ref, wk_ref, wv_ref, kwn_ref, vwn_ref,
     biasc_ref, bpast_ref, bnew_ref, bwin_ref, mslc_ref, expand_ref, o_ref,
     q_scr, m_scr, l_scr, acc_scr, comb_scr, bm_scr) = refs[2 * p_n:]
    j = pl.program_id(1)
    page = k_pages[0].shape[2]
    rows = NSA_KV * HPG * ts
    half = HPG * ts

    def gate_col(gn, c):
        return jnp.concatenate([gn[:, c * 8 + hd:c * 8 + hd + 1] for hd in range(NSA_HEADS)], axis=0)

    @pl.when(j == 0)
    def _():
        _stack_queries_sample(zq_ref, q_scr, ts)
        q = q_scr[...].astype(BF16)
        gn = jax.nn.sigmoid(gn_ref[...])
        s_c = lax.dot_general(q, kc_ref[0].astype(BF16), _NT, preferred_element_type=F32) + biasc_ref[...]
        mx, p, den = _softmax_tile(s_c)
        p_c = p / jnp.where(den > 0.0, den, 1.0)
        comb_scr[...] = gate_col(gn, 0) * jnp.dot(p_c.astype(BF16), vc_ref[0].astype(BF16), preferred_element_type=F32)
        imp = jnp.concatenate(
            [sum(p_c[g * half + h * ts:g * half + (h + 1) * ts] for h in range(HPG)) for g in range(NSA_KV)], axis=0)
        imp_hi, imp_lo = _split_hi_lo(imp)
        p_slc = (jnp.dot(imp_hi, mslc_ref[...], preferred_element_type=F32)
                 + jnp.dot(imp_lo, mslc_ref[...], preferred_element_type=F32))
        lane = lax.broadcasted_iota(jnp.int32, p_slc.shape, 1)
        tpos = past_len + (lax.broadcasted_iota(jnp.int32, p_slc.shape, 0) & (ts - 1))
        qb = tpos >> 6
        forced = (lane == 0) | (lane == qb) | (lane == qb - 1)
        score = jnp.where(lane > qb, NEG_INF, p_slc + jnp.where(forced, FORCE_BONUS, 0.0))
        sel = _rank_select(score, SLC_TOPN, lane, n_slc)
        bm_scr[...] = jnp.concatenate(
            [sel[g * ts:(g + 1) * ts] for g in range(NSA_KV) for _ in range(HPG)], axis=0).astype(BF16)
        m_scr[...] = jnp.full(m_scr.shape, M_FLOOR, F32)
        l_scr[...] = jnp.zeros(l_scr.shape, F32)
        acc_scr[...] = jnp.zeros(acc_scr.shape, F32)
        s_w = jnp.dot(q, wk_ref[0].astype(BF16), preferred_element_type=F32) + bwin_ref[...]
        _online_update(s_w, wv_ref[0].astype(BF16), m_scr, l_scr, acc_scr, v_t=True)
        s_n = lax.dot_general(q, _pad_rows(kwn_ref[...], LANES).astype(BF16), _NT, preferred_element_type=F32) + bnew_ref[...]
        _online_update(s_n, _pad_rows(vwn_ref[...], LANES).astype(BF16), m_scr, l_scr, acc_scr)
        l = l_scr[...]
        comb_scr[...] = comb_scr[...] + gate_col(gn, 2) * (acc_scr[...] / jnp.where(l > 0.0, l, 1.0))
        m_scr[...] = jnp.full(m_scr.shape, M_FLOOR, F32)
        l_scr[...] = jnp.zeros(l_scr.shape, F32)
        acc_scr[...] = jnp.zeros(acc_scr.shape, F32)

    q = q_scr[...].astype(BF16)
    nk = p_n * page
    k_t = jnp.concatenate([r[0].astype(BF16) for r in k_pages], axis=1)
    v_t = jnp.concatenate([r[0].astype(BF16) for r in v_pages], axis=1)
    last = pl.num_programs(1) - 1
    blk_step = nk // SLC_LEN
    expand = expand_ref[pl.ds(pl.multiple_of((last - j) * blk_step, blk_step), slc_lanes), :]
    vis = jnp.dot(bm_scr[...], expand, preferred_element_type=F32)
    s = (jnp.dot(q, k_t, preferred_element_type=F32) + bpast_ref[jnp.where(j == last, 1, 0)]
         + jnp.where(vis > 0.5, 0.0, NEG_INF))
    _online_update(s, v_t, m_scr, l_scr, acc_scr, v_t=True)

    @pl.when(j == pl.num_programs(1) - 1)
    def _():
        gn = jax.nn.sigmoid(gn_ref[...])
        s_n = lax.dot_general(q, _pad_rows(ksn_ref[...], LANES).astype(BF16), _NT, preferred_element_type=F32) + bnew_ref[...]
        _online_update(s_n, _pad_rows(vsn_ref[...], LANES).astype(BF16), m_scr, l_scr, acc_scr)
        l = l_scr[...]
        comb = comb_scr[...] + gate_col(gn, 1) * (acc_scr[...] / jnp.where(l > 0.0, l, 1.0))
        _unstack_heads([comb[:half], comb[half:]], o_ref, ts)


def _nsa_sample(z, n_p, kc, vc, pool_k, pool_v, win_k, win_v, page_table, rel_table, ts):
    bs, n_pages = page_table.shape
    n_phys, _, page = pool_k.shape
    past_len = n_pages * page
    p_n = PAGES_PER_STEP
    n_steps = n_pages // p_n
    nk = p_n * page
    n_sub = kc.shape[1]
    n_cmp = n_sub - 1
    n_slc = -(-(past_len + ts) // SLC_LEN)
    slc_lanes = -(-n_slc // LANES) * LANES
    win_len = win_k.shape[2]
    assert n_pages % p_n == 0 and ts & (ts - 1) == 0 and ts <= SLC_LEN and past_len % SLC_LEN == 0 and n_p % ts == 0
    assert n_sub % LANES == 0 and win_len + ts >= WINDOW
    tbl = rel_table[:, :NSA_HEADS].reshape(REL_BUCKETS, NSA_KV, HPG)
    rows = NSA_HEADS * ts
    cmp_end = jnp.arange(n_sub) * CMP_STRIDE + CMP_LEN - 1
    biasc = _sample_row_bias(tbl, ts, cmp_end, past_len, jnp.broadcast_to(jnp.arange(n_sub) < n_cmp, (ts, n_sub)))
    bpast = _past_bias_tiles(tbl, ts, past_len, nk)
    bnew = _sample_row_bias(tbl, ts, past_len + jnp.arange(LANES), past_len, jnp.broadcast_to(jnp.arange(LANES) < ts, (ts, LANES)))
    wpos = past_len - win_len + jnp.arange(win_len)
    in_win = ((past_len + jnp.arange(ts))[:, None] - wpos[None, :]) < WINDOW
    bwin = _sample_row_bias(tbl, ts, wpos, past_len, in_win)
    mslc = jnp.pad(_slc_matrix(n_cmp, n_slc), ((0, n_sub - n_cmp), (0, slc_lanes - n_slc))).astype(BF16)
    expand = _block_expand_matrix(slc_lanes, nk, n_steps)
    row_blk = n_p // ts
    zrow = lambda name, width: pl.BlockSpec((ts, width), (lambda b, j, pt, cb=_Z_OFF[name] // width: (row_blk + b, cb)))
    seq3 = lambda shape: pl.BlockSpec((1,) + shape, lambda b, j, pt: (b, 0, 0))
    const2 = lambda shape: pl.BlockSpec(shape, lambda b, j, pt: (0, 0))
    grid_spec = pltpu.PrefetchScalarGridSpec(
        num_scalar_prefetch=1,
        grid=(bs, n_steps),
        in_specs=_page_specs(p_n, (LANES, page)) + _page_specs(p_n, (LANES, page)) + [
            zrow("q_n", 512), zrow("g_n", LANES),
            seq3((n_sub, LANES)), seq3((n_sub, LANES)),
            zrow("ks", LANES), zrow("vs", LANES),
            seq3((LANES, win_len)), seq3((LANES, win_len)),
            zrow("kw", LANES), zrow("vw", LANES),
            const2((rows, n_sub)),
            pl.BlockSpec((2, rows, nk), lambda b, j, pt: (0, 0, 0)),
            const2((rows, LANES)), const2((rows, win_len)), const2((n_sub, slc_lanes)), const2(expand.shape),
        ],
        out_specs=pl.BlockSpec((ts, 512), lambda b, j, pt: (b, 0)),
        scratch_shapes=[
            pltpu.VMEM((rows, LANES), F32),
            pltpu.VMEM((rows,1), F32), pltpu.VMEM((rows, 1), F32), pltpu.VMEM((rows, LANES), F32),
            pltpu.VMEM((rows, LANES), F32), pltpu.VMEM((rows, slc_lanes), BF16),
        ],
    )
    return pl.pallas_call(
        functools.partial(_nsa_sample_kernel, ts=ts, past_len=past_len, n_slc=n_slc, slc_lanes=slc_lanes),
        grid_spec=grid_spec,
        out_shape=jax.ShapeDtypeStruct((bs * ts, 512), F32),
        compiler_params=_params(("arbitrary", "arbitrary")),
        name="nsa_sample",
    )(page_table, *([pool_k] * p_n), *([pool_v] * p_n), z, z, kc, vc, z, z, win_k, win_v, z, z,
      biasc, bpast, bnew, bwin, mslc, expand)


def _moba_sample_kernel(pt_ref, *refs, ts, n_past_blocks):
    del pt_ref
    p_n = PAGES_PER_STEP
    k_pages, v_pages = refs[:p_n], refs[p_n:2 * p_n]
    (zq_ref, kn_ref, vn_ref, bpast_ref, bnew_ref, o_ref, q_scr, m_all, l_all, acc_all, km_scr) = refs[2 * p_n:]
    j = pl.program_id(1)
    page = k_pages[0].shape[2]
    ppb = MOBA_BLOCK // page
    bps = p_n // ppb
    rows = MOBA_KV * HPG * ts
    half = HPG * ts
    lane = lax.broadcasted_iota(jnp.int32, (rows, LANES), 1)
    kcol = lax.broadcasted_iota(jnp.int32, (LANES, LANES), 1)
    last = pl.num_programs(1) - 1

    @pl.when(j == 0)
    def _():
        _stack_queries_sample(zq_ref, q_scr, ts)
        m_all[...] = jnp.full(m_all.shape, NEG_INF, F32)
        l_all[...] = jnp.zeros(l_all.shape, F32)
        km_scr[...] = jnp.zeros(km_scr.shape, F32)

    q = q_scr[...].astype(BF16)
    for i in range(bps):
        n = j * bps + i
        k_t = jnp.concatenate([k_pages[i * ppb + r][0] for r in range(ppb)], axis=1)
        v_t = jnp.concatenate([v_pages[i * ppb + r][0].astype(BF16) for r in range(ppb)], axis=1)
        s = (jnp.dot(q, k_t.astype(BF16), preferred_element_type=F32)
             + bpast_ref[jnp.where(j == last, 1, 0), :, i * MOBA_BLOCK:(i + 1) * MOBA_BLOCK])
        mx, p, den = _softmax_tile(s)
        m_all[...] = jnp.where(lane == n, mx, m_all[...])
        l_all[...] = jnp.where(lane == n, den, l_all[...])
        acc_all[n] = _pv(p, v_t, True)
        km_scr[...] = jnp.where(kcol == n, jnp.mean(k_t, axis=1, keepdims=True), km_scr[...])

    @pl.when(j == pl.num_programs(1) - 1)
    def _():
        s_o = lax.dot_general(q, _pad_rows(kn_ref[...], LANES).astype(BF16), _NT, preferred_element_type=F32) + bnew_ref[...]
        m_o, p_o, l_o = _softmax_tile(s_o)
        acc_o = jnp.dot(p_o.astype(BF16), _pad_rows(vn_ref[...], LANES).astype(BF16), preferred_element_type=F32)
        km_hi, km_lo = _split_hi_lo(km_scr[...])
        gs = jnp.dot(q, km_hi, preferred_element_type=F32) + jnp.dot(q, km_lo, preferred_element_type=F32)
        gs = jnp.where(lane < n_past_blocks, gs, NEG_INF)
        sel = _take_topk(gs, min(MOBA_TOPK, n_past_blocks), lane)
        m_sel = jnp.where(sel > 0.5, m_all[...], NEG_INF)
        m_fin = jnp.maximum(jnp.max(m_sel, axis=-1, keepdims=True), m_o)
        wgt = jnp.where(sel > 0.5, jnp.exp(m_sel - m_fin), 0.0)
        w_o = jnp.exp(m_o - m_fin)
        den = jnp.sum(wgt * l_all[...], axis=-1, keepdims=True) + w_o * l_o
        num = w_o * acc_o
        for n in range(n_past_blocks):
            num = num + wgt[:, n:n + 1] * acc_all[n]
        res = num / den
        _unstack_heads([res[:half], res[half:]], o_ref, ts)


def _moba_sample(z, n_p, pool_k, pool_v, page_table, rel_table, ts):
    bs, n_pages = page_table.shape
    n_phys, _, page = pool_k.shape
    past_len = n_pages * page
    p_n = PAGES_PER_STEP
    n_steps = n_pages // p_n
    nk = p_n * page
    nb_past = past_len // MOBA_BLOCK
    assert (MOBA_BLOCK % page == 0 and p_n % (MOBA_BLOCK // page) == 0 and past_len % MOBA_BLOCK == 0
            and ts <= MOBA_BLOCK and nb_past <= LANES and n_pages % p_n == 0 and n_p % ts == 0)
    tbl = rel_table[:, NSA_HEADS:].reshape(REL_BUCKETS, MOBA_KV, HPG)
    rows = MOBA_HEADS * ts
    bpast = _past_bias_tiles(tbl, ts, past_len, nk)
    bnew = _sample_row_bias(tbl, ts, past_len + jnp.arange(LANES), past_len, jnp.broadcast_to(jnp.arange(LANES) < ts, (ts, LANES)))
    row_blk = n_p // ts
    zrow = lambda name, width: pl.BlockSpec((ts, width), (lambda b, j, pt, cb=_Z_OFF[name] // width: (row_blk + b, cb)))
    grid_spec = pltpu.PrefetchScalarGridSpec(
        num_scalar_prefetch=1,
        grid=(bs, n_steps),
        in_specs=_page_specs(p_n, (LANES, page)) + _page_specs(p_n, (LANES, page)) + [
            zrow("q_m", 512), zrow("k_m", LANES), zrow("v_m", LANES),
            pl.BlockSpec((2, rows, nk), lambda b, j, pt: (0, 0, 0)),
            pl.BlockSpec((rows, LANES), lambda b, j, pt: (0, 0)),
        ],
        out_specs=pl.BlockSpec((ts, 512), lambda b, j, pt: (b, 0)),
        scratch_shapes=[
            pltpu.VMEM((rows, LANES), F32),
            pltpu.VMEM((rows,LANES), F32), pltpu.VMEM((rows, LANES), F32),
            pltpu.VMEM((nb_past, rows, LANES), F32), pltpu.VMEM((LANES, LANES), F32),
        ],
    )
    return pl.pallas_call(
        functools.partial(_moba_sample_kernel, ts=ts, n_past_blocks=nb_past),
        grid_spec=grid_spec,
        out_shape=jax.ShapeDtypeStruct((bs * ts, 512), F32),
        compiler_params=_params(("arbitrary", "arbitrary")),
        name="moba_sample",
    )(page_table, *([pool_k] * p_n), *([pool_v] * p_n), z, z, z, bpast, bnew)


def _memkv_kernel(m_ref, gain_ref, w_ref, kgain_ref, seg_ref, k_ref, v_ref):
    x = m_ref[...]
    xb = (x * lax.rsqrt(jnp.mean(x * x, axis=-1, keepdims=True) + RMS_EPS) * gain_ref[...]).astype(BF16)
    kvw = k_ref.shape[1]
    k = jnp.dot(xb, w_ref[:, :kvw], preferred_element_type=F32)
    for h in range(MEM_HEADS):
        kh = k[:, h * MEM_HEAD_DIM:(h + 1) * MEM_HEAD_DIM]
        ss = _group_sumsq(kh, seg_ref[...])
        k_ref[:, h * MEM_HEAD_DIM:(h + 1) * MEM_HEAD_DIM] = kh * lax.rsqrt(ss * (1.0 / MEM_HEAD_DIM) + RMS_EPS) * kgain_ref[...]
    v_ref[...] = jnp.dot(xb, w_ref[:, kvw:], preferred_element_type=F32)


def _memory_kv_pallas(mem, norm_gain, w_mem_kv, k_gain, tm=256):
    n, d = mem.shape
    kvw = w_mem_kv.shape[1] // 2
    assert n % tm == 0
    row = lambda i: (i, 0)
    const = lambda i: (0, 0)
    return pl.pallas_call(
        _memkv_kernel,
        grid=(n // tm,),
        in_specs=[pl.BlockSpec((tm, d), row), pl.BlockSpec((1, d), const), pl.BlockSpec((d, 2 * kvw), const),
                  pl.BlockSpec((1, MEM_HEAD_DIM), const), pl.BlockSpec((LANES, LANES), const)],
        out_specs=[pl.BlockSpec((tm, kvw), row), pl.BlockSpec((tm, kvw), row)],
        out_shape=[jax.ShapeDtypeStruct((n, kvw), F32)] * 2,
        compiler_params=_params(("arbitrary",)),
        name="memory_kv",
    )(mem, norm_gain[None, :], w_mem_kv.astype(BF16), k_gain[None, :], _seg_matrix(MEM_HEAD_DIM))


def _mem_attend_kernel(zq_ref, mk_ref, mv_ref, o_ref):
    for h in range(MEM_HEADS):
        sl = slice(h * MEM_HEAD_DIM, (h + 1) * MEM_HEAD_DIM)
        q = (zq_ref[:, sl] * (MEM_HEAD_DIM ** -0.5)).astype(BF16)
        s = lax.dot_general(q, mk_ref[0, :, sl].astype(BF16), _NT, preferred_element_type=F32)
        p = jnp.exp(s - jnp.max(s, axis=-1, keepdims=True))
        o = jnp.dot(p.astype(BF16), mv_ref[0, :, sl].astype(BF16), preferred_element_type=F32)
        o_ref[:, sl] = o / jnp.sum(p, axis=-1, keepdims=True)


def _mem_attend_pallas(z, row0, n_seq, t_seq, tq, mk, mv):
    nq = t_seq // tq
    assert t_seq % tq == 0 and row0 % tq == 0
    m_len = mk.shape[1]
    width = MEM_HEADS * MEM_HEAD_DIM
    cb = _Z_OFF["q_x"] // width
    return pl.pallas_call(
        _mem_attend_kernel,
        grid=(n_seq, nq),
        in_specs=[pl.BlockSpec((tq, width), lambda b, i: (row0 // tq + b * nq + i, cb)),
                  pl.BlockSpec((1, m_len, width), lambda b, i: (b, 0, 0)),
                  pl.BlockSpec((1, m_len, width), lambda b, i: (b, 0, 0))],
        out_specs=pl.BlockSpec((tq, width), lambda b, i: (b * nq + i, 0)),
        out_shape=jax.ShapeDtypeStruct((n_seq * t_seq, width), F32),
        compiler_params=_params(("arbitrary", "arbitrary")),
        name="mem_attend",
    )(z, mk, mv)


def _zcol(z, name, width):
    return z[..., _Z_OFF[name]:_Z_OFF[name] + width]


def kernel(x_prompt, x_sample, cache_cmp_k, cache_cmp_v, cache_slc_k, cache_slc_v, cache_moba_k, cache_moba_v, cache_win_k, cache_win_v, cache_mem_k, cache_mem_v, page_table, mem_prompt, attn_norm_gain, w_in, nsa_q_gain, nsa_k_gain, cmp_pos_emb, cmp_w1, cmp_b1, cmp_w2, moba_q_gain, moba_k_gain, mem_norm_gain, w_mem_kv, mem_q_gain, mem_k_gain, rel_bias_table, w_branch, w_out, ffn_norm_gain, peer_w_q, peer_sub_keys, peer_u, peer_v):
    assert w_in.shape[0] == 1
    l = 0
    bp, tp, d = x_prompt.shape
    bs, ts, _ = x_sample.shape
    past_len = page_table.shape[1] * cache_cmp_k.shape[2]
    n_p, n_s = bp * tp, bs * ts

    x_all = jnp.concatenate([x_prompt.reshape(n_p, d), x_sample.reshape(n_s, d)], axis=0)
    hgain = _head_gain_row({"nsa_q": nsa_q_gain[l], "nsa_k1": nsa_k_gain[l, 1], "nsa_k2": nsa_k_gain[l, 2],
                            "moba_q": moba_q_gain[l], "moba_k": moba_k_gain[l], "mem_q": mem_q_gain[l]})
    z = _inproj(x_all, attn_norm_gain[l][None, :], _pack_w_in(w_in[l]), hgain)
    zp, zs = "prompt", "sample"

    def rows(group, name):
        col = _zcol(z, name, LANES)
        return col[:n_p].reshape(bp, tp, LANES) if group == zp else col[n_p:].reshape(bs, ts, LANES)

    kc_rows_p, vc_rows_p = rows(zp, "kc"), rows(zp, "vc")
    wk = _compress_weights(cmp_pos_emb[l, 0], cmp_w1[l, 0], cmp_b1[l, 0], cmp_w2[l, 0])
    wv = _compress_weights(cmp_pos_emb[l, 1], cmp_w1[l, 1], cmp_b1[l, 1], cmp_w2[l, 1])
    sub = lambda r: r.reshape(bp, tp // CMP_STRIDE, CMP_STRIDE * LANES)
    kgain = jnp.tile(nsa_k_gain[l, 0], 2)[None, :]
    kc_p = _compress(sub(kc_rows_p), wk, kgain, True)
    vc_p = _compress(sub(vc_rows_p), wv, kgain, False)
    o_nsa_p = _nsa_prompt(z, kc_p, vc_p, rel_bias_table, bp, tp)
    o_moba_p = _moba_prompt(z, rel_bias_table, bp, tp)
    m_len = mem_prompt.shape[1]
    mk_p, mv_p = _memory_kv_pallas(mem_prompt.reshape(bp * m_len, d), mem_norm_gain[l], w_mem_kv[l], mem_k_gain[l])
    mem_w = MEM_HEADS * MEM_HEAD_DIM
    o_mem_p = _mem_attend_pallas(z, 0, bp, tp, 256, mk_p.reshape(bp, m_len, mem_w), mv_p.reshape(bp, m_len, mem_w))
    mk_p = mk_p.reshape(bp, m_len, MEM_HEADS, MEM_HEAD_DIM)
    mv_p = mv_p.reshape(bp, m_len, MEM_HEADS, MEM_HEAD_DIM)

    keys_on_lanes = lambda c: jnp.transpose(c[l], (0, 2, 3, 1)).reshape(c.shape[1], LANES, c.shape[2])
    kc_s = _compress_paged(keys_on_lanes(cache_cmp_k), page_table, wk, kgain, True)
    vc_s = _compress_paged(keys_on_lanes(cache_cmp_v), page_table, wv, kgain, False)
    win_len = cache_win_k.shape[2]
    win_k = cache_win_k[l].reshape(bs, win_len, LANES)
    win_v = cache_win_v[l].reshape(bs, win_len, LANES)
    o_nsa_s = _nsa_sample(z, n_p, kc_s, vc_s, keys_on_lanes(cache_slc_k), keys_on_lanes(cache_slc_v),
                          keys_on_lanes(cache_win_k), keys_on_lanes(cache_win_v), page_table, rel_bias_table, ts)
    o_moba_s = _moba_sample(z, n_p, keys_on_lanes(cache_moba_k), keys_on_lanes(cache_moba_v), page_table,
                            rel_bias_table, ts)
    o_mem_s = _mem_attend_pallas(z, n_p, bs, ts, ts, cache_mem_k[l].reshape(bs, -1, mem_w), cache_mem_v[l].reshape(bs, -1, mem_w))
    kw_all = jnp.concatenate([win_k, rows(zs, "kw")], axis=1)
    vw_all = jnp.concatenate([win_v, rows(zs, "vw")], axis=1)

    cat = lambda a, b_: jnp.concatenate([a.reshape(n_p, 512), b_.reshape(n_s, 512)], axis=0)
    x1, xn_b = _merge(cat(o_nsa_p, o_nsa_s), cat(o_moba_p, o_moba_s), cat(o_mem_p, o_mem_s), z, x_all,
                      w_branch[l], w_out[l], ffn_norm_gain[l])
    y = _peer(x1, xn_b, peer_w_q[l], peer_sub_keys[l], peer_u[l].astype(BF16), peer_v[l].astype(BF16).T)
    y_p = y[:n_p].reshape(bp, tp, d)
    y_s = y[n_p:].reshape(bs, ts, d)

    def st(r, b_, t_):
        return r.reshape(1, b_, t_, 2, HEAD_DIM)

    keep_p = min(WINDOW, tp)
    p_win_k = st(rows(zp, "kw")[:, tp - keep_p:], bp, keep_p)
    p_win_v = st(rows(zp, "vw")[:, tp - keep_p:], bp, keep_p)
    keep_s = min(WINDOW, past_len + ts)
    s_win_k = st(kw_all[:, kw_all.shape[1] - keep_s:], bs, keep_s)
    s_win_v = st(vw_all[:, vw_all.shape[1] - keep_s:], bs, keep_s)
    return (y_p, y_s,
            st(kc_rows_p, bp, tp), st(vc_rows_p, bp, tp), st(rows(zp, "ks"), bp, tp), st(rows(zp, "vs"), bp, tp),
            st(rows(zp, "k_m"), bp, tp), st(rows(zp, "v_m"), bp, tp), p_win_k, p_win_v, mk_p[None], mv_p[None],
            st(rows(zs, "kc"), bs, ts), st(rows(zs, "vc"), bs, ts), st(rows(zs, "ks"), bs, ts), st(rows(zs, "vs"), bs, ts),
            st(rows(zs, "k_m"), bs, ts), st(rows(zs, "v_m"), bs, ts), s_win_k, s_win_v)
```

```python
import functools
import math

import jax
import jax.numpy as jnp
import numpy as np
from jax import lax
from jax.experimental import pallas as pl
from jax.experimental.pallas import tpu as pltpu

HEAD_DIM = 64
NSA_HEADS = 8
NSA_KV = 2
HPG = 4
CMP_LEN = 32
CMP_STRIDE = 16
SLC_LEN = 64
SLC_TOPN = 16
WINDOW = 512
FORCE_BONUS = 1000.0
MOBA_HEADS = 8
MOBA_KV = 2
MOBA_BLOCK = 256
MOBA_TOPK = 3
MEM_HEADS = 4
MEM_HEAD_DIM = 128
REL_BUCKETS = 32
REL_MAX_DIST = 128
PEER_HEADS = 8
PEER_NKEYS = 128
PEER_TOPK = 16
RMS_EPS = 1e-6
NEG_INF = -1e30
HALF_NEG = 0.5 * NEG_INF
M_FLOOR = 0.1 * NEG_INF
KNOCKED = -3.0e38
LOG2E = 1.4426950408889634
LANES = 128
VMEM_LIMIT = 56 * 1024 * 1024
ATT_TQ = 128
ATT_TK = 128

F32 = jnp.float32
BF16 = jnp.bfloat16
HI = lax.Precision.HIGHEST
_NT = (((1,), (1,)), ((), ()))


def _rel_bucket(dist):
    n = jnp.maximum(dist, 0)
    exact = REL_BUCKETS // 2
    nf = jnp.maximum(n, 1).astype(F32)
    large = exact + (jnp.log(nf / exact) / math.log(REL_MAX_DIST / exact) * (REL_BUCKETS - exact)).astype(jnp.int32)
    return jnp.where(n < exact, n, jnp.minimum(large, REL_BUCKETS - 1))


def _bias_lookup(tbl, dist):
    onehot = jax.nn.one_hot(_rel_bucket(dist), REL_BUCKETS, dtype=F32)
    return jnp.tensordot(onehot, tbl, axes=1, precision=HI)


def _gelu_tanh(x):
    inner = x * (0.7978845608028654 + 0.035677408136300125 * (x * x))
    return (0.5 * x) * (1.0 + jnp.tanh(inner))


def _split_hi_lo(x):
    hi = x.astype(BF16)
    lo = (x - hi.astype(F32)).astype(BF16)
    return hi, lo


def _dot3(a, b_hi, b_lo):
    a_hi, a_lo = _split_hi_lo(a)
    return (jnp.dot(a_hi, b_hi, preferred_element_type=F32) + jnp.dot(a_lo, b_hi, preferred_element_type=F32)
            + jnp.dot(a_hi, b_lo, preferred_element_type=F32))


def _seg_matrix(hd):
    i = np.arange(LANES)
    return jnp.asarray((i[:, None] // hd) == (i[None, :] // hd), BF16)


def _group_sumsq(z, seg):
    sq_hi, sq_lo = _split_hi_lo(z * z)
    return jnp.dot(sq_hi, seg, preferred_element_type=F32) + jnp.dot(sq_lo, seg, preferred_element_type=F32)


def _params(sem):
    return pltpu.CompilerParams(dimension_semantics=sem, vmem_limit_bytes=VMEM_LIMIT)


_Z_GROUPS = (
    ("g_merge", 3072, 0), ("q_n", 512, 64), ("q_m", 512, 64), ("q_x", 512, 128),
    ("kc", 128, 0), ("vc", 128, 0), ("ks", 128, 64), ("vs", 128, 0), ("kw", 128, 64), ("vw", 128, 0),
    ("k_m", 128, 64), ("v_m", 128, 0), ("g_n", 128, 0),
)
_Z_OFF = {}
_o = 0
for _n, _w, _h in _Z_GROUPS:
    _Z_OFF[_n] = _o
    _o += _w
Z_COLS = _o
IN_SPLITS = (512, 128, 128, 128, 128, 128, 128, 24, 512, 128, 128, 512, 3072)
_SRC_NAMES = ("q_n", "kc", "vc", "ks", "vs", "kw", "vw", "g_n", "q_m", "k_m", "v_m", "q_x", "g_merge")


def _pack_w_in(w_in):
    cuts = np.cumsum((0,) + IN_SPLITS)
    parts = {n: w_in[:, cuts[i]:cuts[i + 1]] for i, n in enumerate(_SRC_NAMES)}
    parts["g_n"] = jnp.pad(parts["g_n"], ((0, 0), (0, LANES - 24)))
    return jnp.concatenate([parts[n] for n, _, _ in _Z_GROUPS], axis=1).astype(BF16)


def _head_gain_row(gains):
    g = {
        "q_n": jnp.tile(gains["nsa_q"], 8), "ks": jnp.tile(gains["nsa_k1"], 2), "kw": jnp.tile(gains["nsa_k2"], 2),
        "q_m": jnp.tile(gains["moba_q"], 8), "k_m": jnp.tile(gains["moba_k"], 2), "q_x": jnp.tile(gains["mem_q"], 4),
    }
    return jnp.concatenate([g.get(n, jnp.ones((w,), F32)) for n, w, _ in _Z_GROUPS])[None, :]


def _inproj_kernel(x_ref, gain_ref, w_ref, hgain_ref, seg64_ref, seg128_ref, z_ref):
    x = x_ref[...]
    ms = jnp.mean(x * x, axis=-1, keepdims=True)
    xb = (x * lax.rsqrt(ms + RMS_EPS) * gain_ref[...]).astype(BF16)
    for name, width, hd in _Z_GROUPS:
        off = _Z_OFF[name]
        for c in range(0, width, 512):
            cw = min(512, width - c)
            lo_c, hi_c = off + c, off + c + cw
            z = jnp.dot(xb, w_ref[:, lo_c:hi_c], preferred_element_type=F32)
            if hd:
                seg = seg64_ref[...] if hd == 64 else seg128_ref[...]
                for j in range(0, cw, LANES):
                    zj = z[:, j:j + LANES]
                    ss = _group_sumsq(zj, seg)
                    z_ref[:, lo_c + j:lo_c + j + LANES] = (
                        zj * lax.rsqrt(ss * (1.0 / hd) + RMS_EPS) * hgain_ref[:, lo_c + j:lo_c + j + LANES])
            else:
                z_ref[:, lo_c:hi_c] = z


def _inproj(x, gain, w_packed, hgain, tm=256):
    n, d = x.shape
    assert n % tm == 0
    const = lambda i: (0, 0)
    return pl.pallas_call(
        _inproj_kernel,
        grid=(n // tm,),
        in_specs=[
            pl.BlockSpec((tm, d), lambda i: (i, 0)),
            pl.BlockSpec((1, d), const),
            pl.BlockSpec((d, Z_COLS), const),
            pl.BlockSpec((1, Z_COLS), const),
            pl.BlockSpec((LANES, LANES), const),
            pl.BlockSpec((LANES, LANES), const),
        ],
        out_specs=pl.BlockSpec((tm, Z_COLS), lambda i: (i, 0)),
        out_shape=jax.ShapeDtypeStruct((n, Z_COLS), F32),
        compiler_params=_params(("arbitrary",)),
        name="inproj",
    )(x, gain, w_packed, hgain, _seg_matrix(64), _seg_matrix(128))


def _compress_weights(pe, w1, b1, w2):
    w1r = w1.reshape(2, CMP_STRIDE, HEAD_DIM, -1)
    eye = jnp.eye(2, dtype=F32)
    wbig = jnp.einsum("rlde,gh->lgdrhe", w1r, eye).reshape(CMP_STRIDE * LANES, 2 * LANES)
    pos = jnp.einsum("rld,rlde->e", pe.reshape(2, CMP_STRIDE, HEAD_DIM), w1r, precision=HI)
    cvec = jnp.tile(b1 + pos, 2)[None, :]
    w2big = jnp.einsum("ed,gh->gehd", w2, eye).reshape(LANES, LANES)
    return _split_hi_lo(wbig) + (cvec,) + _split_hi_lo(w2big)


def _compress_kernel(x_ref, wh_ref, wl_ref, c_ref, w2h_ref, w2l_ref, seg_ref, gain_ref, o_ref, *, norm):
    ns = x_ref.shape[1]
    a = _dot3(x_ref[0], wh_ref[...], wl_ref[...])
    nxt = pltpu.roll(a[:, LANES:], ns - 1, axis=0)
    h = a[:, :LANES] + nxt + c_ref[...]
    o = _dot3(_gelu_tanh(h), w2h_ref[...], w2l_ref[...])
    if norm:
        o = o * lax.rsqrt(_group_sumsq(o, seg_ref[...]) * (1.0 / HEAD_DIM) + RMS_EPS) * gain_ref[...]
    o_ref[0] = o


def _compress(x_sub, weights, gain, norm):
    b, ns, kk = x_sub.shape
    wh, wl, cvec, w2h, w2l = weights
    const = lambda i: (0, 0)
    return pl.pallas_call(
        functools.partial(_compress_kernel, norm=norm),
        grid=(b,),
        in_specs=[
            pl.BlockSpec((1, ns, kk), lambda i: (i, 0, 0)),
            pl.BlockSpec((kk, 2 * LANES), const), pl.BlockSpec((kk, 2 * LANES), const),
            pl.BlockSpec((1, LANES), const),
            pl.BlockSpec((LANES, LANES), const), pl.BlockSpec((LANES, LANES), const),
            pl.BlockSpec((LANES, LANES), const), pl.BlockSpec((1, LANES), const),
        ],
        out_specs=pl.BlockSpec((1, ns, LANES), lambda i: (i, 0, 0)),
        out_shape=jax.ShapeDtypeStruct((b, ns, LANES), F32),
        compiler_params=_params(("arbitrary",)),
        name="nsa_compress",
    )(x_sub, wh, wl, cvec, w2h, w2l, _seg_matrix(HEAD_DIM), gain)


def _toeplitz_tiles(tbl, deltas, tq, tk, hi):
    d = (jnp.asarray(deltas, jnp.int32)[:, None, None] + jnp.arange(tq)[None, :, None] - jnp.arange(tk)[None, None, :])
    b = jnp.where(((d >= 0) & (d < hi))[..., None, None], _bias_lookup(tbl, d), NEG_INF)
    return jnp.transpose(b, (3, 0, 4, 1, 2)).reshape(tbl.shape[1], len(deltas), HPG * tq, tk)


def _with_masked_tile(tiles):
    return jnp.concatenate([jnp.full_like(tiles[:, :1], NEG_INF), tiles], axis=1)


def _stack_queries(zq_ref, write, tq):
    lane = lax.broadcasted_iota(jnp.int32, (tq, LANES), 1)
    for g in range(2):
        keep = (lane < HEAD_DIM) if g == 0 else (lane >= HEAD_DIM)
        for h in range(HPG):
            hd = HPG * g + h
            chunk = zq_ref[:, LANES * (hd // 2):LANES * (hd // 2 + 1)] * (HEAD_DIM ** -0.5)
            if hd % 2 != g:
                chunk = pltpu.roll(chunk, HEAD_DIM, axis=1)
            write(g, h, jnp.where(keep, chunk, 0.0))


def _stack_queries_prompt(zq_ref, q_scr, tq):
    def write(g, h, val):
        q_scr[g, h * tq:(h + 1) * tq, :] = val.astype(BF16)
    _stack_queries(zq_ref, write, tq)


def _stack_queries_sample(zq_ref, q_scr, ts):
    def write(g, h, val):
        q_scr[(g * HPG + h) * ts:(g * HPG + h + 1) * ts, :] = val
    _stack_queries(zq_ref, write, ts)


def _unstack_heads(parts, o_ref, tq):
    lane = lax.broadcasted_iota(jnp.int32, (tq, LANES), 1)
    for j in range(4):
        g = j // 2
        he = (2 * j) % HPG
        left = parts[g][he * tq:(he + 1) * tq, :]
        right = parts[g][(he + 1) * tq:(he + 2) * tq, :]
        if g == 0:
            right = pltpu.roll(right, HEAD_DIM, axis=1)
        else:
            left = pltpu.roll(left, HEAD_DIM, axis=1)
        o_ref[:, LANES * j:LANES * (j + 1)] = jnp.where(lane < HEAD_DIM, left, right)


KEY_TILES_PER_STEP = 8


def _flash(qs, k_ref, v_ref, bias_ref, tile_index, bms, blk_shift, rep, kb_lo, kb_hi, qt, m_scr, l_scr, acc_scr, tk):
    m_scr[...] = jnp.full(m_scr.shape, M_FLOOR, F32)
    l_scr[...] = jnp.zeros(l_scr.shape, F32)
    acc_scr[...] = jnp.zeros(acc_scr.shape, F32)
    nk = tk * KEY_TILES_PER_STEP
    blk_row = lax.broadcasted_iota(jnp.int32, (LANES, nk), 0)
    key_lane = lax.broadcasted_iota(jnp.int32, (LANES, nk), 1)

    def body(kb, carry):
        k0 = pl.multiple_of(kb * nk, nk)
        k = k_ref[pl.ds(k0, nk), :].astype(BF16)
        v = v_ref[pl.ds(k0, nk), :].astype(BF16)
        tiles = [tile_index(qt - (kb * KEY_TILES_PER_STEP + c)) for c in range(KEY_TILES_PER_STEP)]
        if bms is not None:
            expand = jnp.where(blk_row == ((k0 + key_lane) >> blk_shift), 1.0, 0.0).astype(BF16)
        for g, q in enumerate(qs):
            s = lax.dot_general(q, k, _NT, preferred_element_type=F32)
            s = s + jnp.concatenate([bias_ref[g, t] for t in tiles], axis=1)
            if bms is not None:
                vis = jnp.dot(bms[g], expand, preferred_element_type=F32)
                if rep > 1:
                    s = s + jnp.concatenate([jnp.where(vis > 0.5, 0.0, NEG_INF)] * rep, axis=0)
                else:
                    s = jnp.where(vis > 0.5, s, NEG_INF)
            m_old = m_scr[g]
            m_new = jnp.maximum(m_old, jnp.max(s, axis=-1, keepdims=True))
            p = jnp.exp(s - m_new)
            alpha = jnp.exp(m_old - m_new)
            l_scr[g] = alpha * l_scr[g] + jnp.sum(p, axis=-1, keepdims=True)
            acc_scr[g] = alpha * acc_scr[g] + jnp.dot(p.astype(BF16), v, preferred_element_type=F32)
            m_scr[g] = m_new
        return carry

    lax.fori_loop(kb_lo, kb_hi, body, 0)
    outs = []
    for g in range(len(qs)):
        l = l_scr[g]
        outs.append(acc_scr[g] / jnp.where(l > 0.0, l, 1.0))
    return outs


def _causal_tile_index(n_near):
    return lambda rel: jnp.clip(rel + 1, 0, n_near)


def _window_tile_index(n_win):
    return lambda rel: jnp.where(rel >= n_win, 0, jnp.clip(rel + 1, 0, n_win))


def _rank_select(score, count, lane, n_valid):
    width = score.shape[-1]
    rank = jnp.zeros(score.shape, F32)
    for s in range(1, width):
        if n_valid <= s <= width - n_valid:
            continue
        other = pltpu.roll(score, s, axis=1)
        rank = rank + jnp.where(other > score, 1.0, 0.0)
        if s < n_valid:
            rank = rank + jnp.where(other == score, jnp.where(lane >= s, 1.0, 0.0), 0.0)
    return jnp.where((rank < count) & (score > HALF_NEG), 1.0, 0.0)


def _take_topk(score, count, lane):
    sel = jnp.zeros(score.shape, F32)
    cur = score
    lane_f = lane.astype(F32)
    for _ in range(count):
        mx = jnp.max(cur, axis=-1, keepdims=True)
        idx = jnp.min(jnp.where(cur == mx, lane_f, float(score.shape[-1])), axis=-1, keepdims=True)
        hit = lane_f == idx
        sel = jnp.where(hit, jnp.where(mx > HALF_NEG, 1.0, 0.0), sel)
        cur = jnp.where(hit, KNOCKED, cur)
    return sel


def _nsa_prompt_kernel(zq_ref, gn_ref, kc_ref, vc_ref, ks_ref, vs_ref, kw_ref, vw_ref, biasc_ref, bslc_ref, bwin_ref,
                       mslc_ref, o_ref, q_scr, m_scr, l_scr, acc_scr, comb_scr, *, tq, tk, n_slc, n_slc_tiles, n_win_tiles):
    qt = pl.program_id(1)
    q0 = qt * tq
    _stack_queries_prompt(zq_ref, q_scr, tq)
    gn = jax.nn.sigmoid(gn_ref[...])
    lane = lax.broadcasted_iota(jnp.int32, (tq, LANES), 1)
    tpos = q0 + lax.broadcasted_iota(jnp.int32, (tq, LANES), 0)
    qb = tpos >> 6

    def gate_col(c, g):
        return jnp.concatenate([gn[:, c * 8 + g * HPG + h:c * 8 + g * HPG + h + 1] for h in range(HPG)], axis=0)

    kc = kc_ref[0].astype(BF16)
    vc = vc_ref[0].astype(BF16)
    qs = [q_scr[g] for g in range(NSA_KV)]
    bms = []
    for g, q in enumerate(qs):
        s_c = lax.dot_general(q, kc, _NT, preferred_element_type=F32) + biasc_ref[g]
        mx = jnp.maximum(jnp.max(s_c, axis=-1, keepdims=True), M_FLOOR)
        p = jnp.exp(s_c - mx)
        den = jnp.sum(p, axis=-1, keepdims=True)
        p_c = p / jnp.where(den > 0.0, den, 1.0)
        comb_scr[g] = gate_col(0, g) * jnp.dot(p_c.astype(BF16), vc, preferred_element_type=F32)
        imp = p_c[0:tq] + p_c[tq:2 * tq] + p_c[2 * tq:3 * tq] + p_c[3 * tq:4 * tq]
        imp_hi, imp_lo = _split_hi_lo(imp)
        p_slc = (jnp.dot(imp_hi, mslc_ref[...], preferred_element_type=F32)
                 + jnp.dot(imp_lo, mslc_ref[...], preferred_element_type=F32))
        forced = (lane == 0) | (lane == qb) | (lane == qb - 1)
        score = jnp.where(lane > qb, NEG_INF, p_slc + jnp.where(forced, FORCE_BONUS, 0.0))
        bms.append(lax.cond(q0 + tq <= SLC_TOPN * SLC_LEN,
                            lambda score=score: jnp.where(score > HALF_NEG, 1.0, 0.0),
                            lambda score=score: _rank_select(score, SLC_TOPN, lane, n_slc)).astype(BF16))
    kb_hi = qt // KEY_TILES_PER_STEP + 1
    o_s = _flash(qs, ks_ref, vs_ref, bslc_ref, _causal_tile_index(n_slc_tiles), bms, 6, HPG, 0, kb_hi, qt,
                 m_scr, l_scr, acc_scr, tk)
    kb_lo = jnp.maximum(qt - (n_win_tiles - 1), 0) // KEY_TILES_PER_STEP
    o_w = _flash(qs, kw_ref, vw_ref, bwin_ref, _window_tile_index(n_win_tiles), None, 0, 1, kb_lo, kb_hi, qt,
                 m_scr, l_scr, acc_scr, tk)
    for g in range(NSA_KV):
        comb_scr[g] = comb_scr[g] + gate_col(1, g) * o_s[g] + gate_col(2, g) * o_w[g]
    _unstack_heads([comb_scr[0], comb_scr[1]], o_ref, tq)


def _zspec(rows, name, width, row_map):
    cb = _Z_OFF[name] // width
    assert _Z_OFF[name] % width == 0
    return pl.BlockSpec((rows, width), lambda b, i: (row_map(b, i), cb))


def _nsa_prompt(z, kc, vc, rel_table, bp, tp):
    tq, tk = ATT_TQ, ATT_TK
    nq = tp // tq
    n_cmp = tp // CMP_STRIDE - 1
    n_slc = tp // SLC_LEN
    assert tp % tq == 0 and n_cmp < LANES and n_slc <= LANES and kc.shape[1] == LANES
    tbl = rel_table[:, :NSA_HEADS].reshape(REL_BUCKETS, NSA_KV, HPG)
    t = jnp.arange(tp)
    dist_c = t[:, None] - (jnp.arange(LANES) * CMP_STRIDE + CMP_LEN - 1)[None, :]
    valid_c = (dist_c >= 0) & (jnp.arange(LANES) < n_cmp)[None, :]
    biasc = jnp.where(valid_c[..., None, None], _bias_lookup(tbl, dist_c), NEG_INF)
    biasc = jnp.transpose(biasc.reshape(nq, tq, LANES, NSA_KV, HPG), (0, 3, 4, 1, 2)).reshape(nq, NSA_KV, HPG * tq, LANES)
    slc_deltas = (0, tq, 2 * tq)
    win_deltas = tuple(range(0, WINDOW + 1, tq))
    assert tp % (tk * KEY_TILES_PER_STEP) == 0 and n_slc <= LANES // 2
    bslc = _with_masked_tile(_toeplitz_tiles(tbl, slc_deltas, tq, tk, 1 << 30))
    bwin = _with_masked_tile(_toeplitz_tiles(tbl, win_deltas, tq, tk, WINDOW))
    mslc = jnp.pad(_slc_matrix(n_cmp, n_slc), ((0, LANES - n_cmp), (0, LANES - n_slc))).astype(BF16)
    kern = functools.partial(_nsa_prompt_kernel, tq=tq, tk=tk, n_slc=n_slc, n_slc_tiles=len(slc_deltas),
                             n_win_tiles=len(win_deltas))
    seq = lambda b, i: b
    tile = lambda b, i: b * nq + i
    full4 = lambda b, i: (0, 0, 0, 0)
    return pl.pallas_call(
        kern,
        grid=(bp, nq),
        in_specs=[
            _zspec(tq, "q_n", 512, tile), _zspec(tq, "g_n", LANES, tile),
            pl.BlockSpec((1, LANES, LANES), lambda b, i: (b, 0, 0)), pl.BlockSpec((1, LANES, LANES), lambda b, i: (b, 0, 0)),
        ] + [pl.BlockSpec((tp, LANES), (lambda b, i, cb=_Z_OFF[nm] // LANES: (b, cb))) for nm in ("ks", "vs", "kw", "vw")] + [
            pl.BlockSpec((None, NSA_KV, HPG * tq, LANES), lambda b, i: (i, 0, 0, 0)),
            pl.BlockSpec(bslc.shape, full4), pl.BlockSpec(bwin.shape, full4),
            pl.BlockSpec((LANES, LANES), lambda b, i: (0, 0)),
        ],
        out_specs=pl.BlockSpec((tq, 512), lambda b, i: (b * nq + i, 0)),
        out_shape=jax.ShapeDtypeStruct((bp * tp, 512), F32),
        scratch_shapes=[
            pltpu.VMEM((NSA_KV, HPG * tq, LANES), BF16),
            pltpu.VMEM((2, HPG * tq, 1), F32), pltpu.VMEM((2, HPG * tq, 1), F32), pltpu.VMEM((2, HPG * tq, LANES), F32),
            pltpu.VMEM((NSA_KV, HPG * tq, LANES), F32),
        ],
        compiler_params=_params(("arbitrary", "arbitrary")),
        name="nsa_prompt",
    )(z, z, kc, vc, z, z, z, z, biasc, bslc, bwin, mslc)


def _slc_matrix(n_cmp, n_slc):
    ratio, c_sub = SLC_LEN // CMP_STRIDE, CMP_LEN // CMP_STRIDE
    m = np.zeros((n_cmp, n_slc), np.float32)
    for j in range(n_slc):
        for o in range(-(c_sub - 1), ratio):
            k = ratio * j + o
            if 0 <= k < n_cmp:
                m[k, j] += float(min(o + c_sub, ratio) - max(o, 0))
    return jnp.asarray(m)


def _moba_prompt_kernel(zq_ref, k_ref, v_ref, bias_ref, o_ref, q_scr, m_scr, l_scr, acc_scr, *, tq, tk, nb, n_tiles):
    qt = pl.program_id(1)
    q0 = qt * tq
    _stack_queries_prompt(zq_ref, q_scr, tq)
    rows = HPG * tq
    lane = lax.broadcasted_iota(jnp.int32, (rows, LANES), 1)
    trow = q0 + (lax.broadcasted_iota(jnp.int32, (rows, LANES), 0) & (tq - 1))
    own = trow >> 8
    kmean = jnp.concatenate(
        [jnp.mean(k_ref[n * MOBA_BLOCK:(n + 1) * MOBA_BLOCK, :], axis=0, keepdims=True) for n in range(nb)]
        + [jnp.zeros((LANES - nb, LANES), F32)], axis=0)
    km_hi, km_lo = _split_hi_lo(kmean)
    qs = [q_scr[g] for g in range(MOBA_KV)]
    bms = []
    for q in qs:
        gs = (lax.dot_general(q, km_hi, _NT, preferred_element_type=F32)
              + lax.dot_general(q, km_lo, _NT, preferred_element_type=F32))
        gs = jnp.where(lane < own, gs, NEG_INF)
        sel = _take_topk(gs, min(MOBA_TOPK, nb - 1), lane)
        bms.append(jnp.where(lane == own, 1.0, sel).astype(BF16))
    res = _flash(qs, k_ref, v_ref, bias_ref, _causal_tile_index(n_tiles), bms, 8, 1, 0,
                 qt // KEY_TILES_PER_STEP + 1, qt, m_scr, l_scr, acc_scr, tk)
    _unstack_heads(res, o_ref, tq)


def _moba_prompt(z, rel_table, bp, tp):
    tq, tk = ATT_TQ, ATT_TK
    nq = tp // tq
    nb = tp // MOBA_BLOCK
    assert tp % MOBA_BLOCK == 0 and tq & (tq - 1) == 0
    tbl = rel_table[:, NSA_HEADS:].reshape(REL_BUCKETS, MOBA_KV, HPG)
    deltas = (0, tq, 2 * tq)
    assert tp % (tk * KEY_TILES_PER_STEP) == 0
    bias = _with_masked_tile(_toeplitz_tiles(tbl, deltas, tq, tk, 1 << 30))
    kern = functools.partial(_moba_prompt_kernel, tq=tq, tk=tk, nb=nb, n_tiles=len(deltas))
    return pl.pallas_call(
        kern,
        grid=(bp, nq),
        in_specs=[
            _zspec(tq, "q_m", 512, lambda b, i: b * nq + i),
            pl.BlockSpec((tp, LANES), lambda b, i: (b, _Z_OFF["k_m"] // LANES)),
            pl.BlockSpec((tp, LANES), lambda b, i: (b, _Z_OFF["v_m"] // LANES)),
            pl.BlockSpec(bias.shape, lambda b, i: (0, 0, 0, 0)),
        ],
        out_specs=pl.BlockSpec((tq, 512), lambda b, i: (b * nq + i, 0)),
        out_shape=jax.ShapeDtypeStruct((bp * tp, 512), F32),
        scratch_shapes=[
            pltpu.VMEM((MOBA_KV, HPG * tq, LANES), BF16),
            pltpu.VMEM((2, HPG * tq, 1), F32), pltpu.VMEM((2, HPG * tq, 1), F32), pltpu.VMEM((2, HPG * tq, LANES), F32),
        ],
        compiler_params=_params(("arbitrary", "arbitrary")),
        name="moba_prompt",
    )(z, z, z, bias)


def _merge_kernel(on_ref, om_ref, ox_ref, gm_ref, x_ref, wb_ref, wo_ref, fg_ref, x1_ref, xn_ref):
    d = x_ref.shape[1]
    merged = jnp.zeros(x_ref.shape, F32)
    for c, o_ref in enumerate((on_ref, om_ref, ox_ref)):
        proj = jnp.dot(o_ref[...].astype(BF16), wb_ref[c], preferred_element_type=F32)
        merged = merged + jax.nn.sigmoid(gm_ref[:, c * d:(c + 1) * d]) * proj
    x1 = x_ref[...] + jnp.dot(merged.astype(BF16), wo_ref[...], preferred_element_type=F32)
    x1_ref[...] = x1
    ms = jnp.mean(x1 * x1, axis=-1, keepdims=True)
    xn_ref[...] = (x1 * lax.rsqrt(ms + RMS_EPS) * fg_ref[...]).astype(BF16)


def _merge(o_nsa, o_moba, o_mem, z, x, w_branch, w_out, ffn_gain, tm=256):
    n, d = x.shape
    assert n % tm == 0 and _Z_OFF["g_merge"] == 0
    row = lambda i: (i, 0)
    return pl.pallas_call(
        _merge_kernel,
        grid=(n // tm,),
        in_specs=[
            pl.BlockSpec((tm, 512), row), pl.BlockSpec((tm, 512), row), pl.BlockSpec((tm, 512), row),
            pl.BlockSpec((tm, 3 * d), row), pl.BlockSpec((tm, d), row),
            pl.BlockSpec((3, 512, d), lambda i: (0, 0, 0)), pl.BlockSpec((d, d), lambda i: (0, 0)),
            pl.BlockSpec((1, d), lambda i: (0, 0)),
        ],
        out_specs=[pl.BlockSpec((tm, d), row), pl.BlockSpec((tm, d), row)],
        out_shape=[jax.ShapeDtypeStruct((n, d), F32), jax.ShapeDtypeStruct((n, d), BF16)],
        compiler_params=_params(("arbitrary",)),
        name="merge_outproj",
    )(o_nsa, o_moba, o_mem, z, x, w_branch.astype(BF16), w_out.astype(BF16), ffn_gain[None, :])


def _peer_route_kernel(xT_ref, wqh_ref, wql_ref, skh_ref, skl_ref, s1_ref, s2_ref, tau_ref):
    xT = xT_ref[...]
    qT = jnp.dot(wqh_ref[...], xT, preferred_element_type=F32) + jnp.dot(wql_ref[...], xT, preferred_element_type=F32)
    half = HEAD_DIM

    def top_sorted(s):
        vals, cur = [], s
        for _ in range(PEER_TOPK + 1):
            mx = jnp.max(cur, axis=0, keepdims=True)
            vals.append(mx)
            cur = jnp.where(cur == mx, KNOCKED, cur)
        return vals

    taus = []
    for h in range(PEER_HEADS):
        sc = []
        for p in range(2):
            r0 = (h * 2 + p) * half
            q_hi, q_lo = _split_hi_lo(qT[r0:r0 + half, :])
            sc.append(jnp.dot(skh_ref[h, p], q_hi, preferred_element_type=F32)
                      + jnp.dot(skl_ref[h, p], q_hi, preferred_element_type=F32)
                      + jnp.dot(skh_ref[h, p], q_lo, preferred_element_type=F32))
        a, b = top_sorted(sc[0]), top_sorted(sc[1])
        n_rank = PEER_TOPK + 1
        cand = jnp.concatenate([a[i] + b[j] for i in range(n_rank) for j in range(n_rank // (i + 1))], axis=0)
        cur = cand
        for _ in range(PEER_TOPK):
            tau = jnp.max(cur, axis=0, keepdims=True)
            cur = jnp.where(cur == tau, KNOCKED, cur)
        nxt = jnp.max(cur, axis=0, keepdims=True)
        top = a[0] + b[0]
        zsum = jnp.sum(jnp.where(cand >= tau, jnp.exp(cand - top), 0.0), axis=0, keepdims=True)
        shift = top + jnp.log(zsum)
        s1_ref[h] = (sc[0] - shift) * LOG2E
        s2_ref[h] = sc[1] * LOG2E
        taus.append((0.5 * (tau + nxt) - shift) * LOG2E)
    tau_ref[...] = jnp.concatenate(taus, axis=0)


def _peer_route(xT, peer_w_q, sub_keys, tt=256):
    d, n = xT.shape
    assert n % tt == 0
    wqh, wql = _split_hi_lo(peer_w_q.T)
    skh, skl = _split_hi_lo(sub_keys)
    tok3 = lambda i: (0, 0, i)
    big = jax.ShapeDtypeStruct((PEER_HEADS, PEER_NKEYS, n), F32)
    return pl.pallas_call(
        _peer_route_kernel,
        grid=(n // tt,),
        in_specs=[
            pl.BlockSpec((d, tt), lambda i: (0, i)),
            pl.BlockSpec((d, d), lambda i: (0, 0)), pl.BlockSpec((d, d), lambda i: (0, 0)),
            pl.BlockSpec(sub_keys.shape, lambda i: (0, 0, 0, 0)), pl.BlockSpec(sub_keys.shape, lambda i: (0, 0, 0, 0)),
        ],
        out_specs=[pl.BlockSpec((PEER_HEADS, PEER_NKEYS, tt), tok3)] * 2 + [pl.BlockSpec((PEER_HEADS, tt), lambda i: (0, i))],
        out_shape=[big] * 2 + [jax.ShapeDtypeStruct((PEER_HEADS, n), F32)],
        compiler_params=_params(("arbitrary",)),
        name="peer_route",
    )(xT, wqh, wql, skh, skl)


def _peer_dense_kernel(xT_ref, u_ref, vT_ref, s1_ref, s2_ref, tau_ref, o_ref, a_ref, *, ec, tt):
    c = pl.program_id(1)

    @pl.when(c == 0)
    def _():
        o_ref[...] = jnp.zeros_like(o_ref)

    h_t = jnp.dot(u_ref[...], xT_ref[...], preferred_element_type=F32)
    for k in range(ec // LANES):
        for lc in range(0, tt, LANES):
            wk = jnp.zeros((LANES, LANES), F32)
            for h in range(PEER_HEADS):
                csum = s2_ref[h, :, lc:lc + LANES] + s1_ref[h, k:k + 1, lc:lc + LANES]
                wk = wk + jnp.where(csum >= tau_ref[h:h + 1, lc:lc + LANES], jnp.exp2(csum), 0.0)
            g = _gelu_tanh(h_t[k * LANES:(k + 1) * LANES, lc:lc + LANES])
            a_ref[k * LANES:(k + 1) * LANES, lc:lc + LANES] = (wk * g).astype(BF16)
    o_ref[...] += jnp.dot(vT_ref[...], a_ref[...], preferred_element_type=F32)


def _peer_dense(xT, u_b, vT_b, s1T, s2T, tauT, ec=1024):
    d, n = xT.shape
    tt = next(t for t in (1024, 512, 256) if n % t == 0)
    assert n % tt == 0 and ec == 8 * PEER_NKEYS
    ne = u_b.shape[0]
    kern = functools.partial(_peer_dense_kernel, ec=ec, tt=tt)
    allk = pl.BlockSpec((PEER_HEADS, PEER_NKEYS, tt), lambda i, c: (0, 0, i))
    chunk = pl.BlockSpec((PEER_HEADS, ec // PEER_NKEYS, tt), lambda i, c: (0, c, i))
    return pl.pallas_call(
        kern,
        grid=(n // tt, ne // ec),
        in_specs=[
            pl.BlockSpec((d, tt), lambda i, c: (0, i)),
            pl.BlockSpec((ec, d), lambda i, c: (c, 0)),
            pl.BlockSpec((d, ec), lambda i, c: (0, c)),
            chunk, allk,
            pl.BlockSpec((PEER_HEADS, tt), lambda i, c: (0, i)),
        ],
        out_specs=pl.BlockSpec((d, tt), lambda i, c: (0, i)),
        out_shape=jax.ShapeDtypeStruct((d, n), F32),
        scratch_shapes=[pltpu.VMEM((ec, tt), BF16)],
        compiler_params=_params(("arbitrary", "arbitrary")),
        name="peer_dense",
    )(xT, u_b, vT_b, s1T, s2T, tauT)


def _peer(x1, xn_b, peer_w_q, sub_keys, u_b, vT_b):
    xT = xn_b.T
    s1T, s2T, tauT = _peer_route(xT, peer_w_q, sub_keys)
    out_t = _peer_dense(xT, u_b, vT_b, s1T, s2T, tauT)
    return x1 + out_t.T


PAGES_PER_STEP = 64


def _page_specs(n_lead, inner):
    zeros = (0,) * len(inner)
    return [pl.BlockSpec((1,) + inner, (lambda b, j, pt, p=p: (pt[b, j * PAGES_PER_STEP + p],) + zeros))
            for p in range(n_lead)]


def _compress_paged_kernel(pt_ref, *refs, norm):
    del pt_ref
    p_n = PAGES_PER_STEP
    x_refs = refs[:p_n]
    perm_ref, wh_ref, c_ref, w2h_ref, w2l_ref, seg_ref, gain_ref, o_ref, x_scr = refs[p_n:]
    j = pl.program_id(1)
    pair_sub = 2 * x_refs[0].shape[2] // CMP_STRIDE
    for pp in range(p_n // 2):
        x_t = jnp.concatenate([x_refs[2 * pp][0], x_refs[2 * pp + 1][0]], axis=1).astype(BF16)
        rows = lax.dot_general(perm_ref[...], x_t, _NT, preferred_element_type=F32).astype(BF16)
        base = pl.multiple_of((j * (p_n // 2) + pp) * pair_sub, pair_sub)
        for l in range(CMP_STRIDE):
            x_scr[pl.ds(base, pair_sub), l * LANES:(l + 1) * LANES] = rows[l * pair_sub:(l + 1) * pair_sub, :]

    @pl.when(j == pl.num_programs(1) - 1)
    def _():
        ns = x_scr.shape[0]
        a = jnp.dot(x_scr[...], wh_ref[...], preferred_element_type=F32)
        nxt = pltpu.roll(a[:, LANES:], ns - 1, axis=0)
        h = a[:, :LANES] + nxt + c_ref[...]
        o = _dot3(_gelu_tanh(h), w2h_ref[...], w2l_ref[...])
        if norm:
            o = o * lax.rsqrt(_group_sumsq(o, seg_ref[...]) * (1.0 / HEAD_DIM) + RMS_EPS) * gain_ref[...]
        o_ref[0] = o


def _compress_paged(pool, page_table, weights, gain, norm):
    n_phys, _, page = pool.shape
    b, n_pages = page_table.shape
    sub = page // CMP_STRIDE
    kk = CMP_STRIDE * LANES
    ns = n_pages * sub
    assert n_pages % PAGES_PER_STEP == 0 and page % CMP_STRIDE == 0 and PAGES_PER_STEP % 2 == 0 and (2 * sub) % 16 == 0
    wh, _, cvec, w2h, w2l = weights
    const = lambda b_, j, pt: (0, 0)
    r = np.arange(2 * page)
    perm = jnp.asarray(((r[:, None] // (2 * sub)) + CMP_STRIDE * (r[:, None] % (2 * sub))) == r[None, :], BF16)
    grid_spec = pltpu.PrefetchScalarGridSpec(
        num_scalar_prefetch=1,
        grid=(b, n_pages // PAGES_PER_STEP),
        in_specs=_page_specs(PAGES_PER_STEP, (LANES, page)) + [
            pl.BlockSpec((2 * page, 2 * page), const), pl.BlockSpec((kk, 2 * LANES), const),
            pl.BlockSpec((1, LANES), const),
            pl.BlockSpec((LANES, LANES), const), pl.BlockSpec((LANES, LANES), const),
            pl.BlockSpec((LANES, LANES), const), pl.BlockSpec((1, LANES), const),
        ],
        out_specs=pl.BlockSpec((1, ns, LANES), lambda b_, j, pt: (b_, 0, 0)),
        scratch_shapes=[pltpu.VMEM((ns, kk), BF16)],
    )
    return pl.pallas_call(
        functools.partial(_compress_paged_kernel, norm=norm),
        grid_spec=grid_spec,
        out_shape=jax.ShapeDtypeStruct((b, ns, LANES), F32),
        compiler_params=_params(("arbitrary", "arbitrary")),
        name="nsa_compress_paged",
    )(page_table, *([pool] * PAGES_PER_STEP), perm, wh, cvec, w2h, w2l, _seg_matrix(HEAD_DIM), gain)


def _softmax_tile(s):
    mx = jnp.maximum(jnp.max(s, axis=-1, keepdims=True), M_FLOOR)
    p = jnp.exp(s - mx)
    return mx, p, jnp.sum(p, axis=-1, keepdims=True)


def _pv(p, v, v_t):
    if v_t:
        return lax.dot_general(p.astype(BF16), v, _NT, preferred_element_type=F32)
    return jnp.dot(p.astype(BF16), v, preferred_element_type=F32)


def _online_update(s, v, m_scr, l_scr, acc_scr, v_t=False):
    m_old = m_scr[...]
    m_new = jnp.maximum(m_old, jnp.max(s, axis=-1, keepdims=True))
    p = jnp.exp(s - m_new)
    alpha = jnp.exp(m_old - m_new)
    l_scr[...] = alpha * l_scr[...] + jnp.sum(p, axis=-1, keepdims=True)
    acc_scr[...] = alpha * acc_scr[...] + _pv(p, v, v_t)
    m_scr[...] = m_new


def _pad_rows(x, rows):
    return jnp.concatenate([x, jnp.zeros((rows - x.shape[0], x.shape[1]), x.dtype)], axis=0)


def _sample_row_bias(tbl, ts, kpos, qpos0, lo_ok):
    d = (qpos0 + jnp.arange(ts))[:, None] - kpos[None, :]
    b = jnp.where(((d >= 0) & lo_ok)[..., None, None], _bias_lookup(tbl, d), NEG_INF)
    return jnp.transpose(b, (2, 3, 0, 1)).reshape(-1, kpos.shape[0])


def _past_bias_tiles(tbl, ts, past_len, nk):
    assert nk >= LANES
    far = jnp.repeat(tbl[REL_BUCKETS - 1].reshape(-1), ts)[:, None]
    far_tile = jnp.broadcast_to(far, (far.shape[0], nk))
    near = _sample_row_bias(tbl, ts, past_len - LANES + jnp.arange(LANES), past_len, jnp.ones((ts, LANES), bool))
    return jnp.stack([far_tile, jnp.concatenate([far_tile[:, :nk - LANES], near], axis=1)])


def _block_expand_matrix(n_blk_lanes, nk, n_steps):
    blk_step = nk // SLC_LEN
    assert nk % SLC_LEN == 0 and blk_step % 16 == 0
    r = np.arange(n_blk_lanes + (n_steps - 1) * blk_step)[:, None]
    key_blk = (np.arange(nk) // SLC_LEN)[None, :]
    return jnp.asarray(r == (n_steps - 1) * blk_step + key_blk, BF16)


def _nsa_sample_kernel(pt_ref, *refs, ts, past_len, n_slc, slc_lanes):
    del pt_ref
    p_n = PAGES_PER_STEP
    k_pages, v_pages = refs[:p_n], refs[p_n:2 * p_n]
    (zq_ref, gn_ref, kc_ref, vc_ref, ksn_ref, vsn_ref, wk_ref, wv_ref, kwn_ref, vwn_ref,
     biasc_ref, bpast_ref, bnew_ref, bwin_ref, mslc_ref, expand_ref, o_ref,
     q_scr, m_scr, l_scr, acc_scr, comb_scr, bm_scr) = refs[2 * p_n:]
    j = pl.program_id(1)
    page = k_pages[0].shape[2]
    rows = NSA_KV * HPG * ts
    half = HPG * ts

    def gate_col(gn, c):
        return jnp.concatenate([gn[:, c * 8 + hd:c * 8 + hd + 1] for hd in range(NSA_HEADS)], axis=0)

    @pl.when(j == 0)
    def _():
        _stack_queries_sample(zq_ref, q_scr, ts)
        q = q_scr[...].astype(BF16)
        gn = jax.nn.sigmoid(gn_ref[...])
        s_c = lax.dot_general(q, kc_ref[0].astype(BF16), _NT, preferred_element_type=F32) + biasc_ref[...]
        mx, p, den = _softmax_tile(s_c)
        p_c = p / jnp.where(den > 0.0, den, 1.0)
        comb_scr[...] = gate_col(gn, 0) * jnp.dot(p_c.astype(BF16), vc_ref[0].astype(BF16), preferred_element_type=F32)
        imp = jnp.concatenate(
            [sum(p_c[g * half + h * ts:g * half + (h + 1) * ts] for h in range(HPG)) for g in range(NSA_KV)], axis=0)
        imp_hi, imp_lo = _split_hi_lo(imp)
        p_slc = (jnp.dot(imp_hi, mslc_ref[...], preferred_element_type=F32)
                 + jnp.dot(imp_lo, mslc_ref[...], preferred_element_type=F32))
        lane = lax.broadcasted_iota(jnp.int32, p_slc.shape, 1)
        tpos = past_len + (lax.broadcasted_iota(jnp.int32, p_slc.shape, 0) & (ts - 1))
        qb = tpos >> 6
        forced = (lane == 0) | (lane == qb) | (lane == qb - 1)
        score = jnp.where(lane > qb, NEG_INF, p_slc + jnp.where(forced, FORCE_BONUS, 0.0))
        sel = _rank_select(score, SLC_TOPN, lane, n_slc)
        bm_scr[...] = jnp.concatenate(
            [sel[g * ts:(g + 1) * ts] for g in range(NSA_KV) for _ in range(HPG)], axis=0).astype(BF16)
        m_scr[...] = jnp.full(m_scr.shape, M_FLOOR, F32)
        l_scr[...] = jnp.zeros(l_scr.shape, F32)
        acc_scr[...] = jnp.zeros(acc_scr.shape, F32)
        s_w = jnp.dot(q, wk_ref[0].astype(BF16), preferred_element_type=F32) + bwin_ref[...]
        _online_update(s_w, wv_ref[0].astype(BF16), m_scr, l_scr, acc_scr, v_t=True)
        s_n = lax.dot_general(q, _pad_rows(kwn_ref[...], LANES).astype(BF16), _NT, preferred_element_type=F32) + bnew_ref[...]
        _online_update(s_n, _pad_rows(vwn_ref[...], LANES).astype(BF16), m_scr, l_scr, acc_scr)
        l = l_scr[...]
        comb_scr[...] = comb_scr[...] + gate_col(gn, 2) * (acc_scr[...] / jnp.where(l > 0.0, l, 1.0))
        m_scr[...] = jnp.full(m_scr.shape, M_FLOOR, F32)
        l_scr[...] = jnp.zeros(l_scr.shape, F32)
        acc_scr[...] = jnp.zeros(acc_scr.shape, F32)

    q = q_scr[...].astype(BF16)
    nk = p_n * page
    k_t = jnp.concatenate([r[0].astype(BF16) for r in k_pages], axis=1)
    v_t = jnp.concatenate([r[0].astype(BF16) for r in v_pages], axis=1)
    last = pl.num_programs(1) - 1
    blk_step = nk // SLC_LEN
    expand = expand_ref[pl.ds(pl.multiple_of((last - j) * blk_step, blk_step), slc_lanes), :]
    vis = jnp.dot(bm_scr[...], expand, preferred_element_type=F32)
    s = (jnp.dot(q, k_t, preferred_element_type=F32) + bpast_ref[jnp.where(j == last, 1, 0)]
         + jnp.where(vis > 0.5, 0.0, NEG_INF))
    _online_update(s, v_t, m_scr, l_scr, acc_scr, v_t=True)

    @pl.when(j == pl.num_programs(1) - 1)
    def _():
        gn = jax.nn.sigmoid(gn_ref[...])
        s_n = lax.dot_general(q, _pad_rows(ksn_ref[...], LANES).astype(BF16), _NT, preferred_element_type=F32) + bnew_ref[...]
        _online_update(s_n, _pad_rows(vsn_ref[...], LANES).astype(BF16), m_scr, l_scr, acc_scr)
        l = l_scr[...]
        comb = comb_scr[...] + gate_col(gn, 1) * (acc_scr[...] / jnp.where(l > 0.0, l, 1.0))
        _unstack_heads([comb[:half], comb[half:]], o_ref, ts)


def _nsa_sample(z, n_p, kc, vc, pool_k, pool_v, win_k, win_v, page_table, rel_table, ts):
    bs, n_pages = page_table.shape
    n_phys, _, page = pool_k.shape
    past_len = n_pages * page
    p_n = PAGES_PER_STEP
    n_steps = n_pages // p_n
    nk = p_n * page
    n_sub = kc.shape[1]
    n_cmp = n_sub - 1
    n_slc = -(-(past_len + ts) // SLC_LEN)
    slc_lanes = -(-n_slc // LANES) * LANES
    win_len = win_k.shape[2]
    assert n_pages % p_n == 0 and ts & (ts - 1) == 0 and ts <= SLC_LEN and past_len % SLC_LEN == 0 and n_p % ts == 0
    assert n_sub % LANES == 0 and win_len + ts >= WINDOW
    tbl = rel_table[:, :NSA_HEADS].reshape(REL_BUCKETS, NSA_KV, HPG)
    rows = NSA_HEADS * ts
    cmp_end = jnp.arange(n_sub) * CMP_STRIDE + CMP_LEN - 1
    biasc = _sample_row_bias(tbl, ts, cmp_end, past_len, jnp.broadcast_to(jnp.arange(n_sub) < n_cmp, (ts, n_sub)))
    bpast = _past_bias_tiles(tbl, ts, past_len, nk)
    bnew = _sample_row_bias(tbl, ts, past_len + jnp.arange(LANES), past_len, jnp.broadcast_to(jnp.arange(LANES) < ts, (ts, LANES)))
    wpos = past_len - win_len + jnp.arange(win_len)
    in_win = ((past_len + jnp.arange(ts))[:, None] - wpos[None, :]) < WINDOW
    bwin = _sample_row_bias(tbl, ts, wpos, past_len, in_win)
    mslc = jnp.pad(_slc_matrix(n_cmp, n_slc), ((0, n_sub - n_cmp), (0, slc_lanes - n_slc))).astype(BF16)
    expand = _block_expand_matrix(slc_lanes, nk, n_steps)
    row_blk = n_p // ts
    zrow = lambda name, width: pl.BlockSpec((ts, width), (lambda b, j, pt, cb=_Z_OFF[name] // width: (row_blk + b, cb)))
    seq3 = lambda shape: pl.BlockSpec((1,) + shape, lambda b, j, pt: (b, 0, 0))
    const2 = lambda shape: pl.BlockSpec(shape, lambda b, j, pt: (0, 0))
    grid_spec = pltpu.PrefetchScalarGridSpec(
        num_scalar_prefetch=1,
        grid=(bs, n_steps),
        in_specs=_page_specs(p_n, (LANES, page)) + _page_specs(p_n, (LANES, page)) + [
            zrow("q_n", 512), zrow("g_n", LANES),
            seq3((n_sub, LANES)), seq3((n_sub, LANES)),
            zrow("ks", LANES), zrow("vs", LANES),
            seq3((LANES, win_len)), seq3((LANES, win_len)),
            zrow("kw", LANES), zrow("vw", LANES),
            const2((rows, n_sub)),
            pl.BlockSpec((2, rows, nk), lambda b, j, pt: (0, 0, 0)),
            const2((rows, LANES)), const2((rows, win_len)), const2((n_sub, slc_lanes)), const2(expand.shape),
        ],
        out_specs=pl.BlockSpec((ts, 512), lambda b, j, pt: (b, 0)),
        scratch_shapes=[
            pltpu.VMEM((rows, LANES), F32),
            pltpu.VMEM((rows,1), F32), pltpu.VMEM((rows, 1), F32), pltpu.VMEM((rows, LANES), F32),
            pltpu.VMEM((rows, LANES), F32), pltpu.VMEM((rows, slc_lanes), BF16),
        ],
    )
    return pl.pallas_call(
        functools.partial(_nsa_sample_kernel, ts=ts, past_len=past_len, n_slc=n_slc, slc_lanes=slc_lanes),
        grid_spec=grid_spec,
        out_shape=jax.ShapeDtypeStruct((bs * ts, 512), F32),
        compiler_params=_params(("arbitrary", "arbitrary")),
        name="nsa_sample",
    )(page_table, *([pool_k] * p_n), *([pool_v] * p_n), z, z, kc, vc, z, z, win_k, win_v, z, z,
      biasc, bpast, bnew, bwin, mslc, expand)


def _moba_sample_kernel(pt_ref, *refs, ts, n_past_blocks):
    del pt_ref
    p_n = PAGES_PER_STEP
    k_pages, v_pages = refs[:p_n], refs[p_n:2 * p_n]
    (zq_ref, kn_ref, vn_ref, bpast_ref, bnew_ref, o_ref, q_scr, m_all, l_all, acc_all, km_scr) = refs[2 * p_n:]
    j = pl.program_id(1)
    page = k_pages[0].shape[2]
    ppb = MOBA_BLOCK // page
    bps = p_n // ppb
    rows = MOBA_KV * HPG * ts
    half = HPG * ts
    lane = lax.broadcasted_iota(jnp.int32, (rows, LANES), 1)
    kcol = lax.broadcasted_iota(jnp.int32, (LANES, LANES), 1)
    last = pl.num_programs(1) - 1

    @pl.when(j == 0)
    def _():
        _stack_queries_sample(zq_ref, q_scr, ts)
        m_all[...] = jnp.full(m_all.shape, NEG_INF, F32)
        l_all[...] = jnp.zeros(l_all.shape, F32)
        km_scr[...] = jnp.zeros(km_scr.shape, F32)

    q = q_scr[...].astype(BF16)
    for i in range(bps):
        n = j * bps + i
        k_t = jnp.concatenate([k_pages[i * ppb + r][0] for r in range(ppb)], axis=1)
        v_t = jnp.concatenate([v_pages[i * ppb + r][0].astype(BF16) for r in range(ppb)], axis=1)
        s = (jnp.dot(q, k_t.astype(BF16), preferred_element_type=F32)
             + bpast_ref[jnp.where(j == last, 1, 0), :, i * MOBA_BLOCK:(i + 1) * MOBA_BLOCK])
        mx, p, den = _softmax_tile(s)
        m_all[...] = jnp.where(lane == n, mx, m_all[...])
        l_all[...] = jnp.where(lane == n, den, l_all[...])
        acc_all[n] = _pv(p, v_t, True)
        km_scr[...] = jnp.where(kcol == n, jnp.mean(k_t, axis=1, keepdims=True), km_scr[...])

    @pl.when(j == pl.num_programs(1) - 1)
    def _():
        s_o = lax.dot_general(q, _pad_rows(kn_ref[...], LANES).astype(BF16), _NT, preferred_element_type=F32) + bnew_ref[...]
        m_o, p_o, l_o = _softmax_tile(s_o)
        acc_o = jnp.dot(p_o.astype(BF16), _pad_rows(vn_ref[...], LANES).astype(BF16), preferred_element_type=F32)
        km_hi, km_lo = _split_hi_lo(km_scr[...])
        gs = jnp.dot(q, km_hi, preferred_element_type=F32) + jnp.dot(q, km_lo, preferred_element_type=F32)
        gs = jnp.where(lane < n_past_blocks, gs, NEG_INF)
        sel = _take_topk(gs, min(MOBA_TOPK, n_past_blocks), lane)
        m_sel = jnp.where(sel > 0.5, m_all[...], NEG_INF)
        m_fin = jnp.maximum(jnp.max(m_sel, axis=-1, keepdims=True), m_o)
        wgt = jnp.where(sel > 0.5, jnp.exp(m_sel - m_fin), 0.0)
        w_o = jnp.exp(m_o - m_fin)
        den = jnp.sum(wgt * l_all[...], axis=-1, keepdims=True) + w_o * l_o
        num = w_o * acc_o
        for n in range(n_past_blocks):
            num = num + wgt[:, n:n + 1] * acc_all[n]
        res = num / den
        _unstack_heads([res[:half], res[half:]], o_ref, ts)


def _moba_sample(z, n_p, pool_k, pool_v, page_table, rel_table, ts):
    bs, n_pages = page_table.shape
    n_phys, _, page = pool_k.shape
    past_len = n_pages * page
    p_n = PAGES_PER_STEP
    n_steps = n_pages // p_n
    nk = p_n * page
    nb_past = past_len // MOBA_BLOCK
    assert (MOBA_BLOCK % page == 0 and p_n % (MOBA_BLOCK // page) == 0 and past_len % MOBA_BLOCK == 0
            and ts <= MOBA_BLOCK and nb_past <= LANES and n_pages % p_n == 0 and n_p % ts == 0)
    tbl = rel_table[:, NSA_HEADS:].reshape(REL_BUCKETS, MOBA_KV, HPG)
    rows = MOBA_HEADS * ts
    bpast = _past_bias_tiles(tbl, ts, past_len, nk)
    bnew = _sample_row_bias(tbl, ts, past_len + jnp.arange(LANES), past_len, jnp.broadcast_to(jnp.arange(LANES) < ts, (ts, LANES)))
    row_blk = n_p // ts
    zrow = lambda name, width: pl.BlockSpec((ts, width), (lambda b, j, pt, cb=_Z_OFF[name] // width: (row_blk + b, cb)))
    grid_spec = pltpu.PrefetchScalarGridSpec(
        num_scalar_prefetch=1,
        grid=(bs, n_steps),
        in_specs=_page_specs(p_n, (LANES, page)) + _page_specs(p_n, (LANES, page)) + [
            zrow("q_m", 512), zrow("k_m", LANES), zrow("v_m", LANES),
            pl.BlockSpec((2, rows, nk), lambda b, j, pt: (0, 0, 0)),
            pl.BlockSpec((rows, LANES), lambda b, j, pt: (0, 0)),
        ],
        out_specs=pl.BlockSpec((ts, 512), lambda b, j, pt: (b, 0)),
        scratch_shapes=[
            pltpu.VMEM((rows, LANES), F32),
            pltpu.VMEM((rows,LANES), F32), pltpu.VMEM((rows, LANES), F32),
            pltpu.VMEM((nb_past, rows, LANES), F32), pltpu.VMEM((LANES, LANES), F32),
        ],
    )
    return pl.pallas_call(
        functools.partial(_moba_sample_kernel, ts=ts, n_past_blocks=nb_past),
        grid_spec=grid_spec,
        out_shape=jax.ShapeDtypeStruct((bs * ts, 512), F32),
        compiler_params=_params(("arbitrary", "arbitrary")),
        name="moba_sample",
    )(page_table, *([pool_k] * p_n), *([pool_v] * p_n), z, z, z, bpast, bnew)


def _memkv_kernel(m_ref, gain_ref, w_ref, kgain_ref, seg_ref, k_ref, v_ref):
    x = m_ref[...]
    xb = (x * lax.rsqrt(jnp.mean(x * x, axis=-1, keepdims=True) + RMS_EPS) * gain_ref[...]).astype(BF16)
    kvw = k_ref.shape[1]
    k = jnp.dot(xb, w_ref[:, :kvw], preferred_element_type=F32)
    for h in range(MEM_HEADS):
        kh = k[:, h * MEM_HEAD_DIM:(h + 1) * MEM_HEAD_DIM]
        ss = _group_sumsq(kh, seg_ref[...])
        k_ref[:, h * MEM_HEAD_DIM:(h + 1) * MEM_HEAD_DIM] = kh * lax.rsqrt(ss * (1.0 / MEM_HEAD_DIM) + RMS_EPS) * kgain_ref[...]
    v_ref[...] = jnp.dot(xb, w_ref[:, kvw:], preferred_element_type=F32)


def _memory_kv_pallas(mem, norm_gain, w_mem_kv, k_gain, tm=256):
    n, d = mem.shape
    kvw = w_mem_kv.shape[1] // 2
    assert n % tm == 0
    row = lambda i: (i, 0)
    const = lambda i: (0, 0)
    return pl.pallas_call(
        _memkv_kernel,
        grid=(n // tm,),
        in_specs=[pl.BlockSpec((tm, d), row), pl.BlockSpec((1, d), const), pl.BlockSpec((d, 2 * kvw), const),
                  pl.BlockSpec((1, MEM_HEAD_DIM), const), pl.BlockSpec((LANES, LANES), const)],
        out_specs=[pl.BlockSpec((tm, kvw), row), pl.BlockSpec((tm, kvw), row)],
        out_shape=[jax.ShapeDtypeStruct((n, kvw), F32)] * 2,
        compiler_params=_params(("arbitrary",)),
        name="memory_kv",
    )(mem, norm_gain[None, :], w_mem_kv.astype(BF16), k_gain[None, :], _seg_matrix(MEM_HEAD_DIM))


def _mem_attend_kernel(zq_ref, mk_ref, mv_ref, o_ref):
    for h in range(MEM_HEADS):
        sl = slice(h * MEM_HEAD_DIM, (h + 1) * MEM_HEAD_DIM)
        q = (zq_ref[:, sl] * (MEM_HEAD_DIM ** -0.5)).astype(BF16)
        s = lax.dot_general(q, mk_ref[0, :, sl].astype(BF16), _NT, preferred_element_type=F32)
        p = jnp.exp(s - jnp.max(s, axis=-1, keepdims=True))
        o = jnp.dot(p.astype(BF16), mv_ref[0, :, sl].astype(BF16), preferred_element_type=F32)
        o_ref[:, sl] = o / jnp.sum(p, axis=-1, keepdims=True)


def _mem_attend_pallas(z, row0, n_seq, t_seq, tq, mk, mv):
    nq = t_seq // tq
    assert t_seq % tq == 0 and row0 % tq == 0
    m_len = mk.shape[1]
    width = MEM_HEADS * MEM_HEAD_DIM
    cb = _Z_OFF["q_x"] // width
    return pl.pallas_call(
        _mem_attend_kernel,
        grid=(n_seq, nq),
        in_specs=[pl.BlockSpec((tq, width), lambda b, i: (row0 // tq + b * nq + i, cb)),
                  pl.BlockSpec((1, m_len, width), lambda b, i: (b, 0, 0)),
                  pl.BlockSpec((1, m_len, width), lambda b, i: (b, 0, 0))],
        out_specs=pl.BlockSpec((tq, width), lambda b, i: (b * nq + i, 0)),
        out_shape=jax.ShapeDtypeStruct((n_seq * t_seq, width), F32),
        compiler_params=_params(("arbitrary", "arbitrary")),
        name="mem_attend",
    )(z, mk, mv)


def _zcol(z, name, width):
    return z[..., _Z_OFF[name]:_Z_OFF[name] + width]


def kernel(x_prompt, x_sample, cache_cmp_k, cache_cmp_v, cache_slc_k, cache_slc_v, cache_moba_k, cache_moba_v, cache_win_k, cache_win_v, cache_mem_k, cache_mem_v, page_table, mem_prompt, attn_norm_gain, w_in, nsa_q_gain, nsa_k_gain, cmp_pos_emb, cmp_w1, cmp_b1, cmp_w2, moba_q_gain, moba_k_gain, mem_norm_gain, w_mem_kv, mem_q_gain, mem_k_gain, rel_bias_table, w_branch, w_out, ffn_norm_gain, peer_w_q, peer_sub_keys, peer_u, peer_v):
    assert w_in.shape[0] == 1
    l = 0
    bp, tp, d = x_prompt.shape
    bs, ts, _ = x_sample.shape
    past_len = page_table.shape[1] * cache_cmp_k.shape[2]
    n_p, n_s = bp * tp, bs * ts

    x_all = jnp.concatenate([x_prompt.reshape(n_p, d), x_sample.reshape(n_s, d)], axis=0)
    hgain = _head_gain_row({"nsa_q": nsa_q_gain[l], "nsa_k1": nsa_k_gain[l, 1], "nsa_k2": nsa_k_gain[l, 2],
                            "moba_q": moba_q_gain[l], "moba_k": moba_k_gain[l], "mem_q": mem_q_gain[l]})
    z = _inproj(x_all, attn_norm_gain[l][None, :], _pack_w_in(w_in[l]), hgain)
    zp, zs = "prompt", "sample"

    def rows(group, name):
        col = _zcol(z, name, LANES)
        return col[:n_p].reshape(bp, tp, LANES) if group == zp else col[n_p:].reshape(bs, ts, LANES)

    kc_rows_p, vc_rows_p = rows(zp, "kc"), rows(zp, "vc")
    wk = _compress_weights(cmp_pos_emb[l, 0], cmp_w1[l, 0], cmp_b1[l, 0], cmp_w2[l, 0])
    wv = _compress_weights(cmp_pos_emb[l, 1], cmp_w1[l, 1], cmp_b1[l, 1], cmp_w2[l, 1])
    sub = lambda r: r.reshape(bp, tp // CMP_STRIDE, CMP_STRIDE * LANES)
    kgain = jnp.tile(nsa_k_gain[l, 0], 2)[None, :]
    kc_p = _compress(sub(kc_rows_p), wk, kgain, True)
    vc_p = _compress(sub(vc_rows_p), wv, kgain, False)
    o_nsa_p = _nsa_prompt(z, kc_p, vc_p, rel_bias_table, bp, tp)
    o_moba_p = _moba_prompt(z, rel_bias_table, bp, tp)
    m_len = mem_prompt.shape[1]
    mk_p, mv_p = _memory_kv_pallas(mem_prompt.reshape(bp * m_len, d), mem_norm_gain[l], w_mem_kv[l], mem_k_gain[l])
    mem_w = MEM_HEADS * MEM_HEAD_DIM
    o_mem_p = _mem_attend_pallas(z, 0, bp, tp, 256, mk_p.reshape(bp, m_len, mem_w), mv_p.reshape(bp, m_len, mem_w))
    mk_p = mk_p.reshape(bp, m_len, MEM_HEADS, MEM_HEAD_DIM)
    mv_p = mv_p.reshape(bp, m_len, MEM_HEADS, MEM_HEAD_DIM)

    keys_on_lanes = lambda c: jnp.transpose(c[l], (0, 2, 3, 1)).reshape(c.shape[1], LANES, c.shape[2])
    kc_s = _compress_paged(keys_on_lanes(cache_cmp_k), page_table, wk, kgain, True)
    vc_s = _compress_paged(keys_on_lanes(cache_cmp_v), page_table, wv, kgain, False)
    win_len = cache_win_k.shape[2]
    win_k = cache_win_k[l].reshape(bs, win_len, LANES)
    win_v = cache_win_v[l].reshape(bs, win_len, LANES)
    o_nsa_s = _nsa_sample(z, n_p, kc_s, vc_s, keys_on_lanes(cache_slc_k), keys_on_lanes(cache_slc_v),
                          keys_on_lanes(cache_win_k), keys_on_lanes(cache_win_v), page_table, rel_bias_table, ts)
    o_moba_s = _moba_sample(z, n_p, keys_on_lanes(cache_moba_k), keys_on_lanes(cache_moba_v), page_table,
                            rel_bias_table, ts)
    o_mem_s = _mem_attend_pallas(z, n_p, bs, ts, ts, cache_mem_k[l].reshape(bs, -1, mem_w), cache_mem_v[l].reshape(bs, -1, mem_w))
    kw_all = jnp.concatenate([win_k, rows(zs, "kw")], axis=1)
    vw_all = jnp.concatenate([win_v, rows(zs, "vw")], axis=1)

    cat = lambda a, b_: jnp.concatenate([a.reshape(n_p, 512), b_.reshape(n_s, 512)], axis=0)
    x1, xn_b = _merge(cat(o_nsa_p, o_nsa_s), cat(o_moba_p, o_moba_s), cat(o_mem_p, o_mem_s), z, x_all,
                      w_branch[l], w_out[l], ffn_norm_gain[l])
    y = _peer(x1, xn_b, peer_w_q[l], peer_sub_keys[l], peer_u[l].astype(BF16), peer_v[l].astype(BF16).T)
    y_p = y[:n_p].reshape(bp, tp, d)
    y_s = y[n_p:].reshape(bs, ts, d)

    def st(r, b_, t_):
        return r.reshape(1, b_, t_, 2, HEAD_DIM)

    keep_p = min(WINDOW, tp)
    p_win_k = st(rows(zp, "kw")[:, tp - keep_p:], bp, keep_p)
    p_win_v = st(rows(zp, "vw")[:, tp - keep_p:], bp, keep_p)
    keep_s = min(WINDOW, past_len + ts)
    s_win_k = st(kw_all[:, kw_all.shape[1] - keep_s:], bs, keep_s)
    s_win_v = st(vw_all[:, vw_all.shape[1] - keep_s:], bs, keep_s)
    return (y_p, y_s,
            st(kc_rows_p, bp, tp), st(vc_rows_p, bp, tp), st(rows(zp, "ks"), bp, tp), st(rows(zp, "vs"), bp, tp),
            st(rows(zp, "k_m"), bp, tp), st(rows(zp, "v_m"), bp, tp), p_win_k, p_win_v, mk_p[None], mv_p[None],
            st(rows(zs, "kc"), bs, ts), st(rows(zs, "vc"), bs, ts), st(rows(zs, "ks"), bs, ts), st(rows(zs, "vs"), bs, ts),
            st(rows(zs, "k_m"), bs, ts), st(rows(zs, "v_m"), bs, ts), s_win_k, s_win_v)
```
